```python
import jax, jax.numpy as jnp
from jax import lax
import numpy as np

D_MODEL = 1024
BATCH = 32
SEQ = 256
DEPTH = 2
DEC_BATCH = 4
DEC_SEQ = 2048
PAST_LEN = 256

GRID_W = 64
HEAD_DIM = 64
N_HEADS_A = 8
N_KV_A = 2
N_HEADS_B = 8
N_KV_B = 2
Q_BLOCK = 128
WINDOW = 128
ROPE_THETA = 10000.0
SSM_WIDTH = D_MODEL
SSM_GROUP = 16
SSM_GROUPS = SSM_WIDTH // SSM_GROUP
SSM_STATE = 64
N_EXPERTS = 16
EC_FACTOR = 2
D_FF = 2 * D_MODEL
EPS = 1e-6
NEG_INF = -1e30
ATTN_SPLITS = (N_HEADS_A * HEAD_DIM, N_KV_A * HEAD_DIM, N_KV_A * HEAD_DIM,
               N_HEADS_B * HEAD_DIM, N_KV_B * HEAD_DIM, N_KV_B * HEAD_DIM)
ATTN_IN = sum(ATTN_SPLITS)
ATTN_OUT = (N_HEADS_A + N_HEADS_B) * HEAD_DIM

kernel_name = 'hybrid_diffusion_prefix_trunk_step'


def rms_norm(x, gain):
    x32 = x.astype(jnp.float32)
    y = x32 * lax.rsqrt(jnp.mean(x32 * x32, axis=-1, keepdims=True) + EPS)
    return (y * gain.astype(jnp.float32)).astype(x.dtype)


def adaln(cond, w_mod, b_mod):
    m = jnp.einsum('nd,de->ne', jax.nn.silu(cond), w_mod) + b_mod
    return jnp.split(m[:, None, :], 6, axis=-1)


def modulate(x, shift, scale):
    return x * (1.0 + scale) + shift


def axial_rope_angles(n_tokens):
    rows = n_tokens // GRID_W
    row = jnp.broadcast_to(jnp.arange(rows, dtype=jnp.float32)[:, None], (rows, GRID_W)).reshape(-1)
    col = jnp.broadcast_to(jnp.arange(GRID_W, dtype=jnp.float32)[None, :], (rows, GRID_W)).reshape(-1)
    n_freq = HEAD_DIM // 4
    inv_freq = ROPE_THETA ** (-jnp.arange(n_freq, dtype=jnp.float32) / n_freq)
    ang = jnp.concatenate([row[:, None] * inv_freq, col[:, None] * inv_freq], axis=-1)
    return jnp.cos(ang), jnp.sin(ang)


def apply_rope(x, cos, sin):
    B, T, H, D = x.shape
    xp = x.astype(jnp.float32).reshape(B, T, H, D // 2, 2)
    x0, x1 = xp[..., 0], xp[..., 1]
    c, s = cos[None, :, None, :], sin[None, :, None, :]
    return jnp.stack([x0 * c - x1 * s, x0 * s + x1 * c], axis=-1).reshape(B, T, H, D).astype(x.dtype)


def softmax_with_sink(s, sink):
    if sink is None:
        return jax.nn.softmax(s, axis=-1)
    col = jnp.broadcast_to(sink.astype(jnp.float32)[:, :, None, None], s.shape[:-1] + (1,))
    return jax.nn.softmax(jnp.concatenate([s, col], axis=-1), axis=-1)[..., :-1]


def dense_block_attention(q, k, v, sink):
    B, T, Hkv, G, hd = q.shape
    nb = T // Q_BLOCK
    qb = jnp.moveaxis(q.reshape(B, nb, Q_BLOCK, Hkv, G, hd), 1, 0)
    k32 = k.astype(jnp.float32)
    v32 = v.astype(jnp.float32)

    def one_block(qi):
        s = jnp.einsum('bqhgd,bkhd->bhgqk', qi.astype(jnp.float32) * hd ** -0.5, k32)
        p = softmax_with_sink(s, sink)
        return jnp.einsum('bhgqk,bkhd->bqhgd', p, v32)

    o = lax.map(one_block, qb)
    return jnp.moveaxis(o, 0, 1).reshape(B, T, Hkv * G * hd).astype(q.dtype)


def banded_attention(q, k, v, k_ctx, v_ctx, sink):
    B, T, Hkv, G, hd = q.shape
    nb = T // Q_BLOCK
    span = Q_BLOCK + 2 * WINDOW
    starts = jnp.arange(nb) * Q_BLOCK
    idx = starts[:, None] + jnp.arange(span)[None, :]
    kpos = idx - WINDOW
    qpos = starts[:, None] + jnp.arange(Q_BLOCK)[None, :]
    valid = ((jnp.abs(qpos[:, :, None] - kpos[:, None, :]) <= WINDOW)
             & (kpos >= 0)[:, None, :] & (kpos < T)[:, None, :])
    pad = ((0, 0), (WINDOW, WINDOW), (0, 0), (0, 0))
    kb = jnp.pad(k, pad)[:, idx].astype(jnp.float32)
    vb = jnp.pad(v, pad)[:, idx].astype(jnp.float32)
    qb = q.reshape(B, nb, Q_BLOCK, Hkv, G, hd).astype(jnp.float32) * hd ** -0.5
    s_loc = jnp.einsum('bnqhgd,bnkhd->bnhgqk', qb, kb)
    s_loc = jnp.where(valid[None, :, None, None], s_loc, NEG_INF)
    s_ctx = jnp.einsum('bnqhgd,bkhd->bnhgqk', qb, k_ctx.astype(jnp.float32))
    p = softmax_with_sink(jnp.concatenate([s_loc, s_ctx], axis=-1), sink)
    o = (jnp.einsum('bnhgqk,bnkhd->bnqhgd', p[..., :span], vb)
         + jnp.einsum('bnhgqk,bkhd->bnqhgd', p[..., span:], v_ctx.astype(jnp.float32)))
    return o.reshape(B, T, Hkv * G * hd).astype(q.dtype)


def attn_project(h, w_in, qn_a, kn_a, qn_b, kn_b):
    B, T, _ = h.shape
    proj = jnp.einsum('btd,de->bte', h, w_in)
    cuts = [sum(ATTN_SPLITS[:i]) for i in range(1, len(ATTN_SPLITS))]
    qa, ka, va, qb, kb, vb = jnp.split(proj, cuts, axis=-1)
    heads = lambda t: t.reshape(B, T, -1, HEAD_DIM)
    return (rms_norm(heads(qa), qn_a), rms_norm(heads(ka), kn_a), heads(va),
            rms_norm(heads(qb), qn_b), rms_norm(heads(kb), kn_b), heads(vb))


def group_heads(q, n_kv):
    B, T, H, D = q.shape
    return q.reshape(B, T, n_kv, H // n_kv, D)


def attn_mixer_context(h, w_in, qn_a, kn_a, qn_b, kn_b, sink_b, w_out):
    qa, ka, va, qb, kb, vb = attn_project(h, w_in, qn_a, kn_a, qn_b, kn_b)
    sink = sink_b.reshape(N_KV_B, N_HEADS_B // N_KV_B)
    oa = dense_block_attention(group_heads(qa, N_KV_A), ka, va, None)
    ob = dense_block_attention(group_heads(qb, N_KV_B), kb, vb, sink)
    out = jnp.einsum('bte,ed->btd', jnp.concatenate([oa, ob], axis=-1), w_out)
    return out, (ka, va, kb, vb)


def attn_mixer_latent(h, w_in, qn_a, kn_a, qn_b, kn_b, sink_b, w_out, ck_a, cv_a, ck_b, cv_b, cos, sin):
    qa, ka, va, qb, kb, vb = attn_project(h, w_in, qn_a, kn_a, qn_b, kn_b)
    qa, ka, qb, kb = [apply_rope(t, cos, sin) for t in (qa, ka, qb, kb)]
    sink = sink_b.reshape(N_KV_B, N_HEADS_B // N_KV_B)
    k_all = jnp.concatenate([ck_a.astype(ka.dtype), ka], axis=1)
    v_all = jnp.concatenate([cv_a.astype(va.dtype), va], axis=1)
    oa = dense_block_attention(group_heads(qa, N_KV_A), k_all, v_all, None)
    ob = banded_attention(group_heads(qb, N_KV_B), kb, vb, ck_b, cv_b, sink)
    return jnp.einsum('bte,ed->btd', jnp.concatenate([oa, ob], axis=-1), w_out)


def ssm_combine(left, right):
    a_l, b_l = left
    a_r, b_r = right
    return a_l * a_r, a_r * b_l + b_r


def linear_scan(u, lam_dt, lam_bar, b_bar, h0):
    bu = jnp.einsum('gpc,btgc->btgp', b_bar, u)
    a = jnp.broadcast_to(lam_bar, bu.shape)
    _, h = lax.associative_scan(ssm_combine, (a, bu), axis=1)
    if h0 is not None:
        steps = jnp.arange(1, u.shape[1] + 1, dtype=jnp.float32)[:, None, None]
        h = h + jnp.exp(lam_dt[None] * steps)[None] * h0[:, None]
    return h


def ssm_core(h, w_in, lam_re, lam_im, b_re, b_im, c_re, c_im, log_dt, d_skip, w_out, h0):
    B, T, _ = h.shape
    u = jnp.einsum('btd,de->bte', h, w_in).astype(jnp.float32)
    uc = u.reshape(B, T, SSM_GROUPS, SSM_GROUP).astype(jnp.complex64)
    lam = lax.complex(lam_re.astype(jnp.float32), lam_im.astype(jnp.float32))
    lam_dt = lam * jnp.exp(log_dt.astype(jnp.float32))[..., None]
    lam_bar = jnp.exp(lam_dt)
    b_bar = ((lam_bar - 1.0) / lam)[..., None] * lax.complex(b_re.astype(jnp.float32), b_im.astype(jnp.float32))
    c_c = lax.complex(c_re.astype(jnp.float32), c_im.astype(jnp.float32))
    h0f = None if h0 is None else h0[:, 0]
    h0b = None if h0 is None else h0[:, 1]
    hf = linear_scan(uc, lam_dt[0], lam_bar[0], b_bar[0], h0f)
    hb = linear_scan(jnp.flip(uc, axis=1), lam_dt[1], lam_bar[1], b_bar[1], h0b)
    y_f = jnp.einsum('gcp,btgp->btgc', c_c[0], hf)
    y_b = jnp.flip(jnp.einsum('gcp,btgp->btgc', c_c[1], hb), axis=1)
    y = jnp.real(y_f + y_b).reshape(B, T, SSM_WIDTH) + d_skip.astype(jnp.float32) * u
    a, g = jnp.split(jnp.einsum('bte,ef->btf', jax.nn.gelu(y), w_out.astype(jnp.float32)), 2, axis=-1)
    return (a * jax.nn.sigmoid(g)).astype(h.dtype), hf, hb


def ssm_mixer_context(h, w_in, lam_re, lam_im, b_re, b_im, c_re, c_im, log_dt, d_skip, w_out):
    out, hf, hb = ssm_core(h, w_in, lam_re, lam_im, b_re, b_im, c_re, c_im, log_dt, d_skip, w_out, None)
    final = jnp.stack([hf[:, -1], hb[:, -1]], axis=1)
    return out, (jnp.real(final), jnp.imag(final))


def ssm_mixer_latent(h, w_in, lam_re, lam_im, b_re, b_im, c_re, c_im, log_dt, d_skip, w_out, st_re, st_im):
    h0 = lax.complex(st_re.astype(jnp.float32), st_im.astype(jnp.float32))
    out, _, _ = ssm_core(h, w_in, lam_re, lam_im, b_re, b_im, c_re, c_im, log_dt, d_skip, w_out, h0)
    return out


def ec_moe(h, router, w_gate, w_up, w_down):
    B, T, D = h.shape
    cap = EC_FACTOR * T // N_EXPERTS
    aff = jax.nn.softmax(jnp.einsum('btd,de->bte', h.astype(jnp.float32), router.astype(jnp.float32)), axis=-1)
    gate, idx = lax.top_k(jnp.swapaxes(aff, 1, 2), cap)
    xe = jax.vmap(lambda hb, ib: hb[ib])(h, idx)
    g = jnp.einsum('becd,edf->becf', xe, w_gate)
    u = jnp.einsum('becd,edf->becf', xe, w_up)
    ye = jnp.einsum('becf,efd->becd', jax.nn.silu(g) * u, w_down) * gate[..., None].astype(h.dtype)
    return jax.vmap(lambda ib, yb: jnp.zeros((T, D), yb.dtype).at[ib.reshape(-1)].add(yb.reshape(-1, D)))(idx, ye)


def setup_inputs(seed: int = 0) -> dict:
    key = jax.random.key(seed)
    ks = jax.random.split(key, 64)
    counter = [0]

    def nxt():
        counter[0] += 1
        return ks[counter[0]]

    def normal(shape, scale):
        return jax.random.normal(nxt(), shape, jnp.float32) * scale

    def gain(n):
        return 1.0 + normal((n,), 0.01)

    D = D_MODEL
    inputs = dict(
        x_prompt=normal((BATCH, SEQ, D), 1.0),
        x_sample=normal((DEC_BATCH, DEC_SEQ, D), 1.0),
        c=normal((DEC_BATCH, D), 1.0),
        cache_k_a_l0=normal((DEC_BATCH, PAST_LEN, N_KV_A, HEAD_DIM), 1.0),
        cache_v_a_l0=normal((DEC_BATCH, PAST_LEN, N_KV_A, HEAD_DIM), 1.0),
        cache_k_b_l0=normal((DEC_BATCH, PAST_LEN, N_KV_B, HEAD_DIM), 1.0),
        cache_v_b_l0=normal((DEC_BATCH, PAST_LEN, N_KV_B, HEAD_DIM), 1.0),
        state_ssm_re_l1=normal((DEC_BATCH, 2, SSM_GROUPS, SSM_STATE), 0.1),
        state_ssm_im_l1=normal((DEC_BATCH, 2, SSM_GROUPS, SSM_STATE), 0.1),
        c_ctx=normal((D,), 1.0),
        mod_w_l0=normal((D, 6 * D), 0.5 * D ** -0.5),
        mod_b_l0=normal((6 * D,), 0.02),
        norm_mix_l0=gain(D),
        attn_w_in_l0=normal((D, ATTN_IN), D ** -0.5),
        q_norm_a_l0=gain(HEAD_DIM),
        k_norm_a_l0=gain(HEAD_DIM),
        q_norm_b_l0=gain(HEAD_DIM),
        k_norm_b_l0=gain(HEAD_DIM),
        sink_b_l0=normal((N_HEADS_B,), 0.5),
        attn_w_out_l0=normal((ATTN_OUT, D), ATTN_OUT ** -0.5),
        norm_ffn_l0=gain(D),
        router_l0=normal((D, N_EXPERTS), D ** -0.5),
        moe_w_gate_l0=normal((N_EXPERTS, D, D_FF), D ** -0.5),
        moe_w_up_l0=normal((N_EXPERTS, D, D_FF), D ** -0.5),
        moe_w_down_l0=normal((N_EXPERTS, D_FF, D), D_FF ** -0.5),
        mod_w_l1=normal((D, 6 * D), 0.5 * D ** -0.5),
        mod_b_l1=normal((6 * D,), 0.02),
        norm_mix_l1=gain(D),
        ssm_w_in_l1=normal((D, SSM_WIDTH), D ** -0.5),
        ssm_lambda_re_l1=-0.5 + normal((2, SSM_GROUPS, SSM_STATE), 0.01),
        ssm_lambda_im_l1=jnp.broadcast_to(jnp.pi * jnp.arange(SSM_STATE, dtype=jnp.float32), (2, SSM_GROUPS, SSM_STATE)) + normal((2, SSM_GROUPS, SSM_STATE), 0.01),
        ssm_b_re_l1=normal((2, SSM_GROUPS, SSM_STATE, SSM_GROUP), (2 * SSM_GROUP) ** -0.5),
        ssm_b_im_l1=normal((2, SSM_GROUPS, SSM_STATE, SSM_GROUP), (2 * SSM_GROUP) ** -0.5),
        ssm_c_re_l1=normal((2, SSM_GROUPS, SSM_GROUP, SSM_STATE), SSM_STATE ** -0.5),
        ssm_c_im_l1=normal((2, SSM_GROUPS, SSM_GROUP, SSM_STATE), SSM_STATE ** -0.5),
        ssm_log_dt_l1=jax.random.uniform(nxt(), (2, SSM_GROUPS), jnp.float32, np.log(0.001).astype(np.float32), np.log(0.1).astype(np.float32)),
        ssm_d_l1=normal((SSM_WIDTH,), 1.0),
        ssm_w_out_l1=normal((SSM_WIDTH, 2 * D), SSM_WIDTH ** -0.5),
        norm_ffn_l1=gain(D),
        router_l1=normal((D, N_EXPERTS), D ** -0.5),
        moe_w_gate_l1=normal((N_EXPERTS, D, D_FF), D ** -0.5),
        moe_w_up_l1=normal((N_EXPERTS, D, D_FF), D ** -0.5),
        moe_w_down_l1=normal((N_EXPERTS, D_FF, D), D_FF ** -0.5),
    )
    return inputs


def reference(x_prompt, x_sample, c, cache_k_a_l0, cache_v_a_l0, cache_k_b_l0, cache_v_b_l0,
              state_ssm_re_l1, state_ssm_im_l1, c_ctx,
              mod_w_l0, mod_b_l0, norm_mix_l0, attn_w_in_l0, q_norm_a_l0, k_norm_a_l0, q_norm_b_l0, k_norm_b_l0,
              sink_b_l0, attn_w_out_l0, norm_ffn_l0, router_l0, moe_w_gate_l0, moe_w_up_l0, moe_w_down_l0,
              mod_w_l1, mod_b_l1, norm_mix_l1, ssm_w_in_l1, ssm_lambda_re_l1, ssm_lambda_im_l1, ssm_b_re_l1, ssm_b_im_l1,
              ssm_c_re_l1, ssm_c_im_l1, ssm_log_dt_l1, ssm_d_l1, ssm_w_out_l1, norm_ffn_l1, router_l1,
              moe_w_gate_l1, moe_w_up_l1, moe_w_down_l1):
    layers = [
        {'mod': (mod_w_l0, mod_b_l0), 'norm_mix': norm_mix_l0, 'norm_ffn': norm_ffn_l0,
         'moe': (router_l0, moe_w_gate_l0, moe_w_up_l0, moe_w_down_l0),
         'mixer': (attn_w_in_l0, q_norm_a_l0, k_norm_a_l0, q_norm_b_l0, k_norm_b_l0, sink_b_l0, attn_w_out_l0),
         'cache': (cache_k_a_l0, cache_v_a_l0, cache_k_b_l0, cache_v_b_l0)},
        {'mod': (mod_w_l1, mod_b_l1), 'norm_mix': norm_mix_l1, 'norm_ffn': norm_ffn_l1,
         'moe': (router_l1, moe_w_gate_l1, moe_w_up_l1, moe_w_down_l1),
         'mixer': (ssm_w_in_l1, ssm_lambda_re_l1, ssm_lambda_im_l1, ssm_b_re_l1, ssm_b_im_l1,
                   ssm_c_re_l1, ssm_c_im_l1, ssm_log_dt_l1, ssm_d_l1, ssm_w_out_l1),
         'cache': (state_ssm_re_l1, state_ssm_im_l1)},
    ]
    cos, sin = axial_rope_angles(x_sample.shape[1])
    xp, xs = x_prompt, x_sample
    new_state = []
    for layer in range(DEPTH):
        p = layers[layer]
        sh1p, sc1p, g1p, sh2p, sc2p, g2p = adaln(c_ctx[None], *p['mod'])
        sh1s, sc1s, g1s, sh2s, sc2s, g2s = adaln(c, *p['mod'])
        hp = modulate(rms_norm(xp, p['norm_mix']), sh1p, sc1p)
        hs = modulate(rms_norm(xs, p['norm_mix']), sh1s, sc1s)
        if layer % 2 == 0:
            mp, ctx_tensors = attn_mixer_context(hp, *p['mixer'])
            ms = attn_mixer_latent(hs, *p['mixer'], *p['cache'], cos, sin)
        else:
            mp, ctx_tensors = ssm_mixer_context(hp, *p['mixer'])
            ms = ssm_mixer_latent(hs, *p['mixer'], *p['cache'])
        new_state.extend(ctx_tensors)
        xp = xp + g1p * mp
        xs = xs + g1s * ms
        hp = modulate(rms_norm(xp, p['norm_ffn']), sh2p, sc2p)
        hs = modulate(rms_norm(xs, p['norm_ffn']), sh2s, sc2s)
        xp = xp + g2p * ec_moe(hp, *p['moe'])
        xs = xs + g2s * ec_moe(hs, *p['moe'])
    k_a_l0, v_a_l0, k_b_l0, v_b_l0, ssm_re_l1, ssm_im_l1 = new_state
    return (xp, xs, k_a_l0, v_a_l0, k_b_l0, v_b_l0, ssm_re_l1, ssm_im_l1)
```

```python
import functools
import itertools

import jax
import jax.numpy as jnp
import numpy as np
from jax import lax
from jax.experimental import pallas as pl
from jax.experimental.pallas import tpu as pltpu

F32, BF16, I32 = jnp.float32, jnp.bfloat16, jnp.int32

D_MODEL = 1024
GRID_W = 64
HEAD_DIM = 64
N_HEADS = 8
N_KV = 2
WINDOW = 128
ROPE_THETA = 10000.0
SSM_GROUP = 16
SSM_GROUPS = D_MODEL // SSM_GROUP
SSM_STATE = 64
N_EXPERTS = 16
EC_FACTOR = 2
D_FF = 2 * D_MODEL
EPS = 1e-6
NEG_INF = -1e30
QKV_COLS = 2 * (N_HEADS + 2 * N_KV) * HEAD_DIM
ATTN_OUT = 2 * N_HEADS * HEAD_DIM

SSM_CHUNK = 16
SSM_ROWS = SSM_CHUNK * SSM_GROUP
SSM_GB = 4
SSM_PASSES = 3

TM = 256
VMEM_LIMIT = 56 * 1024 * 1024

NN = (((1,), (0,)), ((), ()))
NT = (((1,), (1,)), ((), ()))


def _dot(a, b, dims=NN):
    return lax.dot_general(a, b, dims, preferred_element_type=F32)


def _split2(x):
    hi = x.astype(BF16)
    lo = (x - hi.astype(F32)).astype(BF16)
    return hi, lo


def _split3(x):
    hi = x.astype(BF16)
    r = x - hi.astype(F32)
    mid = r.astype(BF16)
    lo = (r - mid.astype(F32)).astype(BF16)
    return hi, mid, lo


def _dot3(a, b, dims=NN):
    ah, al = _split2(a)
    bh, bl = _split2(b)
    return _dot(ah, bh, dims) + (_dot(ah, bl, dims) + _dot(al, bh, dims))


def _mm(a, b, passes):
    if passes == 1:
        return _dot(a.astype(BF16), b.astype(BF16))
    return _dot3(a, b)


def _dot_sel(x, sel):
    hi, mid, lo = _split3(x)
    return _dot(hi, sel) + (_dot(mid, sel) + _dot(lo, sel))


def _sigmoid(x):
    return 1.0 / (1.0 + jnp.exp(-x))


def _norm_mod(x, gain, shift, scale):
    ms = jnp.mean(x * x, axis=-1, keepdims=True)
    y = x * lax.rsqrt(ms + EPS) * gain
    return y * (1.0 + scale) + shift


def _params(*sem):
    return pltpu.CompilerParams(dimension_semantics=sem, vmem_limit_bytes=VMEM_LIMIT)


def _adaln_kernel(c_ref, w_ref, b_ref, o_ref):
    c = c_ref[...]
    s = c * _sigmoid(c)
    o_ref[...] = _dot3(s, w_ref[...]) + b_ref[...]


def _adaln(cond8, w_mod, b_mod):
    d, e = w_mod.shape
    tn = 1536
    return pl.pallas_call(
        _adaln_kernel,
        grid=(e // tn,),
        in_specs=[pl.BlockSpec((8, d), lambda j: (0, 0)),
                  pl.BlockSpec((d, tn), lambda j: (0, j)),
                  pl.BlockSpec((1, tn), lambda j: (0, j))],
        out_specs=pl.BlockSpec((8, tn), lambda j: (0, j)),
        out_shape=jax.ShapeDtypeStruct((8, e), F32),
        compiler_params=_params("parallel"),
        name="adaln",
    )(cond8, w_mod, b_mod.reshape(1, e))


def _mod_rows(cond8, w_mod, b_mod):
    m = _adaln(cond8, w_mod, b_mod).reshape(8, 6, D_MODEL)
    return jnp.pad(m, ((0, 0), (0, 2), (0, 0)))


def _mod_spec(rows_per_cond):
    if rows_per_cond is None:
        return pl.BlockSpec((1, 8, D_MODEL), lambda i: (0, 0, 0))
    return pl.BlockSpec((1, 8, D_MODEL), lambda i: (1 + (i * TM) // rows_per_cond, 0, 0))


def _qkv_kernel(x_ref, mod_ref, gain_ref, w_ref, hg_ref, bd_ref, cos_ref, sin_ref,
                q_ref, ka_ref, va_ref, kb_ref, vb_ref, *, rope):
    h = _norm_mod(x_ref[...], gain_ref[...], mod_ref[0, 0:1, :], mod_ref[0, 1:2, :])
    proj = _dot(h.astype(BF16), w_ref[...])
    bd = bd_ref[...]

    def head_norm(blk, g):
        hi, lo = _split2(blk * blk)
        ms = _dot(hi, bd) + _dot(lo, bd)
        return blk * lax.rsqrt(ms + EPS) * g

    def rotary(blk):
        w = blk.shape[1]
        even = (lax.broadcasted_iota(I32, blk.shape, 1) & 1) == 0
        swapped = jnp.where(even, pltpu.roll(blk, w - 1, 1), pltpu.roll(blk, 1, 1))
        return blk * cos_ref[:, :w] + swapped * sin_ref[:, :w]

    def qk(c0):
        blk = head_norm(proj[:, c0:c0 + 256], hg_ref[:, c0:c0 + 256])
        return rotary(blk) if rope else blk

    q_ref[:, 0:256] = qk(0).astype(q_ref.dtype)
    q_ref[:, 256:512] = qk(256).astype(q_ref.dtype)
    q_ref[:, 512:768] = qk(768).astype(q_ref.dtype)
    q_ref[:, 768:1024] = qk(1024).astype(q_ref.dtype)
    kva = qk(512)
    ka_ref[...] = kva[:, :128].astype(ka_ref.dtype)
    va_ref[...] = proj[:, 640:768].astype(va_ref.dtype)
    kvb = qk(1280)
    kb_ref[...] = kvb[:, :128].astype(kb_ref.dtype)
    vb_ref[...] = proj[:, 1408:1536].astype(vb_ref.dtype)


def _qkv(x, mod, gain, w_bf, hgain, bd, cos_t, sin_t, *, rows_per_cond, seq, rope, kv_dtype):
    n = x.shape[0]
    tiles_per_seq = seq // TM
    row = lambda i: (i, 0)
    const = lambda i: (0, 0)
    pos = lambda i: (i % tiles_per_seq, 0)
    kv_shape = jax.ShapeDtypeStruct((n, 128), kv_dtype)
    return pl.pallas_call(
        functools.partial(_qkv_kernel, rope=rope),
        grid=(n // TM,),
        in_specs=[pl.BlockSpec((TM, D_MODEL), row),
                  _mod_spec(rows_per_cond),
                  pl.BlockSpec((1, D_MODEL), const),
                  pl.BlockSpec((D_MODEL, QKV_COLS), const),
                  pl.BlockSpec((1, QKV_COLS), const),
                  pl.BlockSpec((256, 256), const),
                  pl.BlockSpec((TM, 256), pos),
                  pl.BlockSpec((TM, 256), pos)],
        out_specs=[pl.BlockSpec((TM, ATTN_OUT), row)] + [pl.BlockSpec((TM, 128), row)] * 4,
        out_shape=[jax.ShapeDtypeStruct((n, ATTN_OUT), BF16)] + [kv_shape] * 4,
        compiler_params=_params("parallel"),
        name="qkv_rope" if rope else "qkv",
    )(x, mod, gain, w_bf, hgain, bd, cos_t, sin_t)


def _pad_variants(kk):
    left = lax.broadcasted_iota(I32, kk.shape, 1) < HEAD_DIM
    rolled = pltpu.roll(kk, HEAD_DIM, 1)
    zero = jnp.zeros_like(kk)
    return {(0, 0): jnp.where(left, kk, zero).astype(BF16),
            (0, 1): jnp.where(left, zero, rolled).astype(BF16),
            (1, 0): jnp.where(left, rolled, zero).astype(BF16),
            (1, 1): jnp.where(left, zero, kk).astype(BF16)}


def _head_attention(qp, keys, vals, masks, sink):
    scores = []
    for kblk, mask in zip(keys, masks):
        s = _dot(qp, kblk, NT)
        if mask is not None:
            s = jnp.where(mask, s, NEG_INF)
        scores.append(s)
    m = scores[0].max(axis=-1, keepdims=True)
    for s in scores[1:]:
        m = jnp.maximum(m, s.max(axis=-1, keepdims=True))
    if sink is not None:
        m = jnp.maximum(m, sink)
    den = None
    out = None
    for s, vblk in zip(scores, vals):
        p = jnp.exp(s - m)
        ps = p.sum(axis=-1, keepdims=True)
        den = ps if den is None else den + ps
        o = _dot(p.astype(BF16), vblk)
        out = o if out is None else out + o
    if sink is not None:
        den = den + jnp.exp(sink - m)
    return out / den


def _attn_ctx_kernel(sink_ref, q_ref, ka_ref, va_ref, kb_ref, vb_ref, o_ref):
    for mixer, (k_ref, v_ref) in enumerate(((ka_ref, va_ref), (kb_ref, vb_ref))):
        kvar = _pad_variants(k_ref[...].astype(F32))
        vvar = _pad_variants(v_ref[...].astype(F32))
        for t in range(4):
            tile = mixer * 4 + t
            kv = t // 2
            qp = q_ref[:, tile * 128:(tile + 1) * 128]
            acc = None
            for par in range(2):
                sink = sink_ref[2 * t + par] if mixer == 1 else None
                o = _head_attention(qp, [kvar[(kv, par)]], [vvar[(kv, par)]], [None], sink)
                acc = o if acc is None else acc + o
            o_ref[:, tile * 128:(tile + 1) * 128] = acc.astype(o_ref.dtype)


def _attn_ctx(sink, q, ka, va, kb, vb, seq):
    n = q.shape[0]
    row = lambda b: (b, 0)
    kv_spec = pl.BlockSpec((seq, 128), row)
    return pl.pallas_call(
        _attn_ctx_kernel,
        grid=(n // seq,),
        in_specs=[pl.BlockSpec(memory_space=pltpu.SMEM),
                  pl.BlockSpec((seq, ATTN_OUT), row), kv_spec, kv_spec, kv_spec, kv_spec],
        out_specs=pl.BlockSpec((seq, ATTN_OUT), row),
        out_shape=jax.ShapeDtypeStruct((n, ATTN_OUT), BF16),
        compiler_params=_params("parallel"),
        name="attn_ctx",
    )(sink, q, ka, va, kb, vb)


def _attn_lat_kernel(sink_ref, q_ref, ka_ref, va_ref, kb_ref, vb_ref,
                     cka_ref, cva_ref, ckb_ref, cvb_ref, o_ref, *, tq, seq):
    qi = pl.program_id(1)
    span = tq + 2 * WINDOW
    ck = _pad_variants(cka_ref[0])
    cv = _pad_variants(cva_ref[0])
    lk = _pad_variants(ka_ref[...].astype(F32))
    lv = _pad_variants(va_ref[...].astype(F32))
    for t in range(4):
        kv = t // 2
        qp = q_ref[:, t * 128:(t + 1) * 128]
        acc = None
        for par in range(2):
            o = _head_attention(qp, [ck[(kv, par)], lk[(kv, par)]], [cv[(kv, par)], lv[(kv, par)]],
                                [None, None], None)
            acc = o if acc is None else acc + o
        o_ref[:, t * 128:(t + 1) * 128] = acc.astype(o_ref.dtype)
    lo = jnp.clip(qi * tq - WINDOW, 0, seq - span)
    lo = pl.multiple_of(lo, 128)
    qpos = qi * tq + lax.broadcasted_iota(I32, (tq, span), 0)
    kpos = lo + lax.broadcasted_iota(I32, (tq, span), 1)
    band = jnp.abs(qpos - kpos) <= WINDOW
    ck = _pad_variants(ckb_ref[0])
    cv = _pad_variants(cvb_ref[0])
    lk = _pad_variants(kb_ref[pl.ds(lo, span), :].astype(F32))
    lv = _pad_variants(vb_ref[pl.ds(lo, span), :].astype(F32))
    for t in range(4):
        kv = t // 2
        tile = 4 + t
        qp = q_ref[:, tile * 128:(tile + 1) * 128]
        acc = None
        for par in range(2):
            sink = sink_ref[2 * t + par]
            o = _head_attention(qp, [lk[(kv, par)], ck[(kv, par)]], [lv[(kv, par)], cv[(kv, par)]],
                                [band, None], sink)
            acc = o if acc is None else acc + o
        o_ref[:, tile * 128:(tile + 1) * 128] = acc.astype(o_ref.dtype)


def _attn_lat(sink, q, ka, va, kb, vb, cka, cva, ckb, cvb, seq, tq=256):
    n = q.shape[0]
    nb = n // seq
    nq = seq // tq
    qrow = lambda b, i: (b * nq + i, 0)
    brow = lambda b, i: (b, 0)
    kv_spec = pl.BlockSpec((seq, 128), brow)
    past = cka.shape[1]
    c_spec = pl.BlockSpec((1, past, 128), lambda b, i: (b, 0, 0))
    return pl.pallas_call(
        functools.partial(_attn_lat_kernel, tq=tq, seq=seq),
        grid=(nb, nq),
        in_specs=[pl.BlockSpec(memory_space=pltpu.SMEM),
                  pl.BlockSpec((tq, ATTN_OUT), qrow), kv_spec, kv_spec, kv_spec, kv_spec,
                  c_spec, c_spec, c_spec, c_spec],
        out_specs=pl.BlockSpec((tq, ATTN_OUT), qrow),
        out_shape=jax.ShapeDtypeStruct((n, ATTN_OUT), BF16),
        compiler_params=_params("parallel", "parallel"),
        name="attn_lat",
    )(sink, q, ka, va, kb, vb, cka, cva, ckb, cvb)


def _postmix_kernel(m_ref, x_ref, mod_ref, w_ref, gain_ref, rt_ref, x1_ref, h2_ref, aff_ref, *, project):
    if project:
        m = _dot(m_ref[...], w_ref[...])
    else:
        m = m_ref[...]
    x1 = x_ref[...] + mod_ref[0, 2:3, :] * m
    x1_ref[...] = x1
    h2 = _norm_mod(x1, gain_ref[...], mod_ref[0, 3:4, :], mod_ref[0, 4:5, :])
    h2_ref[...] = h2.astype(h2_ref.dtype)
    logits = _dot3(rt_ref[...], h2, NT)
    e = jnp.exp(logits - logits.max(axis=0, keepdims=True))
    aff_ref[...] = e / e.sum(axis=0, keepdims=True)


def _postmix(m, x, mod, w_bf, gain, router_t, *, rows_per_cond, project):
    n = x.shape[0]
    row = lambda i: (i, 0)
    const = lambda i: (0, 0)
    return pl.pallas_call(
        functools.partial(_postmix_kernel, project=project),
        grid=(n // TM,),
        in_specs=[pl.BlockSpec((TM, D_MODEL), row),
                  pl.BlockSpec((TM, D_MODEL), row),
                  _mod_spec(rows_per_cond),
                  pl.BlockSpec(w_bf.shape, const),
                  pl.BlockSpec((1, D_MODEL), const),
                  pl.BlockSpec((N_EXPERTS, D_MODEL), const)],
        out_specs=[pl.BlockSpec((TM, D_MODEL), row),
                   pl.BlockSpec((TM, D_MODEL), row),
                   pl.BlockSpec((N_EXPERTS, TM), lambda i: (0, i))],
        out_shape=[jax.ShapeDtypeStruct((n, D_MODEL), F32),
                   jax.ShapeDtypeStruct((n, D_MODEL), BF16),
                   jax.ShapeDtypeStruct((N_EXPERTS, n), F32)],
        compiler_params=_params("parallel"),
        name="postmix_proj" if project else "postmix",
    )(m, x, mod, w_bf, gain, router_t)


def _route_kernel(aff_ref, slot_ref, gate_ref, *, seq, cap, nseg):
    aff = jnp.concatenate([aff_ref[:, s * seq:(s + 1) * seq] for s in range(nseg)], axis=0)
    rows = aff.shape[0]
    capf = jnp.float32(cap)
    thr_bits = jnp.zeros((rows, 1), I32)
    for bit in range(30, -1, -1):
        cand = thr_bits | (1 << bit)
        cnt = jnp.where(aff >= pltpu.bitcast(cand, F32), 1.0, 0.0).sum(axis=1, keepdims=True)
        thr_bits = jnp.where(cnt >= capf, cand, thr_bits)
    thr = pltpu.bitcast(thr_bits, F32)
    gt = aff > thr
    eq = aff == thr
    n_gt = jnp.where(gt, 1.0, 0.0).sum(axis=1, keepdims=True)
    pw = min(seq, 256)
    tri = jnp.where(lax.broadcasted_iota(I32, (pw, pw), 0) < lax.broadcasted_iota(I32, (pw, pw), 1),
                    1.0, 0.0).astype(BF16)

    def count_before(flag):
        ones = jnp.where(flag, 1.0, 0.0)
        parts = []
        run = jnp.zeros((rows, 1), F32)
        for c0 in range(0, seq, pw):
            blk = ones[:, c0:c0 + pw]
            parts.append(_dot(blk.astype(BF16), tri) + run)
            run = run + blk.sum(axis=1, keepdims=True)
        return jnp.concatenate(parts, axis=1) if len(parts) > 1 else parts[0]

    sel = gt | (eq & (count_before(eq) < capf - n_gt))
    rank = count_before(sel)
    expert = lax.broadcasted_iota(I32, (rows, seq), 0) & (N_EXPERTS - 1)
    slot = jnp.where(sel, expert * cap + rank.astype(I32), -1)
    gate = jnp.where(sel, aff, 0.0)
    for s in range(nseg):
        slot_ref[:, s * seq:(s + 1) * seq] = slot[s * N_EXPERTS:(s + 1) * N_EXPERTS, :]
        gate_ref[:, s * seq:(s + 1) * seq] = gate[s * N_EXPERTS:(s + 1) * N_EXPERTS, :]


def _route(aff_t, seq, cap, nseg):
    n = aff_t.shape[1]
    spec = pl.BlockSpec((N_EXPERTS, nseg * seq), lambda i: (0, i))
    return pl.pallas_call(
        functools.partial(_route_kernel, seq=seq, cap=cap, nseg=nseg),
        grid=(n // (nseg * seq),),
        in_specs=[spec],
        out_specs=[spec, spec],
        out_shape=[jax.ShapeDtypeStruct((N_EXPERTS, n), I32), jax.ShapeDtypeStruct((N_EXPERTS, n), F32)],
        compiler_params=_params("parallel"),
        name="route",
    )(aff_t)


def _dispatch_kernel(slot_ref, h_ref, x_ref, *, cap, eg):
    seq = h_ref.shape[0]
    m = eg * cap
    h = h_ref[...]
    for grp in range(N_EXPERTS // eg):
        rid = lax.broadcasted_iota(I32, (m, seq), 0) + grp * m
        hit = None
        for e in range(grp * eg, (grp + 1) * eg):
            he = slot_ref[e:e + 1, :] == rid
            hit = he if hit is None else (hit | he)
        sel = jnp.where(hit, 1.0, 0.0).astype(BF16)
        x_ref[grp * m:(grp + 1) * m, :] = _dot(sel, h).astype(x_ref.dtype)


def _dispatch(slot, h, seq, cap, eg):
    n = h.shape[0]
    nb = n // seq
    return pl.pallas_call(
        functools.partial(_dispatch_kernel, cap=cap, eg=eg),
        grid=(nb,),
        in_specs=[pl.BlockSpec((N_EXPERTS, seq), lambda b: (0, b)),
                  pl.BlockSpec((seq, D_MODEL), lambda b: (b, 0))],
        out_specs=pl.BlockSpec((N_EXPERTS * cap, D_MODEL), lambda b: (b, 0)),
        out_shape=jax.ShapeDtypeStruct((nb * N_EXPERTS * cap, D_MODEL), BF16),
        compiler_params=_params("parallel"),
        name="moe_dispatch",
    )(slot, h)


FFN_TF = 512
FFN_RC = 512


def _ffn_kernel(xa_ref, xb_ref, wg_ref, wu_ref, wd_ref, ya_ref, yb_ref, acc_ref):
    j = pl.program_id(1)
    wg = wg_ref[0].astype(BF16)
    wu = wu_ref[0].astype(BF16)
    wd = wd_ref[0].astype(BF16)
    ra = xa_ref.shape[0] * xa_ref.shape[2]

    def row_chunks(ref, base):
        nb, _, cap, d = ref.shape
        rc = min(FFN_RC, nb * cap)
        for r0 in range(0, nb * cap, rc):
            if cap >= rc:
                b, c0 = divmod(r0, cap)
                yield base + r0, ref[b, 0, c0:c0 + rc, :]
            else:
                yield base + r0, ref[r0 // cap:(r0 + rc) // cap, 0, :, :].reshape(rc, d)

    for r0, x in itertools.chain(row_chunks(xa_ref, 0), row_chunks(xb_ref, ra)):
        rc = x.shape[0]
        g = _dot(x, wg)
        u = _dot(x, wu)
        mid = (g * _sigmoid(g) * u).astype(BF16)
        y = _dot(mid, wd)

        @pl.when(j == 0)
        def _():
            acc_ref[r0:r0 + rc, :] = y

        @pl.when(j > 0)
        def _():
            acc_ref[r0:r0 + rc, :] += y

    @pl.when(j == pl.num_programs(1) - 1)
    def _():
        for ref, base in ((ya_ref, 0), (yb_ref, ra)):
            nb, _, cap, d = ref.shape
            ref[...] = acc_ref[base:base + nb * cap, :].reshape(nb, 1, cap, d).astype(ref.dtype)


def _ffn(xa, xb, w_gate, w_up, w_down):
    ba, _, ca, d = xa.shape
    bb, _, cb, _ = xb.shape
    nj = D_FF // FFN_TF
    xa_spec = pl.BlockSpec((ba, 1, ca, d), lambda e, j: (0, e, 0, 0))
    xb_spec = pl.BlockSpec((bb, 1, cb, d), lambda e, j: (0, e, 0, 0))
    return pl.pallas_call(
        _ffn_kernel,
        grid=(N_EXPERTS, nj),
        in_specs=[xa_spec, xb_spec,
                  pl.BlockSpec((1, d, FFN_TF), lambda e, j: (e, 0, j)),
                  pl.BlockSpec((1, d, FFN_TF), lambda e, j: (e, 0, j)),
                  pl.BlockSpec((1, FFN_TF, d), lambda e, j: (e, j, 0))],
        out_specs=[xa_spec, xb_spec],
        out_shape=[jax.ShapeDtypeStruct(xa.shape, BF16), jax.ShapeDtypeStruct(xb.shape, BF16)],
        scratch_shapes=[pltpu.VMEM((ba * ca + bb * cb, d), F32)],
        compiler_params=_params("parallel", "arbitrary"),
        name="moe_ffn",
    )(xa, xb, w_gate, w_up, w_down)


def _combine_kernel(slot_ref, gate_ref, y_ref, x_ref, mod_ref, o_ref, *, cap, eg):
    tt = x_ref.shape[0]
    m = eg * cap
    acc = jnp.zeros((tt, D_MODEL), F32)
    for grp in range(N_EXPERTS // eg):
        lid = lax.broadcasted_iota(I32, (tt, m), 1) + grp * m
        w = jnp.zeros((tt, m), F32)
        for e in range(grp * eg, (grp + 1) * eg):
            w = w + jnp.where(slot_ref[:, e:e + 1] == lid, gate_ref[:, e:e + 1], 0.0)
        acc = acc + _dot(w.astype(BF16), y_ref[grp * m:(grp + 1) * m, :])
    o_ref[...] = x_ref[...] + mod_ref[0, 5:6, :] * acc


def _combine(slot_t, gate_t, y, x, mod, *, seq, cap, eg, tt, rows_per_cond):
    n = x.shape[0]
    nt = seq // tt
    row = lambda b, i: (b * nt + i, 0)
    if rows_per_cond is None:
        mod_spec = pl.BlockSpec((1, 8, D_MODEL), lambda b, i: (0, 0, 0))
    else:
        mod_spec = pl.BlockSpec((1, 8, D_MODEL), lambda b, i: (1 + b, 0, 0))
    return pl.pallas_call(
        functools.partial(_combine_kernel, cap=cap, eg=eg),
        grid=(n // seq, nt),
        in_specs=[pl.BlockSpec((tt, N_EXPERTS), row),
                  pl.BlockSpec((tt, N_EXPERTS), row),
                  pl.BlockSpec((N_EXPERTS * cap, D_MODEL), lambda b, i: (b, 0)),
                  pl.BlockSpec((tt, D_MODEL), row),
                  mod_spec],
        out_specs=pl.BlockSpec((tt, D_MODEL), row),
        out_shape=jax.ShapeDtypeStruct((n, D_MODEL), F32),
        compiler_params=_params("parallel", "parallel"),
        name="moe_combine",
    )(slot_t, gate_t, y, x, mod)


def _moe_pair(hp, affp, x1p, hs, affs, x1s, mod, w_gate, w_up, w_down, seq_p, seq_s):
    n_p, n_s = hp.shape[0], hs.shape[0]
    cap_p = EC_FACTOR * seq_p // N_EXPERTS
    cap_s = EC_FACTOR * seq_s // N_EXPERTS
    eg_p = max(1, 512 // cap_p)
    eg_s = max(1, 256 // cap_s)
    slot_p, gate_p = _route(affp, seq_p, cap_p, nseg=min(8, n_p // seq_p))
    slot_s, gate_s = _route(affs, seq_s, cap_s, nseg=min(4, n_s // seq_s))
    xp = _dispatch(slot_p, hp, seq_p, cap_p, eg_p).reshape(n_p // seq_p, N_EXPERTS, cap_p, D_MODEL)
    xs = _dispatch(slot_s, hs, seq_s, cap_s, eg_s).reshape(n_s // seq_s, N_EXPERTS, cap_s, D_MODEL)
    ys, yp = _ffn(xs, xp, w_gate, w_up, w_down)
    outp = _combine(slot_p.T, gate_p.T, yp.reshape(-1, D_MODEL), x1p, mod, seq=seq_p, cap=cap_p, eg=eg_p,
                    tt=seq_p, rows_per_cond=None)
    outs = _combine(slot_s.T, gate_s.T, ys.reshape(-1, D_MODEL), x1s, mod, seq=seq_s, cap=cap_s, eg=eg_s,
                    tt=min(512, seq_s), rows_per_cond=seq_s)
    return outp, outs


def _ssm_in_kernel(x_ref, mod_ref, gain_ref, wt_ref, ut_ref):
    h = _norm_mod(x_ref[0], gain_ref[...], mod_ref[0, 0:1, :], mod_ref[0, 1:2, :])
    ut_ref[0] = _dot(wt_ref[...], h.astype(BF16), NT)


def _ssm_in(xperm, mod, gain, wt_bf, *, cols_per_cond, tc):
    l, bk, d = xperm.shape
    if cols_per_cond is None:
        mod_spec = pl.BlockSpec((1, 8, d), lambda j, i: (0, 0, 0))
    else:
        mod_spec = pl.BlockSpec((1, 8, d), lambda j, i: (1 + (i * tc) // cols_per_cond, 0, 0))
    return pl.pallas_call(
        _ssm_in_kernel,
        grid=(l, bk // tc),
        in_specs=[pl.BlockSpec((1, tc, d), lambda j, i: (j, i, 0)),
                  mod_spec,
                  pl.BlockSpec((1, d), lambda j, i: (0, 0)),
                  pl.BlockSpec((d, d), lambda j, i: (0, 0))],
        out_specs=pl.BlockSpec((1, d, tc), lambda j, i: (j, 0, i)),
        out_shape=jax.ShapeDtypeStruct((l, d, bk), F32),
        compiler_params=_params("parallel", "parallel"),
        name="ssm_in",
    )(xperm, mod, gain, wt_bf)


def _ssm_core_kernel(ut_ref, al_ref, br_ref, rr_ref, qq_ref, lam_ref, h0_ref, dsk_ref, yt_ref, fs_ref,
                     *, nchunk, nbatch, has_h0):
    bk = ut_ref.shape[2]
    rows = SSM_ROWS
    p = SSM_STATE
    assert nchunk & (nchunk - 1) == 0
    ri = lax.broadcasted_iota(I32, (rows, rows), 0) >> 4
    cj = lax.broadcasted_iota(I32, (rows, rows), 1) >> 4
    lane = lax.broadcasted_iota(I32, (p, bk), 1)
    kidx = lane & (nchunk - 1)
    colsel = lax.broadcasted_iota(I32, (bk, 128), 0)
    bsel = lax.broadcasted_iota(I32, (bk, 128), 1)
    sel_last = jnp.where(colsel == bsel * nchunk + (nchunk - 1), 1.0, 0.0).astype(BF16)
    sel_first = jnp.where(colsel == bsel * nchunk, 1.0, 0.0).astype(BF16)

    def cmul(ar, ai, xr, xi):
        return ar * xr - ai * xi, ar * xi + ai * xr

    def scan(sr, si, lr, li, h0r, h0i, reverse):
        edge = (nchunk - 1) if reverse else 0
        if has_h0:
            h0cr = jnp.zeros((p, bk), F32)
            h0ci = jnp.zeros((p, bk), F32)
            for b in range(nbatch):
                at = lane == (b * nchunk + edge)
                h0cr = jnp.where(at, h0r[:, b:b + 1], h0cr)
                h0ci = jnp.where(at, h0i[:, b:b + 1], h0ci)
            ar, ai = cmul(lr, li, h0cr, h0ci)
            er, ei = sr + ar, si + ai
        else:
            er, ei = sr, si
        ar, ai = lr, li
        s = 1
        while s < nchunk:
            if reverse:
                ok = kidx < nchunk - s
                tr, ti = pltpu.roll(er, bk - s, 1), pltpu.roll(ei, bk - s, 1)
            else:
                ok = kidx >= s
                tr, ti = pltpu.roll(er, s, 1), pltpu.roll(ei, s, 1)
            tr = jnp.where(ok, tr, 0.0)
            ti = jnp.where(ok, ti, 0.0)
            dr, di = cmul(ar, ai, tr, ti)
            er, ei = er + dr, ei + di
            ar, ai = cmul(ar, ai, ar, ai)
            s *= 2
        if reverse:
            inner = kidx < nchunk - 1
            hr, hi = pltpu.roll(er, bk - 1, 1), pltpu.roll(ei, bk - 1, 1)
        else:
            inner = kidx >= 1
            hr, hi = pltpu.roll(er, 1, 1), pltpu.roll(ei, 1, 1)
        if has_h0:
            hr = jnp.where(inner, hr, h0cr)
            hi = jnp.where(inner, hi, h0ci)
        else:
            hr = jnp.where(inner, hr, 0.0)
            hi = jnp.where(inner, hi, 0.0)
        sel = sel_first if reverse else sel_last
        return hr, hi, _dot_sel(er, sel), _dot_sel(ei, sel)

    for gg in range(SSM_GB):
        x = ut_ref[:, gg * SSM_GROUP:(gg + 1) * SSM_GROUP, :].reshape(rows, bk)
        mf = jnp.where(ri >= cj, _dot3(al_ref[gg, 0], br_ref[gg, 0]), 0.0)
        mb = jnp.where(cj >= ri, _dot3(al_ref[gg, 1], br_ref[gg, 1]), 0.0)
        stack = jnp.concatenate([mf + mb, rr_ref[gg, 0], rr_ref[gg, 1]], axis=0)
        res = _mm(stack, x, SSM_PASSES)
        y = res[:rows]
        lam = lam_ref[gg]
        h0 = h0_ref[gg]
        lb = lambda v: jnp.broadcast_to(v[:, 0:1], (p, bk))
        hfr, hfi, ffr, ffi = scan(res[rows:rows + p], res[rows + p:rows + 2 * p], lb(lam[0]), lb(lam[1]),
                                  h0[0], h0[1], False)
        hbr, hbi, fbr, fbi = scan(res[rows + 2 * p:rows + 3 * p], res[rows + 3 * p:rows + 4 * p],
                                  lb(lam[2]), lb(lam[3]), h0[2], h0[3], True)
        states = jnp.concatenate([hfr, hfi, hbr, hbi], axis=0)
        qq = jnp.concatenate([qq_ref[gg, 0], qq_ref[gg, 1]], axis=1)
        y = y + _mm(qq, states, SSM_PASSES) + dsk_ref[gg][:, 0:1] * x
        yt_ref[:, gg * SSM_GROUP:(gg + 1) * SSM_GROUP, :] = y.reshape(SSM_CHUNK, SSM_GROUP, bk)
        fs_ref[gg, 0] = ffr
        fs_ref[gg, 1] = ffi
        fs_ref[gg, 2] = fbr
        fs_ref[gg, 3] = fbi


def _ssm_core(ut, mats, h0, *, nchunk, nbatch, has_h0):
    al, br, rr, qq, lam, dsk = mats
    l, d, bk = ut.shape
    g = SSM_GROUPS
    gb = SSM_GB
    lead = lambda i: (i, 0, 0, 0)
    return pl.pallas_call(
        functools.partial(_ssm_core_kernel, nchunk=nchunk, nbatch=nbatch, has_h0=has_h0),
        grid=(g // gb,),
        in_specs=[pl.BlockSpec((l, gb * SSM_GROUP, bk), lambda i: (0, i, 0)),
                  pl.BlockSpec((gb, 2, SSM_ROWS, 2 * SSM_STATE), lead),
                  pl.BlockSpec((gb, 2, 2 * SSM_STATE, SSM_ROWS), lead),
                  pl.BlockSpec((gb, 2, 2 * SSM_STATE, SSM_ROWS), lead),
                  pl.BlockSpec((gb, 2, SSM_ROWS, 2 * SSM_STATE), lead),
                  pl.BlockSpec((gb, 4, SSM_STATE, 128), lead),
                  pl.BlockSpec((gb, 4, SSM_STATE, 128), lead),
                  pl.BlockSpec((gb, SSM_ROWS, 128), lambda i: (i, 0, 0))],
        out_specs=[pl.BlockSpec((l, gb * SSM_GROUP, bk), lambda i: (0, i, 0)),
                   pl.BlockSpec((gb, 4, SSM_STATE, 128), lead)],
        out_shape=[jax.ShapeDtypeStruct((l, d, bk), F32),
                   jax.ShapeDtypeStruct((g, 4, SSM_STATE, 128), F32)],
        compiler_params=_params("parallel"),
        name="ssm_core",
    )(ut, al, br, rr, qq, lam, h0, dsk)


def _ssm_out_kernel(yt_ref, w_ref, m_ref):
    y = yt_ref[0].T
    act = 0.5 * y * (1.0 + jnp.tanh(0.7978845608028654 * (y + 0.044715 * (y * y * y))))
    ag = _dot(act.astype(BF16), w_ref[...])
    d = m_ref.shape[2]
    m_ref[0] = ag[:, :d] * _sigmoid(ag[:, d:])


def _ssm_out(yt, w_bf, tc):
    l, d, bk = yt.shape
    return pl.pallas_call(
        _ssm_out_kernel,
        grid=(l, bk // tc),
        in_specs=[pl.BlockSpec((1, d, tc), lambda j, i: (j, 0, i)),
                  pl.BlockSpec((d, 2 * d), lambda j, i: (0, 0))],
        out_specs=pl.BlockSpec((1, tc, d), lambda j, i: (j, i, 0)),
        out_shape=jax.ShapeDtypeStruct((l, bk, d), F32),
        compiler_params=_params("parallel", "parallel"),
        name="ssm_out",
    )(yt, w_bf)


def _ssm_matrices(lam_re, lam_im, b_re, b_im, c_re, c_im, log_dt, d_skip):
    l = SSM_CHUNK
    dt = jnp.exp(log_dt)[..., None]
    a = lam_re * dt
    th = lam_im * dt

    def power(n):
        nn = n.astype(F32)[None, None, :, None]
        mag = jnp.exp(nn * a[:, :, None, :])
        ang = nn * th[:, :, None, :]
        return mag * jnp.cos(ang), mag * jnp.sin(ang)

    lbr, lbi = jnp.exp(a) * jnp.cos(th), jnp.exp(a) * jnp.sin(th)
    den = lam_re * lam_re + lam_im * lam_im
    nr, ni = lbr - 1.0, lbi
    cr = (nr * lam_re + ni * lam_im) / den
    ci = (ni * lam_re - nr * lam_im) / den
    bbr = cr[..., None] * b_re - ci[..., None] * b_im
    bbi = cr[..., None] * b_im + ci[..., None] * b_re
    idx = jnp.arange(l)
    g = SSM_GROUPS

    def left(pr, pi):
        re = c_re[:, :, None] * pr[:, :, :, None, :] - c_im[:, :, None] * pi[:, :, :, None, :]
        im = c_re[:, :, None] * pi[:, :, :, None, :] + c_im[:, :, None] * pr[:, :, :, None, :]
        return jnp.concatenate([re, -im], axis=-1).reshape(2, g, l * SSM_GROUP, 2 * SSM_STATE)

    def right(pr, pi):
        prt, pit = jnp.swapaxes(pr, 2, 3), jnp.swapaxes(pi, 2, 3)
        re = prt[..., None] * bbr[:, :, :, None, :] - pit[..., None] * bbi[:, :, :, None, :]
        im = prt[..., None] * bbi[:, :, :, None, :] + pit[..., None] * bbr[:, :, :, None, :]
        return jnp.concatenate([re.reshape(2, g, SSM_STATE, l * SSM_GROUP),
                                im.reshape(2, g, SSM_STATE, l * SSM_GROUP)], axis=2)

    pick = lambda f, bwd: jnp.stack([f[0], bwd[1]], axis=1)
    al = pick(left(*power(idx)), left(*power(-idx)))
    br = pick(right(*power(-idx)), right(*power(idx)))
    rr = pick(right(*power(l - 1 - idx)), right(*power(idx)))
    qq = pick(left(*power(idx + 1)), left(*power(l - idx)))
    plr, pli = power(jnp.array([l]))
    lam = jnp.stack([plr[0, :, 0], pli[0, :, 0], plr[1, :, 0], pli[1, :, 0]], axis=1)
    lam = jnp.broadcast_to(lam[..., None], (g, 4, SSM_STATE, 128))
    dsk = jnp.tile(d_skip.reshape(g, 1, SSM_GROUP), (1, l, 1)).reshape(g, l * SSM_GROUP, 1)
    dsk = jnp.broadcast_to(dsk, (g, l * SSM_GROUP, 128))
    return al, br, rr, qq, lam, dsk


def _to_chunks(x, nb, seq):
    k = seq // SSM_CHUNK
    return x.reshape(nb, k, SSM_CHUNK, -1).transpose(2, 0, 1, 3).reshape(SSM_CHUNK, nb * k, -1)


def _from_chunks(x, nb, seq):
    k = seq // SSM_CHUNK
    return x.reshape(SSM_CHUNK, nb, k, -1).transpose(1, 2, 0, 3).reshape(nb * seq, -1)


def _ssm_mixer(x, mod, gain, wt_bf, mats, w_out_bf, h0, *, nb, seq, latent):
    k = seq // SSM_CHUNK
    tc = 128
    ut = _ssm_in(_to_chunks(x, nb, seq), mod, gain, wt_bf, cols_per_cond=k if latent else None, tc=tc)
    yt, fs = _ssm_core(ut, mats, h0, nchunk=k, nbatch=nb, has_h0=latent)
    m = _from_chunks(_ssm_out(yt, w_out_bf, tc), nb, seq)
    return m, fs


def _rope_tables(seq):
    t = jnp.arange(seq)
    row = (t // GRID_W).astype(F32)
    col = (t % GRID_W).astype(F32)
    n_freq = HEAD_DIM // 4
    inv_freq = ROPE_THETA ** (-jnp.arange(n_freq, dtype=F32) / n_freq)
    ang = jnp.concatenate([row[:, None] * inv_freq, col[:, None] * inv_freq], axis=-1)
    cos = jnp.repeat(jnp.cos(ang), 2, axis=-1)
    sin = jnp.repeat(jnp.sin(ang), 2, axis=-1)
    sign = jnp.tile(jnp.array([-1.0, 1.0], F32), HEAD_DIM // 2)
    return jnp.tile(cos, (1, 4)), jnp.tile(sin * sign, (1, 4))


def _head_gains(qn_a, kn_a, qn_b, kn_b):
    scale = HEAD_DIM ** -0.5
    ones = jnp.ones((N_KV * HEAD_DIM,), F32)
    return jnp.concatenate([jnp.tile(qn_a, N_HEADS) * scale, jnp.tile(kn_a, N_KV), ones,
                            jnp.tile(qn_b, N_HEADS) * scale, jnp.tile(kn_b, N_KV), ones]).reshape(1, QKV_COLS)


def kernel(x_prompt, x_sample, c, cache_k_a_l0, cache_v_a_l0, cache_k_b_l0, cache_v_b_l0, state_ssm_re_l1, state_ssm_im_l1, c_ctx, mod_w_l0, mod_b_l0, norm_mix_l0, attn_w_in_l0, q_norm_a_l0, k_norm_a_l0, q_norm_b_l0, k_norm_b_l0, sink_b_l0, attn_w_out_l0, norm_ffn_l0, router_l0, moe_w_gate_l0, moe_w_up_l0, moe_w_down_l0, mod_w_l1, mod_b_l1, norm_mix_l1, ssm_w_in_l1, ssm_lambda_re_l1, ssm_lambda_im_l1, ssm_b_re_l1, ssm_b_im_l1, ssm_c_re_l1, ssm_c_im_l1, ssm_log_dt_l1, ssm_d_l1, ssm_w_out_l1, norm_ffn_l1, router_l1, moe_w_gate_l1, moe_w_up_l1, moe_w_down_l1):
    bp, sp, d = x_prompt.shape
    bs, ss, _ = x_sample.shape
    past = cache_k_a_l0.shape[1]
    assert d == D_MODEL and bs <= 7 and sp % TM == 0 and ss % TM == 0
    xp = x_prompt.reshape(bp * sp, d)
    xs = x_sample.reshape(bs * ss, d)
    cond8 = jnp.concatenate([c_ctx[None], c, jnp.zeros((7 - bs, d), F32)], axis=0)
    row1 = lambda v: v.reshape(1, -1)

    mod0 = _mod_rows(cond8, mod_w_l0, mod_b_l0)
    w_in = attn_w_in_l0.astype(BF16)
    hgain = _head_gains(q_norm_a_l0, k_norm_a_l0, q_norm_b_l0, k_norm_b_l0)
    lane = np.arange(256)
    bd = jnp.asarray((lane[:, None] // HEAD_DIM == lane[None, :] // HEAD_DIM) / HEAD_DIM, BF16)
    cos_t, sin_t = _rope_tables(ss)
    qp, kap, vap, kbp, vbp = _qkv(xp, mod0, row1(norm_mix_l0), w_in, hgain, bd, cos_t, sin_t,
                                  rows_per_cond=None, seq=sp, rope=False, kv_dtype=F32)
    qs, kas, vas, kbs, vbs = _qkv(xs, mod0, row1(norm_mix_l0), w_in, hgain, bd, cos_t, sin_t,
                                  rows_per_cond=ss, seq=ss, rope=True, kv_dtype=BF16)
    op = _attn_ctx(sink_b_l0, qp, kap, vap, kbp, vbp, sp)
    cache = lambda t: t.reshape(bs, past, N_KV * HEAD_DIM)
    os_ = _attn_lat(sink_b_l0, qs, kas, vas, kbs, vbs, cache(cache_k_a_l0), cache(cache_v_a_l0),
                    cache(cache_k_b_l0), cache(cache_v_b_l0), ss)
    w_out = attn_w_out_l0.astype(BF16)
    x1p, hp, affp = _postmix(op, xp, mod0, w_out, row1(norm_ffn_l0), router_l0.T, rows_per_cond=None, project=True)
    x1s, hs, affs = _postmix(os_, xs, mod0, w_out, row1(norm_ffn_l0), router_l0.T, rows_per_cond=ss, project=True)
    xp, xs = _moe_pair(hp, affp, x1p, hs, affs, x1s, mod0, moe_w_gate_l0, moe_w_up_l0, moe_w_down_l0, sp, ss)

    mod1 = _mod_rows(cond8, mod_w_l1, mod_b_l1)
    mats = _ssm_matrices(ssm_lambda_re_l1, ssm_lambda_im_l1, ssm_b_re_l1, ssm_b_im_l1, ssm_c_re_l1, ssm_c_im_l1,
                         ssm_log_dt_l1, ssm_d_l1)
    wt = ssm_w_in_l1.T.astype(BF16)
    w_so = ssm_w_out_l1.astype(BF16)
    h0s = jnp.stack([state_ssm_re_l1[:, 0], state_ssm_im_l1[:, 0], state_ssm_re_l1[:, 1], state_ssm_im_l1[:, 1]],
                    axis=0)
    h0s = jnp.pad(h0s.transpose(2, 0, 3, 1), ((0, 0), (0, 0), (0, 0), (0, 128 - bs)))
    h0p = jnp.zeros((SSM_GROUPS, 4, SSM_STATE, 128), F32)
    mp, fsp = _ssm_mixer(xp, mod1, row1(norm_mix_l1), wt, mats, w_so, h0p, nb=bp, seq=sp, latent=False)
    ms, _ = _ssm_mixer(xs, mod1, row1(norm_mix_l1), wt, mats, w_so, h0s, nb=bs, seq=ss, latent=True)
    dummy_w = jnp.zeros((8, 128), BF16)
    x1p, hp, affp = _postmix(mp, xp, mod1, dummy_w, row1(norm_ffn_l1), router_l1.T, rows_per_cond=None, project=False)
    x1s, hs, affs = _postmix(ms, xs, mod1, dummy_w, row1(norm_ffn_l1), router_l1.T, rows_per_cond=ss, project=False)
    xp, xs = _moe_pair(hp, affp, x1p, hs, affs, x1s, mod1, moe_w_gate_l1, moe_w_up_l1, moe_w_down_l1, sp, ss)

    kv_out = lambda t: t.reshape(bp, sp, N_KV, HEAD_DIM)
    fin = fsp[:, :, :, :bp].transpose(3, 1, 0, 2)
    ssm_re = jnp.stack([fin[:, 0], fin[:, 2]], axis=1)
    ssm_im = jnp.stack([fin[:, 1], fin[:, 3]], axis=1)
    return (xp.reshape(bp, sp, d), xs.reshape(bs, ss, d), kv_out(kap), kv_out(vap), kv_out(kbp), kv_out(vbp),
            ssm_re, ssm_im)
```

```python
import functools
import itertools

import jax
import jax.numpy as jnp
import numpy as np
from jax import lax
from jax.experimental import pallas as pl
from jax.experimental.pallas import tpu as pltpu

F32, BF16, I32 = jnp.float32, jnp.bfloat16, jnp.int32

D_MODEL = 1024
GRID_W = 64
HEAD_DIM = 64
N_HEADS = 8
N_KV = 2
WINDOW = 128
ROPE_THETA = 10000.0
SSM_GROUP = 16
SSM_GROUPS = D_MODEL // SSM_GROUP
SSM_STATE = 64
N_EXPERTS = 16
EC_FACTOR = 2
D_FF = 2 * D_MODEL
EPS = 1e-6
NEG_INF = -1e30
QKV_COLS = 2 * (N_HEADS + 2 * N_KV) * HEAD_DIM
ATTN_OUT = 2 * N_HEADS * HEAD_DIM

SSM_CHUNK = 16
SSM_ROWS = SSM_CHUNK * SSM_GROUP
SSM_GB = 4
SSM_PASSES = 1

TM = 256
VMEM_LIMIT = 56 * 1024 * 1024

NN = (((1,), (0,)), ((), ()))
NT = (((1,), (1,)), ((), ()))


def _dot(a, b, dims=NN):
    return lax.dot_general(a, b, dims, preferred_element_type=F32)


def _split2(x):
    hi = x.astype(BF16)
    lo = (x - hi.astype(F32)).astype(BF16)
    return hi, lo


def _split3(x):
    hi = x.astype(BF16)
    r = x - hi.astype(F32)
    mid = r.astype(BF16)
    lo = (r - mid.astype(F32)).astype(BF16)
    return hi, mid, lo


def _dot3(a, b, dims=NN):
    ah, al = _split2(a)
    bh, bl = _split2(b)
    return _dot(ah, bh, dims) + (_dot(ah, bl, dims) + _dot(al, bh, dims))


def _mm(a, b, passes):
    if passes == 1:
        return _dot(a.astype(BF16), b.astype(BF16))
    return _dot3(a, b)


def _dot_sel(x, sel):
    hi, mid, lo = _split3(x)
    return _dot(hi, sel) + (_dot(mid, sel) + _dot(lo, sel))


def _sigmoid(x):
    return 1.0 / (1.0 + jnp.exp(-x))


def _norm_mod(x, gain, shift, scale):
    ms = jnp.mean(x * x, axis=-1, keepdims=True)
    y = x * lax.rsqrt(ms + EPS) * gain
    return y * (1.0 + scale) + shift


def _params(*sem):
    return pltpu.CompilerParams(dimension_semantics=sem, vmem_limit_bytes=VMEM_LIMIT)


def _adaln_kernel(c_ref, w_ref, b_ref, o_ref):
    c = c_ref[...]
    s = c * _sigmoid(c)
    o_ref[...] = _dot3(s, w_ref[...]) + b_ref[...]


def _adaln(cond8, w_mod, b_mod):
    d, e = w_mod.shape
    tn = 1536
    return pl.pallas_call(
        _adaln_kernel,
        grid=(e // tn,),
        in_specs=[pl.BlockSpec((8, d), lambda j: (0, 0)),
                  pl.BlockSpec((d, tn), lambda j: (0, j)),
                  pl.BlockSpec((1, tn), lambda j: (0, j))],
        out_specs=pl.BlockSpec((8, tn), lambda j: (0, j)),
        out_shape=jax.ShapeDtypeStruct((8, e), F32),
        compiler_params=_params("parallel"),
        name="adaln",
    )(cond8, w_mod, b_mod.reshape(1, e))


def _mod_rows(cond8, w_mod, b_mod):
    m = _adaln(cond8, w_mod, b_mod).reshape(8, 6, D_MODEL)
    return jnp.pad(m, ((0, 0), (0, 2), (0, 0)))


def _mod_spec(rows_per_cond):
    if rows_per_cond is None:
        return pl.BlockSpec((1, 8, D_MODEL), lambda i: (0, 0, 0))
    return pl.BlockSpec((1, 8, D_MODEL), lambda i: (1 + (i * TM) // rows_per_cond, 0, 0))


def _qkv_kernel(x_ref, mod_ref, gain_ref, w_ref, hg_ref, bd_ref, cos_ref, sin_ref,
                q_ref, ka_ref, va_ref, kb_ref, vb_ref, *, rope, transposed_kv):
    h = _norm_mod(x_ref[...], gain_ref[...], mod_ref[0, 0:1, :], mod_ref[0, 1:2, :])
    proj = _dot(h.astype(BF16), w_ref[...])
    bd = bd_ref[...]

    def head_norm(blk, g):
        hi, lo = _split2(blk * blk)
        ms = _dot(hi, bd) + _dot(lo, bd)
        return blk * lax.rsqrt(ms + EPS) * g

    def rotary(blk):
        w = blk.shape[1]
        even = (lax.broadcasted_iota(I32, blk.shape, 1) & 1) == 0
        swapped = jnp.where(even, pltpu.roll(blk, w - 1, 1), pltpu.roll(blk, 1, 1))
        return blk * cos_ref[:, :w] + swapped * sin_ref[:, :w]

    def qk(c0):
        blk = head_norm(proj[:, c0:c0 + 256], hg_ref[:, c0:c0 + 256])
        return rotary(blk) if rope else blk

    q_ref[:, 0:256] = qk(0).astype(q_ref.dtype)
    q_ref[:, 256:512] = qk(256).astype(q_ref.dtype)
    q_ref[:, 512:768] = qk(768).astype(q_ref.dtype)
    q_ref[:, 768:1024] = qk(1024).astype(q_ref.dtype)
    kva = qk(512)
    kvb = qk(1280)
    outs = ((ka_ref, kva[:, :128]), (va_ref, proj[:, 640:768]), (kb_ref, kvb[:, :128]), (vb_ref, proj[:, 1408:1536]))
    for ref, val in outs:
        if transposed_kv:
            ref[0] = val.T.astype(ref.dtype)
        else:
            ref[...] = val.astype(ref.dtype)


def _qkv(x, mod, gain, w_bf, hgain, bd, cos_t, sin_t, *, rows_per_cond, seq, rope, kv_dtype, transposed_kv):
    n = x.shape[0]
    tiles_per_seq = seq // TM
    row = lambda i: (i, 0)
    const = lambda i: (0, 0)
    pos = lambda i: (i % tiles_per_seq, 0)
    if transposed_kv:
        assert seq == TM
        kv_shape = jax.ShapeDtypeStruct((n // seq, 128, seq), kv_dtype)
        kv_spec = pl.BlockSpec((1, 128, seq), lambda i: (i, 0, 0))
    else:
        kv_shape = jax.ShapeDtypeStruct((n, 128), kv_dtype)
        kv_spec = pl.BlockSpec((TM, 128), row)
    return pl.pallas_call(
        functools.partial(_qkv_kernel, rope=rope, transposed_kv=transposed_kv),
        grid=(n // TM,),
        in_specs=[pl.BlockSpec((TM, D_MODEL), row),
                  _mod_spec(rows_per_cond),
                  pl.BlockSpec((1, D_MODEL), const),
                  pl.BlockSpec((D_MODEL, QKV_COLS), const),
                  pl.BlockSpec((1, QKV_COLS), const),
                  pl.BlockSpec((256, 256), const),
                  pl.BlockSpec((TM, 256), pos),
                  pl.BlockSpec((TM, 256), pos)],
        out_specs=[pl.BlockSpec((TM, ATTN_OUT), row)] + [kv_spec] * 4,
        out_shape=[jax.ShapeDtypeStruct((n, ATTN_OUT), BF16)] + [kv_shape] * 4,
        compiler_params=_params("parallel"),
        name="qkv_rope" if rope else "qkv",
    )(x, mod, gain, w_bf, hgain, bd, cos_t, sin_t)


def _pad_variants(kk):
    left = lax.broadcasted_iota(I32, kk.shape, 1) < HEAD_DIM
    rolled = pltpu.roll(kk, HEAD_DIM, 1)
    zero = jnp.zeros_like(kk)
    return {(0, 0): jnp.where(left, kk, zero).astype(BF16),
            (0, 1): jnp.where(left, zero, rolled).astype(BF16),
            (1, 0): jnp.where(left, rolled, zero).astype(BF16),
            (1, 1): jnp.where(left, zero, kk).astype(BF16)}


def _head_attention(qp, keys, vals, masks, sink, transposed=False):
    kdims, vdims = (NN, NT) if transposed else (NT, NN)
    scores = []
    for kblk, mask in zip(keys, masks):
        s = _dot(qp, kblk, kdims)
        if mask is not None:
            s = jnp.where(mask, s, NEG_INF)
        scores.append(s)
    m = scores[0].max(axis=-1, keepdims=True)
    for s in scores[1:]:
        m = jnp.maximum(m, s.max(axis=-1, keepdims=True))
    if sink is not None:
        m = jnp.maximum(m, sink)
    den = None
    out = None
    for s, vblk in zip(scores, vals):
        p = jnp.exp(s - m)
        ps = p.sum(axis=-1, keepdims=True)
        den = ps if den is None else den + ps
        o = _dot(p.astype(BF16), vblk, vdims)
        out = o if out is None else out + o
    if sink is not None:
        den = den + jnp.exp(sink - m)
    return out / den


def _pad_variants_t(kt):
    top = lax.broadcasted_iota(I32, kt.shape, 0) < HEAD_DIM
    zero = jnp.zeros((HEAD_DIM, kt.shape[1]), F32)
    return {(0, 0): jnp.where(top, kt, 0.0).astype(BF16),
            (0, 1): jnp.concatenate([zero, kt[:HEAD_DIM]], axis=0).astype(BF16),
            (1, 0): jnp.concatenate([kt[HEAD_DIM:], zero], axis=0).astype(BF16),
            (1, 1): jnp.where(top, 0.0, kt).astype(BF16)}


def _attn_ctx_kernel(sink_ref, q_ref, ka_ref, va_ref, kb_ref, vb_ref, o_ref):
    for mixer, (k_ref, v_ref) in enumerate(((ka_ref, va_ref), (kb_ref, vb_ref))):
        kvar = _pad_variants_t(k_ref[0])
        vvar = _pad_variants_t(v_ref[0])
        for t in range(4):
            tile = mixer * 4 + t
            kv = t // 2
            qp = q_ref[:, tile * 128:(tile + 1) * 128]
            acc = None
            for par in range(2):
                sink = sink_ref[2 * t + par] if mixer == 1 else None
                o = _head_attention(qp, [kvar[(kv, par)]], [vvar[(kv, par)]], [None], sink, transposed=True)
                acc = o if acc is None else acc + o
            o_ref[:, tile * 128:(tile + 1) * 128] = acc.astype(o_ref.dtype)


def _attn_ctx(sink, q, ka, va, kb, vb, seq):
    n = q.shape[0]
    row = lambda b: (b, 0)
    kv_spec = pl.BlockSpec((1, 128, seq), lambda b: (b, 0, 0))
    return pl.pallas_call(
        _attn_ctx_kernel,
        grid=(n // seq,),
        in_specs=[pl.BlockSpec(memory_space=pltpu.SMEM),
                  pl.BlockSpec((seq, ATTN_OUT), row), kv_spec, kv_spec, kv_spec, kv_spec],
        out_specs=pl.BlockSpec((seq, ATTN_OUT), row),
        out_shape=jax.ShapeDtypeStruct((n, ATTN_OUT), BF16),
        compiler_params=_params("parallel"),
        name="attn_ctx",
    )(sink, q, ka, va, kb, vb)


def _attn_lat_kernel(sink_ref, q_ref, ka_ref, va_ref, kb_ref, vb_ref,
                     cka_ref, cva_ref, ckb_ref, cvb_ref, o_ref, *, tq, seq):
    qi = pl.program_id(1)
    span = tq + 2 * WINDOW
    ck = _pad_variants(cka_ref[0])
    cv = _pad_variants(cva_ref[0])
    lk = _pad_variants(ka_ref[...].astype(F32))
    lv = _pad_variants(va_ref[...].astype(F32))
    for t in range(4):
        kv = t // 2
        qp = q_ref[:, t * 128:(t + 1) * 128]
        acc = None
        for par in range(2):
            o = _head_attention(qp, [ck[(kv, par)], lk[(kv, par)]], [cv[(kv, par)], lv[(kv, par)]],
                                [None, None], None)
            acc = o if acc is None else acc + o
        o_ref[:, t * 128:(t + 1) * 128] = acc.astype(o_ref.dtype)
    lo = jnp.clip(qi * tq - WINDOW, 0, seq - span)
    lo = pl.multiple_of(lo, 128)
    qpos = qi * tq + lax.broadcasted_iota(I32, (tq, span), 0)
    kpos = lo + lax.broadcasted_iota(I32, (tq, span), 1)
    band = jnp.abs(qpos - kpos) <= WINDOW
    ck = _pad_variants(ckb_ref[0])
    cv = _pad_variants(cvb_ref[0])
    lk = _pad_variants(kb_ref[pl.ds(lo, span), :].astype(F32))
    lv = _pad_variants(vb_ref[pl.ds(lo, span), :].astype(F32))
    for t in range(4):
        kv = t // 2
        tile = 4 + t
        qp = q_ref[:, tile * 128:(tile + 1) * 128]
        acc = None
        for par in range(2):
            sink = sink_ref[2 * t + par]
            o = _head_attention(qp, [lk[(kv, par)], ck[(kv, par)]], [lv[(kv, par)], cv[(kv, par)]],
                                [band, None], sink)
            acc = o if acc is None else acc + o
        o_ref[:, tile * 128:(tile + 1) * 128] = acc.astype(o_ref.dtype)


def _attn_lat(sink, q, ka, va, kb, vb, cka, cva, ckb, cvb, seq, tq=256):
    n = q.shape[0]
    nb = n // seq
    nq = seq // tq
    qrow = lambda b, i: (b * nq + i, 0)
    brow = lambda b, i: (b, 0)
    kv_spec = pl.BlockSpec((seq, 128), brow)
    past = cka.shape[1]
    c_spec = pl.BlockSpec((1, past, 128), lambda b, i: (b, 0, 0))
    return pl.pallas_call(
        functools.partial(_attn_lat_kernel, tq=tq, seq=seq),
        grid=(nb, nq),
        in_specs=[pl.BlockSpec(memory_space=pltpu.SMEM),
                  pl.BlockSpec((tq, ATTN_OUT), qrow), kv_spec, kv_spec, kv_spec, kv_spec,
                  c_spec, c_spec, c_spec, c_spec],
        out_specs=pl.BlockSpec((tq, ATTN_OUT), qrow),
        out_shape=jax.ShapeDtypeStruct((n, ATTN_OUT), BF16),
        compiler_params=_params("parallel", "parallel"),
        name="attn_lat",
    )(sink, q, ka, va, kb, vb, cka, cva, ckb, cvb)


def _postmix_kernel(m_ref, x_ref, mod_ref, w_ref, gain_ref, rt_ref, x1_ref, h2_ref, aff_ref, *, project):
    if project:
        m = _dot(m_ref[...], w_ref[...])
    else:
        m = m_ref[...]
    x1 = x_ref[...] + mod_ref[0, 2:3, :] * m
    x1_ref[...] = x1
    h2 = _norm_mod(x1, gain_ref[...], mod_ref[0, 3:4, :], mod_ref[0, 4:5, :])
    h2_ref[...] = h2.astype(h2_ref.dtype)
    logits = _dot3(rt_ref[...], h2, NT)
    e = jnp.exp(logits - logits.max(axis=0, keepdims=True))
    aff_ref[...] = e / e.sum(axis=0, keepdims=True)


def _postmix(m, x, mod, w_bf, gain, router_t, *, rows_per_cond, project):
    n = x.shape[0]
    row = lambda i: (i, 0)
    const = lambda i: (0, 0)
    return pl.pallas_call(
        functools.partial(_postmix_kernel, project=project),
        grid=(n // TM,),
        in_specs=[pl.BlockSpec((TM, D_MODEL), row),
                  pl.BlockSpec((TM, D_MODEL), row),
                  _mod_spec(rows_per_cond),
                  pl.BlockSpec(w_bf.shape, const),
                  pl.BlockSpec((1, D_MODEL), const),
                  pl.BlockSpec((N_EXPERTS, D_MODEL), const)],
        out_specs=[pl.BlockSpec((TM, D_MODEL), row),
                   pl.BlockSpec((TM, D_MODEL), row),
                   pl.BlockSpec((N_EXPERTS, TM), lambda i: (0, i))],
        out_shape=[jax.ShapeDtypeStruct((n, D_MODEL), F32),
                   jax.ShapeDtypeStruct((n, D_MODEL), BF16),
                   jax.ShapeDtypeStruct((N_EXPERTS, n), F32)],
        compiler_params=_params("parallel"),
        name="postmix_proj" if project else "postmix",
    )(m, x, mod, w_bf, gain, router_t)


def _route_kernel(aff_ref, slot_ref, gate_ref, *, seq, cap, nseg):
    aff = jnp.concatenate([aff_ref[:, s * seq:(s + 1) * seq] for s in range(nseg)], axis=0)
    rows = aff.shape[0]
    capf = jnp.float32(cap)
    thr_bits = jnp.zeros((rows, 1), I32)
    for bit in range(30, -1, -1):
        cand = thr_bits | (1 << bit)
        cnt = jnp.where(aff >= pltpu.bitcast(cand, F32), 1.0, 0.0).sum(axis=1, keepdims=True)
        thr_bits = jnp.where(cnt >= capf, cand, thr_bits)
    thr = pltpu.bitcast(thr_bits, F32)
    gt = aff > thr
    eq = aff == thr
    n_gt = jnp.where(gt, 1.0, 0.0).sum(axis=1, keepdims=True)
    pw = min(seq, 256)
    tri = jnp.where(lax.broadcasted_iota(I32, (pw, pw), 0) < lax.broadcasted_iota(I32, (pw, pw), 1),
                    1.0, 0.0).astype(BF16)

    def count_before(flag):
        ones = jnp.where(flag, 1.0, 0.0)
        parts = []
        run = jnp.zeros((rows, 1), F32)
        for c0 in range(0, seq, pw):
            blk = ones[:, c0:c0 + pw]
            parts.append(_dot(blk.astype(BF16), tri) + run)
            run = run + blk.sum(axis=1, keepdims=True)
        return jnp.concatenate(parts, axis=1) if len(parts) > 1 else parts[0]

    sel = gt | (eq & (count_before(eq) < capf - n_gt))
    rank = count_before(sel)
    expert = lax.broadcasted_iota(I32, (rows, seq), 0) & (N_EXPERTS - 1)
    slot = jnp.where(sel, expert * cap + rank.astype(I32), -1)
    gate = jnp.where(sel, aff, 0.0)
    for s in range(nseg):
        slot_ref[:, s * seq:(s + 1) * seq] = slot[s * N_EXPERTS:(s + 1) * N_EXPERTS, :]
        gate_ref[:, s * seq:(s + 1) * seq] = gate[s * N_EXPERTS:(s + 1) * N_EXPERTS, :]


def _route(aff_t, seq, cap, nseg):
    n = aff_t.shape[1]
    spec = pl.BlockSpec((N_EXPERTS, nseg * seq), lambda i: (0, i))
    return pl.pallas_call(
        functools.partial(_route_kernel, seq=seq, cap=cap, nseg=nseg),
        grid=(n // (nseg * seq),),
        in_specs=[spec],
        out_specs=[spec, spec],
        out_shape=[jax.ShapeDtypeStruct((N_EXPERTS, n), I32), jax.ShapeDtypeStruct((N_EXPERTS, n), F32)],
        compiler_params=_params("parallel"),
        name="route",
    )(aff_t)


def _dispatch_kernel(slot_ref, h_ref, x_ref, *, cap, eg):
    seq = h_ref.shape[0]
    m = eg * cap
    h = h_ref[...]
    for grp in range(N_EXPERTS // eg):
        rid = lax.broadcasted_iota(I32, (m, seq), 0) + grp * m
        hit = None
        for e in range(grp * eg, (grp + 1) * eg):
            he = slot_ref[e:e + 1, :] == rid
            hit = he if hit is None else (hit | he)
        sel = jnp.where(hit, 1.0, 0.0).astype(BF16)
        x_ref[grp * m:(grp + 1) * m, :] = _dot(sel, h).astype(x_ref.dtype)


def _dispatch(slot, h, seq, cap, eg):
    n = h.shape[0]
    nb = n // seq
    return pl.pallas_call(
        functools.partial(_dispatch_kernel, cap=cap, eg=eg),
        grid=(nb,),
        in_specs=[pl.BlockSpec((N_EXPERTS, seq), lambda b: (0, b)),
                  pl.BlockSpec((seq, D_MODEL), lambda b: (b, 0))],
        out_specs=pl.BlockSpec((N_EXPERTS * cap, D_MODEL), lambda b: (b, 0)),
        out_shape=jax.ShapeDtypeStruct((nb * N_EXPERTS * cap, D_MODEL), BF16),
        compiler_params=_params("parallel"),
        name="moe_dispatch",
    )(slot, h)


FFN_TF = 512
FFN_RC = 512


def _ffn_kernel(xa_ref, xb_ref, wg_ref, wu_ref, wd_ref, ya_ref, yb_ref, acc_ref):
    j = pl.program_id(1)
    wg = wg_ref[0].astype(BF16)
    wu = wu_ref[0].astype(BF16)
    wd = wd_ref[0].astype(BF16)
    ra = xa_ref.shape[0] * xa_ref.shape[2]

    def row_chunks(ref, base):
        nb, _, cap, d = ref.shape
        rc = min(FFN_RC, nb * cap)
        for r0 in range(0, nb * cap, rc):
            if cap >= rc:
                b, c0 = divmod(r0, cap)
                yield base + r0, ref[b, 0, c0:c0 + rc, :]
            else:
                yield base + r0, ref[r0 // cap:(r0 + rc) // cap, 0, :, :].reshape(rc, d)

    for r0, x in itertools.chain(row_chunks(xa_ref, 0), row_chunks(xb_ref, ra)):
        rc = x.shape[0]
        g = _dot(x, wg)
        u = _dot(x, wu)
        mid = (g * _sigmoid(g) * u).astype(BF16)
        y = _dot(mid, wd)

        @pl.when(j == 0)
        def _():
            acc_ref[r0:r0 + rc, :] = y

        @pl.when(j > 0)
        def _():
            acc_ref[r0:r0 + rc, :] += y

    @pl.when(j == pl.num_programs(1) - 1)
    def _():
        for ref, base in ((ya_ref, 0), (yb_ref, ra)):
            nb, _, cap, d = ref.shape
            ref[...] = acc_ref[base:base + nb * cap, :].reshape(nb, 1, cap, d).astype(ref.dtype)


def _ffn(xa, xb, w_gate, w_up, w_down):
    ba, _, ca, d = xa.shape
    bb, _, cb, _ = xb.shape
    nj = D_FF // FFN_TF
    xa_spec = pl.BlockSpec((ba, 1, ca, d), lambda e, j: (0, e, 0, 0))
    xb_spec = pl.BlockSpec((bb, 1, cb, d), lambda e, j: (0, e, 0, 0))
    return pl.pallas_call(
        _ffn_kernel,
        grid=(N_EXPERTS, nj),
        in_specs=[xa_spec, xb_spec,
                  pl.BlockSpec((1, d, FFN_TF), lambda e, j: (e, 0, j)),
                  pl.BlockSpec((1, d, FFN_TF), lambda e, j: (e, 0, j)),
                  pl.BlockSpec((1, FFN_TF, d), lambda e, j: (e, j, 0))],
        out_specs=[xa_spec, xb_spec],
        out_shape=[jax.ShapeDtypeStruct(xa.shape, BF16), jax.ShapeDtypeStruct(xb.shape, BF16)],
        scratch_shapes=[pltpu.VMEM((ba * ca + bb * cb, d), F32)],
        compiler_params=_params("parallel", "arbitrary"),
        name="moe_ffn",
    )(xa, xb, w_gate, w_up, w_down)


def _combine_kernel(slot_ref, gate_ref, y_ref, x_ref, mod_ref, o_ref, *, cap, eg):
    tt = x_ref.shape[0]
    m = eg * cap
    acc = jnp.zeros((tt, D_MODEL), F32)
    for grp in range(N_EXPERTS // eg):
        lid = lax.broadcasted_iota(I32, (tt, m), 1) + grp * m
        w = jnp.zeros((tt, m), F32)
        for e in range(grp * eg, (grp + 1) * eg):
            w = w + jnp.where(slot_ref[:, e:e + 1] == lid, gate_ref[:, e:e + 1], 0.0)
        acc = acc + _dot(w.astype(BF16), y_ref[grp * m:(grp + 1) * m, :])
    o_ref[...] = x_ref[...] + mod_ref[0, 5:6, :] * acc


def _combine(slot_t, gate_t, y, x, mod, *, seq, cap, eg, tt, rows_per_cond):
    n = x.shape[0]
    nt = seq // tt
    row = lambda b, i: (b * nt + i, 0)
    if rows_per_cond is None:
        mod_spec = pl.BlockSpec((1, 8, D_MODEL), lambda b, i: (0, 0, 0))
    else:
        mod_spec = pl.BlockSpec((1, 8, D_MODEL), lambda b, i: (1 + b, 0, 0))
    return pl.pallas_call(
        functools.partial(_combine_kernel, cap=cap, eg=eg),
        grid=(n // seq, nt),
        in_specs=[pl.BlockSpec((tt, N_EXPERTS), row),
                  pl.BlockSpec((tt, N_EXPERTS), row),
                  pl.BlockSpec((N_EXPERTS * cap, D_MODEL), lambda b, i: (b, 0)),
                  pl.BlockSpec((tt, D_MODEL), row),
                  mod_spec],
        out_specs=pl.BlockSpec((tt, D_MODEL), row),
        out_shape=jax.ShapeDtypeStruct((n, D_MODEL), F32),
        compiler_params=_params("parallel", "parallel"),
        name="moe_combine",
    )(slot_t, gate_t, y, x, mod)


def _moe_pair(hp, affp, x1p, hs, affs, x1s, mod, w_gate, w_up, w_down, seq_p, seq_s):
    n_p, n_s = hp.shape[0], hs.shape[0]
    cap_p = EC_FACTOR * seq_p // N_EXPERTS
    cap_s = EC_FACTOR * seq_s // N_EXPERTS
    eg_p = max(1, 512 // cap_p)
    eg_s = max(1, 256 // cap_s)
    slot_p, gate_p = _route(affp, seq_p, cap_p, nseg=min(8, n_p // seq_p))
    slot_s, gate_s = _route(affs, seq_s, cap_s, nseg=min(4, n_s // seq_s))
    xp = _dispatch(slot_p, hp, seq_p, cap_p, eg_p).reshape(n_p // seq_p, N_EXPERTS, cap_p, D_MODEL)
    xs = _dispatch(slot_s, hs, seq_s, cap_s, eg_s).reshape(n_s // seq_s, N_EXPERTS, cap_s, D_MODEL)
    ys, yp = _ffn(xs, xp, w_gate, w_up, w_down)
    outp = _combine(slot_p.T, gate_p.T, yp.reshape(-1, D_MODEL), x1p, mod, seq=seq_p, cap=cap_p, eg=eg_p,
                    tt=seq_p, rows_per_cond=None)
    outs = _combine(slot_s.T, gate_s.T, ys.reshape(-1, D_MODEL), x1s, mod, seq=seq_s, cap=cap_s, eg=eg_s,
                    tt=min(512, seq_s), rows_per_cond=seq_s)
    return outp, outs


def _ssm_in_kernel(x_ref, mod_ref, gain_ref, wt_ref, ut_ref):
    h = _norm_mod(x_ref[0], gain_ref[...], mod_ref[0, 0:1, :], mod_ref[0, 1:2, :])
    ut_ref[0] = _dot(wt_ref[...], h.astype(BF16), NT)


def _ssm_in(xperm, mod, gain, wt_bf, *, cols_per_cond, tc):
    l, bk, d = xperm.shape
    if cols_per_cond is None:
        mod_spec = pl.BlockSpec((1, 8, d), lambda j, i: (0, 0, 0))
    else:
        mod_spec = pl.BlockSpec((1, 8, d), lambda j, i: (1 + (i * tc) // cols_per_cond, 0, 0))
    return pl.pallas_call(
        _ssm_in_kernel,
        grid=(l, bk // tc),
        in_specs=[pl.BlockSpec((1, tc, d), lambda j, i: (j, i, 0)),
                  mod_spec,
                  pl.BlockSpec((1, d), lambda j, i: (0, 0)),
                  pl.BlockSpec((d, d), lambda j, i: (0, 0))],
        out_specs=pl.BlockSpec((1, d, tc), lambda j, i: (j, 0, i)),
        out_shape=jax.ShapeDtypeStruct((l, d, bk), F32),
        compiler_params=_params("parallel", "parallel"),
        name="ssm_in",
    )(xperm, mod, gain, wt_bf)


def _ssm_core_kernel(utp_ref, uts_ref, lamp_ref, c_ref, bt_ref, dsk_ref, h0_ref, ytp_ref, yts_ref, fs_ref,
                     *, kp, ks, nbp, nbs):
    rows = SSM_ROWS
    p = SSM_STATE
    lc = SSM_CHUNK
    ri = lax.broadcasted_iota(I32, (rows, rows), 0)
    cj = lax.broadcasted_iota(I32, (rows, rows), 1)
    causal = (ri >> 4) >= (cj >> 4)
    anticausal = (cj >> 4) >= (ri >> 4)
    diag = ri == cj
    leftc = lax.broadcasted_iota(I32, (lc, 128), 1) < p
    nrow = lax.broadcasted_iota(I32, (lc, 128), 0).astype(F32)
    eye = lax.broadcasted_iota(I32, (p, 128), 0) == lax.broadcasted_iota(I32, (p, 128), 1)

    def cmul(ar, ai, xr, xi):
        return ar * xr - ai * xi, ar * xi + ai * xr

    def expand_rows(t):
        return jnp.broadcast_to(t[:, None, :], (lc, SSM_GROUP, 128)).reshape(rows, 128)

    def tile_rows(t):
        return jnp.broadcast_to(t[None, :, :], (lc, SSM_GROUP, 128)).reshape(rows, 128)

    def to_col(row):
        return jnp.where(eye, jnp.broadcast_to(row, (p, 128)), 0.0).sum(axis=1, keepdims=True)

    def operands(gg, d):
        lp = lamp_ref[gg, d]
        lre, lim = lp[0:1], lp[1:2]
        dt = jnp.exp(lp[2:3])
        a, th = lre * dt, lim * dt
        ang = nrow * th
        cs, sn = jnp.cos(ang), jnp.sin(ang)
        ep, em = jnp.exp(nrow * a), jnp.exp(-(nrow * a))
        pr, pi = ep * cs, ep * sn
        nr, ni = em * cs, -(em * sn)
        l1r, l1i = pr[1:2], pi[1:2]
        lmr, lmi = pr[lc - 1:lc], pi[lc - 1:lc]
        llr, lli = cmul(lmr, lmi, l1r, l1i)
        den = lre * lre + lim * lim
        cr = ((l1r - 1.0) * lre + l1i * lim) / den
        ci = (l1i * lre - (l1r - 1.0) * lim) / den
        btr, bti = bt_ref[gg, d, 0], bt_ref[gg, d, 1]
        bbr, bbi = cr * btr - ci * bti, cr * bti + ci * btr
        u1 = tile_rows(jnp.where(leftc, bbr, bbi))
        u2 = tile_rows(jnp.where(leftc, bbi, bbr))
        c1 = tile_rows(c_ref[gg, d, 0])
        c2 = tile_rows(c_ref[gg, d, 1])

        def left_form(xr, xi):
            return (c1 * expand_rows(jnp.where(leftc, xr, -xi))
                    + c2 * expand_rows(jnp.where(leftc, -xi, -xr)))

        def right_form(xr, xi):
            return u1 * expand_rows(xr) + u2 * expand_rows(jnp.where(leftc, -xi, xi))

        if d == 0:
            al = left_form(pr, pi)
            brt = right_form(nr, ni)
            rrt = right_form(*cmul(lmr, lmi, nr, ni))
            qq = left_form(*cmul(l1r, l1i, pr, pi))
            mat = jnp.where(causal, _dot3(al, brt, NT), 0.0)
        else:
            al = left_form(nr, ni)
            brt = right_form(pr, pi)
            rrt = brt
            qq = left_form(*cmul(llr, lli, nr, ni))
            mat = jnp.where(anticausal, _dot3(al, brt, NT), 0.0)
        return mat, rrt.T, qq, to_col(llr), to_col(lli)

    def scan(sr, si, lr, li, h0r, h0i, reverse, nchunk, nbatch, sel):
        bk = sr.shape[1]
        lane = lax.broadcasted_iota(I32, (p, bk), 1)
        kidx = lane & (nchunk - 1)
        edge = (nchunk - 1) if reverse else 0
        if h0r is not None:
            h0cr = jnp.zeros((p, bk), F32)
            h0ci = jnp.zeros((p, bk), F32)
            for b in range(nbatch):
                at = lane == (b * nchunk + edge)
                h0cr = jnp.where(at, h0r[:, b:b + 1], h0cr)
                h0ci = jnp.where(at, h0i[:, b:b + 1], h0ci)
            ar, ai = cmul(lr, li, h0cr, h0ci)
            er, ei = sr + ar, si + ai
        else:
            er, ei = sr, si
        ar, ai = lr, li
        s = 1
        while s < nchunk:
            if reverse:
                ok = kidx < nchunk - s
                tr, ti = pltpu.roll(er, bk - s, 1), pltpu.roll(ei, bk - s, 1)
            else:
                ok = kidx >= s
                tr, ti = pltpu.roll(er, s, 1), pltpu.roll(ei, s, 1)
            tr = jnp.where(ok, tr, 0.0)
            ti = jnp.where(ok, ti, 0.0)
            dr, di = cmul(ar, ai, tr, ti)
            er, ei = er + dr, ei + di
            ar, ai = cmul(ar, ai, ar, ai)
            s *= 2
        if reverse:
            inner = kidx < nchunk - 1
            hr, hi = pltpu.roll(er, bk - 1, 1), pltpu.roll(ei, bk - 1, 1)
        else:
            inner = kidx >= 1
            hr, hi = pltpu.roll(er, 1, 1), pltpu.roll(ei, 1, 1)
        hr = jnp.where(inner, hr, h0cr if h0r is not None else 0.0)
        hi = jnp.where(inner, hi, h0ci if h0r is not None else 0.0)
        fin = None if sel is None else (_dot_sel(er, sel), _dot_sel(ei, sel))
        return hr, hi, fin

    def final_selectors(bk, nchunk, nbatch):
        col = lax.broadcasted_iota(I32, (bk, nbatch), 0)
        bat = lax.broadcasted_iota(I32, (bk, nbatch), 1)
        last = jnp.where(col == bat * nchunk + (nchunk - 1), 1.0, 0.0).astype(BF16)
        first = jnp.where(col == bat * nchunk, 1.0, 0.0).astype(BF16)
        return last, first

    sel_last, sel_first = final_selectors(utp_ref.shape[2], kp, nbp)

    for gg in range(SSM_GB):
        mf, rf, qf, lfr, lfi = operands(gg, 0)
        mb, rb, qb, lbr, lbi = operands(gg, 1)
        skip = jnp.where(diag, jnp.broadcast_to(dsk_ref[gg], (rows, rows)), 0.0)
        stack = jnp.concatenate([mf + mb + skip, rf, rb], axis=0)
        qq = jnp.concatenate([qf, qb], axis=1)
        h0 = h0_ref[gg]
        for ut_ref, yt_ref, nchunk, nbatch, latent in ((utp_ref, ytp_ref, kp, nbp, False),
                                                       (uts_ref, yts_ref, ks, nbs, True)):
            bk = ut_ref.shape[2]
            x = ut_ref[:, gg * SSM_GROUP:(gg + 1) * SSM_GROUP, :].reshape(rows, bk)
            res = _mm(stack, x, SSM_PASSES)
            hfr, hfi, ff = scan(res[rows:rows + p], res[rows + p:rows + 2 * p], lfr, lfi,
                                h0[0] if latent else None, h0[1] if latent else None,
                                False, nchunk, nbatch, None if latent else sel_last)
            hbr, hbi, fb = scan(res[rows + 2 * p:rows + 3 * p], res[rows + 3 * p:rows + 4 * p], lbr, lbi,
                                h0[2] if latent else None, h0[3] if latent else None,
                                True, nchunk, nbatch, None if latent else sel_first)
            states = jnp.concatenate([hfr, hfi, hbr, hbi], axis=0)
            y = res[:rows] + _mm(qq, states, SSM_PASSES)
            yt_ref[:, gg * SSM_GROUP:(gg + 1) * SSM_GROUP, :] = y.reshape(lc, SSM_GROUP, bk)
            if not latent:
                fs_ref[gg, 0] = ff[0]
                fs_ref[gg, 1] = ff[1]
                fs_ref[gg, 2] = fb[0]
                fs_ref[gg, 3] = fb[1]


def _ssm_core(utp, uts, ops, h0, *, kp, ks, nbp, nbs):
    lamp, c2, bt2, dsk = ops
    l, d, bkp = utp.shape
    bks = uts.shape[2]
    g = SSM_GROUPS
    gb = SSM_GB
    lead4 = lambda i: (i, 0, 0, 0)
    lead5 = lambda i: (i, 0, 0, 0, 0)
    ut_spec = lambda bk: pl.BlockSpec((l, gb * SSM_GROUP, bk), lambda i: (0, i, 0))
    return pl.pallas_call(
        functools.partial(_ssm_core_kernel, kp=kp, ks=ks, nbp=nbp, nbs=nbs),
        grid=(g // gb,),
        in_specs=[ut_spec(bkp), ut_spec(bks),
                  pl.BlockSpec((gb, 2, 8, 128), lead4),
                  pl.BlockSpec((gb, 2, 2, SSM_GROUP, 128), lead5),
                  pl.BlockSpec((gb, 2, 2, SSM_GROUP, 128), lead5),
                  pl.BlockSpec((gb, 1, SSM_ROWS), lambda i: (i, 0, 0)),
                  pl.BlockSpec((gb, 4, SSM_STATE, nbs), lead4)],
        out_specs=[ut_spec(bkp), ut_spec(bks),
                   pl.BlockSpec((gb, 4, SSM_STATE, nbp), lead4)],
        out_shape=[jax.ShapeDtypeStruct((l, d, bkp), F32),
                   jax.ShapeDtypeStruct((l, d, bks), F32),
                   jax.ShapeDtypeStruct((g, 4, SSM_STATE, nbp), F32)],
        compiler_params=_params("parallel"),
        name="ssm_core",
    )(utp, uts, lamp, c2, bt2, dsk, h0)


def _ssm_out_kernel(yt_ref, w_ref, m_ref):
    y = yt_ref[0].T
    act = 0.5 * y * (1.0 + jnp.tanh(0.7978845608028654 * (y + 0.044715 * (y * y * y))))
    ag = _dot(act.astype(BF16), w_ref[...])
    d = m_ref.shape[2]
    m_ref[0] = ag[:, :d] * _sigmoid(ag[:, d:])


def _ssm_out(yt, w_bf, tc):
    l, d, bk = yt.shape
    return pl.pallas_call(
        _ssm_out_kernel,
        grid=(l, bk // tc),
        in_specs=[pl.BlockSpec((1, d, tc), lambda j, i: (j, 0, i)),
                  pl.BlockSpec((d, 2 * d), lambda j, i: (0, 0))],
        out_specs=pl.BlockSpec((1, tc, d), lambda j, i: (j, i, 0)),
        out_shape=jax.ShapeDtypeStruct((l, bk, d), F32),
        compiler_params=_params("parallel", "parallel"),
        name="ssm_out",
    )(yt, w_bf)


def _ssm_operand_params(lam_re, lam_im, b_re, b_im, c_re, c_im, log_dt, d_skip):
    g, l = SSM_GROUPS, SSM_CHUNK
    dup = lambda t: jnp.concatenate([t, t], axis=-1)
    lamp = jnp.stack([lam_re, lam_im, jnp.broadcast_to(log_dt[..., None], lam_re.shape)], axis=2)
    lamp = dup(jnp.pad(lamp, ((0, 0), (0, 0), (0, 5), (0, 0)))).transpose(1, 0, 2, 3)
    c2 = dup(jnp.stack([c_re, c_im], axis=2)).transpose(1, 0, 2, 3, 4)
    bt2 = dup(jnp.stack([jnp.swapaxes(b_re, -1, -2), jnp.swapaxes(b_im, -1, -2)], axis=2)).transpose(1, 0, 2, 3, 4)
    dsk = jnp.tile(d_skip.reshape(g, 1, SSM_GROUP), (1, 1, l))
    return lamp, c2, bt2, dsk


def _to_chunks(x, nb, seq):
    k = seq // SSM_CHUNK
    return x.reshape(nb, k, SSM_CHUNK, -1).transpose(2, 0, 1, 3).reshape(SSM_CHUNK, nb * k, -1)


def _from_chunks(x, nb, seq):
    k = seq // SSM_CHUNK
    return x.reshape(SSM_CHUNK, nb, k, -1).transpose(1, 2, 0, 3).reshape(nb * seq, -1)


def _ssm_mixers(xp, xs, mod, gain, wt_bf, ops, w_out_bf, h0, *, nbp, sp, nbs, ss):
    kp, ks = sp // SSM_CHUNK, ss // SSM_CHUNK
    tc = 128
    utp = _ssm_in(_to_chunks(xp, nbp, sp), mod, gain, wt_bf, cols_per_cond=None, tc=tc)
    uts = _ssm_in(_to_chunks(xs, nbs, ss), mod, gain, wt_bf, cols_per_cond=ks, tc=tc)
    ytp, yts, fs = _ssm_core(utp, uts, ops, h0, kp=kp, ks=ks, nbp=nbp, nbs=nbs)
    mp = _from_chunks(_ssm_out(ytp, w_out_bf, tc), nbp, sp)
    ms = _from_chunks(_ssm_out(yts, w_out_bf, tc), nbs, ss)
    return mp, ms, fs


def _rope_tables(seq):
    t = jnp.arange(seq)
    row = (t // GRID_W).astype(F32)
    col = (t % GRID_W).astype(F32)
    n_freq = HEAD_DIM // 4
    inv_freq = ROPE_THETA ** (-jnp.arange(n_freq, dtype=F32) / n_freq)
    ang = jnp.concatenate([row[:, None] * inv_freq, col[:, None] * inv_freq], axis=-1)
    cos = jnp.repeat(jnp.cos(ang), 2, axis=-1)
    sin = jnp.repeat(jnp.sin(ang), 2, axis=-1)
    sign = jnp.tile(jnp.array([-1.0, 1.0], F32), HEAD_DIM // 2)
    return jnp.tile(cos, (1, 4)), jnp.tile(sin * sign, (1, 4))


def _head_gains(qn_a, kn_a, qn_b, kn_b):
    scale = HEAD_DIM ** -0.5
    ones = jnp.ones((N_KV * HEAD_DIM,), F32)
    return jnp.concatenate([jnp.tile(qn_a, N_HEADS) * scale, jnp.tile(kn_a, N_KV), ones,
                            jnp.tile(qn_b, N_HEADS) * scale, jnp.tile(kn_b, N_KV), ones]).reshape(1, QKV_COLS)


def kernel(x_prompt, x_sample, c, cache_k_a_l0, cache_v_a_l0, cache_k_b_l0, cache_v_b_l0, state_ssm_re_l1, state_ssm_im_l1, c_ctx, mod_w_l0, mod_b_l0, norm_mix_l0, attn_w_in_l0, q_norm_a_l0, k_norm_a_l0, q_norm_b_l0, k_norm_b_l0, sink_b_l0, attn_w_out_l0, norm_ffn_l0, router_l0, moe_w_gate_l0, moe_w_up_l0, moe_w_down_l0, mod_w_l1, mod_b_l1, norm_mix_l1, ssm_w_in_l1, ssm_lambda_re_l1, ssm_lambda_im_l1, ssm_b_re_l1, ssm_b_im_l1, ssm_c_re_l1, ssm_c_im_l1, ssm_log_dt_l1, ssm_d_l1, ssm_w_out_l1, norm_ffn_l1, router_l1, moe_w_gate_l1, moe_w_up_l1, moe_w_down_l1):
    bp, sp, d = x_prompt.shape
    bs, ss, _ = x_sample.shape
    past = cache_k_a_l0.shape[1]
    assert d == D_MODEL and bs <= 7 and sp % TM == 0 and ss % TM == 0
    xp = x_prompt.reshape(bp * sp, d)
    xs = x_sample.reshape(bs * ss, d)
    cond8 = jnp.concatenate([c_ctx[None], c, jnp.zeros((7 - bs, d), F32)], axis=0)
    row1 = lambda v: v.reshape(1, -1)

    mod0 = _mod_rows(cond8, mod_w_l0, mod_b_l0)
    w_in = attn_w_in_l0.astype(BF16)
    hgain = _head_gains(q_norm_a_l0, k_norm_a_l0, q_norm_b_l0, k_norm_b_l0)
    lane = np.arange(256)
    bd = jnp.asarray((lane[:, None] // HEAD_DIM == lane[None, :] // HEAD_DIM) / HEAD_DIM, BF16)
    cos_t, sin_t = _rope_tables(ss)
    qp, kap, vap, kbp, vbp = _qkv(xp, mod0, row1(norm_mix_l0), w_in, hgain, bd, cos_t, sin_t,
                                  rows_per_cond=None, seq=sp, rope=False, kv_dtype=F32, transposed_kv=True)
    qs, kas, vas, kbs, vbs = _qkv(xs, mod0, row1(norm_mix_l0), w_in, hgain, bd, cos_t, sin_t,
                                  rows_per_cond=ss, seq=ss, rope=True, kv_dtype=BF16, transposed_kv=False)
    op = _attn_ctx(sink_b_l0, qp, kap, vap, kbp, vbp, sp)
    cache = lambda t: t.reshape(bs, past, N_KV * HEAD_DIM)
    os_ = _attn_lat(sink_b_l0, qs, kas, vas, kbs, vbs, cache(cache_k_a_l0), cache(cache_v_a_l0),
                    cache(cache_k_b_l0), cache(cache_v_b_l0), ss)
    w_out = attn_w_out_l0.astype(BF16)
    x1p, hp, affp = _postmix(op, xp, mod0, w_out, row1(norm_ffn_l0), router_l0.T, rows_per_cond=None, project=True)
    x1s, hs, affs = _postmix(os_, xs, mod0, w_out, row1(norm_ffn_l0), router_l0.T, rows_per_cond=ss, project=True)
    xp, xs = _moe_pair(hp, affp, x1p, hs, affs, x1s, mod0, moe_w_gate_l0, moe_w_up_l0, moe_w_down_l0, sp, ss)

    mod1 = _mod_rows(cond8, mod_w_l1, mod_b_l1)
    ops = _ssm_operand_params(ssm_lambda_re_l1, ssm_lambda_im_l1, ssm_b_re_l1, ssm_b_im_l1, ssm_c_re_l1, ssm_c_im_l1,
                              ssm_log_dt_l1, ssm_d_l1)
    wt = ssm_w_in_l1.T.astype(BF16)
    w_so = ssm_w_out_l1.astype(BF16)
    h0 = jnp.stack([state_ssm_re_l1[:, 0], state_ssm_im_l1[:, 0], state_ssm_re_l1[:, 1], state_ssm_im_l1[:, 1]],
                   axis=0).transpose(2, 0, 3, 1)
    mp, ms, fsp = _ssm_mixers(xp, xs, mod1, row1(norm_mix_l1), wt, ops, w_so, h0, nbp=bp, sp=sp, nbs=bs, ss=ss)
    dummy_w = jnp.zeros((8, 128), BF16)
    x1p, hp, affp = _postmix(mp, xp, mod1, dummy_w, row1(norm_ffn_l1), router_l1.T, rows_per_cond=None, project=False)
    x1s, hs, affs = _postmix(ms, xs, mod1, dummy_w, row1(norm_ffn_l1), router_l1.T, rows_per_cond=ss, project=False)
    xp, xs = _moe_pair(hp, affp, x1p, hs, affs, x1s, mod1, moe_w_gate_l1, moe_w_up_l1, moe_w_down_l1, sp, ss)

    kv_out = lambda t: t.reshape(bp, N_KV, HEAD_DIM, sp).transpose(0, 3, 1, 2)
    fin = fsp.transpose(3, 1, 0, 2)
    ssm_re = jnp.stack([fin[:, 0], fin[:, 2]], axis=1)
    ssm_im = jnp.stack([fin[:, 1], fin[:, 3]], axis=1)
    return (xp.reshape(bp, sp, d), xs.reshape(bs, ss, d), kv_out(kap), kv_out(vap), kv_out(kbp), kv_out(vbp),
            ssm_re, ssm_im)
```

```python
import functools
import itertools

import jax
import jax.numpy as jnp
import numpy as np
from jax import lax
from jax.experimental import pallas as pl
from jax.experimental.pallas import tpu as pltpu

F32, BF16, I32 = jnp.float32, jnp.bfloat16, jnp.int32

D_MODEL = 1024
GRID_W = 64
HEAD_DIM = 64
N_HEADS = 8
N_KV = 2
WINDOW = 128
ROPE_THETA = 10000.0
SSM_GROUP = 16
SSM_GROUPS = D_MODEL // SSM_GROUP
SSM_STATE = 64
N_EXPERTS = 16
EC_FACTOR = 2
D_FF = 2 * D_MODEL
EPS = 1e-6
NEG_INF = -1e30
QKV_COLS = 2 * (N_HEADS + 2 * N_KV) * HEAD_DIM
ATTN_OUT = 2 * N_HEADS * HEAD_DIM

SSM_CHUNK = 16
SSM_ROWS = SSM_CHUNK * SSM_GROUP
SSM_GB = 4
SSM_PASSES = 1

SSM_TC = 512

TM = 512
VMEM_LIMIT = 56 * 1024 * 1024

NN = (((1,), (0,)), ((), ()))
NT = (((1,), (1,)), ((), ()))


def _dot(a, b, dims=NN):
    return lax.dot_general(a, b, dims, preferred_element_type=F32)


def _split2(x):
    hi = x.astype(BF16)
    lo = (x - hi.astype(F32)).astype(BF16)
    return hi, lo


def _split3(x):
    hi = x.astype(BF16)
    r = x - hi.astype(F32)
    mid = r.astype(BF16)
    lo = (r - mid.astype(F32)).astype(BF16)
    return hi, mid, lo


def _dot3(a, b, dims=NN):
    ah, al = _split2(a)
    bh, bl = _split2(b)
    return _dot(ah, bh, dims) + (_dot(ah, bl, dims) + _dot(al, bh, dims))


def _mm(a, b, passes):
    if passes == 1:
        return _dot(a.astype(BF16), b.astype(BF16))
    return _dot3(a, b)


def _dot_sel(x, sel):
    hi, mid, lo = _split3(x)
    return _dot(hi, sel) + (_dot(mid, sel) + _dot(lo, sel))


def _sigmoid(x):
    return 1.0 / (1.0 + jnp.exp(-x))


def _norm_mod(x, gain, shift, scale):
    ms = jnp.mean(x * x, axis=-1, keepdims=True)
    y = x * lax.rsqrt(ms + EPS) * gain
    return y * (1.0 + scale) + shift


def _params(*sem):
    return pltpu.CompilerParams(dimension_semantics=sem, vmem_limit_bytes=VMEM_LIMIT)


def _adaln_kernel(c_ref, w_ref, b_ref, o_ref):
    c = c_ref[...]
    s = c * _sigmoid(c)
    o_ref[...] = _dot3(s, w_ref[...]) + b_ref[...]


def _adaln(cond8, w_mod, b_mod):
    d, e = w_mod.shape
    tn = 1536
    return pl.pallas_call(
        _adaln_kernel,
        grid=(e // tn,),
        in_specs=[pl.BlockSpec((8, d), lambda j: (0, 0)),
                  pl.BlockSpec((d, tn), lambda j: (0, j)),
                  pl.BlockSpec((1, tn), lambda j: (0, j))],
        out_specs=pl.BlockSpec((8, tn), lambda j: (0, j)),
        out_shape=jax.ShapeDtypeStruct((8, e), F32),
        compiler_params=_params("parallel"),
        name="adaln",
    )(cond8, w_mod, b_mod.reshape(1, e))


def _mod_rows(cond8, w_mod, b_mod):
    m = _adaln(cond8, w_mod, b_mod).reshape(8, 6, D_MODEL)
    return jnp.pad(m, ((0, 0), (0, 2), (0, 0)))


def _mod_spec(rows_per_cond):
    if rows_per_cond is None:
        return pl.BlockSpec((1, 8, D_MODEL), lambda i: (0, 0, 0))
    return pl.BlockSpec((1, 8, D_MODEL), lambda i: (1 + (i * TM) // rows_per_cond, 0, 0))


def _qkv_kernel(x_ref, mod_ref, gain_ref, w_ref, hg_ref, bd_ref, cos_ref, sin_ref,
                q_ref, ka_ref, va_ref, kb_ref, vb_ref, *, rope, transposed_kv):
    h = _norm_mod(x_ref[...], gain_ref[...], mod_ref[0, 0:1, :], mod_ref[0, 1:2, :])
    proj = _dot(h.astype(BF16), w_ref[...])
    bd = bd_ref[...]

    def head_norm(blk, g):
        hi, lo = _split2(blk * blk)
        ms = _dot(hi, bd) + _dot(lo, bd)
        return blk * lax.rsqrt(ms + EPS) * g

    def rotary(blk):
        w = blk.shape[1]
        even = (lax.broadcasted_iota(I32, blk.shape, 1) & 1) == 0
        swapped = jnp.where(even, pltpu.roll(blk, w - 1, 1), pltpu.roll(blk, 1, 1))
        return blk * cos_ref[:, :w] + swapped * sin_ref[:, :w]

    def qk(c0):
        blk = head_norm(proj[:, c0:c0 + 256], hg_ref[:, c0:c0 + 256])
        return rotary(blk) if rope else blk

    q_ref[:, 0:256] = qk(0).astype(q_ref.dtype)
    q_ref[:, 256:512] = qk(256).astype(q_ref.dtype)
    q_ref[:, 512:768] = qk(768).astype(q_ref.dtype)
    q_ref[:, 768:1024] = qk(1024).astype(q_ref.dtype)
    kva = qk(512)
    kvb = qk(1280)
    outs = ((ka_ref, kva[:, :128]), (va_ref, proj[:, 640:768]), (kb_ref, kvb[:, :128]), (vb_ref, proj[:, 1408:1536]))
    for ref, val in outs:
        if transposed_kv:
            seq = ref.shape[2]
            for r in range(ref.shape[0]):
                ref[r] = val[r * seq:(r + 1) * seq].T.astype(ref.dtype)
        else:
            ref[...] = val.astype(ref.dtype)


def _qkv(x, mod, gain, w_bf, hgain, bd, cos_t, sin_t, *, rows_per_cond, seq, rope, kv_dtype, transposed_kv):
    n = x.shape[0]
    tiles_per_seq = max(1, seq // TM)
    row = lambda i: (i, 0)
    const = lambda i: (0, 0)
    pos = lambda i: (i % tiles_per_seq, 0)
    if transposed_kv:
        assert TM % seq == 0
        kv_shape = jax.ShapeDtypeStruct((n // seq, 128, seq), kv_dtype)
        kv_spec = pl.BlockSpec((TM // seq, 128, seq), lambda i: (i, 0, 0))
    else:
        kv_shape = jax.ShapeDtypeStruct((n, 128), kv_dtype)
        kv_spec = pl.BlockSpec((TM, 128), row)
    return pl.pallas_call(
        functools.partial(_qkv_kernel, rope=rope, transposed_kv=transposed_kv),
        grid=(n // TM,),
        in_specs=[pl.BlockSpec((TM, D_MODEL), row),
                  _mod_spec(rows_per_cond),
                  pl.BlockSpec((1, D_MODEL), const),
                  pl.BlockSpec((D_MODEL, QKV_COLS), const),
                  pl.BlockSpec((1, QKV_COLS), const),
                  pl.BlockSpec((256, 256), const),
                  pl.BlockSpec((TM, 256), pos),
                  pl.BlockSpec((TM, 256), pos)],
        out_specs=[pl.BlockSpec((TM, ATTN_OUT), row)] + [kv_spec] * 4,
        out_shape=[jax.ShapeDtypeStruct((n, ATTN_OUT), BF16)] + [kv_shape] * 4,
        compiler_params=_params("parallel"),
        name="qkv_rope" if rope else "qkv",
    )(x, mod, gain, w_bf, hgain, bd, cos_t, sin_t)


def _pad_variants(kk):
    left = lax.broadcasted_iota(I32, kk.shape, 1) < HEAD_DIM
    rolled = pltpu.roll(kk, HEAD_DIM, 1)
    zero = jnp.zeros_like(kk)
    return {(0, 0): jnp.where(left, kk, zero).astype(BF16),
            (0, 1): jnp.where(left, zero, rolled).astype(BF16),
            (1, 0): jnp.where(left, rolled, zero).astype(BF16),
            (1, 1): jnp.where(left, zero, kk).astype(BF16)}


def _head_attention(qp, keys, vals, masks, sink, transposed=False):
    kdims, vdims = (NN, NT) if transposed else (NT, NN)
    scores = []
    for kblk, mask in zip(keys, masks):
        s = _dot(qp, kblk, kdims)
        if mask is not None:
            s = jnp.where(mask, s, NEG_INF)
        scores.append(s)
    m = scores[0].max(axis=-1, keepdims=True)
    for s in scores[1:]:
        m = jnp.maximum(m, s.max(axis=-1, keepdims=True))
    if sink is not None:
        m = jnp.maximum(m, sink)
    den = None
    out = None
    for s, vblk in zip(scores, vals):
        p = jnp.exp(s - m)
        ps = p.sum(axis=-1, keepdims=True)
        den = ps if den is None else den + ps
        o = _dot(p.astype(BF16), vblk, vdims)
        out = o if out is None else out + o
    if sink is not None:
        den = den + jnp.exp(sink - m)
    return out / den


def _pad_variants_t(kt):
    top = lax.broadcasted_iota(I32, kt.shape, 0) < HEAD_DIM
    zero = jnp.zeros((HEAD_DIM, kt.shape[1]), F32)
    return {(0, 0): jnp.where(top, kt, 0.0).astype(BF16),
            (0, 1): jnp.concatenate([zero, kt[:HEAD_DIM]], axis=0).astype(BF16),
            (1, 0): jnp.concatenate([kt[HEAD_DIM:], zero], axis=0).astype(BF16),
            (1, 1): jnp.where(top, 0.0, kt).astype(BF16)}


def _attn_ctx_kernel(sink_ref, q_ref, ka_ref, va_ref, kb_ref, vb_ref, o_ref):
    for mixer, (k_ref, v_ref) in enumerate(((ka_ref, va_ref), (kb_ref, vb_ref))):
        kvar = _pad_variants_t(k_ref[0])
        vvar = _pad_variants_t(v_ref[0])
        for t in range(4):
            tile = mixer * 4 + t
            kv = t // 2
            qp = q_ref[:, tile * 128:(tile + 1) * 128]
            acc = None
            for par in range(2):
                sink = sink_ref[2 * t + par] if mixer == 1 else None
                o = _head_attention(qp, [kvar[(kv, par)]], [vvar[(kv, par)]], [None], sink, transposed=True)
                acc = o if acc is None else acc + o
            o_ref[:, tile * 128:(tile + 1) * 128] = acc.astype(o_ref.dtype)


def _attn_ctx(sink, q, ka, va, kb, vb, seq):
    n = q.shape[0]
    row = lambda b: (b, 0)
    kv_spec = pl.BlockSpec((1, 128, seq), lambda b: (b, 0, 0))
    return pl.pallas_call(
        _attn_ctx_kernel,
        grid=(n // seq,),
        in_specs=[pl.BlockSpec(memory_space=pltpu.SMEM),
                  pl.BlockSpec((seq, ATTN_OUT), row), kv_spec, kv_spec, kv_spec, kv_spec],
        out_specs=pl.BlockSpec((seq, ATTN_OUT), row),
        out_shape=jax.ShapeDtypeStruct((n, ATTN_OUT), BF16),
        compiler_params=_params("parallel"),
        name="attn_ctx",
    )(sink, q, ka, va, kb, vb)


def _attn_lat_kernel(sink_ref, q_ref, ka_ref, va_ref, kb_ref, vb_ref,
                     cka_ref, cva_ref, ckb_ref, cvb_ref, o_ref, *, tq, seq):
    qi = pl.program_id(1)
    span = tq + 2 * WINDOW
    ck = _pad_variants(cka_ref[0])
    cv = _pad_variants(cva_ref[0])
    lk = _pad_variants(ka_ref[...].astype(F32))
    lv = _pad_variants(va_ref[...].astype(F32))
    for t in range(4):
        kv = t // 2
        qp = q_ref[:, t * 128:(t + 1) * 128]
        acc = None
        for par in range(2):
            o = _head_attention(qp, [ck[(kv, par)], lk[(kv, par)]], [cv[(kv, par)], lv[(kv, par)]],
                                [None, None], None)
            acc = o if acc is None else acc + o
        o_ref[:, t * 128:(t + 1) * 128] = acc.astype(o_ref.dtype)
    lo = jnp.clip(qi * tq - WINDOW, 0, seq - span)
    lo = pl.multiple_of(lo, 128)
    qpos = qi * tq + lax.broadcasted_iota(I32, (tq, span), 0)
    kpos = lo + lax.broadcasted_iota(I32, (tq, span), 1)
    band = jnp.abs(qpos - kpos) <= WINDOW
    ck = _pad_variants(ckb_ref[0])
    cv = _pad_variants(cvb_ref[0])
    lk = _pad_variants(kb_ref[pl.ds(lo, span), :].astype(F32))
    lv = _pad_variants(vb_ref[pl.ds(lo, span), :].astype(F32))
    for t in range(4):
        kv = t // 2
        tile = 4 + t
        qp = q_ref[:, tile * 128:(tile + 1) * 128]
        acc = None
        for par in range(2):
            sink = sink_ref[2 * t + par]
            o = _head_attention(qp, [lk[(kv, par)], ck[(kv, par)]], [lv[(kv, par)], cv[(kv, par)]],
                                [band, None], sink)
            acc = o if acc is None else acc + o
        o_ref[:, tile * 128:(tile + 1) * 128] = acc.astype(o_ref.dtype)


def _attn_lat(sink, q, ka, va, kb, vb, cka, cva, ckb, cvb, seq, tq=256):
    n = q.shape[0]
    nb = n // seq
    nq = seq // tq
    qrow = lambda b, i: (b * nq + i, 0)
    brow = lambda b, i: (b, 0)
    kv_spec = pl.BlockSpec((seq, 128), brow)
    past = cka.shape[1]
    c_spec = pl.BlockSpec((1, past, 128), lambda b, i: (b, 0, 0))
    return pl.pallas_call(
        functools.partial(_attn_lat_kernel, tq=tq, seq=seq),
        grid=(nb, nq),
        in_specs=[pl.BlockSpec(memory_space=pltpu.SMEM),
                  pl.BlockSpec((tq, ATTN_OUT), qrow), kv_spec, kv_spec, kv_spec, kv_spec,
                  c_spec, c_spec, c_spec, c_spec],
        out_specs=pl.BlockSpec((tq, ATTN_OUT), qrow),
        out_shape=jax.ShapeDtypeStruct((n, ATTN_OUT), BF16),
        compiler_params=_params("parallel", "parallel"),
        name="attn_lat",
    )(sink, q, ka, va, kb, vb, cka, cva, ckb, cvb)


def _postmix_kernel(m_ref, x_ref, mod_ref, w_ref, gain_ref, rt_ref, x1_ref, h2_ref, aff_ref, *, project):
    if project:
        m = _dot(m_ref[...], w_ref[...])
    else:
        m = m_ref[...]
    x1 = x_ref[...] + mod_ref[0, 2:3, :] * m
    x1_ref[...] = x1
    h2 = _norm_mod(x1, gain_ref[...], mod_ref[0, 3:4, :], mod_ref[0, 4:5, :])
    h2_ref[...] = h2.astype(h2_ref.dtype)
    logits = _dot3(rt_ref[...], h2, NT)
    e = jnp.exp(logits - logits.max(axis=0, keepdims=True))
    aff_ref[...] = e / e.sum(axis=0, keepdims=True)


def _postmix(m, x, mod, w_bf, gain, router_t, *, rows_per_cond, project):
    n = x.shape[0]
    row = lambda i: (i, 0)
    const = lambda i: (0, 0)
    return pl.pallas_call(
        functools.partial(_postmix_kernel, project=project),
        grid=(n // TM,),
        in_specs=[pl.BlockSpec((TM, D_MODEL), row),
                  pl.BlockSpec((TM, D_MODEL), row),
                  _mod_spec(rows_per_cond),
                  pl.BlockSpec(w_bf.shape, const),
                  pl.BlockSpec((1, D_MODEL), const),
                  pl.BlockSpec((N_EXPERTS, D_MODEL), const)],
        out_specs=[pl.BlockSpec((TM, D_MODEL), row),
                   pl.BlockSpec((TM, D_MODEL), row),
                   pl.BlockSpec((N_EXPERTS, TM), lambda i: (0, i))],
        out_shape=[jax.ShapeDtypeStruct((n, D_MODEL), F32),
                   jax.ShapeDtypeStruct((n, D_MODEL), BF16),
                   jax.ShapeDtypeStruct((N_EXPERTS, n), F32)],
        compiler_params=_params("parallel"),
        name="postmix_proj" if project else "postmix",
    )(m, x, mod, w_bf, gain, router_t)


def _route_kernel(aff_ref, slot_ref, gate_ref, *, seq, cap, nseg):
    aff = jnp.concatenate([aff_ref[:, s * seq:(s + 1) * seq] for s in range(nseg)], axis=0)
    rows = aff.shape[0]
    capf = jnp.float32(cap)
    thr_bits = jnp.zeros((rows, 1), I32)
    for bit in range(30, -1, -1):
        cand = thr_bits | (1 << bit)
        cnt = jnp.where(aff >= pltpu.bitcast(cand, F32), 1.0, 0.0).sum(axis=1, keepdims=True)
        thr_bits = jnp.where(cnt >= capf, cand, thr_bits)
    thr = pltpu.bitcast(thr_bits, F32)
    gt = aff > thr
    eq = aff == thr
    n_gt = jnp.where(gt, 1.0, 0.0).sum(axis=1, keepdims=True)
    pw = min(seq, 256)
    tri = jnp.where(lax.broadcasted_iota(I32, (pw, pw), 0) < lax.broadcasted_iota(I32, (pw, pw), 1),
                    1.0, 0.0).astype(BF16)

    def count_before(flag):
        ones = jnp.where(flag, 1.0, 0.0)
        parts = []
        run = jnp.zeros((rows, 1), F32)
        for c0 in range(0, seq, pw):
            blk = ones[:, c0:c0 + pw]
            parts.append(_dot(blk.astype(BF16), tri) + run)
            run = run + blk.sum(axis=1, keepdims=True)
        return jnp.concatenate(parts, axis=1) if len(parts) > 1 else parts[0]

    sel = gt | (eq & (count_before(eq) < capf - n_gt))
    rank = count_before(sel)
    expert = lax.broadcasted_iota(I32, (rows, seq), 0) & (N_EXPERTS - 1)
    slot = jnp.where(sel, expert * cap + rank.astype(I32), -1)
    gate = jnp.where(sel, aff, 0.0)
    for s in range(nseg):
        slot_ref[:, s * seq:(s + 1) * seq] = slot[s * N_EXPERTS:(s + 1) * N_EXPERTS, :]
        gate_ref[:, s * seq:(s + 1) * seq] = gate[s * N_EXPERTS:(s + 1) * N_EXPERTS, :]


def _route(aff_t, seq, cap, nseg):
    n = aff_t.shape[1]
    spec = pl.BlockSpec((N_EXPERTS, nseg * seq), lambda i: (0, i))
    return pl.pallas_call(
        functools.partial(_route_kernel, seq=seq, cap=cap, nseg=nseg),
        grid=(n // (nseg * seq),),
        in_specs=[spec],
        out_specs=[spec, spec],
        out_shape=[jax.ShapeDtypeStruct((N_EXPERTS, n), I32), jax.ShapeDtypeStruct((N_EXPERTS, n), F32)],
        compiler_params=_params("parallel"),
        name="route",
    )(aff_t)


def _dispatch_kernel(slot_ref, h_ref, x_ref, *, cap, eg):
    seq = h_ref.shape[0]
    m = eg * cap
    h = h_ref[...]
    for grp in range(N_EXPERTS // eg):
        rid = lax.broadcasted_iota(I32, (m, seq), 0) + grp * m
        hit = None
        for e in range(grp * eg, (grp + 1) * eg):
            he = slot_ref[e:e + 1, :] == rid
            hit = he if hit is None else (hit | he)
        sel = jnp.where(hit, 1.0, 0.0).astype(BF16)
        x_ref[grp * m:(grp + 1) * m, :] = _dot(sel, h).astype(x_ref.dtype)


def _dispatch(slot, h, seq, cap, eg):
    n = h.shape[0]
    nb = n // seq
    return pl.pallas_call(
        functools.partial(_dispatch_kernel, cap=cap, eg=eg),
        grid=(nb,),
        in_specs=[pl.BlockSpec((N_EXPERTS, seq), lambda b: (0, b)),
                  pl.BlockSpec((seq, D_MODEL), lambda b: (b, 0))],
        out_specs=pl.BlockSpec((N_EXPERTS * cap, D_MODEL), lambda b: (b, 0)),
        out_shape=jax.ShapeDtypeStruct((nb * N_EXPERTS * cap, D_MODEL), BF16),
        compiler_params=_params("parallel"),
        name="moe_dispatch",
    )(slot, h)


FFN_TF = 512
FFN_RC = 512


def _ffn_kernel(xa_ref, xb_ref, wg_ref, wu_ref, wd_ref, ya_ref, yb_ref, acc_ref):
    j = pl.program_id(1)
    wg = wg_ref[0].astype(BF16)
    wu = wu_ref[0].astype(BF16)
    wd = wd_ref[0].astype(BF16)
    ra = xa_ref.shape[0] * xa_ref.shape[2]

    def row_chunks(ref, base):
        nb, _, cap, d = ref.shape
        rc = min(FFN_RC, nb * cap)
        for r0 in range(0, nb * cap, rc):
            if cap >= rc:
                b, c0 = divmod(r0, cap)
                yield base + r0, ref[b, 0, c0:c0 + rc, :]
            else:
                yield base + r0, ref[r0 // cap:(r0 + rc) // cap, 0, :, :].reshape(rc, d)

    for r0, x in itertools.chain(row_chunks(xa_ref, 0), row_chunks(xb_ref, ra)):
        rc = x.shape[0]
        g = _dot(x, wg)
        u = _dot(x, wu)
        mid = (g * _sigmoid(g) * u).astype(BF16)
        y = _dot(mid, wd)

        @pl.when(j == 0)
        def _():
            acc_ref[r0:r0 + rc, :] = y

        @pl.when(j > 0)
        def _():
            acc_ref[r0:r0 + rc, :] += y

    @pl.when(j == pl.num_programs(1) - 1)
    def _():
        for ref, base in ((ya_ref, 0), (yb_ref, ra)):
            nb, _, cap, d = ref.shape
            ref[...] = acc_ref[base:base + nb * cap, :].reshape(nb, 1, cap, d).astype(ref.dtype)


def _ffn(xa, xb, w_gate, w_up, w_down):
    ba, _, ca, d = xa.shape
    bb, _, cb, _ = xb.shape
    nj = D_FF // FFN_TF
    xa_spec = pl.BlockSpec((ba, 1, ca, d), lambda e, j: (0, e, 0, 0))
    xb_spec = pl.BlockSpec((bb, 1, cb, d), lambda e, j: (0, e, 0, 0))
    return pl.pallas_call(
        _ffn_kernel,
        grid=(N_EXPERTS, nj),
        in_specs=[xa_spec, xb_spec,
                  pl.BlockSpec((1, d, FFN_TF), lambda e, j: (e, 0, j)),
                  pl.BlockSpec((1, d, FFN_TF), lambda e, j: (e, 0, j)),
                  pl.BlockSpec((1, FFN_TF, d), lambda e, j: (e, j, 0))],
        out_specs=[xa_spec, xb_spec],
        out_shape=[jax.ShapeDtypeStruct(xa.shape, BF16), jax.ShapeDtypeStruct(xb.shape, BF16)],
        scratch_shapes=[pltpu.VMEM((ba * ca + bb * cb, d), F32)],
        compiler_params=_params("parallel", "arbitrary"),
        name="moe_ffn",
    )(xa, xb, w_gate, w_up, w_down)


def _combine_kernel(slot_ref, gate_ref, y_ref, x_ref, mod_ref, o_ref, *, cap, eg):
    tt = x_ref.shape[0]
    m = eg * cap
    acc = jnp.zeros((tt, D_MODEL), F32)
    for grp in range(N_EXPERTS // eg):
        lid = lax.broadcasted_iota(I32, (tt, m), 1) + grp * m
        w = jnp.zeros((tt, m), F32)
        for e in range(grp * eg, (grp + 1) * eg):
            w = w + jnp.where(slot_ref[:, e:e + 1] == lid, gate_ref[:, e:e + 1], 0.0)
        acc = acc + _dot(w.astype(BF16), y_ref[grp * m:(grp + 1) * m, :])
    o_ref[...] = x_ref[...] + mod_ref[0, 5:6, :] * acc


def _combine(slot_t, gate_t, y, x, mod, *, seq, cap, eg, tt, rows_per_cond):
    n = x.shape[0]
    nt = seq // tt
    row = lambda b, i: (b * nt + i, 0)
    if rows_per_cond is None:
        mod_spec = pl.BlockSpec((1, 8, D_MODEL), lambda b, i: (0, 0, 0))
    else:
        mod_spec = pl.BlockSpec((1, 8, D_MODEL), lambda b, i: (1 + b, 0, 0))
    return pl.pallas_call(
        functools.partial(_combine_kernel, cap=cap, eg=eg),
        grid=(n // seq, nt),
        in_specs=[pl.BlockSpec((tt, N_EXPERTS), row),
                  pl.BlockSpec((tt, N_EXPERTS), row),
                  pl.BlockSpec((N_EXPERTS * cap, D_MODEL), lambda b, i: (b, 0)),
                  pl.BlockSpec((tt, D_MODEL), row),
                  mod_spec],
        out_specs=pl.BlockSpec((tt, D_MODEL), row),
        out_shape=jax.ShapeDtypeStruct((n, D_MODEL), F32),
        compiler_params=_params("parallel", "parallel"),
        name="moe_combine",
    )(slot_t, gate_t, y, x, mod)


def _moe_pair(hp, affp, x1p, hs, affs, x1s, mod, w_gate, w_up, w_down, seq_p, seq_s):
    n_p, n_s = hp.shape[0], hs.shape[0]
    cap_p = EC_FACTOR * seq_p // N_EXPERTS
    cap_s = EC_FACTOR * seq_s // N_EXPERTS
    eg_p = max(1, 512 // cap_p)
    eg_s = max(1, 256 // cap_s)
    slot_p, gate_p = _route(affp, seq_p, cap_p, nseg=min(8, n_p // seq_p))
    slot_s, gate_s = _route(affs, seq_s, cap_s, nseg=min(4, n_s // seq_s))
    xp = _dispatch(slot_p, hp, seq_p, cap_p, eg_p).reshape(n_p // seq_p, N_EXPERTS, cap_p, D_MODEL)
    xs = _dispatch(slot_s, hs, seq_s, cap_s, eg_s).reshape(n_s // seq_s, N_EXPERTS, cap_s, D_MODEL)
    ys, yp = _ffn(xs, xp, w_gate, w_up, w_down)
    outp = _combine(slot_p.T, gate_p.T, yp.reshape(-1, D_MODEL), x1p, mod, seq=seq_p, cap=cap_p, eg=eg_p,
                    tt=seq_p, rows_per_cond=None)
    outs = _combine(slot_s.T, gate_s.T, ys.reshape(-1, D_MODEL), x1s, mod, seq=seq_s, cap=cap_s, eg=eg_s,
                    tt=min(512, seq_s), rows_per_cond=seq_s)
    return outp, outs


def _ssm_in_kernel(x_ref, mod_ref, gain_ref, wt_ref, ut_ref, *, cols_per_cond):
    tc = x_ref.shape[1]
    if cols_per_cond is None:
        h = _norm_mod(x_ref[0], gain_ref[...], mod_ref[0, 0:1, :], mod_ref[0, 1:2, :]).astype(BF16)
    else:
        first = 1 + pl.program_id(1) * (tc // cols_per_cond)
        parts = []
        for s in range(tc // cols_per_cond):
            m = mod_ref[first + s]
            parts.append(_norm_mod(x_ref[0, s * cols_per_cond:(s + 1) * cols_per_cond, :], gain_ref[...],
                                   m[0:1, :], m[1:2, :]).astype(BF16))
        h = jnp.concatenate(parts, axis=0) if len(parts) > 1 else parts[0]
    ut_ref[0] = _dot(wt_ref[...], h, NT)


def _ssm_in(xperm, mod, gain, wt_bf, *, cols_per_cond, tc):
    l, bk, d = xperm.shape
    assert cols_per_cond is None or tc % cols_per_cond == 0
    return pl.pallas_call(
        functools.partial(_ssm_in_kernel, cols_per_cond=cols_per_cond),
        grid=(l, bk // tc),
        in_specs=[pl.BlockSpec((1, tc, d), lambda j, i: (j, i, 0)),
                  pl.BlockSpec(mod.shape, lambda j, i: (0, 0, 0)),
                  pl.BlockSpec((1, d), lambda j, i: (0, 0)),
                  pl.BlockSpec((d, d), lambda j, i: (0, 0))],
        out_specs=pl.BlockSpec((1, d, tc), lambda j, i: (j, 0, i)),
        out_shape=jax.ShapeDtypeStruct((l, d, bk), F32),
        compiler_params=_params("parallel", "parallel"),
        name="ssm_in",
    )(xperm, mod, gain, wt_bf)


def _ssm_core_kernel(utp_ref, uts_ref, lamp_ref, c_ref, bt_ref, dsk_ref, h0_ref, ytp_ref, yts_ref, fs_ref,
                     *, kp, ks, nbp, nbs):
    rows = SSM_ROWS
    p = SSM_STATE
    lc = SSM_CHUNK
    ri = lax.broadcasted_iota(I32, (rows, rows), 0)
    cj = lax.broadcasted_iota(I32, (rows, rows), 1)
    causal = (ri >> 4) >= (cj >> 4)
    anticausal = (cj >> 4) >= (ri >> 4)
    diag = ri == cj
    leftc = lax.broadcasted_iota(I32, (lc, 128), 1) < p
    nrow = lax.broadcasted_iota(I32, (lc, 128), 0).astype(F32)
    eye = lax.broadcasted_iota(I32, (p, 128), 0) == lax.broadcasted_iota(I32, (p, 128), 1)

    def cmul(ar, ai, xr, xi):
        return ar * xr - ai * xi, ar * xi + ai * xr

    def expand_rows(t):
        return jnp.broadcast_to(t[:, None, :], (lc, SSM_GROUP, 128)).reshape(rows, 128)

    def tile_rows(t):
        return jnp.broadcast_to(t[None, :, :], (lc, SSM_GROUP, 128)).reshape(rows, 128)

    def to_col(row):
        return jnp.where(eye, jnp.broadcast_to(row, (p, 128)), 0.0).sum(axis=1, keepdims=True)

    def operands(gg, d):
        lp = lamp_ref[gg, d]
        lre, lim = lp[0:1], lp[1:2]
        dt = jnp.exp(lp[2:3])
        a, th = lre * dt, lim * dt
        ang = nrow * th
        cs, sn = jnp.cos(ang), jnp.sin(ang)
        ep, em = jnp.exp(nrow * a), jnp.exp(-(nrow * a))
        pr, pi = ep * cs, ep * sn
        nr, ni = em * cs, -(em * sn)
        l1r, l1i = pr[1:2], pi[1:2]
        lmr, lmi = pr[lc - 1:lc], pi[lc - 1:lc]
        llr, lli = cmul(lmr, lmi, l1r, l1i)
        den = lre * lre + lim * lim
        cr = ((l1r - 1.0) * lre + l1i * lim) / den
        ci = (l1i * lre - (l1r - 1.0) * lim) / den
        btr, bti = bt_ref[gg, d, 0], bt_ref[gg, d, 1]
        bbr, bbi = cr * btr - ci * bti, cr * bti + ci * btr
        u1 = tile_rows(jnp.where(leftc, bbr, bbi))
        u2 = tile_rows(jnp.where(leftc, bbi, bbr))
        c1 = tile_rows(c_ref[gg, d, 0])
        c2 = tile_rows(c_ref[gg, d, 1])

        def left_form(xr, xi):
            return (c1 * expand_rows(jnp.where(leftc, xr, -xi))
                    + c2 * expand_rows(jnp.where(leftc, -xi, -xr)))

        def right_form(xr, xi):
            return u1 * expand_rows(xr) + u2 * expand_rows(jnp.where(leftc, -xi, xi))

        if d == 0:
            al = left_form(pr, pi)
            brt = right_form(nr, ni)
            rrt = right_form(*cmul(lmr, lmi, nr, ni))
            qq = left_form(*cmul(l1r, l1i, pr, pi))
            mat = jnp.where(causal, _dot3(al, brt, NT), 0.0)
        else:
            al = left_form(nr, ni)
            brt = right_form(pr, pi)
            rrt = brt
            qq = left_form(*cmul(llr, lli, nr, ni))
            mat = jnp.where(anticausal, _dot3(al, brt, NT), 0.0)
        return mat, rrt.T, qq, to_col(llr), to_col(lli)

    def scan(sr, si, lr, li, h0r, h0i, reverse, nchunk, nbatch, sel):
        bk = sr.shape[1]
        lane = lax.broadcasted_iota(I32, (p, bk), 1)
        kidx = lane & (nchunk - 1)
        edge = (nchunk - 1) if reverse else 0
        if h0r is not None:
            h0cr = jnp.zeros((p, bk), F32)
            h0ci = jnp.zeros((p, bk), F32)
            for b in range(nbatch):
                at = lane == (b * nchunk + edge)
                h0cr = jnp.where(at, h0r[:, b:b + 1], h0cr)
                h0ci = jnp.where(at, h0i[:, b:b + 1], h0ci)
            ar, ai = cmul(lr, li, h0cr, h0ci)
            er, ei = sr + ar, si + ai
        else:
            er, ei = sr, si
        ar, ai = lr, li
        s = 1
        while s < nchunk:
            if reverse:
                ok = kidx < nchunk - s
                tr, ti = pltpu.roll(er, bk - s, 1), pltpu.roll(ei, bk - s, 1)
            else:
                ok = kidx >= s
                tr, ti = pltpu.roll(er, s, 1), pltpu.roll(ei, s, 1)
            tr = jnp.where(ok, tr, 0.0)
            ti = jnp.where(ok, ti, 0.0)
            dr, di = cmul(ar, ai, tr, ti)
            er, ei = er + dr, ei + di
            ar, ai = cmul(ar, ai, ar, ai)
            s *= 2
        if reverse:
            inner = kidx < nchunk - 1
            hr, hi = pltpu.roll(er, bk - 1, 1), pltpu.roll(ei, bk - 1, 1)
        else:
            inner = kidx >= 1
            hr, hi = pltpu.roll(er, 1, 1), pltpu.roll(ei, 1, 1)
        hr = jnp.where(inner, hr, h0cr if h0r is not None else 0.0)
        hi = jnp.where(inner, hi, h0ci if h0r is not None else 0.0)
        fin = None if sel is None else (_dot_sel(er, sel), _dot_sel(ei, sel))
        return hr, hi, fin

    def final_selectors(bk, nchunk, nbatch):
        col = lax.broadcasted_iota(I32, (bk, nbatch), 0)
        bat = lax.broadcasted_iota(I32, (bk, nbatch), 1)
        last = jnp.where(col == bat * nchunk + (nchunk - 1), 1.0, 0.0).astype(BF16)
        first = jnp.where(col == bat * nchunk, 1.0, 0.0).astype(BF16)
        return last, first

    sel_last, sel_first = final_selectors(utp_ref.shape[2], kp, nbp)

    for gg in range(SSM_GB):
        mf, rf, qf, lfr, lfi = operands(gg, 0)
        mb, rb, qb, lbr, lbi = operands(gg, 1)
        skip = jnp.where(diag, jnp.broadcast_to(dsk_ref[gg], (rows, rows)), 0.0)
        stack = jnp.concatenate([mf + mb + skip, rf, rb], axis=0)
        qq = jnp.concatenate([qf, qb], axis=1)
        h0 = h0_ref[gg]
        for ut_ref, yt_ref, nchunk, nbatch, latent in ((utp_ref, ytp_ref, kp, nbp, False),
                                                       (uts_ref, yts_ref, ks, nbs, True)):
            bk = ut_ref.shape[2]
            x = ut_ref[:, gg * SSM_GROUP:(gg + 1) * SSM_GROUP, :].reshape(rows, bk)
            res = _mm(stack, x, SSM_PASSES)
            hfr, hfi, ff = scan(res[rows:rows + p], res[rows + p:rows + 2 * p], lfr, lfi,
                                h0[0] if latent else None, h0[1] if latent else None,
                                False, nchunk, nbatch, None if latent else sel_last)
            hbr, hbi, fb = scan(res[rows + 2 * p:rows + 3 * p], res[rows + 3 * p:rows + 4 * p], lbr, lbi,
                                h0[2] if latent else None, h0[3] if latent else None,
                                True, nchunk, nbatch, None if latent else sel_first)
            states = jnp.concatenate([hfr, hfi, hbr, hbi], axis=0)
            y = res[:rows] + _mm(qq, states, SSM_PASSES)
            yt_ref[:, gg * SSM_GROUP:(gg + 1) * SSM_GROUP, :] = y.reshape(lc, SSM_GROUP, bk)
            if not latent:
                fs_ref[gg, 0] = ff[0]
                fs_ref[gg, 1] = ff[1]
                fs_ref[gg, 2] = fb[0]
                fs_ref[gg, 3] = fb[1]


def _ssm_core(utp, uts, ops, h0, *, kp, ks, nbp, nbs):
    lamp, c2, bt2, dsk = ops
    l, d, bkp = utp.shape
    bks = uts.shape[2]
    g = SSM_GROUPS
    gb = SSM_GB
    lead4 = lambda i: (i, 0, 0, 0)
    lead5 = lambda i: (i, 0, 0, 0, 0)
    ut_spec = lambda bk: pl.BlockSpec((l, gb * SSM_GROUP, bk), lambda i: (0, i, 0))
    return pl.pallas_call(
        functools.partial(_ssm_core_kernel, kp=kp, ks=ks, nbp=nbp, nbs=nbs),
        grid=(g // gb,),
        in_specs=[ut_spec(bkp), ut_spec(bks),
                  pl.BlockSpec((gb, 2, 8, 128), lead4),
                  pl.BlockSpec((gb, 2, 2, SSM_GROUP, 128), lead5),
                  pl.BlockSpec((gb, 2, 2, SSM_GROUP, 128), lead5),
                  pl.BlockSpec((gb, 1, SSM_ROWS), lambda i: (i, 0, 0)),
                  pl.BlockSpec((gb, 4, SSM_STATE, nbs), lead4)],
        out_specs=[ut_spec(bkp), ut_spec(bks),
                   pl.BlockSpec((gb, 4, SSM_STATE, nbp), lead4)],
        out_shape=[jax.ShapeDtypeStruct((l, d, bkp), F32),
                   jax.ShapeDtypeStruct((l, d, bks), F32),
                   jax.ShapeDtypeStruct((g, 4, SSM_STATE, nbp), F32)],
        compiler_params=_params("parallel"),
        name="ssm_core",
    )(utp, uts, lamp, c2, bt2, dsk, h0)


def _ssm_out_kernel(yt_ref, w_ref, m_ref):
    y = yt_ref[0].T
    act = 0.5 * y * (1.0 + jnp.tanh(0.7978845608028654 * (y + 0.044715 * (y * y * y))))
    ag = _dot(act.astype(BF16), w_ref[...])
    d = m_ref.shape[2]
    m_ref[0] = ag[:, :d] * _sigmoid(ag[:, d:])


def _ssm_out(yt, w_bf, tc):
    l, d, bk = yt.shape
    return pl.pallas_call(
        _ssm_out_kernel,
        grid=(l, bk // tc),
        in_specs=[pl.BlockSpec((1, d, tc), lambda j, i: (j, 0, i)),
                  pl.BlockSpec((d, 2 * d), lambda j, i: (0, 0))],
        out_specs=pl.BlockSpec((1, tc, d), lambda j, i: (j, i, 0)),
        out_shape=jax.ShapeDtypeStruct((l, bk, d), F32),
        compiler_params=_params("parallel", "parallel"),
        name="ssm_out",
    )(yt, w_bf)


def _ssm_operand_params(lam_re, lam_im, b_re, b_im, c_re, c_im, log_dt, d_skip):
    g, l = SSM_GROUPS, SSM_CHUNK
    dup = lambda t: jnp.concatenate([t, t], axis=-1)
    lamp = jnp.stack([lam_re, lam_im, jnp.broadcast_to(log_dt[..., None], lam_re.shape)], axis=2)
    lamp = dup(jnp.pad(lamp, ((0, 0), (0, 0), (0, 5), (0, 0)))).transpose(1, 0, 2, 3)
    c2 = dup(jnp.stack([c_re, c_im], axis=2)).transpose(1, 0, 2, 3, 4)
    bt2 = dup(jnp.stack([jnp.swapaxes(b_re, -1, -2), jnp.swapaxes(b_im, -1, -2)], axis=2)).transpose(1, 0, 2, 3, 4)
    dsk = jnp.tile(d_skip.reshape(g, 1, SSM_GROUP), (1, 1, l))
    return lamp, c2, bt2, dsk


def _to_chunks(x, nb, seq):
    k = seq // SSM_CHUNK
    return x.reshape(nb, k, SSM_CHUNK, -1).transpose(2, 0, 1, 3).reshape(SSM_CHUNK, nb * k, -1)


def _from_chunks(x, nb, seq):
    k = seq // SSM_CHUNK
    return x.reshape(SSM_CHUNK, nb, k, -1).transpose(1, 2, 0, 3).reshape(nb * seq, -1)


def _ssm_mixers(xp, xs, mod, gain, wt_bf, ops, w_out_bf, h0, *, nbp, sp, nbs, ss):
    kp, ks = sp // SSM_CHUNK, ss // SSM_CHUNK
    tcp, tcs = min(SSM_TC, nbp * kp), min(SSM_TC, nbs * ks)
    utp = _ssm_in(_to_chunks(xp, nbp, sp), mod, gain, wt_bf, cols_per_cond=None, tc=tcp)
    uts = _ssm_in(_to_chunks(xs, nbs, ss), mod, gain, wt_bf, cols_per_cond=ks, tc=tcs)
    ytp, yts, fs = _ssm_core(utp, uts, ops, h0, kp=kp, ks=ks, nbp=nbp, nbs=nbs)
    mp = _from_chunks(_ssm_out(ytp, w_out_bf, tcp), nbp, sp)
    ms = _from_chunks(_ssm_out(yts, w_out_bf, tcs), nbs, ss)
    return mp, ms, fs


def _rope_tables(seq):
    t = jnp.arange(seq)
    row = (t // GRID_W).astype(F32)
    col = (t % GRID_W).astype(F32)
    n_freq = HEAD_DIM // 4
    inv_freq = ROPE_THETA ** (-jnp.arange(n_freq, dtype=F32) / n_freq)
    ang = jnp.concatenate([row[:, None] * inv_freq, col[:, None] * inv_freq], axis=-1)
    cos = jnp.repeat(jnp.cos(ang), 2, axis=-1)
    sin = jnp.repeat(jnp.sin(ang), 2, axis=-1)
    sign = jnp.tile(jnp.array([-1.0, 1.0], F32), HEAD_DIM // 2)
    return jnp.tile(cos, (1, 4)), jnp.tile(sin * sign, (1, 4))


def _head_gains(qn_a, kn_a, qn_b, kn_b):
    scale = HEAD_DIM ** -0.5
    ones = jnp.ones((N_KV * HEAD_DIM,), F32)
    return jnp.concatenate([jnp.tile(qn_a, N_HEADS) * scale, jnp.tile(kn_a, N_KV), ones,
                            jnp.tile(qn_b, N_HEADS) * scale, jnp.tile(kn_b, N_KV), ones]).reshape(1, QKV_COLS)


def kernel(x_prompt, x_sample, c, cache_k_a_l0, cache_v_a_l0, cache_k_b_l0, cache_v_b_l0, state_ssm_re_l1, state_ssm_im_l1, c_ctx, mod_w_l0, mod_b_l0, norm_mix_l0, attn_w_in_l0, q_norm_a_l0, k_norm_a_l0, q_norm_b_l0, k_norm_b_l0, sink_b_l0, attn_w_out_l0, norm_ffn_l0, router_l0, moe_w_gate_l0, moe_w_up_l0, moe_w_down_l0, mod_w_l1, mod_b_l1, norm_mix_l1, ssm_w_in_l1, ssm_lambda_re_l1, ssm_lambda_im_l1, ssm_b_re_l1, ssm_b_im_l1, ssm_c_re_l1, ssm_c_im_l1, ssm_log_dt_l1, ssm_d_l1, ssm_w_out_l1, norm_ffn_l1, router_l1, moe_w_gate_l1, moe_w_up_l1, moe_w_down_l1):
    bp, sp, d = x_prompt.shape
    bs, ss, _ = x_sample.shape
    past = cache_k_a_l0.shape[1]
    assert d == D_MODEL and bs <= 7 and (bp * sp) % TM == 0 and TM % sp == 0 and ss % TM == 0
    xp = x_prompt.reshape(bp * sp, d)
    xs = x_sample.reshape(bs * ss, d)
    cond8 = jnp.concatenate([c_ctx[None], c, jnp.zeros((7 - bs, d), F32)], axis=0)
    row1 = lambda v: v.reshape(1, -1)

    mod0 = _mod_rows(cond8, mod_w_l0, mod_b_l0)
    w_in = attn_w_in_l0.astype(BF16)
    hgain = _head_gains(q_norm_a_l0, k_norm_a_l0, q_norm_b_l0, k_norm_b_l0)
    lane = np.arange(256)
    bd = jnp.asarray((lane[:, None] // HEAD_DIM == lane[None, :] // HEAD_DIM) / HEAD_DIM, BF16)
    cos_t, sin_t = _rope_tables(ss)
    qp, kap, vap, kbp, vbp = _qkv(xp, mod0, row1(norm_mix_l0), w_in, hgain, bd, cos_t, sin_t,
                                  rows_per_cond=None, seq=sp, rope=False, kv_dtype=F32, transposed_kv=True)
    qs, kas, vas, kbs, vbs = _qkv(xs, mod0, row1(norm_mix_l0), w_in, hgain, bd, cos_t, sin_t,
                                  rows_per_cond=ss, seq=ss, rope=True, kv_dtype=BF16, transposed_kv=False)
    op = _attn_ctx(sink_b_l0, qp, kap, vap, kbp, vbp, sp)
    cache = lambda t: t.reshape(bs, past, N_KV * HEAD_DIM)
    os_ = _attn_lat(sink_b_l0, qs, kas, vas, kbs, vbs, cache(cache_k_a_l0), cache(cache_v_a_l0),
                    cache(cache_k_b_l0), cache(cache_v_b_l0), ss)
    w_out = attn_w_out_l0.astype(BF16)
    x1p, hp, affp = _postmix(op, xp, mod0, w_out, row1(norm_ffn_l0), router_l0.T, rows_per_cond=None, project=True)
    x1s, hs, affs = _postmix(os_, xs, mod0, w_out, row1(norm_ffn_l0), router_l0.T, rows_per_cond=ss, project=True)
    xp, xs = _moe_pair(hp, affp, x1p, hs, affs, x1s, mod0, moe_w_gate_l0, moe_w_up_l0, moe_w_down_l0, sp, ss)

    mod1 = _mod_rows(cond8, mod_w_l1, mod_b_l1)
    ops = _ssm_operand_params(ssm_lambda_re_l1, ssm_lambda_im_l1, ssm_b_re_l1, ssm_b_im_l1, ssm_c_re_l1, ssm_c_im_l1,
                              ssm_log_dt_l1, ssm_d_l1)
    wt = ssm_w_in_l1.T.astype(BF16)
    w_so = ssm_w_out_l1.astype(BF16)
    h0 = jnp.stack([state_ssm_re_l1[:, 0], state_ssm_im_l1[:, 0], state_ssm_re_l1[:, 1], state_ssm_im_l1[:, 1]],
                   axis=0).transpose(2, 0, 3, 1)
    mp, ms, fsp = _ssm_mixers(xp, xs, mod1, row1(norm_mix_l1), wt, ops, w_so, h0, nbp=bp, sp=sp, nbs=bs, ss=ss)
    dummy_w = jnp.zeros((8, 128), BF16)
    x1p, hp, affp = _postmix(mp, xp, mod1, dummy_w, row1(norm_ffn_l1), router_l1.T, rows_per_cond=None, project=False)
    x1s, hs, affs = _postmix(ms, xs, mod1, dummy_w, row1(norm_ffn_l1), router_l1.T, rows_per_cond=ss, project=False)
    xp, xs = _moe_pair(hp, affp, x1p, hs, affs, x1s, mod1, moe_w_gate_l1, moe_w_up_l1, moe_w_down_l1, sp, ss)

    kv_out = lambda t: t.reshape(bp, N_KV, HEAD_DIM, sp).transpose(0, 3, 1, 2)
    fin = fsp.transpose(3, 1, 0, 2)
    ssm_re = jnp.stack([fin[:, 0], fin[:, 2]], axis=1)
    ssm_im = jnp.stack([fin[:, 1], fin[:, 3]], axis=1)
    return (xp.reshape(bp, sp, d), xs.reshape(bs, ss, d), kv_out(kap), kv_out(vap), kv_out(kbp), kv_out(vbp),
            ssm_re, ssm_im)
```

```python
import functools
import itertools

import jax
import jax.numpy as jnp
import numpy as np
from jax import lax
from jax.experimental import pallas as pl
from jax.experimental.pallas import tpu as pltpu

F32, BF16, I32 = jnp.float32, jnp.bfloat16, jnp.int32

D_MODEL = 1024
GRID_W = 64
HEAD_DIM = 64
N_HEADS = 8
N_KV = 2
WINDOW = 128
ROPE_THETA = 10000.0
SSM_GROUP = 16
SSM_GROUPS = D_MODEL // SSM_GROUP
SSM_STATE = 64
N_EXPERTS = 16
EC_FACTOR = 2
D_FF = 2 * D_MODEL
EPS = 1e-6
NEG_INF = -1e30
QKV_COLS = 2 * (N_HEADS + 2 * N_KV) * HEAD_DIM
ATTN_OUT = 2 * N_HEADS * HEAD_DIM

SSM_CHUNK = 16
SSM_ROWS = SSM_CHUNK * SSM_GROUP
SSM_GB = 4
SSM_PASSES = 1

SSM_TC = 512

TM = 512

MOE_TT = 256
MOE_W = 64
MOE_EG = 4
VMEM_LIMIT = 56 * 1024 * 1024

NN = (((1,), (0,)), ((), ()))
NT = (((1,), (1,)), ((), ()))


def _dot(a, b, dims=NN):
    return lax.dot_general(a, b, dims, preferred_element_type=F32)


def _split2(x):
    hi = x.astype(BF16)
    lo = (x - hi.astype(F32)).astype(BF16)
    return hi, lo


def _split3(x):
    hi = x.astype(BF16)
    r = x - hi.astype(F32)
    mid = r.astype(BF16)
    lo = (r - mid.astype(F32)).astype(BF16)
    return hi, mid, lo


def _dot3(a, b, dims=NN):
    ah, al = _split2(a)
    bh, bl = _split2(b)
    return _dot(ah, bh, dims) + (_dot(ah, bl, dims) + _dot(al, bh, dims))


def _mm(a, b, passes):
    if passes == 1:
        return _dot(a.astype(BF16), b.astype(BF16))
    return _dot3(a, b)


def _dot_sel(x, sel):
    hi, mid, lo = _split3(x)
    return _dot(hi, sel) + (_dot(mid, sel) + _dot(lo, sel))


def _sigmoid(x):
    return 1.0 / (1.0 + jnp.exp(-x))


def _norm_mod(x, gain, shift, scale):
    ms = jnp.mean(x * x, axis=-1, keepdims=True)
    y = x * lax.rsqrt(ms + EPS) * gain
    return y * (1.0 + scale) + shift


def _params(*sem):
    return pltpu.CompilerParams(dimension_semantics=sem, vmem_limit_bytes=VMEM_LIMIT)


def _adaln_kernel(c_ref, w_ref, b_ref, o_ref):
    c = c_ref[...]
    s = c * _sigmoid(c)
    o_ref[...] = _dot3(s, w_ref[...]) + b_ref[...]


def _adaln(cond8, w_mod, b_mod):
    d, e = w_mod.shape
    tn = 1536
    return pl.pallas_call(
        _adaln_kernel,
        grid=(e // tn,),
        in_specs=[pl.BlockSpec((8, d), lambda j: (0, 0)),
                  pl.BlockSpec((d, tn), lambda j: (0, j)),
                  pl.BlockSpec((1, tn), lambda j: (0, j))],
        out_specs=pl.BlockSpec((8, tn), lambda j: (0, j)),
        out_shape=jax.ShapeDtypeStruct((8, e), F32),
        compiler_params=_params("parallel"),
        name="adaln",
    )(cond8, w_mod, b_mod.reshape(1, e))


def _mod_rows(cond8, w_mod, b_mod):
    m = _adaln(cond8, w_mod, b_mod).reshape(8, 6, D_MODEL)
    return jnp.pad(m, ((0, 0), (0, 2), (0, 0)))


def _mod_spec(rows_per_cond):
    if rows_per_cond is None:
        return pl.BlockSpec((1, 8, D_MODEL), lambda i: (0, 0, 0))
    return pl.BlockSpec((1, 8, D_MODEL), lambda i: (1 + (i * TM) // rows_per_cond, 0, 0))


def _qkv_kernel(x_ref, mod_ref, gain_ref, w_ref, hg_ref, bd_ref, cos_ref, sin_ref,
                q_ref, ka_ref, va_ref, kb_ref, vb_ref, *, rope, transposed_kv):
    h = _norm_mod(x_ref[...], gain_ref[...], mod_ref[0, 0:1, :], mod_ref[0, 1:2, :])
    proj = _dot(h.astype(BF16), w_ref[...])
    bd = bd_ref[...]

    def head_norm(blk, g):
        hi, lo = _split2(blk * blk)
        ms = _dot(hi, bd) + _dot(lo, bd)
        return blk * lax.rsqrt(ms + EPS) * g

    def rotary(blk):
        w = blk.shape[1]
        even = (lax.broadcasted_iota(I32, blk.shape, 1) & 1) == 0
        swapped = jnp.where(even, pltpu.roll(blk, w - 1, 1), pltpu.roll(blk, 1, 1))
        return blk * cos_ref[:, :w] + swapped * sin_ref[:, :w]

    def qk(c0):
        blk = head_norm(proj[:, c0:c0 + 256], hg_ref[:, c0:c0 + 256])
        return rotary(blk) if rope else blk

    q_ref[:, 0:256] = qk(0).astype(q_ref.dtype)
    q_ref[:, 256:512] = qk(256).astype(q_ref.dtype)
    q_ref[:, 512:768] = qk(768).astype(q_ref.dtype)
    q_ref[:, 768:1024] = qk(1024).astype(q_ref.dtype)
    kva = qk(512)
    kvb = qk(1280)
    outs = ((ka_ref, kva[:, :128]), (va_ref, proj[:, 640:768]), (kb_ref, kvb[:, :128]), (vb_ref, proj[:, 1408:1536]))
    for ref, val in outs:
        if transposed_kv:
            seq = ref.shape[2]
            for r in range(ref.shape[0]):
                ref[r] = val[r * seq:(r + 1) * seq].T.astype(ref.dtype)
        else:
            ref[...] = val.astype(ref.dtype)


def _qkv(x, mod, gain, w_bf, hgain, bd, cos_t, sin_t, *, rows_per_cond, seq, rope, kv_dtype, transposed_kv):
    n = x.shape[0]
    tiles_per_seq = max(1, seq // TM)
    row = lambda i: (i, 0)
    const = lambda i: (0, 0)
    pos = lambda i: (i % tiles_per_seq, 0)
    if transposed_kv:
        assert TM % seq == 0
        kv_shape = jax.ShapeDtypeStruct((n // seq, 128, seq), kv_dtype)
        kv_spec = pl.BlockSpec((TM // seq, 128, seq), lambda i: (i, 0, 0))
    else:
        kv_shape = jax.ShapeDtypeStruct((n, 128), kv_dtype)
        kv_spec = pl.BlockSpec((TM, 128), row)
    return pl.pallas_call(
        functools.partial(_qkv_kernel, rope=rope, transposed_kv=transposed_kv),
        grid=(n // TM,),
        in_specs=[pl.BlockSpec((TM, D_MODEL), row),
                  _mod_spec(rows_per_cond),
                  pl.BlockSpec((1, D_MODEL), const),
                  pl.BlockSpec((D_MODEL, QKV_COLS), const),
                  pl.BlockSpec((1, QKV_COLS), const),
                  pl.BlockSpec((256, 256), const),
                  pl.BlockSpec((TM, 256), pos),
                  pl.BlockSpec((TM, 256), pos)],
        out_specs=[pl.BlockSpec((TM, ATTN_OUT), row)] + [kv_spec] * 4,
        out_shape=[jax.ShapeDtypeStruct((n, ATTN_OUT), BF16)] + [kv_shape] * 4,
        compiler_params=_params("parallel"),
        name="qkv_rope" if rope else "qkv",
    )(x, mod, gain, w_bf, hgain, bd, cos_t, sin_t)


def _pad_variants(kk):
    left = lax.broadcasted_iota(I32, kk.shape, 1) < HEAD_DIM
    rolled = pltpu.roll(kk, HEAD_DIM, 1)
    zero = jnp.zeros_like(kk)
    return {(0, 0): jnp.where(left, kk, zero).astype(BF16),
            (0, 1): jnp.where(left, zero, rolled).astype(BF16),
            (1, 0): jnp.where(left, rolled, zero).astype(BF16),
            (1, 1): jnp.where(left, zero, kk).astype(BF16)}


def _head_attention(qp, keys, vals, masks, sink, transposed=False):
    kdims, vdims = (NN, NT) if transposed else (NT, NN)
    scores = []
    for kblk, mask in zip(keys, masks):
        s = _dot(qp, kblk, kdims)
        if mask is not None:
            s = jnp.where(mask, s, NEG_INF)
        scores.append(s)
    m = scores[0].max(axis=-1, keepdims=True)
    for s in scores[1:]:
        m = jnp.maximum(m, s.max(axis=-1, keepdims=True))
    if sink is not None:
        m = jnp.maximum(m, sink)
    den = None
    out = None
    for s, vblk in zip(scores, vals):
        p = jnp.exp(s - m)
        ps = p.sum(axis=-1, keepdims=True)
        den = ps if den is None else den + ps
        o = _dot(p.astype(BF16), vblk, vdims)
        out = o if out is None else out + o
    if sink is not None:
        den = den + jnp.exp(sink - m)
    return out / den


def _pad_variants_t(kt):
    top = lax.broadcasted_iota(I32, kt.shape, 0) < HEAD_DIM
    zero = jnp.zeros((HEAD_DIM, kt.shape[1]), F32)
    return {(0, 0): jnp.where(top, kt, 0.0).astype(BF16),
            (0, 1): jnp.concatenate([zero, kt[:HEAD_DIM]], axis=0).astype(BF16),
            (1, 0): jnp.concatenate([kt[HEAD_DIM:], zero], axis=0).astype(BF16),
            (1, 1): jnp.where(top, 0.0, kt).astype(BF16)}


def _attn_ctx_kernel(sink_ref, q_ref, ka_ref, va_ref, kb_ref, vb_ref, o_ref):
    for mixer, (k_ref, v_ref) in enumerate(((ka_ref, va_ref), (kb_ref, vb_ref))):
        kvar = _pad_variants_t(k_ref[0])
        vvar = _pad_variants_t(v_ref[0])
        for t in range(4):
            tile = mixer * 4 + t
            kv = t // 2
            qp = q_ref[:, tile * 128:(tile + 1) * 128]
            acc = None
            for par in range(2):
                sink = sink_ref[2 * t + par] if mixer == 1 else None
                o = _head_attention(qp, [kvar[(kv, par)]], [vvar[(kv, par)]], [None], sink, transposed=True)
                acc = o if acc is None else acc + o
            o_ref[:, tile * 128:(tile + 1) * 128] = acc.astype(o_ref.dtype)


def _attn_ctx(sink, q, ka, va, kb, vb, seq):
    n = q.shape[0]
    row = lambda b: (b, 0)
    kv_spec = pl.BlockSpec((1, 128, seq), lambda b: (b, 0, 0))
    return pl.pallas_call(
        _attn_ctx_kernel,
        grid=(n // seq,),
        in_specs=[pl.BlockSpec(memory_space=pltpu.SMEM),
                  pl.BlockSpec((seq, ATTN_OUT), row), kv_spec, kv_spec, kv_spec, kv_spec],
        out_specs=pl.BlockSpec((seq, ATTN_OUT), row),
        out_shape=jax.ShapeDtypeStruct((n, ATTN_OUT), BF16),
        compiler_params=_params("parallel"),
        name="attn_ctx",
    )(sink, q, ka, va, kb, vb)


def _attn_lat_kernel(sink_ref, q_ref, ka_ref, va_ref, kb_ref, vb_ref,
                     cka_ref, cva_ref, ckb_ref, cvb_ref, o_ref, *, tq, seq):
    qi = pl.program_id(1)
    span = tq + 2 * WINDOW
    ck = _pad_variants(cka_ref[0])
    cv = _pad_variants(cva_ref[0])
    lk = _pad_variants(ka_ref[...].astype(F32))
    lv = _pad_variants(va_ref[...].astype(F32))
    for t in range(4):
        kv = t // 2
        qp = q_ref[:, t * 128:(t + 1) * 128]
        acc = None
        for par in range(2):
            o = _head_attention(qp, [ck[(kv, par)], lk[(kv, par)]], [cv[(kv, par)], lv[(kv, par)]],
                                [None, None], None)
            acc = o if acc is None else acc + o
        o_ref[:, t * 128:(t + 1) * 128] = acc.astype(o_ref.dtype)
    lo = jnp.clip(qi * tq - WINDOW, 0, seq - span)
    lo = pl.multiple_of(lo, 128)
    qpos = qi * tq + lax.broadcasted_iota(I32, (tq, span), 0)
    kpos = lo + lax.broadcasted_iota(I32, (tq, span), 1)
    band = jnp.abs(qpos - kpos) <= WINDOW
    ck = _pad_variants(ckb_ref[0])
    cv = _pad_variants(cvb_ref[0])
    lk = _pad_variants(kb_ref[pl.ds(lo, span), :].astype(F32))
    lv = _pad_variants(vb_ref[pl.ds(lo, span), :].astype(F32))
    for t in range(4):
        kv = t // 2
        tile = 4 + t
        qp = q_ref[:, tile * 128:(tile + 1) * 128]
        acc = None
        for par in range(2):
            sink = sink_ref[2 * t + par]
            o = _head_attention(qp, [lk[(kv, par)], ck[(kv, par)]], [lv[(kv, par)], cv[(kv, par)]],
                                [band, None], sink)
            acc = o if acc is None else acc + o
        o_ref[:, tile * 128:(tile + 1) * 128] = acc.astype(o_ref.dtype)


def _attn_lat(sink, q, ka, va, kb, vb, cka, cva, ckb, cvb, seq, tq=256):
    n = q.shape[0]
    nb = n // seq
    nq = seq // tq
    qrow = lambda b, i: (b * nq + i, 0)
    brow = lambda b, i: (b, 0)
    kv_spec = pl.BlockSpec((seq, 128), brow)
    past = cka.shape[1]
    c_spec = pl.BlockSpec((1, past, 128), lambda b, i: (b, 0, 0))
    return pl.pallas_call(
        functools.partial(_attn_lat_kernel, tq=tq, seq=seq),
        grid=(nb, nq),
        in_specs=[pl.BlockSpec(memory_space=pltpu.SMEM),
                  pl.BlockSpec((tq, ATTN_OUT), qrow), kv_spec, kv_spec, kv_spec, kv_spec,
                  c_spec, c_spec, c_spec, c_spec],
        out_specs=pl.BlockSpec((tq, ATTN_OUT), qrow),
        out_shape=jax.ShapeDtypeStruct((n, ATTN_OUT), BF16),
        compiler_params=_params("parallel", "parallel"),
        name="attn_lat",
    )(sink, q, ka, va, kb, vb, cka, cva, ckb, cvb)


def _postmix_kernel(m_ref, x_ref, mod_ref, w_ref, gain_ref, rt_ref, x1_ref, h2_ref, aff_ref, *, project):
    if project:
        m = _dot(m_ref[...], w_ref[...])
    else:
        m = m_ref[...]
    x1 = x_ref[...] + mod_ref[0, 2:3, :] * m
    x1_ref[...] = x1
    h2 = _norm_mod(x1, gain_ref[...], mod_ref[0, 3:4, :], mod_ref[0, 4:5, :])
    h2_ref[...] = h2.astype(h2_ref.dtype)
    logits = _dot3(rt_ref[...], h2, NT)
    e = jnp.exp(logits - logits.max(axis=0, keepdims=True))
    aff_ref[...] = e / e.sum(axis=0, keepdims=True)


def _postmix(m, x, mod, w_bf, gain, router_t, *, rows_per_cond, project):
    n = x.shape[0]
    row = lambda i: (i, 0)
    const = lambda i: (0, 0)
    return pl.pallas_call(
        functools.partial(_postmix_kernel, project=project),
        grid=(n // TM,),
        in_specs=[pl.BlockSpec((TM, D_MODEL), row),
                  pl.BlockSpec((TM, D_MODEL), row),
                  _mod_spec(rows_per_cond),
                  pl.BlockSpec(w_bf.shape, const),
                  pl.BlockSpec((1, D_MODEL), const),
                  pl.BlockSpec((N_EXPERTS, D_MODEL), const)],
        out_specs=[pl.BlockSpec((TM, D_MODEL), row),
                   pl.BlockSpec((TM, D_MODEL), row),
                   pl.BlockSpec((N_EXPERTS, TM), lambda i: (0, i))],
        out_shape=[jax.ShapeDtypeStruct((n, D_MODEL), F32),
                   jax.ShapeDtypeStruct((n, D_MODEL), BF16),
                   jax.ShapeDtypeStruct((N_EXPERTS, n), F32)],
        compiler_params=_params("parallel"),
        name="postmix_proj" if project else "postmix",
    )(m, x, mod, w_bf, gain, router_t)


def _route_kernel(aff_ref, slot_ref, gate_ref, pos_ref, tcnt_ref, *, seq, cap, nseg, tt):
    aff = jnp.concatenate([aff_ref[:, s * seq:(s + 1) * seq] for s in range(nseg)], axis=0)
    rows = aff.shape[0]
    capf = jnp.float32(cap)
    thr_bits = jnp.zeros((rows, 1), I32)
    for bit in range(30, -1, -1):
        cand = thr_bits | (1 << bit)
        cnt = jnp.where(aff >= pltpu.bitcast(cand, F32), 1.0, 0.0).sum(axis=1, keepdims=True)
        thr_bits = jnp.where(cnt >= capf, cand, thr_bits)
    thr = pltpu.bitcast(thr_bits, F32)
    gt = aff > thr
    eq = aff == thr
    n_gt = jnp.where(gt, 1.0, 0.0).sum(axis=1, keepdims=True)
    pw = min(seq, 256)
    tri = jnp.where(lax.broadcasted_iota(I32, (pw, pw), 0) < lax.broadcasted_iota(I32, (pw, pw), 1),
                    1.0, 0.0).astype(BF16)

    def count_before(flag):
        ones = jnp.where(flag, 1.0, 0.0)
        parts = []
        run = jnp.zeros((rows, 1), F32)
        for c0 in range(0, seq, pw):
            blk = ones[:, c0:c0 + pw]
            parts.append(_dot(blk.astype(BF16), tri) + run)
            run = run + blk.sum(axis=1, keepdims=True)
        return jnp.concatenate(parts, axis=1) if len(parts) > 1 else parts[0]

    sel = gt | (eq & (count_before(eq) < capf - n_gt))
    rank = count_before(sel)
    expert = lax.broadcasted_iota(I32, (rows, seq), 0) & (N_EXPERTS - 1)
    slot = jnp.where(sel, expert * cap + rank.astype(I32), -1)
    gate = jnp.where(sel, aff, 0.0)
    pos = jnp.where(sel, rank, -1.0)
    nt = seq // tt
    tile_of = jnp.where((lax.broadcasted_iota(I32, (seq, nt), 0) // tt) == lax.broadcasted_iota(I32, (seq, nt), 1),
                        1.0, 0.0).astype(BF16)
    tcnt = _dot(jnp.where(sel, 1.0, 0.0).astype(BF16), tile_of)
    for s in range(nseg):
        rows_s = slice(s * N_EXPERTS, (s + 1) * N_EXPERTS)
        slot_ref[:, s * seq:(s + 1) * seq] = slot[rows_s, :]
        gate_ref[:, s * seq:(s + 1) * seq] = gate[rows_s, :]
        pos_ref[:, s * seq:(s + 1) * seq] = pos[rows_s, :]
        tcnt_ref[0, :, s * nt:(s + 1) * nt] = tcnt[rows_s, :]


def _route(aff_t, seq, cap, nseg, tt):
    n = aff_t.shape[1]
    nt = seq // tt
    steps = n // (nseg * seq)
    spec = pl.BlockSpec((N_EXPERTS, nseg * seq), lambda i: (0, i))
    slot, gate, pos, tcnt = pl.pallas_call(
        functools.partial(_route_kernel, seq=seq, cap=cap, nseg=nseg, tt=tt),
        grid=(steps,),
        in_specs=[spec],
        out_specs=[spec, spec, spec, pl.BlockSpec((1, N_EXPERTS, nseg * nt), lambda i: (i, 0, 0))],
        out_shape=[jax.ShapeDtypeStruct((N_EXPERTS, n), I32), jax.ShapeDtypeStruct((N_EXPERTS, n), F32),
                   jax.ShapeDtypeStruct((N_EXPERTS, n), F32),
                   jax.ShapeDtypeStruct((steps, N_EXPERTS, nseg * nt), F32)],
        compiler_params=_params("parallel"),
        name="route",
    )(aff_t)
    return slot, gate, pos, tcnt.transpose(1, 0, 2).reshape(N_EXPERTS, n // tt)


def _dispatch_kernel(slot_ref, h_ref, x_ref, *, cap):
    seq = h_ref.shape[0]
    m = N_EXPERTS * cap
    owner = jnp.broadcast_to(slot_ref[...][:, None, :], (N_EXPERTS, cap, seq)).reshape(m, seq)
    hit = owner == lax.broadcasted_iota(I32, (m, seq), 0)
    x_ref[...] = _dot(jnp.where(hit, 1.0, 0.0).astype(BF16), h_ref[...]).astype(x_ref.dtype)


def _dispatch(slot, h, seq, cap):
    n = h.shape[0]
    nb = n // seq
    return pl.pallas_call(
        functools.partial(_dispatch_kernel, cap=cap),
        grid=(nb,),
        in_specs=[pl.BlockSpec((N_EXPERTS, seq), lambda b: (0, b)),
                  pl.BlockSpec((seq, D_MODEL), lambda b: (b, 0))],
        out_specs=pl.BlockSpec((N_EXPERTS * cap, D_MODEL), lambda b: (b, 0)),
        out_shape=jax.ShapeDtypeStruct((nb * N_EXPERTS * cap, D_MODEL), BF16),
        compiler_params=_params("parallel"),
        name="moe_dispatch",
    )(slot, h)


def _window(cum_ref, base, e, cap, k=0):
    lo = ((cum_ref[base + e] >> 4) << 4) + k * MOE_W
    return lo, pl.multiple_of(jnp.minimum(lo, cap - MOE_W), 16)


def _extra_windows(cum_ref, base, e):
    lo = (cum_ref[base + e] >> 4) << 4
    return (cum_ref[base + N_EXPERTS + e] - lo + (MOE_W - 1)) >> 6


def _dispatch_win_kernel(cum_ref, slot_ref, h_ref, x_ref, *, cap, nt):
    b, i = pl.program_id(0), pl.program_id(1)
    tt = h_ref.shape[0]
    base = (b * (nt + 1) + i) * N_EXPERTS
    h = h_ref[...]
    row = lax.broadcasted_iota(I32, (MOE_W, tt), 0)

    @pl.when(i == 0)
    def _():
        x_ref[...] = jnp.zeros_like(x_ref)

    def hits(e, lo, ws):
        srow = slot_ref[e:e + 1, :]
        return (srow == row + (e * cap + ws)) & (srow >= e * cap + lo)

    for grp in range(N_EXPERTS // MOE_EG):
        wins = [(e,) + _window(cum_ref, base, e, cap) for e in range(grp * MOE_EG, (grp + 1) * MOE_EG)]
        sel = jnp.concatenate([hits(e, lo, ws) for e, lo, ws in wins], axis=0)
        x = _dot(jnp.where(sel, 1.0, 0.0).astype(BF16), h)
        for q, (e, lo, ws) in enumerate(wins):
            dst = pl.ds(e * cap + ws, MOE_W)
            x_ref[dst, :] += x[q * MOE_W:(q + 1) * MOE_W].astype(x_ref.dtype)

    for e in range(N_EXPERTS):
        def extra(k, carry, e=e):
            lo, ws = _window(cum_ref, base, e, cap, k)
            x = _dot(jnp.where(hits(e, lo, ws), 1.0, 0.0).astype(BF16), h)
            x_ref[pl.ds(e * cap + ws, MOE_W), :] += x.astype(x_ref.dtype)
            return carry
        lax.fori_loop(1, _extra_windows(cum_ref, base, e), extra, 0)


def _dispatch_win(cum, slot, h, seq, cap):
    n = h.shape[0]
    nb, nt = n // seq, seq // MOE_TT
    return pl.pallas_call(
        functools.partial(_dispatch_win_kernel, cap=cap, nt=nt),
        grid_spec=pltpu.PrefetchScalarGridSpec(
            num_scalar_prefetch=1,
            grid=(nb, nt),
            in_specs=[pl.BlockSpec((N_EXPERTS, MOE_TT), lambda b, i, c: (0, b * nt + i)),
                      pl.BlockSpec((MOE_TT, D_MODEL), lambda b, i, c: (b * nt + i, 0))],
            out_specs=pl.BlockSpec((N_EXPERTS * cap, D_MODEL), lambda b, i, c: (b, 0))),
        out_shape=jax.ShapeDtypeStruct((nb * N_EXPERTS * cap, D_MODEL), BF16),
        compiler_params=_params("parallel", "arbitrary"),
        name="moe_dispatch_win",
    )(cum, slot, h)


FFN_TF = 512
FFN_RC = 512


def _ffn_kernel(xa_ref, xb_ref, wg_ref, wu_ref, wd_ref, ya_ref, yb_ref, acc_ref):
    j = pl.program_id(1)
    wg = wg_ref[0].astype(BF16)
    wu = wu_ref[0].astype(BF16)
    wd = wd_ref[0].astype(BF16)
    ra = xa_ref.shape[0] * xa_ref.shape[2]

    def row_chunks(ref, base):
        nb, _, cap, d = ref.shape
        rc = min(FFN_RC, nb * cap)
        for r0 in range(0, nb * cap, rc):
            if cap >= rc:
                b, c0 = divmod(r0, cap)
                yield base + r0, ref[b, 0, c0:c0 + rc, :]
            else:
                yield base + r0, ref[r0 // cap:(r0 + rc) // cap, 0, :, :].reshape(rc, d)

    for r0, x in itertools.chain(row_chunks(xa_ref, 0), row_chunks(xb_ref, ra)):
        rc = x.shape[0]
        g = _dot(x, wg)
        u = _dot(x, wu)
        mid = (g * _sigmoid(g) * u).astype(BF16)
        y = _dot(mid, wd)

        @pl.when(j == 0)
        def _():
            acc_ref[r0:r0 + rc, :] = y

        @pl.when(j > 0)
        def _():
            acc_ref[r0:r0 + rc, :] += y

    @pl.when(j == pl.num_programs(1) - 1)
    def _():
        for ref, base in ((ya_ref, 0), (yb_ref, ra)):
            nb, _, cap, d = ref.shape
            ref[...] = acc_ref[base:base + nb * cap, :].reshape(nb, 1, cap, d).astype(ref.dtype)


def _ffn(xa, xb, w_gate, w_up, w_down):
    ba, _, ca, d = xa.shape
    bb, _, cb, _ = xb.shape
    nj = D_FF // FFN_TF
    xa_spec = pl.BlockSpec((ba, 1, ca, d), lambda e, j: (0, e, 0, 0))
    xb_spec = pl.BlockSpec((bb, 1, cb, d), lambda e, j: (0, e, 0, 0))
    return pl.pallas_call(
        _ffn_kernel,
        grid=(N_EXPERTS, nj),
        in_specs=[xa_spec, xb_spec,
                  pl.BlockSpec((1, d, FFN_TF), lambda e, j: (e, 0, j)),
                  pl.BlockSpec((1, d, FFN_TF), lambda e, j: (e, 0, j)),
                  pl.BlockSpec((1, FFN_TF, d), lambda e, j: (e, j, 0))],
        out_specs=[xa_spec, xb_spec],
        out_shape=[jax.ShapeDtypeStruct(xa.shape, BF16), jax.ShapeDtypeStruct(xb.shape, BF16)],
        scratch_shapes=[pltpu.VMEM((ba * ca + bb * cb, d), F32)],
        compiler_params=_params("parallel", "arbitrary"),
        name="moe_ffn",
    )(xa, xb, w_gate, w_up, w_down)


def _expand(vals_bf, first_expert, width, total):
    e_of_lane = first_expert + lax.broadcasted_iota(I32, (N_EXPERTS, total), 1) // width
    pick = jnp.where(lax.broadcasted_iota(I32, (N_EXPERTS, total), 0) == e_of_lane, 1.0, 0.0).astype(BF16)
    return _dot(vals_bf, pick)


def _combine_kernel(pos_ref, gate_ref, y_ref, x_ref, mod_ref, o_ref, *, cap):
    tt = x_ref.shape[0]
    m = N_EXPERTS * cap
    pos = _expand(pos_ref[...].astype(BF16), 0, cap, m)
    gate = _expand(gate_ref[...].astype(BF16), 0, cap, m)
    rank = (lax.broadcasted_iota(I32, (tt, m), 1) % cap).astype(F32)
    w = jnp.where(pos == rank, gate, 0.0).astype(BF16)
    o_ref[...] = x_ref[...] + mod_ref[0, 5:6, :] * _dot(w, y_ref[...])


def _combine(pos_t, gate_t, y, x, mod, *, seq, cap):
    n = x.shape[0]
    row = lambda b: (b, 0)
    return pl.pallas_call(
        functools.partial(_combine_kernel, cap=cap),
        grid=(n // seq,),
        in_specs=[pl.BlockSpec((seq, N_EXPERTS), row),
                  pl.BlockSpec((seq, N_EXPERTS), row),
                  pl.BlockSpec((N_EXPERTS * cap, D_MODEL), row),
                  pl.BlockSpec((seq, D_MODEL), row),
                  pl.BlockSpec((1, 8, D_MODEL), lambda b: (0, 0, 0))],
        out_specs=pl.BlockSpec((seq, D_MODEL), row),
        out_shape=jax.ShapeDtypeStruct((n, D_MODEL), F32),
        compiler_params=_params("parallel"),
        name="moe_combine",
    )(pos_t, gate_t, y, x, mod)


def _combine_win_kernel(cum_ref, pos_ref, gate_ref, y_ref, x_ref, mod_ref, o_ref, acc_ref, *, cap, nt):
    b, i = pl.program_id(0), pl.program_id(1)
    tt = x_ref.shape[0]
    base = (b * (nt + 1) + i) * N_EXPERTS
    width = MOE_EG * MOE_W
    posb = pos_ref[...].astype(BF16)
    gateb = gate_ref[...].astype(BF16)
    lane = lax.broadcasted_iota(I32, (1, width), 1)
    offset = (lane & (MOE_W - 1)).astype(F32)
    acc = jnp.zeros((tt, D_MODEL), F32)
    for grp in range(N_EXPERTS // MOE_EG):
        wins = [(e,) + _window(cum_ref, base, e, cap) for e in range(grp * MOE_EG, (grp + 1) * MOE_EG)]
        lo_l = jnp.zeros((1, width), F32)
        ws_l = jnp.zeros((1, width), F32)
        for q, (e, lo, ws) in enumerate(wins):
            mine = (lane >> 6) == q
            lo_l = jnp.where(mine, lo.astype(F32), lo_l)
            ws_l = jnp.where(mine, ws.astype(F32), ws_l)
        pos = _expand(posb, grp * MOE_EG, MOE_W, width)
        gate = _expand(gateb, grp * MOE_EG, MOE_W, width)
        w = jnp.where((pos - ws_l == offset) & (pos >= lo_l), gate, 0.0).astype(BF16)
        ywin = jnp.concatenate([y_ref[pl.ds(e * cap + ws, MOE_W), :] for e, lo, ws in wins], axis=0)
        acc = acc + _dot(w, ywin)
    acc_ref[...] = acc

    off64 = lax.broadcasted_iota(I32, (1, MOE_W), 1).astype(F32)
    for e in range(N_EXPERTS):
        def extra(k, carry, e=e):
            lo, ws = _window(cum_ref, base, e, cap, k)
            pos = pos_ref[:, e:e + 1]
            gate = gate_ref[:, e:e + 1].astype(BF16).astype(F32)
            w = jnp.where((pos - ws.astype(F32) == off64) & (pos >= lo.astype(F32)), gate, 0.0).astype(BF16)
            acc_ref[...] += _dot(w, y_ref[pl.ds(e * cap + ws, MOE_W), :])
            return carry
        lax.fori_loop(1, _extra_windows(cum_ref, base, e), extra, 0)

    o_ref[...] = x_ref[...] + mod_ref[0, 5:6, :] * acc_ref[...]


def _combine_win(cum, pos_t, gate_t, y, x, mod, *, seq, cap):
    n = x.shape[0]
    nb, nt = n // seq, seq // MOE_TT
    row = lambda b, i, c: (b * nt + i, 0)
    return pl.pallas_call(
        functools.partial(_combine_win_kernel, cap=cap, nt=nt),
        grid_spec=pltpu.PrefetchScalarGridSpec(
            num_scalar_prefetch=1,
            grid=(nb, nt),
            in_specs=[pl.BlockSpec((MOE_TT, N_EXPERTS), row),
                      pl.BlockSpec((MOE_TT, N_EXPERTS), row),
                      pl.BlockSpec((N_EXPERTS * cap, D_MODEL), lambda b, i, c: (b, 0)),
                      pl.BlockSpec((MOE_TT, D_MODEL), row),
                      pl.BlockSpec((1, 8, D_MODEL), lambda b, i, c: (1 + b, 0, 0))],
            out_specs=pl.BlockSpec((MOE_TT, D_MODEL), row),
            scratch_shapes=[pltpu.VMEM((MOE_TT, D_MODEL), F32)]),
        out_shape=jax.ShapeDtypeStruct((n, D_MODEL), F32),
        compiler_params=_params("parallel", "parallel"),
        name="moe_combine_win",
    )(cum, pos_t, gate_t, y, x, mod)


def _moe_pair(hp, affp, x1p, hs, affs, x1s, mod, w_gate, w_up, w_down, seq_p, seq_s):
    n_p, n_s = hp.shape[0], hs.shape[0]
    nb_p, nb_s = n_p // seq_p, n_s // seq_s
    cap_p = EC_FACTOR * seq_p // N_EXPERTS
    cap_s = EC_FACTOR * seq_s // N_EXPERTS
    assert N_EXPERTS * cap_p <= 512 and cap_s >= MOE_W and cap_s % 16 == 0 and seq_s % MOE_TT == 0
    slot_p, gate_p, pos_p, _ = _route(affp, seq_p, cap_p, nseg=min(8, nb_p), tt=seq_p)
    slot_s, gate_s, pos_s, tcnt = _route(affs, seq_s, cap_s, nseg=min(4, nb_s), tt=MOE_TT)
    nt = seq_s // MOE_TT
    counts = tcnt.T.reshape(nb_s, nt, N_EXPERTS).astype(I32)
    cum = jnp.concatenate([jnp.zeros((nb_s, 1, N_EXPERTS), I32), jnp.cumsum(counts, axis=1)], axis=1).reshape(-1)
    xp = _dispatch(slot_p, hp, seq_p, cap_p).reshape(nb_p, N_EXPERTS, cap_p, D_MODEL)
    xs = _dispatch_win(cum, slot_s, hs, seq_s, cap_s).reshape(nb_s, N_EXPERTS, cap_s, D_MODEL)
    ys, yp = _ffn(xs, xp, w_gate, w_up, w_down)
    outp = _combine(pos_p.T, gate_p.T, yp.reshape(-1, D_MODEL), x1p, mod, seq=seq_p, cap=cap_p)
    outs = _combine_win(cum, pos_s.T, gate_s.T, ys.reshape(-1, D_MODEL), x1s, mod, seq=seq_s, cap=cap_s)
    return outp, outs


def _ssm_in_kernel(x_ref, mod_ref, gain_ref, wt_ref, ut_ref, *, cols_per_cond):
    tc = x_ref.shape[1]
    if cols_per_cond is None:
        h = _norm_mod(x_ref[0], gain_ref[...], mod_ref[0, 0:1, :], mod_ref[0, 1:2, :]).astype(BF16)
    else:
        first = 1 + pl.program_id(1) * (tc // cols_per_cond)
        parts = []
        for s in range(tc // cols_per_cond):
            m = mod_ref[first + s]
            parts.append(_norm_mod(x_ref[0, s * cols_per_cond:(s + 1) * cols_per_cond, :], gain_ref[...],
                                   m[0:1, :], m[1:2, :]).astype(BF16))
        h = jnp.concatenate(parts, axis=0) if len(parts) > 1 else parts[0]
    ut_ref[0] = _dot(wt_ref[...], h, NT)


def _ssm_in(xperm, mod, gain, wt_bf, *, cols_per_cond, tc):
    l, bk, d = xperm.shape
    assert cols_per_cond is None or tc % cols_per_cond == 0
    return pl.pallas_call(
        functools.partial(_ssm_in_kernel, cols_per_cond=cols_per_cond),
        grid=(l, bk // tc),
        in_specs=[pl.BlockSpec((1, tc, d), lambda j, i: (j, i, 0)),
                  pl.BlockSpec(mod.shape, lambda j, i: (0, 0, 0)),
                  pl.BlockSpec((1, d), lambda j, i: (0, 0)),
                  pl.BlockSpec((d, d), lambda j, i: (0, 0))],
        out_specs=pl.BlockSpec((1, d, tc), lambda j, i: (j, 0, i)),
        out_shape=jax.ShapeDtypeStruct((l, d, bk), F32),
        compiler_params=_params("parallel", "parallel"),
        name="ssm_in",
    )(xperm, mod, gain, wt_bf)


def _ssm_core_kernel(utp_ref, uts_ref, lamp_ref, c_ref, bt_ref, dsk_ref, h0_ref, ytp_ref, yts_ref, fs_ref,
                     *, kp, ks, nbp, nbs):
    rows = SSM_ROWS
    p = SSM_STATE
    lc = SSM_CHUNK
    ri = lax.broadcasted_iota(I32, (rows, rows), 0)
    cj = lax.broadcasted_iota(I32, (rows, rows), 1)
    causal = (ri >> 4) >= (cj >> 4)
    anticausal = (cj >> 4) >= (ri >> 4)
    diag = ri == cj
    leftc = lax.broadcasted_iota(I32, (lc, 128), 1) < p
    nrow = lax.broadcasted_iota(I32, (lc, 128), 0).astype(F32)
    eye = lax.broadcasted_iota(I32, (p, 128), 0) == lax.broadcasted_iota(I32, (p, 128), 1)

    def cmul(ar, ai, xr, xi):
        return ar * xr - ai * xi, ar * xi + ai * xr

    def expand_rows(t):
        return jnp.broadcast_to(t[:, None, :], (lc, SSM_GROUP, 128)).reshape(rows, 128)

    def tile_rows(t):
        return jnp.broadcast_to(t[None, :, :], (lc, SSM_GROUP, 128)).reshape(rows, 128)

    def to_col(row):
        return jnp.where(eye, jnp.broadcast_to(row, (p, 128)), 0.0).sum(axis=1, keepdims=True)

    def operands(gg, d):
        lp = lamp_ref[gg, d]
        lre, lim = lp[0:1], lp[1:2]
        dt = jnp.exp(lp[2:3])
        a, th = lre * dt, lim * dt
        ang = nrow * th
        cs, sn = jnp.cos(ang), jnp.sin(ang)
        ep, em = jnp.exp(nrow * a), jnp.exp(-(nrow * a))
        pr, pi = ep * cs, ep * sn
        nr, ni = em * cs, -(em * sn)
        l1r, l1i = pr[1:2], pi[1:2]
        lmr, lmi = pr[lc - 1:lc], pi[lc - 1:lc]
        llr, lli = cmul(lmr, lmi, l1r, l1i)
        den = lre * lre + lim * lim
        cr = ((l1r - 1.0) * lre + l1i * lim) / den
        ci = (l1i * lre - (l1r - 1.0) * lim) / den
        btr, bti = bt_ref[gg, d, 0], bt_ref[gg, d, 1]
        bbr, bbi = cr * btr - ci * bti, cr * bti + ci * btr
        u1 = tile_rows(jnp.where(leftc, bbr, bbi))
        u2 = tile_rows(jnp.where(leftc, bbi, bbr))
        c1 = tile_rows(c_ref[gg, d, 0])
        c2 = tile_rows(c_ref[gg, d, 1])

        def left_form(xr, xi):
            return (c1 * expand_rows(jnp.where(leftc, xr, -xi))
                    + c2 * expand_rows(jnp.where(leftc, -xi, -xr)))

        def right_form(xr, xi):
            return u1 * expand_rows(xr) + u2 * expand_rows(jnp.where(leftc, -xi, xi))

        if d == 0:
            al = left_form(pr, pi)
            brt = right_form(nr, ni)
            rrt = right_form(*cmul(lmr, lmi, nr, ni))
            qq = left_form(*cmul(l1r, l1i, pr, pi))
            mat = jnp.where(causal, _dot3(al, brt, NT), 0.0)
        else:
            al = left_form(nr, ni)
            brt = right_form(pr, pi)
            rrt = brt
            qq = left_form(*cmul(llr, lli, nr, ni))
            mat = jnp.where(anticausal, _dot3(al, brt, NT), 0.0)
        return mat, rrt.T, qq, to_col(llr), to_col(lli)

    def scan(sr, si, lr, li, h0r, h0i, reverse, nchunk, nbatch, sel):
        bk = sr.shape[1]
        lane = lax.broadcasted_iota(I32, (p, bk), 1)
        kidx = lane & (nchunk - 1)
        edge = (nchunk - 1) if reverse else 0
        if h0r is not None:
            h0cr = jnp.zeros((p, bk), F32)
            h0ci = jnp.zeros((p, bk), F32)
            for b in range(nbatch):
                at = lane == (b * nchunk + edge)
                h0cr = jnp.where(at, h0r[:, b:b + 1], h0cr)
                h0ci = jnp.where(at, h0i[:, b:b + 1], h0ci)
            ar, ai = cmul(lr, li, h0cr, h0ci)
            er, ei = sr + ar, si + ai
        else:
            er, ei = sr, si
        ar, ai = lr, li
        s = 1
        while s < nchunk:
            if reverse:
                ok = kidx < nchunk - s
                tr, ti = pltpu.roll(er, bk - s, 1), pltpu.roll(ei, bk - s, 1)
            else:
                ok = kidx >= s
                tr, ti = pltpu.roll(er, s, 1), pltpu.roll(ei, s, 1)
            tr = jnp.where(ok, tr, 0.0)
            ti = jnp.where(ok, ti, 0.0)
            dr, di = cmul(ar, ai, tr, ti)
            er, ei = er + dr, ei + di
            ar, ai = cmul(ar, ai, ar, ai)
            s *= 2
        if reverse:
            inner = kidx < nchunk - 1
            hr, hi = pltpu.roll(er, bk - 1, 1), pltpu.roll(ei, bk - 1, 1)
        else:
            inner = kidx >= 1
            hr, hi = pltpu.roll(er, 1, 1), pltpu.roll(ei, 1, 1)
        hr = jnp.where(inner, hr, h0cr if h0r is not None else 0.0)
        hi = jnp.where(inner, hi, h0ci if h0r is not None else 0.0)
        fin = None if sel is None else (_dot_sel(er, sel), _dot_sel(ei, sel))
        return hr, hi, fin

    def final_selectors(bk, nchunk, nbatch):
        col = lax.broadcasted_iota(I32, (bk, nbatch), 0)
        bat = lax.broadcasted_iota(I32, (bk, nbatch), 1)
        last = jnp.where(col == bat * nchunk + (nchunk - 1), 1.0, 0.0).astype(BF16)
        first = jnp.where(col == bat * nchunk, 1.0, 0.0).astype(BF16)
        return last, first

    sel_last, sel_first = final_selectors(utp_ref.shape[2], kp, nbp)

    for gg in range(SSM_GB):
        mf, rf, qf, lfr, lfi = operands(gg, 0)
        mb, rb, qb, lbr, lbi = operands(gg, 1)
        skip = jnp.where(diag, jnp.broadcast_to(dsk_ref[gg], (rows, rows)), 0.0)
        stack = jnp.concatenate([mf + mb + skip, rf, rb], axis=0)
        qq = jnp.concatenate([qf, qb], axis=1)
        h0 = h0_ref[gg]
        for ut_ref, yt_ref, nchunk, nbatch, latent in ((utp_ref, ytp_ref, kp, nbp, False),
                                                       (uts_ref, yts_ref, ks, nbs, True)):
            bk = ut_ref.shape[2]
            x = ut_ref[:, gg * SSM_GROUP:(gg + 1) * SSM_GROUP, :].reshape(rows, bk)
            res = _mm(stack, x, SSM_PASSES)
            hfr, hfi, ff = scan(res[rows:rows + p], res[rows + p:rows + 2 * p], lfr, lfi,
                                h0[0] if latent else None, h0[1] if latent else None,
                                False, nchunk, nbatch, None if latent else sel_last)
            hbr, hbi, fb = scan(res[rows + 2 * p:rows + 3 * p], res[rows + 3 * p:rows + 4 * p], lbr, lbi,
                                h0[2] if latent else None, h0[3] if latent else None,
                                True, nchunk, nbatch, None if latent else sel_first)
            states = jnp.concatenate([hfr, hfi, hbr, hbi], axis=0)
            y = res[:rows] + _mm(qq, states, SSM_PASSES)
            yt_ref[:, gg * SSM_GROUP:(gg + 1) * SSM_GROUP, :] = y.reshape(lc, SSM_GROUP, bk)
            if not latent:
                fs_ref[gg, 0] = ff[0]
                fs_ref[gg, 1] = ff[1]
                fs_ref[gg, 2] = fb[0]
                fs_ref[gg, 3] = fb[1]


def _ssm_core(utp, uts, ops, h0, *, kp, ks, nbp, nbs):
    lamp, c2, bt2, dsk = ops
    l, d, bkp = utp.shape
    bks = uts.shape[2]
    g = SSM_GROUPS
    gb = SSM_GB
    lead4 = lambda i: (i, 0, 0, 0)
    lead5 = lambda i: (i, 0, 0, 0, 0)
    ut_spec = lambda bk: pl.BlockSpec((l, gb * SSM_GROUP, bk), lambda i: (0, i, 0))
    return pl.pallas_call(
        functools.partial(_ssm_core_kernel, kp=kp, ks=ks, nbp=nbp, nbs=nbs),
        grid=(g // gb,),
        in_specs=[ut_spec(bkp), ut_spec(bks),
                  pl.BlockSpec((gb, 2, 8, 128), lead4),
                  pl.BlockSpec((gb, 2, 2, SSM_GROUP, 128), lead5),
                  pl.BlockSpec((gb, 2, 2, SSM_GROUP, 128), lead5),
                  pl.BlockSpec((gb, 1, SSM_ROWS), lambda i: (i, 0, 0)),
                  pl.BlockSpec((gb, 4, SSM_STATE, nbs), lead4)],
        out_specs=[ut_spec(bkp), ut_spec(bks),
                   pl.BlockSpec((gb, 4, SSM_STATE, nbp), lead4)],
        out_shape=[jax.ShapeDtypeStruct((l, d, bkp), F32),
                   jax.ShapeDtypeStruct((l, d, bks), F32),
                   jax.ShapeDtypeStruct((g, 4, SSM_STATE, nbp), F32)],
        compiler_params=_params("parallel"),
        name="ssm_core",
    )(utp, uts, lamp, c2, bt2, dsk, h0)


def _ssm_out_kernel(yt_ref, w_ref, m_ref):
    y = yt_ref[0].T
    act = 0.5 * y * (1.0 + jnp.tanh(0.7978845608028654 * (y + 0.044715 * (y * y * y))))
    ag = _dot(act.astype(BF16), w_ref[...])
    d = m_ref.shape[2]
    m_ref[0] = ag[:, :d] * _sigmoid(ag[:, d:])


def _ssm_out(yt, w_bf, tc):
    l, d, bk = yt.shape
    return pl.pallas_call(
        _ssm_out_kernel,
        grid=(l, bk // tc),
        in_specs=[pl.BlockSpec((1, d, tc), lambda j, i: (j, 0, i)),
                  pl.BlockSpec((d, 2 * d), lambda j, i: (0, 0))],
        out_specs=pl.BlockSpec((1, tc, d), lambda j, i: (j, i, 0)),
        out_shape=jax.ShapeDtypeStruct((l, bk, d), F32),
        compiler_params=_params("parallel", "parallel"),
        name="ssm_out",
    )(yt, w_bf)


def _ssm_operand_params(lam_re, lam_im, b_re, b_im, c_re, c_im, log_dt, d_skip):
    g, l = SSM_GROUPS, SSM_CHUNK
    dup = lambda t: jnp.concatenate([t, t], axis=-1)
    lamp = jnp.stack([lam_re, lam_im, jnp.broadcast_to(log_dt[..., None], lam_re.shape)], axis=2)
    lamp = dup(jnp.pad(lamp, ((0, 0), (0, 0), (0, 5), (0, 0)))).transpose(1, 0, 2, 3)
    c2 = dup(jnp.stack([c_re, c_im], axis=2)).transpose(1, 0, 2, 3, 4)
    bt2 = dup(jnp.stack([jnp.swapaxes(b_re, -1, -2), jnp.swapaxes(b_im, -1, -2)], axis=2)).transpose(1, 0, 2, 3, 4)
    dsk = jnp.tile(d_skip.reshape(g, 1, SSM_GROUP), (1, 1, l))
    return lamp, c2, bt2, dsk


def _to_chunks(x, nb, seq):
    k = seq // SSM_CHUNK
    return x.reshape(nb, k, SSM_CHUNK, -1).transpose(2, 0, 1, 3).reshape(SSM_CHUNK, nb * k, -1)


def _from_chunks(x, nb, seq):
    k = seq // SSM_CHUNK
    return x.reshape(SSM_CHUNK, nb, k, -1).transpose(1, 2, 0, 3).reshape(nb * seq, -1)


def _ssm_mixers(xp, xs, mod, gain, wt_bf, ops, w_out_bf, h0, *, nbp, sp, nbs, ss):
    kp, ks = sp // SSM_CHUNK, ss // SSM_CHUNK
    tcp, tcs = min(SSM_TC, nbp * kp), min(SSM_TC, nbs * ks)
    utp = _ssm_in(_to_chunks(xp, nbp, sp), mod, gain, wt_bf, cols_per_cond=None, tc=tcp)
    uts = _ssm_in(_to_chunks(xs, nbs, ss), mod, gain, wt_bf, cols_per_cond=ks, tc=tcs)
    ytp, yts, fs = _ssm_core(utp, uts, ops, h0, kp=kp, ks=ks, nbp=nbp, nbs=nbs)
    mp = _from_chunks(_ssm_out(ytp, w_out_bf, tcp), nbp, sp)
    ms = _from_chunks(_ssm_out(yts, w_out_bf, tcs), nbs, ss)
    return mp, ms, fs


def _rope_tables(seq):
    t = jnp.arange(seq)
    row = (t // GRID_W).astype(F32)
    col = (t % GRID_W).astype(F32)
    n_freq = HEAD_DIM // 4
    inv_freq = ROPE_THETA ** (-jnp.arange(n_freq, dtype=F32) / n_freq)
    ang = jnp.concatenate([row[:, None] * inv_freq, col[:, None] * inv_freq], axis=-1)
    cos = jnp.repeat(jnp.cos(ang), 2, axis=-1)
    sin = jnp.repeat(jnp.sin(ang), 2, axis=-1)
    sign = jnp.tile(jnp.array([-1.0, 1.0], F32), HEAD_DIM // 2)
    return jnp.tile(cos, (1, 4)), jnp.tile(sin * sign, (1, 4))


def _head_gains(qn_a, kn_a, qn_b, kn_b):
    scale = HEAD_DIM ** -0.5
    ones = jnp.ones((N_KV * HEAD_DIM,), F32)
    return jnp.concatenate([jnp.tile(qn_a, N_HEADS) * scale, jnp.tile(kn_a, N_KV), ones,
                            jnp.tile(qn_b, N_HEADS) * scale, jnp.tile(kn_b, N_KV), ones]).reshape(1, QKV_COLS)


def kernel(x_prompt, x_sample, c, cache_k_a_l0, cache_v_a_l0, cache_k_b_l0, cache_v_b_l0, state_ssm_re_l1, state_ssm_im_l1, c_ctx, mod_w_l0, mod_b_l0, norm_mix_l0, attn_w_in_l0, q_norm_a_l0, k_norm_a_l0, q_norm_b_l0, k_norm_b_l0, sink_b_l0, attn_w_out_l0, norm_ffn_l0, router_l0, moe_w_gate_l0, moe_w_up_l0, moe_w_down_l0, mod_w_l1, mod_b_l1, norm_mix_l1, ssm_w_in_l1, ssm_lambda_re_l1, ssm_lambda_im_l1, ssm_b_re_l1, ssm_b_im_l1, ssm_c_re_l1, ssm_c_im_l1, ssm_log_dt_l1, ssm_d_l1, ssm_w_out_l1, norm_ffn_l1, router_l1, moe_w_gate_l1, moe_w_up_l1, moe_w_down_l1):
    bp, sp, d = x_prompt.shape
    bs, ss, _ = x_sample.shape
    past = cache_k_a_l0.shape[1]
    assert d == D_MODEL and bs <= 7 and (bp * sp) % TM == 0 and TM % sp == 0 and ss % TM == 0
    xp = x_prompt.reshape(bp * sp, d)
    xs = x_sample.reshape(bs * ss, d)
    cond8 = jnp.concatenate([c_ctx[None], c, jnp.zeros((7 - bs, d), F32)], axis=0)
    row1 = lambda v: v.reshape(1, -1)

    mod0 = _mod_rows(cond8, mod_w_l0, mod_b_l0)
    w_in = attn_w_in_l0.astype(BF16)
    hgain = _head_gains(q_norm_a_l0, k_norm_a_l0, q_norm_b_l0, k_norm_b_l0)
    lane = np.arange(256)
    bd = jnp.asarray((lane[:, None] // HEAD_DIM == lane[None, :] // HEAD_DIM) / HEAD_DIM, BF16)
    cos_t, sin_t = _rope_tables(ss)
    qp, kap, vap, kbp, vbp = _qkv(xp, mod0, row1(norm_mix_l0), w_in, hgain, bd, cos_t, sin_t,
                                  rows_per_cond=None, seq=sp, rope=False, kv_dtype=F32, transposed_kv=True)
    qs, kas, vas, kbs, vbs = _qkv(xs, mod0, row1(norm_mix_l0), w_in, hgain, bd, cos_t, sin_t,
                                  rows_per_cond=ss, seq=ss, rope=True, kv_dtype=BF16, transposed_kv=False)
    op = _attn_ctx(sink_b_l0, qp, kap, vap, kbp, vbp, sp)
    cache = lambda t: t.reshape(bs, past, N_KV * HEAD_DIM)
    os_ = _attn_lat(sink_b_l0, qs, kas, vas, kbs, vbs, cache(cache_k_a_l0), cache(cache_v_a_l0),
                    cache(cache_k_b_l0), cache(cache_v_b_l0), ss)
    w_out = attn_w_out_l0.astype(BF16)
    x1p, hp, affp = _postmix(op, xp, mod0, w_out, row1(norm_ffn_l0), router_l0.T, rows_per_cond=None, project=True)
    x1s, hs, affs = _postmix(os_, xs, mod0, w_out, row1(norm_ffn_l0), router_l0.T, rows_per_cond=ss, project=True)
    xp, xs = _moe_pair(hp, affp, x1p, hs, affs, x1s, mod0, moe_w_gate_l0, moe_w_up_l0, moe_w_down_l0, sp, ss)

    mod1 = _mod_rows(cond8, mod_w_l1, mod_b_l1)
    ops = _ssm_operand_params(ssm_lambda_re_l1, ssm_lambda_im_l1, ssm_b_re_l1, ssm_b_im_l1, ssm_c_re_l1, ssm_c_im_l1,
                              ssm_log_dt_l1, ssm_d_l1)
    wt = ssm_w_in_l1.T.astype(BF16)
    w_so = ssm_w_out_l1.astype(BF16)
    h0 = jnp.stack([state_ssm_re_l1[:, 0], state_ssm_im_l1[:, 0], state_ssm_re_l1[:, 1], state_ssm_im_l1[:, 1]],
                   axis=0).transpose(2, 0, 3, 1)
    mp, ms, fsp = _ssm_mixers(xp, xs, mod1, row1(norm_mix_l1), wt, ops, w_so, h0, nbp=bp, sp=sp, nbs=bs, ss=ss)
    dummy_w = jnp.zeros((8, 128), BF16)
    x1p, hp, affp = _postmix(mp, xp, mod1, dummy_w, row1(norm_ffn_l1), router_l1.T, rows_per_cond=None, project=False)
    x1s, hs, affs = _postmix(ms, xs, mod1, dummy_w, row1(norm_ffn_l1), router_l1.T, rows_per_cond=ss, project=False)
    xp, xs = _moe_pair(hp, affp, x1p, hs, affs, x1s, mod1, moe_w_gate_l1, moe_w_up_l1, moe_w_down_l1, sp, ss)

    kv_out = lambda t: t.reshape(bp, N_KV, HEAD_DIM, sp).transpose(0, 3, 1, 2)
    fin = fsp.transpose(3, 1, 0, 2)
    ssm_re = jnp.stack([fin[:, 0], fin[:, 2]], axis=1)
    ssm_im = jnp.stack([fin[:, 1], fin[:, 3]], axis=1)
    return (xp.reshape(bp, sp, d), xs.reshape(bs, ss, d), kv_out(kap), kv_out(vap), kv_out(kbp), kv_out(vbp),
            ssm_re, ssm_im)
```

```python
import functools
import itertools

import jax
import jax.numpy as jnp
import numpy as np
from jax import lax
from jax.experimental import pallas as pl
from jax.experimental.pallas import tpu as pltpu

F32, BF16, I32 = jnp.float32, jnp.bfloat16, jnp.int32

D_MODEL = 1024
GRID_W = 64
HEAD_DIM = 64
N_HEADS = 8
N_KV = 2
WINDOW = 128
ROPE_THETA = 10000.0
SSM_GROUP = 16
SSM_GROUPS = D_MODEL // SSM_GROUP
SSM_STATE = 64
N_EXPERTS = 16
EC_FACTOR = 2
D_FF = 2 * D_MODEL
EPS = 1e-6
NEG_INF = -1e30
LOG2E = 1.4426950408889634
QKV_COLS = 2 * (N_HEADS + 2 * N_KV) * HEAD_DIM
ATTN_OUT = 2 * N_HEADS * HEAD_DIM

SSM_CHUNK = 16
SSM_ROWS = SSM_CHUNK * SSM_GROUP
SSM_GB = 4
SSM_PASSES = 1

SSM_TC = 512

TM = 512

MOE_TT = 256
MOE_W = 64
MOE_EG = 4
VMEM_LIMIT = 56 * 1024 * 1024

NN = (((1,), (0,)), ((), ()))
NT = (((1,), (1,)), ((), ()))


def _dot(a, b, dims=NN):
    return lax.dot_general(a, b, dims, preferred_element_type=F32)


def _split2(x):
    hi = x.astype(BF16)
    lo = (x - hi.astype(F32)).astype(BF16)
    return hi, lo


def _split3(x):
    hi = x.astype(BF16)
    r = x - hi.astype(F32)
    mid = r.astype(BF16)
    lo = (r - mid.astype(F32)).astype(BF16)
    return hi, mid, lo


def _dot3(a, b, dims=NN):
    ah, al = _split2(a)
    bh, bl = _split2(b)
    return _dot(ah, bh, dims) + (_dot(ah, bl, dims) + _dot(al, bh, dims))


def _mm(a, b, passes):
    if passes == 1:
        return _dot(a.astype(BF16), b.astype(BF16))
    return _dot3(a, b)


def _dot_sel(x, sel):
    hi, mid, lo = _split3(x)
    return _dot(hi, sel) + (_dot(mid, sel) + _dot(lo, sel))


def _sigmoid(x):
    return 1.0 / (1.0 + jnp.exp(-x))


def _norm_mod(x, gain, shift, scale):
    ms = jnp.mean(x * x, axis=-1, keepdims=True)
    y = x * lax.rsqrt(ms + EPS) * gain
    return y * (1.0 + scale) + shift


def _params(*sem):
    return pltpu.CompilerParams(dimension_semantics=sem, vmem_limit_bytes=VMEM_LIMIT)


def _adaln_kernel(c_ref, w_ref, b_ref, o_ref):
    c = c_ref[...]
    s = c * _sigmoid(c)
    o_ref[...] = _dot3(s, w_ref[...]) + b_ref[...]


def _adaln(cond8, w_mod, b_mod):
    d, e = w_mod.shape
    tn = 1536
    return pl.pallas_call(
        _adaln_kernel,
        grid=(e // tn,),
        in_specs=[pl.BlockSpec((8, d), lambda j: (0, 0)),
                  pl.BlockSpec((d, tn), lambda j: (0, j)),
                  pl.BlockSpec((1, tn), lambda j: (0, j))],
        out_specs=pl.BlockSpec((8, tn), lambda j: (0, j)),
        out_shape=jax.ShapeDtypeStruct((8, e), F32),
        compiler_params=_params("parallel"),
        name="adaln",
    )(cond8, w_mod, b_mod.reshape(1, e))


def _mod_rows(cond8, w_mod, b_mod):
    m = _adaln(cond8, w_mod, b_mod).reshape(8, 6, D_MODEL)
    return jnp.pad(m, ((0, 0), (0, 2), (0, 0)))


def _mod_spec(rows_per_cond):
    if rows_per_cond is None:
        return pl.BlockSpec((1, 8, D_MODEL), lambda i: (0, 0, 0))
    return pl.BlockSpec((1, 8, D_MODEL), lambda i: (1 + (i * TM) // rows_per_cond, 0, 0))


def _qkv_kernel(x_ref, mod_ref, gain_ref, w_ref, hg_ref, bd_ref, cos_ref, sin_ref,
                q_ref, ka_ref, va_ref, kb_ref, vb_ref, *, rope, transposed_kv):
    h = _norm_mod(x_ref[...], gain_ref[...], mod_ref[0, 0:1, :], mod_ref[0, 1:2, :])
    proj = _dot(h.astype(BF16), w_ref[...])
    bd = bd_ref[...]

    def head_norm(blk, g):
        hi, lo = _split2(blk * blk)
        ms = _dot(hi, bd) + _dot(lo, bd)
        return blk * lax.rsqrt(ms + EPS) * g

    def rotary(blk):
        w = blk.shape[1]
        even = (lax.broadcasted_iota(I32, blk.shape, 1) & 1) == 0
        swapped = jnp.where(even, pltpu.roll(blk, w - 1, 1), pltpu.roll(blk, 1, 1))
        return blk * cos_ref[:, :w] + swapped * sin_ref[:, :w]

    def qk(c0):
        blk = head_norm(proj[:, c0:c0 + 256], hg_ref[:, c0:c0 + 256])
        return rotary(blk) if rope else blk

    q_ref[:, 0:256] = qk(0).astype(q_ref.dtype)
    q_ref[:, 256:512] = qk(256).astype(q_ref.dtype)
    q_ref[:, 512:768] = qk(768).astype(q_ref.dtype)
    q_ref[:, 768:1024] = qk(1024).astype(q_ref.dtype)
    kva = qk(512)
    kvb = qk(1280)
    outs = ((ka_ref, kva[:, :128]), (va_ref, proj[:, 640:768]), (kb_ref, kvb[:, :128]), (vb_ref, proj[:, 1408:1536]))
    for ref, val in outs:
        if transposed_kv:
            seq = ref.shape[2]
            for r in range(ref.shape[0]):
                ref[r] = val[r * seq:(r + 1) * seq].T.astype(ref.dtype)
        else:
            ref[...] = val.astype(ref.dtype)


def _qkv(x, mod, gain, w_bf, hgain, bd, cos_t, sin_t, *, rows_per_cond, seq, rope, kv_dtype, transposed_kv):
    n = x.shape[0]
    tiles_per_seq = max(1, seq // TM)
    row = lambda i: (i, 0)
    const = lambda i: (0, 0)
    pos = lambda i: (i % tiles_per_seq, 0)
    if transposed_kv:
        assert TM % seq == 0
        kv_shape = jax.ShapeDtypeStruct((n // seq, 128, seq), kv_dtype)
        kv_spec = pl.BlockSpec((TM // seq, 128, seq), lambda i: (i, 0, 0))
    else:
        kv_shape = jax.ShapeDtypeStruct((n, 128), kv_dtype)
        kv_spec = pl.BlockSpec((TM, 128), row)
    return pl.pallas_call(
        functools.partial(_qkv_kernel, rope=rope, transposed_kv=transposed_kv),
        grid=(n // TM,),
        in_specs=[pl.BlockSpec((TM, D_MODEL), row),
                  _mod_spec(rows_per_cond),
                  pl.BlockSpec((1, D_MODEL), const),
                  pl.BlockSpec((D_MODEL, QKV_COLS), const),
                  pl.BlockSpec((1, QKV_COLS), const),
                  pl.BlockSpec((256, 256), const),
                  pl.BlockSpec((TM, 256), pos),
                  pl.BlockSpec((TM, 256), pos)],
        out_specs=[pl.BlockSpec((TM, ATTN_OUT), row)] + [kv_spec] * 4,
        out_shape=[jax.ShapeDtypeStruct((n, ATTN_OUT), BF16)] + [kv_shape] * 4,
        compiler_params=_params("parallel"),
        name="qkv_rope" if rope else "qkv",
    )(x, mod, gain, w_bf, hgain, bd, cos_t, sin_t)


def _pad_variants(kk, ones=False):
    lane = lax.broadcasted_iota(I32, kk.shape, 1)
    left = lane < HEAD_DIM
    rolled = pltpu.roll(kk, HEAD_DIM, 1)
    fill_r = jnp.where(lane == HEAD_DIM, 1.0, 0.0) if ones else jnp.zeros_like(kk)
    fill_l = jnp.where(lane == 0, 1.0, 0.0) if ones else jnp.zeros_like(kk)
    return {(0, 0): jnp.where(left, kk, fill_r).astype(BF16),
            (0, 1): jnp.where(left, fill_l, rolled).astype(BF16),
            (1, 0): jnp.where(left, rolled, fill_r).astype(BF16),
            (1, 1): jnp.where(left, fill_l, kk).astype(BF16)}


def _pad_variants_t(kt, ones=False):
    row = lax.broadcasted_iota(I32, kt.shape, 0)
    top = row < HEAD_DIM
    zero = jnp.zeros((HEAD_DIM, kt.shape[1]), F32)
    lower = jnp.concatenate([zero, kt[:HEAD_DIM]], axis=0)
    upper = jnp.concatenate([kt[HEAD_DIM:], zero], axis=0)
    fill_b = jnp.where(row == HEAD_DIM, 1.0, 0.0) if ones else jnp.zeros_like(kt)
    fill_t = jnp.where(row == 0, 1.0, 0.0) if ones else jnp.zeros_like(kt)
    return {(0, 0): jnp.where(top, kt, fill_b).astype(BF16),
            (0, 1): jnp.where(top, fill_t, lower).astype(BF16),
            (1, 0): jnp.where(top, upper, fill_b).astype(BF16),
            (1, 1): jnp.where(top, fill_t, kt).astype(BF16)}


def _head_attention(qp, keys, vals, masks, sink, par, transposed=False):
    kdims, vdims = (NN, NT) if transposed else (NT, NN)
    scores = []
    for kblk, mask in zip(keys, masks):
        s = _dot(qp, kblk, kdims)
        if mask is not None:
            s = jnp.where(mask, s, NEG_INF)
        scores.append(s)
    m = scores[0].max(axis=-1, keepdims=True)
    for s in scores[1:]:
        m = jnp.maximum(m, s.max(axis=-1, keepdims=True))
    if sink is not None:
        m = jnp.maximum(m, sink)
    out = None
    for s, vblk in zip(scores, vals):
        o = _dot(jnp.exp2((s - m).astype(BF16)), vblk, vdims)
        out = o if out is None else out + o
    ones_lane = HEAD_DIM if par == 0 else 0
    den = out[:, ones_lane:ones_lane + 1]
    if sink is not None:
        den = den + jnp.exp2(sink - m)
    own = (lax.broadcasted_iota(I32, out.shape, 1) < HEAD_DIM) == (par == 0)
    return jnp.where(own, out / den, 0.0)


def _attn_ctx_kernel(sink_ref, q_ref, ka_ref, va_ref, kb_ref, vb_ref, o_ref):
    for mixer, (k_ref, v_ref) in enumerate(((ka_ref, va_ref), (kb_ref, vb_ref))):
        kvar = _pad_variants_t(k_ref[0])
        vvar = _pad_variants_t(v_ref[0], ones=True)
        for t in range(4):
            tile = mixer * 4 + t
            kv = t // 2
            qp = q_ref[:, tile * 128:(tile + 1) * 128]
            acc = None
            for par in range(2):
                sink = sink_ref[2 * t + par] * LOG2E if mixer == 1 else None
                o = _head_attention(qp, [kvar[(kv, par)]], [vvar[(kv, par)]], [None], sink, par, transposed=True)
                acc = o if acc is None else acc + o
            o_ref[:, tile * 128:(tile + 1) * 128] = acc.astype(o_ref.dtype)


def _attn_ctx(sink, q, ka, va, kb, vb, seq):
    n = q.shape[0]
    row = lambda b: (b, 0)
    kv_spec = pl.BlockSpec((1, 128, seq), lambda b: (b, 0, 0))
    return pl.pallas_call(
        _attn_ctx_kernel,
        grid=(n // seq,),
        in_specs=[pl.BlockSpec(memory_space=pltpu.SMEM),
                  pl.BlockSpec((seq, ATTN_OUT), row), kv_spec, kv_spec, kv_spec, kv_spec],
        out_specs=pl.BlockSpec((seq, ATTN_OUT), row),
        out_shape=jax.ShapeDtypeStruct((n, ATTN_OUT), BF16),
        compiler_params=_params("parallel"),
        name="attn_ctx",
    )(sink, q, ka, va, kb, vb)


def _attn_lat_kernel(sink_ref, q_ref, ka_ref, va_ref, kb_ref, vb_ref,
                     cka_ref, cva_ref, ckb_ref, cvb_ref, o_ref, *, tq, seq):
    qi = pl.program_id(1)
    span = tq + 2 * WINDOW
    ck = _pad_variants(cka_ref[0])
    cv = _pad_variants(cva_ref[0], ones=True)
    lk = _pad_variants(ka_ref[...].astype(F32))
    lv = _pad_variants(va_ref[...].astype(F32), ones=True)
    for t in range(4):
        kv = t // 2
        qp = q_ref[:, t * 128:(t + 1) * 128]
        acc = None
        for par in range(2):
            o = _head_attention(qp, [ck[(kv, par)], lk[(kv, par)]], [cv[(kv, par)], lv[(kv, par)]],
                                [None, None], None, par)
            acc = o if acc is None else acc + o
        o_ref[:, t * 128:(t + 1) * 128] = acc.astype(o_ref.dtype)
    lo = jnp.clip(qi * tq - WINDOW, 0, seq - span)
    lo = pl.multiple_of(lo, 128)
    qpos = qi * tq + lax.broadcasted_iota(I32, (tq, span), 0)
    kpos = lo + lax.broadcasted_iota(I32, (tq, span), 1)
    band = jnp.abs(qpos - kpos) <= WINDOW
    ck = _pad_variants(ckb_ref[0])
    cv = _pad_variants(cvb_ref[0], ones=True)
    lk = _pad_variants(kb_ref[pl.ds(lo, span), :].astype(F32))
    lv = _pad_variants(vb_ref[pl.ds(lo, span), :].astype(F32), ones=True)
    for t in range(4):
        kv = t // 2
        tile = 4 + t
        qp = q_ref[:, tile * 128:(tile + 1) * 128]
        acc = None
        for par in range(2):
            sink = sink_ref[2 * t + par] * LOG2E
            o = _head_attention(qp, [lk[(kv, par)], ck[(kv, par)]], [lv[(kv, par)], cv[(kv, par)]],
                                [band, None], sink, par)
            acc = o if acc is None else acc + o
        o_ref[:, tile * 128:(tile + 1) * 128] = acc.astype(o_ref.dtype)


def _attn_lat(sink, q, ka, va, kb, vb, cka, cva, ckb, cvb, seq, tq=256):
    n = q.shape[0]
    nb = n // seq
    nq = seq // tq
    qrow = lambda b, i: (b * nq + i, 0)
    brow = lambda b, i: (b, 0)
    kv_spec = pl.BlockSpec((seq, 128), brow)
    past = cka.shape[1]
    c_spec = pl.BlockSpec((1, past, 128), lambda b, i: (b, 0, 0))
    return pl.pallas_call(
        functools.partial(_attn_lat_kernel, tq=tq, seq=seq),
        grid=(nb, nq),
        in_specs=[pl.BlockSpec(memory_space=pltpu.SMEM),
                  pl.BlockSpec((tq, ATTN_OUT), qrow), kv_spec, kv_spec, kv_spec, kv_spec,
                  c_spec, c_spec, c_spec, c_spec],
        out_specs=pl.BlockSpec((tq, ATTN_OUT), qrow),
        out_shape=jax.ShapeDtypeStruct((n, ATTN_OUT), BF16),
        compiler_params=_params("parallel", "parallel"),
        name="attn_lat",
    )(sink, q, ka, va, kb, vb, cka, cva, ckb, cvb)


def _postmix_kernel(m_ref, x_ref, mod_ref, w_ref, gain_ref, rt_ref, x1_ref, h2_ref, aff_ref, *, project):
    if project:
        m = _dot(m_ref[...], w_ref[...])
    else:
        m = m_ref[...]
    x1 = x_ref[...] + mod_ref[0, 2:3, :] * m
    x1_ref[...] = x1
    h2 = _norm_mod(x1, gain_ref[...], mod_ref[0, 3:4, :], mod_ref[0, 4:5, :])
    h2_ref[...] = h2.astype(h2_ref.dtype)
    logits = _dot3(rt_ref[...], h2, NT)
    e = jnp.exp(logits - logits.max(axis=0, keepdims=True))
    aff_ref[...] = e / e.sum(axis=0, keepdims=True)


def _postmix(m, x, mod, w_bf, gain, router_t, *, rows_per_cond, project):
    n = x.shape[0]
    row = lambda i: (i, 0)
    const = lambda i: (0, 0)
    return pl.pallas_call(
        functools.partial(_postmix_kernel, project=project),
        grid=(n // TM,),
        in_specs=[pl.BlockSpec((TM, D_MODEL), row),
                  pl.BlockSpec((TM, D_MODEL), row),
                  _mod_spec(rows_per_cond),
                  pl.BlockSpec(w_bf.shape, const),
                  pl.BlockSpec((1, D_MODEL), const),
                  pl.BlockSpec((N_EXPERTS, D_MODEL), const)],
        out_specs=[pl.BlockSpec((TM, D_MODEL), row),
                   pl.BlockSpec((TM, D_MODEL), row),
                   pl.BlockSpec((N_EXPERTS, TM), lambda i: (0, i))],
        out_shape=[jax.ShapeDtypeStruct((n, D_MODEL), F32),
                   jax.ShapeDtypeStruct((n, D_MODEL), BF16),
                   jax.ShapeDtypeStruct((N_EXPERTS, n), F32)],
        compiler_params=_params("parallel"),
        name="postmix_proj" if project else "postmix",
    )(m, x, mod, w_bf, gain, router_t)


def _route_kernel(aff_ref, slot_ref, gate_ref, pos_ref, tcnt_ref, *, seq, cap, nseg, tt):
    aff = jnp.concatenate([aff_ref[:, s * seq:(s + 1) * seq] for s in range(nseg)], axis=0)
    rows = aff.shape[0]
    capf = jnp.float32(cap)
    thr_bits = jnp.zeros((rows, 1), I32)
    for bit in range(30, -1, -1):
        cand = thr_bits | (1 << bit)
        cnt = jnp.where(aff >= pltpu.bitcast(cand, F32), 1.0, 0.0).sum(axis=1, keepdims=True)
        thr_bits = jnp.where(cnt >= capf, cand, thr_bits)
    thr = pltpu.bitcast(thr_bits, F32)
    gt = aff > thr
    eq = aff == thr
    n_gt = jnp.where(gt, 1.0, 0.0).sum(axis=1, keepdims=True)
    pw = min(seq, 256)
    tri = jnp.where(lax.broadcasted_iota(I32, (pw, pw), 0) < lax.broadcasted_iota(I32, (pw, pw), 1),
                    1.0, 0.0).astype(BF16)

    def count_before(flag):
        ones = jnp.where(flag, 1.0, 0.0)
        parts = []
        run = jnp.zeros((rows, 1), F32)
        for c0 in range(0, seq, pw):
            blk = ones[:, c0:c0 + pw]
            parts.append(_dot(blk.astype(BF16), tri) + run)
            run = run + blk.sum(axis=1, keepdims=True)
        return jnp.concatenate(parts, axis=1) if len(parts) > 1 else parts[0]

    sel = gt | (eq & (count_before(eq) < capf - n_gt))
    rank = count_before(sel)
    expert = lax.broadcasted_iota(I32, (rows, seq), 0) & (N_EXPERTS - 1)
    slot = jnp.where(sel, expert * cap + rank.astype(I32), -1)
    gate = jnp.where(sel, aff, 0.0)
    pos = jnp.where(sel, rank, -1.0)
    nt = seq // tt
    tile_of = jnp.where((lax.broadcasted_iota(I32, (seq, nt), 0) // tt) == lax.broadcasted_iota(I32, (seq, nt), 1),
                        1.0, 0.0).astype(BF16)
    tcnt = _dot(jnp.where(sel, 1.0, 0.0).astype(BF16), tile_of)
    for s in range(nseg):
        rows_s = slice(s * N_EXPERTS, (s + 1) * N_EXPERTS)
        slot_ref[:, s * seq:(s + 1) * seq] = slot[rows_s, :]
        gate_ref[:, s * seq:(s + 1) * seq] = gate[rows_s, :]
        pos_ref[:, s * seq:(s + 1) * seq] = pos[rows_s, :]
        tcnt_ref[0, :, s * nt:(s + 1) * nt] = tcnt[rows_s, :]


def _route(aff_t, seq, cap, nseg, tt):
    n = aff_t.shape[1]
    nt = seq // tt
    steps = n // (nseg * seq)
    spec = pl.BlockSpec((N_EXPERTS, nseg * seq), lambda i: (0, i))
    slot, gate, pos, tcnt = pl.pallas_call(
        functools.partial(_route_kernel, seq=seq, cap=cap, nseg=nseg, tt=tt),
        grid=(steps,),
        in_specs=[spec],
        out_specs=[spec, spec, spec, pl.BlockSpec((1, N_EXPERTS, nseg * nt), lambda i: (i, 0, 0))],
        out_shape=[jax.ShapeDtypeStruct((N_EXPERTS, n), I32), jax.ShapeDtypeStruct((N_EXPERTS, n), F32),
                   jax.ShapeDtypeStruct((N_EXPERTS, n), F32),
                   jax.ShapeDtypeStruct((steps, N_EXPERTS, nseg * nt), F32)],
        compiler_params=_params("parallel"),
        name="route",
    )(aff_t)
    return slot, gate, pos, tcnt.transpose(1, 0, 2).reshape(N_EXPERTS, n // tt)


def _dispatch_kernel(slot_ref, h_ref, x_ref, *, cap):
    seq = h_ref.shape[0]
    m = N_EXPERTS * cap
    owner = jnp.broadcast_to(slot_ref[...][:, None, :], (N_EXPERTS, cap, seq)).reshape(m, seq)
    hit = owner == lax.broadcasted_iota(I32, (m, seq), 0)
    x_ref[...] = _dot(jnp.where(hit, 1.0, 0.0).astype(BF16), h_ref[...]).astype(x_ref.dtype)


def _dispatch(slot, h, seq, cap):
    n = h.shape[0]
    nb = n // seq
    return pl.pallas_call(
        functools.partial(_dispatch_kernel, cap=cap),
        grid=(nb,),
        in_specs=[pl.BlockSpec((N_EXPERTS, seq), lambda b: (0, b)),
                  pl.BlockSpec((seq, D_MODEL), lambda b: (b, 0))],
        out_specs=pl.BlockSpec((N_EXPERTS * cap, D_MODEL), lambda b: (b, 0)),
        out_shape=jax.ShapeDtypeStruct((nb * N_EXPERTS * cap, D_MODEL), BF16),
        compiler_params=_params("parallel"),
        name="moe_dispatch",
    )(slot, h)


def _window(cum_ref, base, e, cap, k=0):
    lo = ((cum_ref[base + e] >> 4) << 4) + k * MOE_W
    return lo, pl.multiple_of(jnp.minimum(lo, cap - MOE_W), 16)


def _extra_windows(cum_ref, base, e):
    lo = (cum_ref[base + e] >> 4) << 4
    return (cum_ref[base + N_EXPERTS + e] - lo + (MOE_W - 1)) >> 6


def _dispatch_win_kernel(cum_ref, slot_ref, h_ref, x_ref, *, cap, nt):
    b, i = pl.program_id(0), pl.program_id(1)
    tt = h_ref.shape[0]
    base = (b * (nt + 1) + i) * N_EXPERTS
    h = h_ref[...]
    row = lax.broadcasted_iota(I32, (MOE_W, tt), 0)

    @pl.when(i == 0)
    def _():
        x_ref[...] = jnp.zeros_like(x_ref)

    def hits(e, lo, ws):
        srow = slot_ref[e:e + 1, :]
        return (srow == row + (e * cap + ws)) & (srow >= e * cap + lo)

    for grp in range(N_EXPERTS // MOE_EG):
        wins = [(e,) + _window(cum_ref, base, e, cap) for e in range(grp * MOE_EG, (grp + 1) * MOE_EG)]
        sel = jnp.concatenate([hits(e, lo, ws) for e, lo, ws in wins], axis=0)
        x = _dot(jnp.where(sel, 1.0, 0.0).astype(BF16), h)
        for q, (e, lo, ws) in enumerate(wins):
            dst = pl.ds(e * cap + ws, MOE_W)
            x_ref[dst, :] += x[q * MOE_W:(q + 1) * MOE_W].astype(x_ref.dtype)

    for e in range(N_EXPERTS):
        def extra(k, carry, e=e):
            lo, ws = _window(cum_ref, base, e, cap, k)
            x = _dot(jnp.where(hits(e, lo, ws), 1.0, 0.0).astype(BF16), h)
            x_ref[pl.ds(e * cap + ws, MOE_W), :] += x.astype(x_ref.dtype)
            return carry
        lax.fori_loop(1, _extra_windows(cum_ref, base, e), extra, 0)


def _dispatch_win(cum, slot, h, seq, cap):
    n = h.shape[0]
    nb, nt = n // seq, seq // MOE_TT
    return pl.pallas_call(
        functools.partial(_dispatch_win_kernel, cap=cap, nt=nt),
        grid_spec=pltpu.PrefetchScalarGridSpec(
            num_scalar_prefetch=1,
            grid=(nb, nt),
            in_specs=[pl.BlockSpec((N_EXPERTS, MOE_TT), lambda b, i, c: (0, b * nt + i)),
                      pl.BlockSpec((MOE_TT, D_MODEL), lambda b, i, c: (b * nt + i, 0))],
            out_specs=pl.BlockSpec((N_EXPERTS * cap, D_MODEL), lambda b, i, c: (b, 0))),
        out_shape=jax.ShapeDtypeStruct((nb * N_EXPERTS * cap, D_MODEL), BF16),
        compiler_params=_params("parallel", "arbitrary"),
        name="moe_dispatch_win",
    )(cum, slot, h)


FFN_TF = 512
FFN_RC = 512


def _ffn_kernel(xa_ref, xb_ref, wg_ref, wu_ref, wd_ref, ya_ref, yb_ref, acc_ref):
    j = pl.program_id(1)

    @pl.when(j == 0)
    def _():
        acc_ref[...] = jnp.zeros_like(acc_ref)

    wg = wg_ref[0].astype(BF16)
    wu = wu_ref[0].astype(BF16)
    wd = wd_ref[0].astype(BF16)
    ra = xa_ref.shape[0] * xa_ref.shape[2]

    def row_chunks(ref, base):
        nb, _, cap, d = ref.shape
        rc = min(FFN_RC, nb * cap)
        for r0 in range(0, nb * cap, rc):
            if cap >= rc:
                b, c0 = divmod(r0, cap)
                yield base + r0, ref[b, 0, c0:c0 + rc, :]
            else:
                yield base + r0, ref[r0 // cap:(r0 + rc) // cap, 0, :, :].reshape(rc, d)

    for r0, x in itertools.chain(row_chunks(xa_ref, 0), row_chunks(xb_ref, ra)):
        rc = x.shape[0]
        g = _dot(x, wg)
        u = _dot(x, wu)
        mid = (g * _sigmoid(g) * u).astype(BF16)
        acc_ref[r0:r0 + rc, :] += _dot(mid, wd)

    @pl.when(j == pl.num_programs(1) - 1)
    def _():
        for ref, base in ((ya_ref, 0), (yb_ref, ra)):
            nb, _, cap, d = ref.shape
            ref[...] = acc_ref[base:base + nb * cap, :].reshape(nb, 1, cap, d).astype(ref.dtype)


def _ffn(xa, xb, w_gate, w_up, w_down):
    ba, _, ca, d = xa.shape
    bb, _, cb, _ = xb.shape
    nj = D_FF // FFN_TF
    xa_spec = pl.BlockSpec((ba, 1, ca, d), lambda e, j: (0, e, 0, 0))
    xb_spec = pl.BlockSpec((bb, 1, cb, d), lambda e, j: (0, e, 0, 0))
    return pl.pallas_call(
        _ffn_kernel,
        grid=(N_EXPERTS, nj),
        in_specs=[xa_spec, xb_spec,
                  pl.BlockSpec((1, d, FFN_TF), lambda e, j: (e, 0, j)),
                  pl.BlockSpec((1, d, FFN_TF), lambda e, j: (e, 0, j)),
                  pl.BlockSpec((1, FFN_TF, d), lambda e, j: (e, j, 0))],
        out_specs=[xa_spec, xb_spec],
        out_shape=[jax.ShapeDtypeStruct(xa.shape, BF16), jax.ShapeDtypeStruct(xb.shape, BF16)],
        scratch_shapes=[pltpu.VMEM((ba * ca + bb * cb, d), F32)],
        compiler_params=_params("parallel", "arbitrary"),
        name="moe_ffn",
    )(xa, xb, w_gate, w_up, w_down)


def _expand(vals_bf, first_expert, width, total):
    e_of_lane = first_expert + lax.broadcasted_iota(I32, (N_EXPERTS, total), 1) // width
    pick = jnp.where(lax.broadcasted_iota(I32, (N_EXPERTS, total), 0) == e_of_lane, 1.0, 0.0).astype(BF16)
    return _dot(vals_bf, pick)


def _combine_kernel(pos_ref, gate_ref, y_ref, x_ref, mod_ref, o_ref, *, cap):
    tt = x_ref.shape[0]
    m = N_EXPERTS * cap
    pos = _expand(pos_ref[...].astype(BF16), 0, cap, m)
    gate = _expand(gate_ref[...].astype(BF16), 0, cap, m)
    rank = (lax.broadcasted_iota(I32, (tt, m), 1) % cap).astype(F32)
    w = jnp.where(pos == rank, gate, 0.0).astype(BF16)
    o_ref[...] = x_ref[...] + mod_ref[0, 5:6, :] * _dot(w, y_ref[...])


def _combine(pos_t, gate_t, y, x, mod, *, seq, cap):
    n = x.shape[0]
    row = lambda b: (b, 0)
    return pl.pallas_call(
        functools.partial(_combine_kernel, cap=cap),
        grid=(n // seq,),
        in_specs=[pl.BlockSpec((seq, N_EXPERTS), row),
                  pl.BlockSpec((seq, N_EXPERTS), row),
                  pl.BlockSpec((N_EXPERTS * cap, D_MODEL), row),
                  pl.BlockSpec((seq, D_MODEL), row),
                  pl.BlockSpec((1, 8, D_MODEL), lambda b: (0, 0, 0))],
        out_specs=pl.BlockSpec((seq, D_MODEL), row),
        out_shape=jax.ShapeDtypeStruct((n, D_MODEL), F32),
        compiler_params=_params("parallel"),
        name="moe_combine",
    )(pos_t, gate_t, y, x, mod)


def _combine_win_kernel(cum_ref, pos_ref, gate_ref, y_ref, x_ref, mod_ref, o_ref, acc_ref, *, cap, nt):
    b, i = pl.program_id(0), pl.program_id(1)
    tt = x_ref.shape[0]
    base = (b * (nt + 1) + i) * N_EXPERTS
    width = MOE_EG * MOE_W
    posb = pos_ref[...].astype(BF16)
    gateb = gate_ref[...].astype(BF16)
    lane = lax.broadcasted_iota(I32, (1, width), 1)
    offset = (lane & (MOE_W - 1)).astype(F32)
    acc = jnp.zeros((tt, D_MODEL), F32)
    for grp in range(N_EXPERTS // MOE_EG):
        wins = [(e,) + _window(cum_ref, base, e, cap) for e in range(grp * MOE_EG, (grp + 1) * MOE_EG)]
        lo_l = jnp.zeros((1, width), F32)
        ws_l = jnp.zeros((1, width), F32)
        for q, (e, lo, ws) in enumerate(wins):
            mine = (lane >> 6) == q
            lo_l = jnp.where(mine, lo.astype(F32), lo_l)
            ws_l = jnp.where(mine, ws.astype(F32), ws_l)
        pos = _expand(posb, grp * MOE_EG, MOE_W, width)
        gate = _expand(gateb, grp * MOE_EG, MOE_W, width)
        w = jnp.where((pos - ws_l == offset) & (pos >= lo_l), gate, 0.0).astype(BF16)
        ywin = jnp.concatenate([y_ref[pl.ds(e * cap + ws, MOE_W), :] for e, lo, ws in wins], axis=0)
        acc = acc + _dot(w, ywin)
    acc_ref[...] = acc

    off64 = lax.broadcasted_iota(I32, (1, MOE_W), 1).astype(F32)
    for e in range(N_EXPERTS):
        def extra(k, carry, e=e):
            lo, ws = _window(cum_ref, base, e, cap, k)
            pos = pos_ref[:, e:e + 1]
            gate = gate_ref[:, e:e + 1].astype(BF16).astype(F32)
            w = jnp.where((pos - ws.astype(F32) == off64) & (pos >= lo.astype(F32)), gate, 0.0).astype(BF16)
            acc_ref[...] += _dot(w, y_ref[pl.ds(e * cap + ws, MOE_W), :])
            return carry
        lax.fori_loop(1, _extra_windows(cum_ref, base, e), extra, 0)

    o_ref[...] = x_ref[...] + mod_ref[0, 5:6, :] * acc_ref[...]


def _combine_win(cum, pos_t, gate_t, y, x, mod, *, seq, cap):
    n = x.shape[0]
    nb, nt = n // seq, seq // MOE_TT
    row = lambda b, i, c: (b * nt + i, 0)
    return pl.pallas_call(
        functools.partial(_combine_win_kernel, cap=cap, nt=nt),
        grid_spec=pltpu.PrefetchScalarGridSpec(
            num_scalar_prefetch=1,
            grid=(nb, nt),
            in_specs=[pl.BlockSpec((MOE_TT, N_EXPERTS), row),
                      pl.BlockSpec((MOE_TT, N_EXPERTS), row),
                      pl.BlockSpec((N_EXPERTS * cap, D_MODEL), lambda b, i, c: (b, 0)),
                      pl.BlockSpec((MOE_TT, D_MODEL), row),
                      pl.BlockSpec((1, 8, D_MODEL), lambda b, i, c: (1 + b, 0, 0))],
            out_specs=pl.BlockSpec((MOE_TT, D_MODEL), row),
            scratch_shapes=[pltpu.VMEM((MOE_TT, D_MODEL), F32)]),
        out_shape=jax.ShapeDtypeStruct((n, D_MODEL), F32),
        compiler_params=_params("parallel", "parallel"),
        name="moe_combine_win",
    )(cum, pos_t, gate_t, y, x, mod)


def _moe_pair(hp, affp, x1p, hs, affs, x1s, mod, w_gate, w_up, w_down, seq_p, seq_s):
    n_p, n_s = hp.shape[0], hs.shape[0]
    nb_p, nb_s = n_p // seq_p, n_s // seq_s
    cap_p = EC_FACTOR * seq_p // N_EXPERTS
    cap_s = EC_FACTOR * seq_s // N_EXPERTS
    assert N_EXPERTS * cap_p <= 512 and cap_s >= MOE_W and cap_s % 16 == 0 and seq_s % MOE_TT == 0
    slot_p, gate_p, pos_p, _ = _route(affp, seq_p, cap_p, nseg=min(8, nb_p), tt=seq_p)
    slot_s, gate_s, pos_s, tcnt = _route(affs, seq_s, cap_s, nseg=min(4, nb_s), tt=MOE_TT)
    nt = seq_s // MOE_TT
    counts = tcnt.T.reshape(nb_s, nt, N_EXPERTS).astype(I32)
    cum = jnp.concatenate([jnp.zeros((nb_s, 1, N_EXPERTS), I32), jnp.cumsum(counts, axis=1)], axis=1).reshape(-1)
    xp = _dispatch(slot_p, hp, seq_p, cap_p).reshape(nb_p, N_EXPERTS, cap_p, D_MODEL)
    xs = _dispatch_win(cum, slot_s, hs, seq_s, cap_s).reshape(nb_s, N_EXPERTS, cap_s, D_MODEL)
    ys, yp = _ffn(xs, xp, w_gate, w_up, w_down)
    outp = _combine(pos_p.T, gate_p.T, yp.reshape(-1, D_MODEL), x1p, mod, seq=seq_p, cap=cap_p)
    outs = _combine_win(cum, pos_s.T, gate_s.T, ys.reshape(-1, D_MODEL), x1s, mod, seq=seq_s, cap=cap_s)
    return outp, outs


def _ssm_in_kernel(x_ref, mod_ref, gain_ref, wt_ref, ut_ref, *, cols_per_cond):
    tc = x_ref.shape[1]
    if cols_per_cond is None:
        h = _norm_mod(x_ref[0], gain_ref[...], mod_ref[0, 0:1, :], mod_ref[0, 1:2, :]).astype(BF16)
    else:
        first = 1 + pl.program_id(1) * (tc // cols_per_cond)
        parts = []
        for s in range(tc // cols_per_cond):
            m = mod_ref[first + s]
            parts.append(_norm_mod(x_ref[0, s * cols_per_cond:(s + 1) * cols_per_cond, :], gain_ref[...],
                                   m[0:1, :], m[1:2, :]).astype(BF16))
        h = jnp.concatenate(parts, axis=0) if len(parts) > 1 else parts[0]
    ut_ref[0] = _dot(wt_ref[...], h, NT)


def _ssm_in(xperm, mod, gain, wt_bf, *, cols_per_cond, tc):
    l, bk, d = xperm.shape
    assert cols_per_cond is None or tc % cols_per_cond == 0
    return pl.pallas_call(
        functools.partial(_ssm_in_kernel, cols_per_cond=cols_per_cond),
        grid=(l, bk // tc),
        in_specs=[pl.BlockSpec((1, tc, d), lambda j, i: (j, i, 0)),
                  pl.BlockSpec(mod.shape, lambda j, i: (0, 0, 0)),
                  pl.BlockSpec((1, d), lambda j, i: (0, 0)),
                  pl.BlockSpec((d, d), lambda j, i: (0, 0))],
        out_specs=pl.BlockSpec((1, d, tc), lambda j, i: (j, 0, i)),
        out_shape=jax.ShapeDtypeStruct((l, d, bk), F32),
        compiler_params=_params("parallel", "parallel"),
        name="ssm_in",
    )(xperm, mod, gain, wt_bf)


def _ssm_core_kernel(utp_ref, uts_ref, lamp_ref, c_ref, bt_ref, dsk_ref, h0_ref, ytp_ref, yts_ref, fs_ref,
                     *, kp, ks, nbp, nbs):
    rows = SSM_ROWS
    p = SSM_STATE
    lc = SSM_CHUNK
    ri = lax.broadcasted_iota(I32, (rows, rows), 0)
    cj = lax.broadcasted_iota(I32, (rows, rows), 1)
    causal = (ri >> 4) >= (cj >> 4)
    anticausal = (cj >> 4) >= (ri >> 4)
    diag = ri == cj
    leftc = lax.broadcasted_iota(I32, (lc, 128), 1) < p
    nrow = lax.broadcasted_iota(I32, (lc, 128), 0).astype(F32)
    eye = lax.broadcasted_iota(I32, (p, 128), 0) == lax.broadcasted_iota(I32, (p, 128), 1)

    def cmul(ar, ai, xr, xi):
        return ar * xr - ai * xi, ar * xi + ai * xr

    def expand_rows(t):
        return jnp.broadcast_to(t[:, None, :], (lc, SSM_GROUP, 128)).reshape(rows, 128)

    def tile_rows(t):
        return jnp.broadcast_to(t[None, :, :], (lc, SSM_GROUP, 128)).reshape(rows, 128)

    def to_col(row):
        return jnp.where(eye, jnp.broadcast_to(row, (p, 128)), 0.0).sum(axis=1, keepdims=True)

    def operands(gg, d):
        lp = lamp_ref[gg, d]
        lre, lim = lp[0:1], lp[1:2]
        dt = jnp.exp(lp[2:3])
        a, th = lre * dt, lim * dt
        ang = nrow * th
        cs, sn = jnp.cos(ang), jnp.sin(ang)
        ep, em = jnp.exp(nrow * a), jnp.exp(-(nrow * a))
        pr, pi = ep * cs, ep * sn
        nr, ni = em * cs, -(em * sn)
        l1r, l1i = pr[1:2], pi[1:2]
        lmr, lmi = pr[lc - 1:lc], pi[lc - 1:lc]
        llr, lli = cmul(lmr, lmi, l1r, l1i)
        den = lre * lre + lim * lim
        cr = ((l1r - 1.0) * lre + l1i * lim) / den
        ci = (l1i * lre - (l1r - 1.0) * lim) / den
        btr, bti = bt_ref[gg, d, 0], bt_ref[gg, d, 1]
        bbr, bbi = cr * btr - ci * bti, cr * bti + ci * btr
        u1 = tile_rows(jnp.where(leftc, bbr, bbi))
        u2 = tile_rows(jnp.where(leftc, bbi, bbr))
        c1 = tile_rows(c_ref[gg, d, 0])
        c2 = tile_rows(c_ref[gg, d, 1])

        def left_form(xr, xi):
            return (c1 * expand_rows(jnp.where(leftc, xr, -xi))
                    + c2 * expand_rows(jnp.where(leftc, -xi, -xr)))

        def right_form(xr, xi):
            return u1 * expand_rows(xr) + u2 * expand_rows(jnp.where(leftc, -xi, xi))

        if d == 0:
            al = left_form(pr, pi)
            brt = right_form(nr, ni)
            rrt = right_form(*cmul(lmr, lmi, nr, ni))
            qq = left_form(*cmul(l1r, l1i, pr, pi))
            mat = jnp.where(causal, _dot3(al, brt, NT), 0.0)
        else:
            al = left_form(nr, ni)
            brt = right_form(pr, pi)
            rrt = brt
            qq = left_form(*cmul(llr, lli, nr, ni))
            mat = jnp.where(anticausal, _dot3(al, brt, NT), 0.0)
        return mat, rrt.T, qq, to_col(llr), to_col(lli)

    def scan(sr, si, lr, li, h0r, h0i, reverse, nchunk, nbatch, sel):
        bk = sr.shape[1]
        lane = lax.broadcasted_iota(I32, (p, bk), 1)
        kidx = lane & (nchunk - 1)
        edge = (nchunk - 1) if reverse else 0
        if h0r is not None:
            h0cr = jnp.zeros((p, bk), F32)
            h0ci = jnp.zeros((p, bk), F32)
            for b in range(nbatch):
                at = lane == (b * nchunk + edge)
                h0cr = jnp.where(at, h0r[:, b:b + 1], h0cr)
                h0ci = jnp.where(at, h0i[:, b:b + 1], h0ci)
            ar, ai = cmul(lr, li, h0cr, h0ci)
            er, ei = sr + ar, si + ai
        else:
            er, ei = sr, si
        ar, ai = lr, li
        s = 1
        while s < nchunk:
            if reverse:
                ok = kidx < nchunk - s
                tr, ti = pltpu.roll(er, bk - s, 1), pltpu.roll(ei, bk - s, 1)
            else:
                ok = kidx >= s
                tr, ti = pltpu.roll(er, s, 1), pltpu.roll(ei, s, 1)
            tr = jnp.where(ok, tr, 0.0)
            ti = jnp.where(ok, ti, 0.0)
            dr, di = cmul(ar, ai, tr, ti)
            er, ei = er + dr, ei + di
            ar, ai = cmul(ar, ai, ar, ai)
            s *= 2
        if reverse:
            inner = kidx < nchunk - 1
            hr, hi = pltpu.roll(er, bk - 1, 1), pltpu.roll(ei, bk - 1, 1)
        else:
            inner = kidx >= 1
            hr, hi = pltpu.roll(er, 1, 1), pltpu.roll(ei, 1, 1)
        hr = jnp.where(inner, hr, h0cr if h0r is not None else 0.0)
        hi = jnp.where(inner, hi, h0ci if h0r is not None else 0.0)
        fin = None if sel is None else (_dot_sel(er, sel), _dot_sel(ei, sel))
        return hr, hi, fin

    def final_selectors(bk, nchunk, nbatch):
        col = lax.broadcasted_iota(I32, (bk, nbatch), 0)
        bat = lax.broadcasted_iota(I32, (bk, nbatch), 1)
        last = jnp.where(col == bat * nchunk + (nchunk - 1), 1.0, 0.0).astype(BF16)
        first = jnp.where(col == bat * nchunk, 1.0, 0.0).astype(BF16)
        return last, first

    sel_last, sel_first = final_selectors(utp_ref.shape[2], kp, nbp)

    for gg in range(SSM_GB):
        mf, rf, qf, lfr, lfi = operands(gg, 0)
        mb, rb, qb, lbr, lbi = operands(gg, 1)
        skip = jnp.where(diag, jnp.broadcast_to(dsk_ref[gg], (rows, rows)), 0.0)
        stack = jnp.concatenate([mf + mb + skip, rf, rb], axis=0)
        qq = jnp.concatenate([qf, qb], axis=1)
        h0 = h0_ref[gg]
        for ut_ref, yt_ref, nchunk, nbatch, latent in ((utp_ref, ytp_ref, kp, nbp, False),
                                                       (uts_ref, yts_ref, ks, nbs, True)):
            bk = ut_ref.shape[2]
            x = ut_ref[:, gg * SSM_GROUP:(gg + 1) * SSM_GROUP, :].reshape(rows, bk)
            res = _mm(stack, x, SSM_PASSES)
            hfr, hfi, ff = scan(res[rows:rows + p], res[rows + p:rows + 2 * p], lfr, lfi,
                                h0[0] if latent else None, h0[1] if latent else None,
                                False, nchunk, nbatch, None if latent else sel_last)
            hbr, hbi, fb = scan(res[rows + 2 * p:rows + 3 * p], res[rows + 3 * p:rows + 4 * p], lbr, lbi,
                                h0[2] if latent else None, h0[3] if latent else None,
                                True, nchunk, nbatch, None if latent else sel_first)
            states = jnp.concatenate([hfr, hfi, hbr, hbi], axis=0)
            y = res[:rows] + _mm(qq, states, SSM_PASSES)
            yt_ref[:, gg * SSM_GROUP:(gg + 1) * SSM_GROUP, :] = y.reshape(lc, SSM_GROUP, bk)
            if not latent:
                fs_ref[gg, 0] = ff[0]
                fs_ref[gg, 1] = ff[1]
                fs_ref[gg, 2] = fb[0]
                fs_ref[gg, 3] = fb[1]


def _ssm_core(utp, uts, ops, h0, *, kp, ks, nbp, nbs):
    lamp, c2, bt2, dsk = ops
    l, d, bkp = utp.shape
    bks = uts.shape[2]
    g = SSM_GROUPS
    gb = SSM_GB
    lead4 = lambda i: (i, 0, 0, 0)
    lead5 = lambda i: (i, 0, 0, 0, 0)
    ut_spec = lambda bk: pl.BlockSpec((l, gb * SSM_GROUP, bk), lambda i: (0, i, 0))
    return pl.pallas_call(
        functools.partial(_ssm_core_kernel, kp=kp, ks=ks, nbp=nbp, nbs=nbs),
        grid=(g // gb,),
        in_specs=[ut_spec(bkp), ut_spec(bks),
                  pl.BlockSpec((gb, 2, 8, 128), lead4),
                  pl.BlockSpec((gb, 2, 2, SSM_GROUP, 128), lead5),
                  pl.BlockSpec((gb, 2, 2, SSM_GROUP, 128), lead5),
                  pl.BlockSpec((gb, 1, SSM_ROWS), lambda i: (i, 0, 0)),
                  pl.BlockSpec((gb, 4, SSM_STATE, nbs), lead4)],
        out_specs=[ut_spec(bkp), ut_spec(bks),
                   pl.BlockSpec((gb, 4, SSM_STATE, nbp), lead4)],
        out_shape=[jax.ShapeDtypeStruct((l, d, bkp), F32),
                   jax.ShapeDtypeStruct((l, d, bks), F32),
                   jax.ShapeDtypeStruct((g, 4, SSM_STATE, nbp), F32)],
        compiler_params=_params("parallel"),
        name="ssm_core",
    )(utp, uts, lamp, c2, bt2, dsk, h0)


def _ssm_out_kernel(yt_ref, w_ref, m_ref):
    y = yt_ref[0].T
    act = 0.5 * y * (1.0 + jnp.tanh(0.7978845608028654 * (y + 0.044715 * (y * y * y))))
    ag = _dot(act.astype(BF16), w_ref[...])
    d = m_ref.shape[2]
    m_ref[0] = ag[:, :d] * _sigmoid(ag[:, d:])


def _ssm_out(yt, w_bf, tc):
    l, d, bk = yt.shape
    return pl.pallas_call(
        _ssm_out_kernel,
        grid=(l, bk // tc),
        in_specs=[pl.BlockSpec((1, d, tc), lambda j, i: (j, 0, i)),
                  pl.BlockSpec((d, 2 * d), lambda j, i: (0, 0))],
        out_specs=pl.BlockSpec((1, tc, d), lambda j, i: (j, i, 0)),
        out_shape=jax.ShapeDtypeStruct((l, bk, d), F32),
        compiler_params=_params("parallel", "parallel"),
        name="ssm_out",
    )(yt, w_bf)


def _ssm_operand_params(lam_re, lam_im, b_re, b_im, c_re, c_im, log_dt, d_skip):
    g, l = SSM_GROUPS, SSM_CHUNK
    dup = lambda t: jnp.concatenate([t, t], axis=-1)
    lamp = jnp.stack([lam_re, lam_im, jnp.broadcast_to(log_dt[..., None], lam_re.shape)], axis=2)
    lamp = dup(jnp.pad(lamp, ((0, 0), (0, 0), (0, 5), (0, 0)))).transpose(1, 0, 2, 3)
    c2 = dup(jnp.stack([c_re, c_im], axis=2)).transpose(1, 0, 2, 3, 4)
    bt2 = dup(jnp.stack([jnp.swapaxes(b_re, -1, -2), jnp.swapaxes(b_im, -1, -2)], axis=2)).transpose(1, 0, 2, 3, 4)
    dsk = jnp.tile(d_skip.reshape(g, 1, SSM_GROUP), (1, 1, l))
    return lamp, c2, bt2, dsk


def _to_chunks(x, nb, seq):
    k = seq // SSM_CHUNK
    return x.reshape(nb, k, SSM_CHUNK, -1).transpose(2, 0, 1, 3).reshape(SSM_CHUNK, nb * k, -1)


def _from_chunks(x, nb, seq):
    k = seq // SSM_CHUNK
    return x.reshape(SSM_CHUNK, nb, k, -1).transpose(1, 2, 0, 3).reshape(nb * seq, -1)


def _ssm_mixers(xp, xs, mod, gain, wt_bf, ops, w_out_bf, h0, *, nbp, sp, nbs, ss):
    kp, ks = sp // SSM_CHUNK, ss // SSM_CHUNK
    tcp, tcs = min(SSM_TC, nbp * kp), min(SSM_TC, nbs * ks)
    utp = _ssm_in(_to_chunks(xp, nbp, sp), mod, gain, wt_bf, cols_per_cond=None, tc=tcp)
    uts = _ssm_in(_to_chunks(xs, nbs, ss), mod, gain, wt_bf, cols_per_cond=ks, tc=tcs)
    ytp, yts, fs = _ssm_core(utp, uts, ops, h0, kp=kp, ks=ks, nbp=nbp, nbs=nbs)
    mp = _from_chunks(_ssm_out(ytp, w_out_bf, tcp), nbp, sp)
    ms = _from_chunks(_ssm_out(yts, w_out_bf, tcs), nbs, ss)
    return mp, ms, fs


def _rope_tables(seq):
    t = jnp.arange(seq)
    row = (t // GRID_W).astype(F32)
    col = (t % GRID_W).astype(F32)
    n_freq = HEAD_DIM // 4
    inv_freq = ROPE_THETA ** (-jnp.arange(n_freq, dtype=F32) / n_freq)
    ang = jnp.concatenate([row[:, None] * inv_freq, col[:, None] * inv_freq], axis=-1)
    cos = jnp.repeat(jnp.cos(ang), 2, axis=-1)
    sin = jnp.repeat(jnp.sin(ang), 2, axis=-1)
    sign = jnp.tile(jnp.array([-1.0, 1.0], F32), HEAD_DIM // 2)
    return jnp.tile(cos, (1, 4)), jnp.tile(sin * sign, (1, 4))


def _head_gains(qn_a, kn_a, qn_b, kn_b):
    scale = HEAD_DIM ** -0.5 * LOG2E
    ones = jnp.ones((N_KV * HEAD_DIM,), F32)
    return jnp.concatenate([jnp.tile(qn_a, N_HEADS) * scale, jnp.tile(kn_a, N_KV), ones,
                            jnp.tile(qn_b, N_HEADS) * scale, jnp.tile(kn_b, N_KV), ones]).reshape(1, QKV_COLS)


def kernel(x_prompt, x_sample, c, cache_k_a_l0, cache_v_a_l0, cache_k_b_l0, cache_v_b_l0, state_ssm_re_l1, state_ssm_im_l1, c_ctx, mod_w_l0, mod_b_l0, norm_mix_l0, attn_w_in_l0, q_norm_a_l0, k_norm_a_l0, q_norm_b_l0, k_norm_b_l0, sink_b_l0, attn_w_out_l0, norm_ffn_l0, router_l0, moe_w_gate_l0, moe_w_up_l0, moe_w_down_l0, mod_w_l1, mod_b_l1, norm_mix_l1, ssm_w_in_l1, ssm_lambda_re_l1, ssm_lambda_im_l1, ssm_b_re_l1, ssm_b_im_l1, ssm_c_re_l1, ssm_c_im_l1, ssm_log_dt_l1, ssm_d_l1, ssm_w_out_l1, norm_ffn_l1, router_l1, moe_w_gate_l1, moe_w_up_l1, moe_w_down_l1):
    bp, sp, d = x_prompt.shape
    bs, ss, _ = x_sample.shape
    past = cache_k_a_l0.shape[1]
    assert d == D_MODEL and bs <= 7 and (bp * sp) % TM == 0 and TM % sp == 0 and ss % TM == 0
    xp = x_prompt.reshape(bp * sp, d)
    xs = x_sample.reshape(bs * ss, d)
    cond8 = jnp.concatenate([c_ctx[None], c, jnp.zeros((7 - bs, d), F32)], axis=0)
    row1 = lambda v: v.reshape(1, -1)

    mod0 = _mod_rows(cond8, mod_w_l0, mod_b_l0)
    w_in = attn_w_in_l0.astype(BF16)
    hgain = _head_gains(q_norm_a_l0, k_norm_a_l0, q_norm_b_l0, k_norm_b_l0)
    lane = np.arange(256)
    bd = jnp.asarray((lane[:, None] // HEAD_DIM == lane[None, :] // HEAD_DIM) / HEAD_DIM, BF16)
    cos_t, sin_t = _rope_tables(ss)
    qp, kap, vap, kbp, vbp = _qkv(xp, mod0, row1(norm_mix_l0), w_in, hgain, bd, cos_t, sin_t,
                                  rows_per_cond=None, seq=sp, rope=False, kv_dtype=F32, transposed_kv=True)
    qs, kas, vas, kbs, vbs = _qkv(xs, mod0, row1(norm_mix_l0), w_in, hgain, bd, cos_t, sin_t,
                                  rows_per_cond=ss, seq=ss, rope=True, kv_dtype=BF16, transposed_kv=False)
    op = _attn_ctx(sink_b_l0, qp, kap, vap, kbp, vbp, sp)
    cache = lambda t: t.reshape(bs, past, N_KV * HEAD_DIM)
    os_ = _attn_lat(sink_b_l0, qs, kas, vas, kbs, vbs, cache(cache_k_a_l0), cache(cache_v_a_l0),
                    cache(cache_k_b_l0), cache(cache_v_b_l0), ss)
    w_out = attn_w_out_l0.astype(BF16)
    x1p, hp, affp = _postmix(op, xp, mod0, w_out, row1(norm_ffn_l0), router_l0.T, rows_per_cond=None, project=True)
    x1s, hs, affs = _postmix(os_, xs, mod0, w_out, row1(norm_ffn_l0), router_l0.T, rows_per_cond=ss, project=True)
    xp, xs = _moe_pair(hp, affp, x1p, hs, affs, x1s, mod0, moe_w_gate_l0, moe_w_up_l0, moe_w_down_l0, sp, ss)

    mod1 = _mod_rows(cond8, mod_w_l1, mod_b_l1)
    ops = _ssm_operand_params(ssm_lambda_re_l1, ssm_lambda_im_l1, ssm_b_re_l1, ssm_b_im_l1, ssm_c_re_l1, ssm_c_im_l1,
                              ssm_log_dt_l1, ssm_d_l1)
    wt = ssm_w_in_l1.T.astype(BF16)
    w_so = ssm_w_out_l1.astype(BF16)
    h0 = jnp.stack([state_ssm_re_l1[:, 0], state_ssm_im_l1[:, 0], state_ssm_re_l1[:, 1], state_ssm_im_l1[:, 1]],
                   axis=0).transpose(2, 0, 3, 1)
    mp, ms, fsp = _ssm_mixers(xp, xs, mod1, row1(norm_mix_l1), wt, ops, w_so, h0, nbp=bp, sp=sp, nbs=bs, ss=ss)
    dummy_w = jnp.zeros((8, 128), BF16)
    x1p, hp, affp = _postmix(mp, xp, mod1, dummy_w, row1(norm_ffn_l1), router_l1.T, rows_per_cond=None, project=False)
    x1s, hs, affs = _postmix(ms, xs, mod1, dummy_w, row1(norm_ffn_l1), router_l1.T, rows_per_cond=ss, project=False)
    xp, xs = _moe_pair(hp, affp, x1p, hs, affs, x1s, mod1, moe_w_gate_l1, moe_w_up_l1, moe_w_down_l1, sp, ss)

    kv_out = lambda t: t.reshape(bp, N_KV, HEAD_DIM, sp).transpose(0, 3, 1, 2)
    fin = fsp.transpose(3, 1, 0, 2)
    ssm_re = jnp.stack([fin[:, 0], fin[:, 2]], axis=1)
    ssm_im = jnp.stack([fin[:, 1], fin[:, 3]], axis=1)
    return (xp.reshape(bp, sp, d), xs.reshape(bs, ss, d), kv_out(kap), kv_out(vap), kv_out(kbp), kv_out(vbp),
            ssm_re, ssm_im)
```

```python
import functools
import itertools

import jax
import jax.numpy as jnp
import numpy as np
from jax import lax
from jax.experimental import pallas as pl
from jax.experimental.pallas import tpu as pltpu

F32, BF16, I32 = jnp.float32, jnp.bfloat16, jnp.int32

D_MODEL = 1024
GRID_W = 64
HEAD_DIM = 64
N_HEADS = 8
N_KV = 2
WINDOW = 128
ROPE_THETA = 10000.0
SSM_GROUP = 16
SSM_GROUPS = D_MODEL // SSM_GROUP
SSM_STATE = 64
N_EXPERTS = 16
EC_FACTOR = 2
D_FF = 2 * D_MODEL
EPS = 1e-6
NEG_INF = -1e30
LOG2E = 1.4426950408889634
QKV_COLS = 2 * (N_HEADS + 2 * N_KV) * HEAD_DIM
ATTN_OUT = 2 * N_HEADS * HEAD_DIM

SSM_CHUNK = 16
SSM_ROWS = SSM_CHUNK * SSM_GROUP
SSM_GB = 4
SSM_PASSES = 1

SSM_TC = 512

TM = 512

MOE_TT = 256
MOE_W = 64
MOE_EG = 4
MOE_RB = 2
VMEM_LIMIT = 56 * 1024 * 1024

NN = (((1,), (0,)), ((), ()))
NT = (((1,), (1,)), ((), ()))


def _dot(a, b, dims=NN):
    return lax.dot_general(a, b, dims, preferred_element_type=F32)


def _split2(x):
    hi = x.astype(BF16)
    lo = (x - hi.astype(F32)).astype(BF16)
    return hi, lo


def _split3(x):
    hi = x.astype(BF16)
    r = x - hi.astype(F32)
    mid = r.astype(BF16)
    lo = (r - mid.astype(F32)).astype(BF16)
    return hi, mid, lo


def _dot3(a, b, dims=NN):
    ah, al = _split2(a)
    bh, bl = _split2(b)
    return _dot(ah, bh, dims) + (_dot(ah, bl, dims) + _dot(al, bh, dims))


def _mm(a, b, passes):
    if passes == 1:
        return _dot(a.astype(BF16), b.astype(BF16))
    return _dot3(a, b)


def _dot_sel(x, sel):
    hi, mid, lo = _split3(x)
    return _dot(hi, sel) + (_dot(mid, sel) + _dot(lo, sel))


def _sigmoid(x):
    return 1.0 / (1.0 + jnp.exp(-x))


def _norm_mod(x, gain, shift, scale):
    ms = jnp.mean(x * x, axis=-1, keepdims=True)
    y = x * lax.rsqrt(ms + EPS) * gain
    return y * (1.0 + scale) + shift


def _params(*sem):
    return pltpu.CompilerParams(dimension_semantics=sem, vmem_limit_bytes=VMEM_LIMIT)


def _adaln_kernel(c_ref, w_ref, b_ref, o_ref):
    c = c_ref[...]
    s = c * _sigmoid(c)
    o_ref[...] = _dot3(s, w_ref[...]) + b_ref[...]


def _adaln(cond8, w_mod, b_mod):
    d, e = w_mod.shape
    tn = 1536
    return pl.pallas_call(
        _adaln_kernel,
        grid=(e // tn,),
        in_specs=[pl.BlockSpec((8, d), lambda j: (0, 0)),
                  pl.BlockSpec((d, tn), lambda j: (0, j)),
                  pl.BlockSpec((1, tn), lambda j: (0, j))],
        out_specs=pl.BlockSpec((8, tn), lambda j: (0, j)),
        out_shape=jax.ShapeDtypeStruct((8, e), F32),
        compiler_params=_params("parallel"),
        name="adaln",
    )(cond8, w_mod, b_mod.reshape(1, e))


def _mod_rows(cond8, w_mod, b_mod):
    m = _adaln(cond8, w_mod, b_mod).reshape(8, 6, D_MODEL)
    return jnp.pad(m, ((0, 0), (0, 2), (0, 0)))


def _mod_spec(rows_per_cond):
    if rows_per_cond is None:
        return pl.BlockSpec((1, 8, D_MODEL), lambda i: (0, 0, 0))
    return pl.BlockSpec((1, 8, D_MODEL), lambda i: (1 + (i * TM) // rows_per_cond, 0, 0))


def _qkv_kernel(x_ref, mod_ref, gain_ref, w_ref, hg_ref, bd_ref, cos_ref, sin_ref,
                q_ref, ka_ref, va_ref, kb_ref, vb_ref, *, rope, transposed_kv):
    h = _norm_mod(x_ref[...], gain_ref[...], mod_ref[0, 0:1, :], mod_ref[0, 1:2, :])
    proj = _dot(h.astype(BF16), w_ref[...])
    bd = bd_ref[...]

    def head_norm(blk, g):
        hi, lo = _split2(blk * blk)
        ms = _dot(hi, bd) + _dot(lo, bd)
        return blk * lax.rsqrt(ms + EPS) * g

    def rotary(blk):
        w = blk.shape[1]
        even = (lax.broadcasted_iota(I32, blk.shape, 1) & 1) == 0
        swapped = jnp.where(even, pltpu.roll(blk, w - 1, 1), pltpu.roll(blk, 1, 1))
        return blk * cos_ref[:, :w] + swapped * sin_ref[:, :w]

    def qk(c0):
        blk = head_norm(proj[:, c0:c0 + 256], hg_ref[:, c0:c0 + 256])
        return rotary(blk) if rope else blk

    q_ref[:, 0:256] = qk(0).astype(q_ref.dtype)
    q_ref[:, 256:512] = qk(256).astype(q_ref.dtype)
    q_ref[:, 512:768] = qk(768).astype(q_ref.dtype)
    q_ref[:, 768:1024] = qk(1024).astype(q_ref.dtype)
    kva = qk(512)
    kvb = qk(1280)
    outs = ((ka_ref, kva[:, :128]), (va_ref, proj[:, 640:768]), (kb_ref, kvb[:, :128]), (vb_ref, proj[:, 1408:1536]))
    for ref, val in outs:
        if transposed_kv:
            seq = ref.shape[2]
            for r in range(ref.shape[0]):
                ref[r] = val[r * seq:(r + 1) * seq].T.astype(ref.dtype)
        else:
            ref[...] = val.astype(ref.dtype)


def _qkv(x, mod, gain, w_bf, hgain, bd, cos_t, sin_t, *, rows_per_cond, seq, rope, kv_dtype, transposed_kv):
    n = x.shape[0]
    tiles_per_seq = max(1, seq // TM)
    row = lambda i: (i, 0)
    const = lambda i: (0, 0)
    pos = lambda i: (i % tiles_per_seq, 0)
    if transposed_kv:
        assert TM % seq == 0
        kv_shape = jax.ShapeDtypeStruct((n // seq, 128, seq), kv_dtype)
        kv_spec = pl.BlockSpec((TM // seq, 128, seq), lambda i: (i, 0, 0))
    else:
        kv_shape = jax.ShapeDtypeStruct((n, 128), kv_dtype)
        kv_spec = pl.BlockSpec((TM, 128), row)
    return pl.pallas_call(
        functools.partial(_qkv_kernel, rope=rope, transposed_kv=transposed_kv),
        grid=(n // TM,),
        in_specs=[pl.BlockSpec((TM, D_MODEL), row),
                  _mod_spec(rows_per_cond),
                  pl.BlockSpec((1, D_MODEL), const),
                  pl.BlockSpec((D_MODEL, QKV_COLS), const),
                  pl.BlockSpec((1, QKV_COLS), const),
                  pl.BlockSpec((256, 256), const),
                  pl.BlockSpec((TM, 256), pos),
                  pl.BlockSpec((TM, 256), pos)],
        out_specs=[pl.BlockSpec((TM, ATTN_OUT), row)] + [kv_spec] * 4,
        out_shape=[jax.ShapeDtypeStruct((n, ATTN_OUT), BF16)] + [kv_shape] * 4,
        compiler_params=_params("parallel"),
        name="qkv_rope" if rope else "qkv",
    )(x, mod, gain, w_bf, hgain, bd, cos_t, sin_t)


def _pad_variants(kk, ones=False):
    lane = lax.broadcasted_iota(I32, kk.shape, 1)
    left = lane < HEAD_DIM
    rolled = pltpu.roll(kk, HEAD_DIM, 1)
    fill_r = jnp.where(lane == HEAD_DIM, 1.0, 0.0) if ones else jnp.zeros_like(kk)
    fill_l = jnp.where(lane == 0, 1.0, 0.0) if ones else jnp.zeros_like(kk)
    return {(0, 0): jnp.where(left, kk, fill_r).astype(BF16),
            (0, 1): jnp.where(left, fill_l, rolled).astype(BF16),
            (1, 0): jnp.where(left, rolled, fill_r).astype(BF16),
            (1, 1): jnp.where(left, fill_l, kk).astype(BF16)}


def _pad_variants_t(kt):
    top = lax.broadcasted_iota(I32, kt.shape, 0) < HEAD_DIM
    zero = jnp.zeros((HEAD_DIM, kt.shape[1]), F32)
    return {(0, 0): jnp.where(top, kt, 0.0).astype(BF16),
            (0, 1): jnp.concatenate([zero, kt[:HEAD_DIM]], axis=0).astype(BF16),
            (1, 0): jnp.concatenate([kt[HEAD_DIM:], zero], axis=0).astype(BF16),
            (1, 1): jnp.where(top, 0.0, kt).astype(BF16)}


def _head_attention_small(qp, kblk, vblk, sink):
    s = _dot(qp, kblk, NN)
    m = s.max(axis=-1, keepdims=True)
    if sink is not None:
        m = jnp.maximum(m, sink)
    p = jnp.exp2(s - m)
    den = p.sum(axis=-1, keepdims=True)
    if sink is not None:
        den = den + jnp.exp2(sink - m)
    return _dot(p.astype(BF16), vblk, NT) / den


def _head_attention(qp, keys, vals, masks, sink, par):
    scores = []
    for kblk, mask in zip(keys, masks):
        s = _dot(qp, kblk, NT)
        if mask is not None:
            s = jnp.where(mask, s, NEG_INF)
        scores.append(s)
    m = scores[0].max(axis=-1, keepdims=True)
    for s in scores[1:]:
        m = jnp.maximum(m, s.max(axis=-1, keepdims=True))
    if sink is not None:
        m = jnp.maximum(m, sink)
    out = None
    for s, vblk in zip(scores, vals):
        o = _dot(jnp.exp2((s - m).astype(BF16)), vblk)
        out = o if out is None else out + o
    ones_lane = HEAD_DIM if par == 0 else 0
    den = out[:, ones_lane:ones_lane + 1]
    if sink is not None:
        den = den + jnp.exp2(sink - m)
    own = (lax.broadcasted_iota(I32, out.shape, 1) < HEAD_DIM) == (par == 0)
    return jnp.where(own, out / den, 0.0)


def _attn_ctx_kernel(sink_ref, q_ref, ka_ref, va_ref, kb_ref, vb_ref, o_ref):
    for mixer, (k_ref, v_ref) in enumerate(((ka_ref, va_ref), (kb_ref, vb_ref))):
        kvar = _pad_variants_t(k_ref[0])
        vvar = _pad_variants_t(v_ref[0])
        for t in range(4):
            tile = mixer * 4 + t
            kv = t // 2
            qp = q_ref[:, tile * 128:(tile + 1) * 128]
            acc = None
            for par in range(2):
                sink = sink_ref[2 * t + par] * LOG2E if mixer == 1 else None
                o = _head_attention_small(qp, kvar[(kv, par)], vvar[(kv, par)], sink)
                acc = o if acc is None else acc + o
            o_ref[:, tile * 128:(tile + 1) * 128] = acc.astype(o_ref.dtype)


def _attn_ctx(sink, q, ka, va, kb, vb, seq):
    n = q.shape[0]
    row = lambda b: (b, 0)
    kv_spec = pl.BlockSpec((1, 128, seq), lambda b: (b, 0, 0))
    return pl.pallas_call(
        _attn_ctx_kernel,
        grid=(n // seq,),
        in_specs=[pl.BlockSpec(memory_space=pltpu.SMEM),
                  pl.BlockSpec((seq, ATTN_OUT), row), kv_spec, kv_spec, kv_spec, kv_spec],
        out_specs=pl.BlockSpec((seq, ATTN_OUT), row),
        out_shape=jax.ShapeDtypeStruct((n, ATTN_OUT), BF16),
        compiler_params=_params("parallel"),
        name="attn_ctx",
    )(sink, q, ka, va, kb, vb)


def _attn_lat_kernel(sink_ref, q_ref, ka_ref, va_ref, kb_ref, vb_ref,
                     cka_ref, cva_ref, ckb_ref, cvb_ref, o_ref, *, tq, seq):
    qi = pl.program_id(1)
    span = tq + 2 * WINDOW
    ck = _pad_variants(cka_ref[0])
    cv = _pad_variants(cva_ref[0], ones=True)
    lk = _pad_variants(ka_ref[...].astype(F32))
    lv = _pad_variants(va_ref[...].astype(F32), ones=True)
    for t in range(4):
        kv = t // 2
        qp = q_ref[:, t * 128:(t + 1) * 128]
        acc = None
        for par in range(2):
            o = _head_attention(qp, [ck[(kv, par)], lk[(kv, par)]], [cv[(kv, par)], lv[(kv, par)]],
                                [None, None], None, par)
            acc = o if acc is None else acc + o
        o_ref[:, t * 128:(t + 1) * 128] = acc.astype(o_ref.dtype)
    lo = jnp.clip(qi * tq - WINDOW, 0, seq - span)
    lo = pl.multiple_of(lo, 128)
    qpos = qi * tq + lax.broadcasted_iota(I32, (tq, span), 0)
    kpos = lo + lax.broadcasted_iota(I32, (tq, span), 1)
    band = jnp.abs(qpos - kpos) <= WINDOW
    ck = _pad_variants(ckb_ref[0])
    cv = _pad_variants(cvb_ref[0], ones=True)
    lk = _pad_variants(kb_ref[pl.ds(lo, span), :].astype(F32))
    lv = _pad_variants(vb_ref[pl.ds(lo, span), :].astype(F32), ones=True)
    for t in range(4):
        kv = t // 2
        tile = 4 + t
        qp = q_ref[:, tile * 128:(tile + 1) * 128]
        acc = None
        for par in range(2):
            sink = sink_ref[2 * t + par] * LOG2E
            o = _head_attention(qp, [lk[(kv, par)], ck[(kv, par)]], [lv[(kv, par)], cv[(kv, par)]],
                                [band, None], sink, par)
            acc = o if acc is None else acc + o
        o_ref[:, tile * 128:(tile + 1) * 128] = acc.astype(o_ref.dtype)


def _attn_lat(sink, q, ka, va, kb, vb, cka, cva, ckb, cvb, seq, tq=256):
    n = q.shape[0]
    nb = n // seq
    nq = seq // tq
    qrow = lambda b, i: (b * nq + i, 0)
    brow = lambda b, i: (b, 0)
    kv_spec = pl.BlockSpec((seq, 128), brow)
    past = cka.shape[1]
    c_spec = pl.BlockSpec((1, past, 128), lambda b, i: (b, 0, 0))
    return pl.pallas_call(
        functools.partial(_attn_lat_kernel, tq=tq, seq=seq),
        grid=(nb, nq),
        in_specs=[pl.BlockSpec(memory_space=pltpu.SMEM),
                  pl.BlockSpec((tq, ATTN_OUT), qrow), kv_spec, kv_spec, kv_spec, kv_spec,
                  c_spec, c_spec, c_spec, c_spec],
        out_specs=pl.BlockSpec((tq, ATTN_OUT), qrow),
        out_shape=jax.ShapeDtypeStruct((n, ATTN_OUT), BF16),
        compiler_params=_params("parallel", "parallel"),
        name="attn_lat",
    )(sink, q, ka, va, kb, vb, cka, cva, ckb, cvb)


def _postmix_kernel(m_ref, x_ref, mod_ref, w_ref, gain_ref, rt_ref, x1_ref, h2_ref, aff_ref, *, project):
    if project:
        m = _dot(m_ref[...], w_ref[...])
    else:
        m = m_ref[...]
    x1 = x_ref[...] + mod_ref[0, 2:3, :] * m
    x1_ref[...] = x1
    h2 = _norm_mod(x1, gain_ref[...], mod_ref[0, 3:4, :], mod_ref[0, 4:5, :])
    h2_ref[...] = h2.astype(h2_ref.dtype)
    logits = _dot3(rt_ref[...], h2, NT)
    e = jnp.exp(logits - logits.max(axis=0, keepdims=True))
    aff_ref[...] = e / e.sum(axis=0, keepdims=True)


def _postmix(m, x, mod, w_bf, gain, router_t, *, rows_per_cond, project):
    n = x.shape[0]
    row = lambda i: (i, 0)
    const = lambda i: (0, 0)
    return pl.pallas_call(
        functools.partial(_postmix_kernel, project=project),
        grid=(n // TM,),
        in_specs=[pl.BlockSpec((TM, D_MODEL), row),
                  pl.BlockSpec((TM, D_MODEL), row),
                  _mod_spec(rows_per_cond),
                  pl.BlockSpec(w_bf.shape, const),
                  pl.BlockSpec((1, D_MODEL), const),
                  pl.BlockSpec((N_EXPERTS, D_MODEL), const)],
        out_specs=[pl.BlockSpec((TM, D_MODEL), row),
                   pl.BlockSpec((TM, D_MODEL), row),
                   pl.BlockSpec((N_EXPERTS, TM), lambda i: (0, i))],
        out_shape=[jax.ShapeDtypeStruct((n, D_MODEL), F32),
                   jax.ShapeDtypeStruct((n, D_MODEL), BF16),
                   jax.ShapeDtypeStruct((N_EXPERTS, n), F32)],
        compiler_params=_params("parallel"),
        name="postmix_proj" if project else "postmix",
    )(m, x, mod, w_bf, gain, router_t)


def _route_kernel(aff_ref, slot_ref, gate_ref, pos_ref, tcnt_ref, *, seq, cap, nseg, tt):
    aff = jnp.concatenate([aff_ref[:, s * seq:(s + 1) * seq] for s in range(nseg)], axis=0)
    rows = aff.shape[0]
    capf = jnp.float32(cap)
    thr_bits = jnp.zeros((rows, 1), I32)
    for bit in range(30, -1, -1):
        cand = thr_bits | (1 << bit)
        cnt = jnp.where(aff >= pltpu.bitcast(cand, F32), 1.0, 0.0).sum(axis=1, keepdims=True)
        thr_bits = jnp.where(cnt >= capf, cand, thr_bits)
    thr = pltpu.bitcast(thr_bits, F32)
    gt = aff > thr
    eq = aff == thr
    n_gt = jnp.where(gt, 1.0, 0.0).sum(axis=1, keepdims=True)
    pw = min(seq, 256)
    tri = jnp.where(lax.broadcasted_iota(I32, (pw, pw), 0) < lax.broadcasted_iota(I32, (pw, pw), 1),
                    1.0, 0.0).astype(BF16)

    def count_before(flag):
        ones = jnp.where(flag, 1.0, 0.0)
        parts = []
        run = jnp.zeros((rows, 1), F32)
        for c0 in range(0, seq, pw):
            blk = ones[:, c0:c0 + pw]
            parts.append(_dot(blk.astype(BF16), tri) + run)
            run = run + blk.sum(axis=1, keepdims=True)
        return jnp.concatenate(parts, axis=1) if len(parts) > 1 else parts[0]

    sel = gt | (eq & (count_before(eq) < capf - n_gt))
    rank = count_before(sel)
    expert = lax.broadcasted_iota(I32, (rows, seq), 0) & (N_EXPERTS - 1)
    slot = jnp.where(sel, expert * cap + rank.astype(I32), -1)
    gate = jnp.where(sel, aff, 0.0)
    pos = jnp.where(sel, rank, -1.0)
    nt = seq // tt
    tile_of = jnp.where((lax.broadcasted_iota(I32, (seq, nt), 0) // tt) == lax.broadcasted_iota(I32, (seq, nt), 1),
                        1.0, 0.0).astype(BF16)
    tcnt = _dot(jnp.where(sel, 1.0, 0.0).astype(BF16), tile_of)
    for s in range(nseg):
        rows_s = slice(s * N_EXPERTS, (s + 1) * N_EXPERTS)
        slot_ref[:, s * seq:(s + 1) * seq] = slot[rows_s, :]
        gate_ref[:, s * seq:(s + 1) * seq] = gate[rows_s, :]
        pos_ref[:, s * seq:(s + 1) * seq] = pos[rows_s, :]
        tcnt_ref[0, :, s * nt:(s + 1) * nt] = tcnt[rows_s, :]


def _route(aff_t, seq, cap, nseg, tt):
    n = aff_t.shape[1]
    nt = seq // tt
    steps = n // (nseg * seq)
    spec = pl.BlockSpec((N_EXPERTS, nseg * seq), lambda i: (0, i))
    slot, gate, pos, tcnt = pl.pallas_call(
        functools.partial(_route_kernel, seq=seq, cap=cap, nseg=nseg, tt=tt),
        grid=(steps,),
        in_specs=[spec],
        out_specs=[spec, spec, spec, pl.BlockSpec((1, N_EXPERTS, nseg * nt), lambda i: (i, 0, 0))],
        out_shape=[jax.ShapeDtypeStruct((N_EXPERTS, n), I32), jax.ShapeDtypeStruct((N_EXPERTS, n), F32),
                   jax.ShapeDtypeStruct((N_EXPERTS, n), F32),
                   jax.ShapeDtypeStruct((steps, N_EXPERTS, nseg * nt), F32)],
        compiler_params=_params("parallel"),
        name="route",
    )(aff_t)
    return slot, gate, pos, tcnt.transpose(1, 0, 2).reshape(N_EXPERTS, n // tt)


def _dispatch_kernel(slot_ref, h_ref, x_ref, *, cap, seq):
    m = N_EXPERTS * cap
    slot_id = lax.broadcasted_iota(I32, (m, seq), 0)
    for r in range(h_ref.shape[0] // seq):
        slots = slot_ref[:, r * seq:(r + 1) * seq]
        owner = jnp.broadcast_to(slots[:, None, :], (N_EXPERTS, cap, seq)).reshape(m, seq)
        sel = jnp.where(owner == slot_id, 1.0, 0.0).astype(BF16)
        x_ref[r * m:(r + 1) * m, :] = _dot(sel, h_ref[r * seq:(r + 1) * seq, :]).astype(x_ref.dtype)


def _dispatch(slot, h, seq, cap):
    n = h.shape[0]
    nb = n // seq
    rb = MOE_RB if nb % MOE_RB == 0 else 1
    return pl.pallas_call(
        functools.partial(_dispatch_kernel, cap=cap, seq=seq),
        grid=(nb // rb,),
        in_specs=[pl.BlockSpec((N_EXPERTS, rb * seq), lambda b: (0, b)),
                  pl.BlockSpec((rb * seq, D_MODEL), lambda b: (b, 0))],
        out_specs=pl.BlockSpec((rb * N_EXPERTS * cap, D_MODEL), lambda b: (b, 0)),
        out_shape=jax.ShapeDtypeStruct((nb * N_EXPERTS * cap, D_MODEL), BF16),
        compiler_params=_params("parallel"),
        name="moe_dispatch",
    )(slot, h)


def _window(cum_ref, base, e, cap, k=0):
    lo = ((cum_ref[base + e] >> 4) << 4) + k * MOE_W
    return lo, pl.multiple_of(jnp.minimum(lo, cap - MOE_W), 16)


def _extra_windows(cum_ref, base, e):
    lo = (cum_ref[base + e] >> 4) << 4
    return (cum_ref[base + N_EXPERTS + e] - lo + (MOE_W - 1)) >> 6


def _dispatch_win_kernel(cum_ref, slot_ref, h_ref, x_ref, *, cap, nt):
    b, i = pl.program_id(0), pl.program_id(1)
    tt = h_ref.shape[0]
    base = (b * (nt + 1) + i) * N_EXPERTS
    h = h_ref[...]
    row = lax.broadcasted_iota(I32, (MOE_W, tt), 0)

    @pl.when(i == 0)
    def _():
        x_ref[...] = jnp.zeros_like(x_ref)

    def hits(e, lo, ws):
        srow = slot_ref[e:e + 1, :]
        return (srow == row + (e * cap + ws)) & (srow >= e * cap + lo)

    for grp in range(N_EXPERTS // MOE_EG):
        wins = [(e,) + _window(cum_ref, base, e, cap) for e in range(grp * MOE_EG, (grp + 1) * MOE_EG)]
        sel = jnp.concatenate([hits(e, lo, ws) for e, lo, ws in wins], axis=0)
        x = _dot(jnp.where(sel, 1.0, 0.0).astype(BF16), h)
        for q, (e, lo, ws) in enumerate(wins):
            dst = pl.ds(e * cap + ws, MOE_W)
            x_ref[dst, :] += x[q * MOE_W:(q + 1) * MOE_W].astype(x_ref.dtype)

    for e in range(N_EXPERTS):
        def extra(k, carry, e=e):
            lo, ws = _window(cum_ref, base, e, cap, k)
            x = _dot(jnp.where(hits(e, lo, ws), 1.0, 0.0).astype(BF16), h)
            x_ref[pl.ds(e * cap + ws, MOE_W), :] += x.astype(x_ref.dtype)
            return carry
        lax.fori_loop(1, _extra_windows(cum_ref, base, e), extra, 0)


def _dispatch_win(cum, slot, h, seq, cap):
    n = h.shape[0]
    nb, nt = n // seq, seq // MOE_TT
    return pl.pallas_call(
        functools.partial(_dispatch_win_kernel, cap=cap, nt=nt),
        grid_spec=pltpu.PrefetchScalarGridSpec(
            num_scalar_prefetch=1,
            grid=(nb, nt),
            in_specs=[pl.BlockSpec((N_EXPERTS, MOE_TT), lambda b, i, c: (0, b * nt + i)),
                      pl.BlockSpec((MOE_TT, D_MODEL), lambda b, i, c: (b * nt + i, 0))],
            out_specs=pl.BlockSpec((N_EXPERTS * cap, D_MODEL), lambda b, i, c: (b, 0))),
        out_shape=jax.ShapeDtypeStruct((nb * N_EXPERTS * cap, D_MODEL), BF16),
        compiler_params=_params("parallel", "arbitrary"),
        name="moe_dispatch_win",
    )(cum, slot, h)


FFN_TF = 512
FFN_RC = 512


def _ffn_kernel(xa_ref, xb_ref, wg_ref, wu_ref, wd_ref, ya_ref, yb_ref, acc_ref):
    j = pl.program_id(1)

    @pl.when(j == 0)
    def _():
        acc_ref[...] = jnp.zeros_like(acc_ref)

    wg = wg_ref[0].astype(BF16)
    wu = wu_ref[0].astype(BF16)
    wd = wd_ref[0].astype(BF16)
    ra = xa_ref.shape[0] * xa_ref.shape[2]

    def row_chunks(ref, base):
        nb, _, cap, d = ref.shape
        rc = min(FFN_RC, nb * cap)
        for r0 in range(0, nb * cap, rc):
            if cap >= rc:
                b, c0 = divmod(r0, cap)
                yield base + r0, ref[b, 0, c0:c0 + rc, :]
            else:
                yield base + r0, ref[r0 // cap:(r0 + rc) // cap, 0, :, :].reshape(rc, d)

    for r0, x in itertools.chain(row_chunks(xa_ref, 0), row_chunks(xb_ref, ra)):
        rc = x.shape[0]
        g = _dot(x, wg)
        u = _dot(x, wu)
        mid = (g * _sigmoid(g) * u).astype(BF16)
        acc_ref[r0:r0 + rc, :] += _dot(mid, wd)

    @pl.when(j == pl.num_programs(1) - 1)
    def _():
        for ref, base in ((ya_ref, 0), (yb_ref, ra)):
            nb, _, cap, d = ref.shape
            ref[...] = acc_ref[base:base + nb * cap, :].reshape(nb, 1, cap, d).astype(ref.dtype)


def _ffn(xa, xb, w_gate, w_up, w_down):
    ba, _, ca, d = xa.shape
    bb, _, cb, _ = xb.shape
    nj = D_FF // FFN_TF
    xa_spec = pl.BlockSpec((ba, 1, ca, d), lambda e, j: (0, e, 0, 0))
    xb_spec = pl.BlockSpec((bb, 1, cb, d), lambda e, j: (0, e, 0, 0))
    return pl.pallas_call(
        _ffn_kernel,
        grid=(N_EXPERTS, nj),
        in_specs=[xa_spec, xb_spec,
                  pl.BlockSpec((1, d, FFN_TF), lambda e, j: (e, 0, j)),
                  pl.BlockSpec((1, d, FFN_TF), lambda e, j: (e, 0, j)),
                  pl.BlockSpec((1, FFN_TF, d), lambda e, j: (e, j, 0))],
        out_specs=[xa_spec, xb_spec],
        out_shape=[jax.ShapeDtypeStruct(xa.shape, BF16), jax.ShapeDtypeStruct(xb.shape, BF16)],
        scratch_shapes=[pltpu.VMEM((ba * ca + bb * cb, d), F32)],
        compiler_params=_params("parallel", "arbitrary"),
        name="moe_ffn",
    )(xa, xb, w_gate, w_up, w_down)


def _expand(vals_bf, first_expert, width, total):
    e_of_lane = first_expert + lax.broadcasted_iota(I32, (N_EXPERTS, total), 1) // width
    pick = jnp.where(lax.broadcasted_iota(I32, (N_EXPERTS, total), 0) == e_of_lane, 1.0, 0.0).astype(BF16)
    return _dot(vals_bf, pick)


def _combine_kernel(pos_ref, gate_ref, y_ref, x_ref, mod_ref, o_ref, *, cap, seq):
    m = N_EXPERTS * cap
    rank = (lax.broadcasted_iota(I32, (seq, m), 1) % cap).astype(F32)
    for r in range(x_ref.shape[0] // seq):
        rows = slice(r * seq, (r + 1) * seq)
        pos = _expand(pos_ref[rows, :].astype(BF16), 0, cap, m)
        gate = _expand(gate_ref[rows, :].astype(BF16), 0, cap, m)
        w = jnp.where(pos == rank, gate, 0.0).astype(BF16)
        o_ref[rows, :] = x_ref[rows, :] + mod_ref[0, 5:6, :] * _dot(w, y_ref[r * m:(r + 1) * m, :])


def _combine(pos_t, gate_t, y, x, mod, *, seq, cap):
    n = x.shape[0]
    nb = n // seq
    rb = MOE_RB if nb % MOE_RB == 0 else 1
    row = lambda b: (b, 0)
    return pl.pallas_call(
        functools.partial(_combine_kernel, cap=cap, seq=seq),
        grid=(nb // rb,),
        in_specs=[pl.BlockSpec((rb * seq, N_EXPERTS), row),
                  pl.BlockSpec((rb * seq, N_EXPERTS), row),
                  pl.BlockSpec((rb * N_EXPERTS * cap, D_MODEL), row),
                  pl.BlockSpec((rb * seq, D_MODEL), row),
                  pl.BlockSpec((1, 8, D_MODEL), lambda b: (0, 0, 0))],
        out_specs=pl.BlockSpec((rb * seq, D_MODEL), row),
        out_shape=jax.ShapeDtypeStruct((n, D_MODEL), F32),
        compiler_params=_params("parallel"),
        name="moe_combine",
    )(pos_t, gate_t, y, x, mod)


def _combine_win_kernel(cum_ref, pos_ref, gate_ref, y_ref, x_ref, mod_ref, o_ref, acc_ref, *, cap, nt):
    b, i = pl.program_id(0), pl.program_id(1)
    tt = x_ref.shape[0]
    base = (b * (nt + 1) + i) * N_EXPERTS
    width = MOE_EG * MOE_W
    posb = pos_ref[...].astype(BF16)
    gateb = gate_ref[...].astype(BF16)
    lane = lax.broadcasted_iota(I32, (1, width), 1)
    offset = (lane & (MOE_W - 1)).astype(F32)
    acc = jnp.zeros((tt, D_MODEL), F32)
    for grp in range(N_EXPERTS // MOE_EG):
        wins = [(e,) + _window(cum_ref, base, e, cap) for e in range(grp * MOE_EG, (grp + 1) * MOE_EG)]
        lo_l = jnp.zeros((1, width), F32)
        ws_l = jnp.zeros((1, width), F32)
        for q, (e, lo, ws) in enumerate(wins):
            mine = (lane >> 6) == q
            lo_l = jnp.where(mine, lo.astype(F32), lo_l)
            ws_l = jnp.where(mine, ws.astype(F32), ws_l)
        pos = _expand(posb, grp * MOE_EG, MOE_W, width)
        gate = _expand(gateb, grp * MOE_EG, MOE_W, width)
        w = jnp.where((pos - ws_l == offset) & (pos >= lo_l), gate, 0.0).astype(BF16)
        ywin = jnp.concatenate([y_ref[pl.ds(e * cap + ws, MOE_W), :] for e, lo, ws in wins], axis=0)
        acc = acc + _dot(w, ywin)
    acc_ref[...] = acc

    off64 = lax.broadcasted_iota(I32, (1, MOE_W), 1).astype(F32)
    for e in range(N_EXPERTS):
        def extra(k, carry, e=e):
            lo, ws = _window(cum_ref, base, e, cap, k)
            pos = pos_ref[:, e:e + 1]
            gate = gate_ref[:, e:e + 1].astype(BF16).astype(F32)
            w = jnp.where((pos - ws.astype(F32) == off64) & (pos >= lo.astype(F32)), gate, 0.0).astype(BF16)
            acc_ref[...] += _dot(w, y_ref[pl.ds(e * cap + ws, MOE_W), :])
            return carry
        lax.fori_loop(1, _extra_windows(cum_ref, base, e), extra, 0)

    o_ref[...] = x_ref[...] + mod_ref[0, 5:6, :] * acc_ref[...]


def _combine_win(cum, pos_t, gate_t, y, x, mod, *, seq, cap):
    n = x.shape[0]
    nb, nt = n // seq, seq // MOE_TT
    row = lambda b, i, c: (b * nt + i, 0)
    return pl.pallas_call(
        functools.partial(_combine_win_kernel, cap=cap, nt=nt),
        grid_spec=pltpu.PrefetchScalarGridSpec(
            num_scalar_prefetch=1,
            grid=(nb, nt),
            in_specs=[pl.BlockSpec((MOE_TT, N_EXPERTS), row),
                      pl.BlockSpec((MOE_TT, N_EXPERTS), row),
                      pl.BlockSpec((N_EXPERTS * cap, D_MODEL), lambda b, i, c: (b, 0)),
                      pl.BlockSpec((MOE_TT, D_MODEL), row),
                      pl.BlockSpec((1, 8, D_MODEL), lambda b, i, c: (1 + b, 0, 0))],
            out_specs=pl.BlockSpec((MOE_TT, D_MODEL), row),
            scratch_shapes=[pltpu.VMEM((MOE_TT, D_MODEL), F32)]),
        out_shape=jax.ShapeDtypeStruct((n, D_MODEL), F32),
        compiler_params=_params("parallel", "parallel"),
        name="moe_combine_win",
    )(cum, pos_t, gate_t, y, x, mod)


def _moe_pair(hp, affp, x1p, hs, affs, x1s, mod, w_gate, w_up, w_down, seq_p, seq_s):
    n_p, n_s = hp.shape[0], hs.shape[0]
    nb_p, nb_s = n_p // seq_p, n_s // seq_s
    cap_p = EC_FACTOR * seq_p // N_EXPERTS
    cap_s = EC_FACTOR * seq_s // N_EXPERTS
    assert N_EXPERTS * cap_p <= 512 and cap_s >= MOE_W and cap_s % 16 == 0 and seq_s % MOE_TT == 0
    slot_p, gate_p, pos_p, _ = _route(affp, seq_p, cap_p, nseg=min(8, nb_p), tt=seq_p)
    slot_s, gate_s, pos_s, tcnt = _route(affs, seq_s, cap_s, nseg=min(4, nb_s), tt=MOE_TT)
    nt = seq_s // MOE_TT
    counts = tcnt.T.reshape(nb_s, nt, N_EXPERTS).astype(I32)
    cum = jnp.concatenate([jnp.zeros((nb_s, 1, N_EXPERTS), I32), jnp.cumsum(counts, axis=1)], axis=1).reshape(-1)
    xp = _dispatch(slot_p, hp, seq_p, cap_p).reshape(nb_p, N_EXPERTS, cap_p, D_MODEL)
    xs = _dispatch_win(cum, slot_s, hs, seq_s, cap_s).reshape(nb_s, N_EXPERTS, cap_s, D_MODEL)
    ys, yp = _ffn(xs, xp, w_gate, w_up, w_down)
    outp = _combine(pos_p.T, gate_p.T, yp.reshape(-1, D_MODEL), x1p, mod, seq=seq_p, cap=cap_p)
    outs = _combine_win(cum, pos_s.T, gate_s.T, ys.reshape(-1, D_MODEL), x1s, mod, seq=seq_s, cap=cap_s)
    return outp, outs


def _ssm_in_kernel(x_ref, mod_ref, gain_ref, wt_ref, ut_ref, *, cols_per_cond):
    tc = x_ref.shape[1]
    if cols_per_cond is None:
        h = _norm_mod(x_ref[0], gain_ref[...], mod_ref[0, 0:1, :], mod_ref[0, 1:2, :]).astype(BF16)
    else:
        first = 1 + pl.program_id(1) * (tc // cols_per_cond)
        parts = []
        for s in range(tc // cols_per_cond):
            m = mod_ref[first + s]
            parts.append(_norm_mod(x_ref[0, s * cols_per_cond:(s + 1) * cols_per_cond, :], gain_ref[...],
                                   m[0:1, :], m[1:2, :]).astype(BF16))
        h = jnp.concatenate(parts, axis=0) if len(parts) > 1 else parts[0]
    ut_ref[0] = _dot(wt_ref[...], h, NT)


def _ssm_in(xperm, mod, gain, wt_bf, *, cols_per_cond, tc):
    l, bk, d = xperm.shape
    assert cols_per_cond is None or tc % cols_per_cond == 0
    return pl.pallas_call(
        functools.partial(_ssm_in_kernel, cols_per_cond=cols_per_cond),
        grid=(l, bk // tc),
        in_specs=[pl.BlockSpec((1, tc, d), lambda j, i: (j, i, 0)),
                  pl.BlockSpec(mod.shape, lambda j, i: (0, 0, 0)),
                  pl.BlockSpec((1, d), lambda j, i: (0, 0)),
                  pl.BlockSpec((d, d), lambda j, i: (0, 0))],
        out_specs=pl.BlockSpec((1, d, tc), lambda j, i: (j, 0, i)),
        out_shape=jax.ShapeDtypeStruct((l, d, bk), F32),
        compiler_params=_params("parallel", "parallel"),
        name="ssm_in",
    )(xperm, mod, gain, wt_bf)


def _ssm_core_kernel(utp_ref, uts_ref, lamp_ref, c_ref, bt_ref, dsk_ref, h0_ref, ytp_ref, yts_ref, fs_ref,
                     *, kp, ks, nbp, nbs):
    rows = SSM_ROWS
    p = SSM_STATE
    lc = SSM_CHUNK
    ri = lax.broadcasted_iota(I32, (rows, rows), 0)
    cj = lax.broadcasted_iota(I32, (rows, rows), 1)
    causal = (ri >> 4) >= (cj >> 4)
    anticausal = (cj >> 4) >= (ri >> 4)
    diag = ri == cj
    leftc = lax.broadcasted_iota(I32, (lc, 128), 1) < p
    nrow = lax.broadcasted_iota(I32, (lc, 128), 0).astype(F32)
    eye = lax.broadcasted_iota(I32, (p, 128), 0) == lax.broadcasted_iota(I32, (p, 128), 1)

    def cmul(ar, ai, xr, xi):
        return ar * xr - ai * xi, ar * xi + ai * xr

    def expand_rows(t):
        return jnp.broadcast_to(t[:, None, :], (lc, SSM_GROUP, 128)).reshape(rows, 128)

    def tile_rows(t):
        return jnp.broadcast_to(t[None, :, :], (lc, SSM_GROUP, 128)).reshape(rows, 128)

    def to_col(row):
        return jnp.where(eye, jnp.broadcast_to(row, (p, 128)), 0.0).sum(axis=1, keepdims=True)

    def operands(gg, d):
        lp = lamp_ref[gg, d]
        lre, lim = lp[0:1], lp[1:2]
        dt = jnp.exp(lp[2:3])
        a, th = lre * dt, lim * dt
        ang = nrow * th
        cs, sn = jnp.cos(ang), jnp.sin(ang)
        ep, em = jnp.exp(nrow * a), jnp.exp(-(nrow * a))
        pr, pi = ep * cs, ep * sn
        nr, ni = em * cs, -(em * sn)
        l1r, l1i = pr[1:2], pi[1:2]
        lmr, lmi = pr[lc - 1:lc], pi[lc - 1:lc]
        llr, lli = cmul(lmr, lmi, l1r, l1i)
        den = lre * lre + lim * lim
        cr = ((l1r - 1.0) * lre + l1i * lim) / den
        ci = (l1i * lre - (l1r - 1.0) * lim) / den
        btr, bti = bt_ref[gg, d, 0], bt_ref[gg, d, 1]
        bbr, bbi = cr * btr - ci * bti, cr * bti + ci * btr
        u1 = tile_rows(jnp.where(leftc, bbr, bbi))
        u2 = tile_rows(jnp.where(leftc, bbi, bbr))
        c1 = tile_rows(c_ref[gg, d, 0])
        c2 = tile_rows(c_ref[gg, d, 1])

        def left_form(xr, xi):
            return (c1 * expand_rows(jnp.where(leftc, xr, -xi))
                    + c2 * expand_rows(jnp.where(leftc, -xi, -xr)))

        def right_form(xr, xi):
            return u1 * expand_rows(xr) + u2 * expand_rows(jnp.where(leftc, -xi, xi))

        if d == 0:
            al = left_form(pr, pi)
            brt = right_form(nr, ni)
            rrt = right_form(*cmul(lmr, lmi, nr, ni))
            qq = left_form(*cmul(l1r, l1i, pr, pi))
            mat = jnp.where(causal, _dot3(al, brt, NT), 0.0)
        else:
            al = left_form(nr, ni)
            brt = right_form(pr, pi)
            rrt = brt
            qq = left_form(*cmul(llr, lli, nr, ni))
            mat = jnp.where(anticausal, _dot3(al, brt, NT), 0.0)
        return mat, rrt.T, qq, to_col(llr), to_col(lli)

    def scan(sr, si, lr, li, h0r, h0i, reverse, nchunk, nbatch, sel):
        bk = sr.shape[1]
        lane = lax.broadcasted_iota(I32, (p, bk), 1)
        kidx = lane & (nchunk - 1)
        edge = (nchunk - 1) if reverse else 0
        if h0r is not None:
            h0cr = jnp.zeros((p, bk), F32)
            h0ci = jnp.zeros((p, bk), F32)
            for b in range(nbatch):
                at = lane == (b * nchunk + edge)
                h0cr = jnp.where(at, h0r[:, b:b + 1], h0cr)
                h0ci = jnp.where(at, h0i[:, b:b + 1], h0ci)
            ar, ai = cmul(lr, li, h0cr, h0ci)
            er, ei = sr + ar, si + ai
        else:
            er, ei = sr, si
        ar, ai = lr, li
        s = 1
        while s < nchunk:
            if reverse:
                ok = kidx < nchunk - s
                tr, ti = pltpu.roll(er, bk - s, 1), pltpu.roll(ei, bk - s, 1)
            else:
                ok = kidx >= s
                tr, ti = pltpu.roll(er, s, 1), pltpu.roll(ei, s, 1)
            tr = jnp.where(ok, tr, 0.0)
            ti = jnp.where(ok, ti, 0.0)
            dr, di = cmul(ar, ai, tr, ti)
            er, ei = er + dr, ei + di
            ar, ai = cmul(ar, ai, ar, ai)
            s *= 2
        if reverse:
            inner = kidx < nchunk - 1
            hr, hi = pltpu.roll(er, bk - 1, 1), pltpu.roll(ei, bk - 1, 1)
        else:
            inner = kidx >= 1
            hr, hi = pltpu.roll(er, 1, 1), pltpu.roll(ei, 1, 1)
        hr = jnp.where(inner, hr, h0cr if h0r is not None else 0.0)
        hi = jnp.where(inner, hi, h0ci if h0r is not None else 0.0)
        fin = None if sel is None else (_dot_sel(er, sel), _dot_sel(ei, sel))
        return hr, hi, fin

    def final_selectors(bk, nchunk, nbatch):
        col = lax.broadcasted_iota(I32, (bk, nbatch), 0)
        bat = lax.broadcasted_iota(I32, (bk, nbatch), 1)
        last = jnp.where(col == bat * nchunk + (nchunk - 1), 1.0, 0.0).astype(BF16)
        first = jnp.where(col == bat * nchunk, 1.0, 0.0).astype(BF16)
        return last, first

    sel_last, sel_first = final_selectors(utp_ref.shape[2], kp, nbp)

    for gg in range(SSM_GB):
        mf, rf, qf, lfr, lfi = operands(gg, 0)
        mb, rb, qb, lbr, lbi = operands(gg, 1)
        skip = jnp.where(diag, jnp.broadcast_to(dsk_ref[gg], (rows, rows)), 0.0)
        stack = jnp.concatenate([mf + mb + skip, rf, rb], axis=0)
        qq = jnp.concatenate([qf, qb], axis=1)
        h0 = h0_ref[gg]
        for ut_ref, yt_ref, nchunk, nbatch, latent in ((utp_ref, ytp_ref, kp, nbp, False),
                                                       (uts_ref, yts_ref, ks, nbs, True)):
            bk = ut_ref.shape[2]
            x = ut_ref[:, gg * SSM_GROUP:(gg + 1) * SSM_GROUP, :].reshape(rows, bk)
            res = _mm(stack, x, SSM_PASSES)
            hfr, hfi, ff = scan(res[rows:rows + p], res[rows + p:rows + 2 * p], lfr, lfi,
                                h0[0] if latent else None, h0[1] if latent else None,
                                False, nchunk, nbatch, None if latent else sel_last)
            hbr, hbi, fb = scan(res[rows + 2 * p:rows + 3 * p], res[rows + 3 * p:rows + 4 * p], lbr, lbi,
                                h0[2] if latent else None, h0[3] if latent else None,
                                True, nchunk, nbatch, None if latent else sel_first)
            states = jnp.concatenate([hfr, hfi, hbr, hbi], axis=0)
            y = res[:rows] + _mm(qq, states, SSM_PASSES)
            yt_ref[:, gg * SSM_GROUP:(gg + 1) * SSM_GROUP, :] = y.reshape(lc, SSM_GROUP, bk)
            if not latent:
                fs_ref[gg, 0] = ff[0]
                fs_ref[gg, 1] = ff[1]
                fs_ref[gg, 2] = fb[0]
                fs_ref[gg, 3] = fb[1]


def _ssm_core(utp, uts, ops, h0, *, kp, ks, nbp, nbs):
    lamp, c2, bt2, dsk = ops
    l, d, bkp = utp.shape
    bks = uts.shape[2]
    g = SSM_GROUPS
    gb = SSM_GB
    lead4 = lambda i: (i, 0, 0, 0)
    lead5 = lambda i: (i, 0, 0, 0, 0)
    ut_spec = lambda bk: pl.BlockSpec((l, gb * SSM_GROUP, bk), lambda i: (0, i, 0))
    return pl.pallas_call(
        functools.partial(_ssm_core_kernel, kp=kp, ks=ks, nbp=nbp, nbs=nbs),
        grid=(g // gb,),
        in_specs=[ut_spec(bkp), ut_spec(bks),
                  pl.BlockSpec((gb, 2, 8, 128), lead4),
                  pl.BlockSpec((gb, 2, 2, SSM_GROUP, 128), lead5),
                  pl.BlockSpec((gb, 2, 2, SSM_GROUP, 128), lead5),
                  pl.BlockSpec((gb, 1, SSM_ROWS), lambda i: (i, 0, 0)),
                  pl.BlockSpec((gb, 4, SSM_STATE, nbs), lead4)],
        out_specs=[ut_spec(bkp), ut_spec(bks),
                   pl.BlockSpec((gb, 4, SSM_STATE, nbp), lead4)],
        out_shape=[jax.ShapeDtypeStruct((l, d, bkp), F32),
                   jax.ShapeDtypeStruct((l, d, bks), F32),
                   jax.ShapeDtypeStruct((g, 4, SSM_STATE, nbp), F32)],
        compiler_params=_params("parallel"),
        name="ssm_core",
    )(utp, uts, lamp, c2, bt2, dsk, h0)


def _ssm_out_kernel(yt_ref, w_ref, m_ref):
    y = yt_ref[0].T
    act = 0.5 * y * (1.0 + jnp.tanh(0.7978845608028654 * (y + 0.044715 * (y * y * y))))
    ag = _dot(act.astype(BF16), w_ref[...])
    d = m_ref.shape[2]
    m_ref[0] = ag[:, :d] * _sigmoid(ag[:, d:])


def _ssm_out(yt, w_bf, tc):
    l, d, bk = yt.shape
    return pl.pallas_call(
        _ssm_out_kernel,
        grid=(l, bk // tc),
        in_specs=[pl.BlockSpec((1, d, tc), lambda j, i: (j, 0, i)),
                  pl.BlockSpec((d, 2 * d), lambda j, i: (0, 0))],
        out_specs=pl.BlockSpec((1, tc, d), lambda j, i: (j, i, 0)),
        out_shape=jax.ShapeDtypeStruct((l, bk, d), F32),
        compiler_params=_params("parallel", "parallel"),
        name="ssm_out",
    )(yt, w_bf)


def _ssm_operand_params(lam_re, lam_im, b_re, b_im, c_re, c_im, log_dt, d_skip):
    g, l = SSM_GROUPS, SSM_CHUNK
    dup = lambda t: jnp.concatenate([t, t], axis=-1)
    lamp = jnp.stack([lam_re, lam_im, jnp.broadcast_to(log_dt[..., None], lam_re.shape)], axis=2)
    lamp = dup(jnp.pad(lamp, ((0, 0), (0, 0), (0, 5), (0, 0)))).transpose(1, 0, 2, 3)
    c2 = dup(jnp.stack([c_re, c_im], axis=2)).transpose(1, 0, 2, 3, 4)
    bt2 = dup(jnp.stack([jnp.swapaxes(b_re, -1, -2), jnp.swapaxes(b_im, -1, -2)], axis=2)).transpose(1, 0, 2, 3, 4)
    dsk = jnp.tile(d_skip.reshape(g, 1, SSM_GROUP), (1, 1, l))
    return lamp, c2, bt2, dsk


def _to_chunks(x, nb, seq):
    k = seq // SSM_CHUNK
    return x.reshape(nb, k, SSM_CHUNK, -1).transpose(2, 0, 1, 3).reshape(SSM_CHUNK, nb * k, -1)


def _from_chunks(x, nb, seq):
    k = seq // SSM_CHUNK
    return x.reshape(SSM_CHUNK, nb, k, -1).transpose(1, 2, 0, 3).reshape(nb * seq, -1)


def _ssm_mixers(xp, xs, mod, gain, wt_bf, ops, w_out_bf, h0, *, nbp, sp, nbs, ss):
    kp, ks = sp // SSM_CHUNK, ss // SSM_CHUNK
    tcp, tcs = min(SSM_TC, nbp * kp), min(SSM_TC, nbs * ks)
    utp = _ssm_in(_to_chunks(xp, nbp, sp), mod, gain, wt_bf, cols_per_cond=None, tc=tcp)
    uts = _ssm_in(_to_chunks(xs, nbs, ss), mod, gain, wt_bf, cols_per_cond=ks, tc=tcs)
    ytp, yts, fs = _ssm_core(utp, uts, ops, h0, kp=kp, ks=ks, nbp=nbp, nbs=nbs)
    mp = _from_chunks(_ssm_out(ytp, w_out_bf, tcp), nbp, sp)
    ms = _from_chunks(_ssm_out(yts, w_out_bf, tcs), nbs, ss)
    return mp, ms, fs


def _rope_tables(seq):
    t = jnp.arange(seq)
    row = (t // GRID_W).astype(F32)
    col = (t % GRID_W).astype(F32)
    n_freq = HEAD_DIM // 4
    inv_freq = ROPE_THETA ** (-jnp.arange(n_freq, dtype=F32) / n_freq)
    ang = jnp.concatenate([row[:, None] * inv_freq, col[:, None] * inv_freq], axis=-1)
    cos = jnp.repeat(jnp.cos(ang), 2, axis=-1)
    sin = jnp.repeat(jnp.sin(ang), 2, axis=-1)
    sign = jnp.tile(jnp.array([-1.0, 1.0], F32), HEAD_DIM // 2)
    return jnp.tile(cos, (1, 4)), jnp.tile(sin * sign, (1, 4))


def _head_gains(qn_a, kn_a, qn_b, kn_b):
    scale = HEAD_DIM ** -0.5 * LOG2E
    ones = jnp.ones((N_KV * HEAD_DIM,), F32)
    return jnp.concatenate([jnp.tile(qn_a, N_HEADS) * scale, jnp.tile(kn_a, N_KV), ones,
                            jnp.tile(qn_b, N_HEADS) * scale, jnp.tile(kn_b, N_KV), ones]).reshape(1, QKV_COLS)


def kernel(x_prompt, x_sample, c, cache_k_a_l0, cache_v_a_l0, cache_k_b_l0, cache_v_b_l0, state_ssm_re_l1, state_ssm_im_l1, c_ctx, mod_w_l0, mod_b_l0, norm_mix_l0, attn_w_in_l0, q_norm_a_l0, k_norm_a_l0, q_norm_b_l0, k_norm_b_l0, sink_b_l0, attn_w_out_l0, norm_ffn_l0, router_l0, moe_w_gate_l0, moe_w_up_l0, moe_w_down_l0, mod_w_l1, mod_b_l1, norm_mix_l1, ssm_w_in_l1, ssm_lambda_re_l1, ssm_lambda_im_l1, ssm_b_re_l1, ssm_b_im_l1, ssm_c_re_l1, ssm_c_im_l1, ssm_log_dt_l1, ssm_d_l1, ssm_w_out_l1, norm_ffn_l1, router_l1, moe_w_gate_l1, moe_w_up_l1, moe_w_down_l1):
    bp, sp, d = x_prompt.shape
    bs, ss, _ = x_sample.shape
    past = cache_k_a_l0.shape[1]
    assert d == D_MODEL and bs <= 7 and (bp * sp) % TM == 0 and TM % sp == 0 and ss % TM == 0
    xp = x_prompt.reshape(bp * sp, d)
    xs = x_sample.reshape(bs * ss, d)
    cond8 = jnp.concatenate([c_ctx[None], c, jnp.zeros((7 - bs, d), F32)], axis=0)
    row1 = lambda v: v.reshape(1, -1)

    mod0 = _mod_rows(cond8, mod_w_l0, mod_b_l0)
    w_in = attn_w_in_l0.astype(BF16)
    hgain = _head_gains(q_norm_a_l0, k_norm_a_l0, q_norm_b_l0, k_norm_b_l0)
    lane = np.arange(256)
    bd = jnp.asarray((lane[:, None] // HEAD_DIM == lane[None, :] // HEAD_DIM) / HEAD_DIM, BF16)
    cos_t, sin_t = _rope_tables(ss)
    qp, kap, vap, kbp, vbp = _qkv(xp, mod0, row1(norm_mix_l0), w_in, hgain, bd, cos_t, sin_t,
                                  rows_per_cond=None, seq=sp, rope=False, kv_dtype=F32, transposed_kv=True)
    qs, kas, vas, kbs, vbs = _qkv(xs, mod0, row1(norm_mix_l0), w_in, hgain, bd, cos_t, sin_t,
                                  rows_per_cond=ss, seq=ss, rope=True, kv_dtype=BF16, transposed_kv=False)
    op = _attn_ctx(sink_b_l0, qp, kap, vap, kbp, vbp, sp)
    cache = lambda t: t.reshape(bs, past, N_KV * HEAD_DIM)
    os_ = _attn_lat(sink_b_l0, qs, kas, vas, kbs, vbs, cache(cache_k_a_l0), cache(cache_v_a_l0),
                    cache(cache_k_b_l0), cache(cache_v_b_l0), ss)
    w_out = attn_w_out_l0.astype(BF16)
    x1p, hp, affp = _postmix(op, xp, mod0, w_out, row1(norm_ffn_l0), router_l0.T, rows_per_cond=None, project=True)
    x1s, hs, affs = _postmix(os_, xs, mod0, w_out, row1(norm_ffn_l0), router_l0.T, rows_per_cond=ss, project=True)
    xp, xs = _moe_pair(hp, affp, x1p, hs, affs, x1s, mod0, moe_w_gate_l0, moe_w_up_l0, moe_w_down_l0, sp, ss)

    mod1 = _mod_rows(cond8, mod_w_l1, mod_b_l1)
    ops = _ssm_operand_params(ssm_lambda_re_l1, ssm_lambda_im_l1, ssm_b_re_l1, ssm_b_im_l1, ssm_c_re_l1, ssm_c_im_l1,
                              ssm_log_dt_l1, ssm_d_l1)
    wt = ssm_w_in_l1.T.astype(BF16)
    w_so = ssm_w_out_l1.astype(BF16)
    h0 = jnp.stack([state_ssm_re_l1[:, 0], state_ssm_im_l1[:, 0], state_ssm_re_l1[:, 1], state_ssm_im_l1[:, 1]],
                   axis=0).transpose(2, 0, 3, 1)
    mp, ms, fsp = _ssm_mixers(xp, xs, mod1, row1(norm_mix_l1), wt, ops, w_so, h0, nbp=bp, sp=sp, nbs=bs, ss=ss)
    dummy_w = jnp.zeros((8, 128), BF16)
    x1p, hp, affp = _postmix(mp, xp, mod1, dummy_w, row1(norm_ffn_l1), router_l1.T, rows_per_cond=None, project=False)
    x1s, hs, affs = _postmix(ms, xs, mod1, dummy_w, row1(norm_ffn_l1), router_l1.T, rows_per_cond=ss, project=False)
    xp, xs = _moe_pair(hp, affp, x1p, hs, affs, x1s, mod1, moe_w_gate_l1, moe_w_up_l1, moe_w_down_l1, sp, ss)

    kv_out = lambda t: t.reshape(bp, N_KV, HEAD_DIM, sp).transpose(0, 3, 1, 2)
    fin = fsp.transpose(3, 1, 0, 2)
    ssm_re = jnp.stack([fin[:, 0], fin[:, 2]], axis=1)
    ssm_im = jnp.stack([fin[:, 1], fin[:, 3]], axis=1)
    return (xp.reshape(bp, sp, d), xs.reshape(bs, ss, d), kv_out(kap), kv_out(vap), kv_out(kbp), kv_out(vbp),
            ssm_re, ssm_im)
```

```python
import functools
import itertools

import jax
import jax.numpy as jnp
import numpy as np
from jax import lax
from jax.experimental import pallas as pl
from jax.experimental.pallas import tpu as pltpu

F32, BF16, I32 = jnp.float32, jnp.bfloat16, jnp.int32

D_MODEL = 1024
GRID_W = 64
HEAD_DIM = 64
N_HEADS = 8
N_KV = 2
WINDOW = 128
ROPE_THETA = 10000.0
SSM_GROUP = 16
SSM_GROUPS = D_MODEL // SSM_GROUP
SSM_STATE = 64
N_EXPERTS = 16
EC_FACTOR = 2
D_FF = 2 * D_MODEL
EPS = 1e-6
NEG_INF = -1e30
LOG2E = 1.4426950408889634
QKV_COLS = 2 * (N_HEADS + 2 * N_KV) * HEAD_DIM
ATTN_OUT = 2 * N_HEADS * HEAD_DIM

SSM_CHUNK = 16
SSM_ROWS = SSM_CHUNK * SSM_GROUP
SSM_GB = 4
SSM_PASSES = 1

SSM_TC = 512

TM = 512

BAND_TQ = 256

MOE_TT = 256
MOE_W = 64
MOE_EG = 4
MOE_RB = 2
VMEM_LIMIT = 56 * 1024 * 1024

NN = (((1,), (0,)), ((), ()))
NT = (((1,), (1,)), ((), ()))


def _dot(a, b, dims=NN):
    return lax.dot_general(a, b, dims, preferred_element_type=F32)


def _split2(x):
    hi = x.astype(BF16)
    lo = (x - hi.astype(F32)).astype(BF16)
    return hi, lo


def _split3(x):
    hi = x.astype(BF16)
    r = x - hi.astype(F32)
    mid = r.astype(BF16)
    lo = (r - mid.astype(F32)).astype(BF16)
    return hi, mid, lo


def _dot3(a, b, dims=NN):
    ah, al = _split2(a)
    bh, bl = _split2(b)
    return _dot(ah, bh, dims) + (_dot(ah, bl, dims) + _dot(al, bh, dims))


def _mm(a, b, passes):
    if passes == 1:
        return _dot(a.astype(BF16), b.astype(BF16))
    return _dot3(a, b)


def _dot_sel(x, sel):
    hi, mid, lo = _split3(x)
    return _dot(hi, sel) + (_dot(mid, sel) + _dot(lo, sel))


def _sigmoid(x):
    return 1.0 / (1.0 + jnp.exp(-x))


def _norm_mod(x, gain, shift, scale):
    ms = jnp.mean(x * x, axis=-1, keepdims=True)
    y = x * lax.rsqrt(ms + EPS) * gain
    return y * (1.0 + scale) + shift


def _params(*sem):
    return pltpu.CompilerParams(dimension_semantics=sem, vmem_limit_bytes=VMEM_LIMIT)


def _adaln_kernel(c_ref, w_ref, b_ref, o_ref):
    c = c_ref[...]
    s = c * _sigmoid(c)
    o_ref[...] = _dot3(s, w_ref[...]) + b_ref[...]


def _adaln(cond8, w_mod, b_mod):
    d, e = w_mod.shape
    tn = 1536
    return pl.pallas_call(
        _adaln_kernel,
        grid=(e // tn,),
        in_specs=[pl.BlockSpec((8, d), lambda j: (0, 0)),
                  pl.BlockSpec((d, tn), lambda j: (0, j)),
                  pl.BlockSpec((1, tn), lambda j: (0, j))],
        out_specs=pl.BlockSpec((8, tn), lambda j: (0, j)),
        out_shape=jax.ShapeDtypeStruct((8, e), F32),
        compiler_params=_params("parallel"),
        name="adaln",
    )(cond8, w_mod, b_mod.reshape(1, e))


def _mod_rows(cond8, w_mod, b_mod):
    m = _adaln(cond8, w_mod, b_mod).reshape(8, 6, D_MODEL)
    return jnp.pad(m, ((0, 0), (0, 2), (0, 0)))


def _mod_spec(rows_per_cond):
    if rows_per_cond is None:
        return pl.BlockSpec((1, 8, D_MODEL), lambda i: (0, 0, 0))
    return pl.BlockSpec((1, 8, D_MODEL), lambda i: (1 + (i * TM) // rows_per_cond, 0, 0))


def _qkv_kernel(x_ref, mod_ref, gain_ref, w_ref, hg_ref, bd_ref, cos_ref, sin_ref,
                q_ref, ka_ref, va_ref, kb_ref, vb_ref, *, rope, transposed_kv):
    h = _norm_mod(x_ref[...], gain_ref[...], mod_ref[0, 0:1, :], mod_ref[0, 1:2, :])
    proj = _dot(h.astype(BF16), w_ref[...])
    bd = bd_ref[...]

    def head_norm(blk, g):
        ms = _dot((blk * blk).astype(BF16), bd)
        return blk * lax.rsqrt(ms + EPS) * g

    def rotary(blk):
        w = blk.shape[1]
        even = (lax.broadcasted_iota(I32, blk.shape, 1) & 1) == 0
        swapped = jnp.where(even, pltpu.roll(blk, w - 1, 1), pltpu.roll(blk, 1, 1))
        return blk * cos_ref[:, :w] + swapped * sin_ref[:, :w]

    def qk(c0):
        blk = head_norm(proj[:, c0:c0 + 256], hg_ref[:, c0:c0 + 256])
        return rotary(blk) if rope else blk

    q_ref[:, 0:256] = qk(0).astype(q_ref.dtype)
    q_ref[:, 256:512] = qk(256).astype(q_ref.dtype)
    q_ref[:, 512:768] = qk(768).astype(q_ref.dtype)
    q_ref[:, 768:1024] = qk(1024).astype(q_ref.dtype)
    kva = qk(512)
    kvb = qk(1280)
    outs = ((ka_ref, kva[:, :128]), (va_ref, proj[:, 640:768]), (kb_ref, kvb[:, :128]), (vb_ref, proj[:, 1408:1536]))
    for ref, val in outs:
        if transposed_kv:
            seq = ref.shape[2]
            for r in range(ref.shape[0]):
                ref[r] = val[r * seq:(r + 1) * seq].T.astype(ref.dtype)
        else:
            ref[...] = val.astype(ref.dtype)


def _qkv(x, mod, gain, w_bf, hgain, bd, cos_t, sin_t, *, rows_per_cond, seq, rope, kv_dtype, transposed_kv):
    n = x.shape[0]
    tiles_per_seq = max(1, seq // TM)
    row = lambda i: (i, 0)
    const = lambda i: (0, 0)
    pos = lambda i: (i % tiles_per_seq, 0)
    if transposed_kv:
        assert TM % seq == 0
        kv_shape = jax.ShapeDtypeStruct((n // seq, 128, seq), kv_dtype)
        kv_spec = pl.BlockSpec((TM // seq, 128, seq), lambda i: (i, 0, 0))
    else:
        kv_shape = jax.ShapeDtypeStruct((n, 128), kv_dtype)
        kv_spec = pl.BlockSpec((TM, 128), row)
    return pl.pallas_call(
        functools.partial(_qkv_kernel, rope=rope, transposed_kv=transposed_kv),
        grid=(n // TM,),
        in_specs=[pl.BlockSpec((TM, D_MODEL), row),
                  _mod_spec(rows_per_cond),
                  pl.BlockSpec((1, D_MODEL), const),
                  pl.BlockSpec((D_MODEL, QKV_COLS), const),
                  pl.BlockSpec((1, QKV_COLS), const),
                  pl.BlockSpec((256, 256), const),
                  pl.BlockSpec((TM, 256), pos),
                  pl.BlockSpec((TM, 256), pos)],
        out_specs=[pl.BlockSpec((TM, ATTN_OUT), row)] + [kv_spec] * 4,
        out_shape=[jax.ShapeDtypeStruct((n, ATTN_OUT), BF16)] + [kv_shape] * 4,
        compiler_params=_params("parallel"),
        name="qkv_rope" if rope else "qkv",
    )(x, mod, gain, w_bf, hgain, bd, cos_t, sin_t)


def _pad_variants(kk, ones=False):
    lane = lax.broadcasted_iota(I32, kk.shape, 1)
    left = lane < HEAD_DIM
    rolled = pltpu.roll(kk, HEAD_DIM, 1)
    fill_r = jnp.where(lane == HEAD_DIM, 1.0, 0.0) if ones else jnp.zeros_like(kk)
    fill_l = jnp.where(lane == 0, 1.0, 0.0) if ones else jnp.zeros_like(kk)
    return {(0, 0): jnp.where(left, kk, fill_r).astype(BF16),
            (0, 1): jnp.where(left, fill_l, rolled).astype(BF16),
            (1, 0): jnp.where(left, rolled, fill_r).astype(BF16),
            (1, 1): jnp.where(left, fill_l, kk).astype(BF16)}


def _pad_variants_t(kt):
    top = lax.broadcasted_iota(I32, kt.shape, 0) < HEAD_DIM
    zero = jnp.zeros((HEAD_DIM, kt.shape[1]), F32)
    return {(0, 0): jnp.where(top, kt, 0.0).astype(BF16),
            (0, 1): jnp.concatenate([zero, kt[:HEAD_DIM]], axis=0).astype(BF16),
            (1, 0): jnp.concatenate([kt[HEAD_DIM:], zero], axis=0).astype(BF16),
            (1, 1): jnp.where(top, 0.0, kt).astype(BF16)}


def _head_attention_small(qp, kblk, vblk, sink):
    s = _dot(qp, kblk, NN)
    m = s.max(axis=-1, keepdims=True)
    if sink is not None:
        m = jnp.maximum(m, sink)
    p = jnp.exp2(s - m)
    den = p.sum(axis=-1, keepdims=True)
    if sink is not None:
        den = den + jnp.exp2(sink - m)
    return _dot(p.astype(BF16), vblk, NT) / den


def _head_attention(qp, keys, vals, masks, sink, par):
    scores = []
    for kblk, mask in zip(keys, masks):
        s = _dot(qp, kblk, NT)
        if mask is not None:
            s = jnp.where(mask, s, NEG_INF)
        scores.append(s)
    m = scores[0].max(axis=-1, keepdims=True)
    for s in scores[1:]:
        m = jnp.maximum(m, s.max(axis=-1, keepdims=True))
    if sink is not None:
        m = jnp.maximum(m, sink)
    out = None
    for s, vblk in zip(scores, vals):
        o = _dot(jnp.exp2((s - m).astype(BF16)), vblk)
        out = o if out is None else out + o
    ones_lane = HEAD_DIM if par == 0 else 0
    den = out[:, ones_lane:ones_lane + 1]
    if sink is not None:
        den = den + jnp.exp2(sink - m)
    own = (lax.broadcasted_iota(I32, out.shape, 1) < HEAD_DIM) == (par == 0)
    return jnp.where(own, out / den, 0.0)


def _attn_ctx_kernel(sink_ref, q_ref, ka_ref, va_ref, kb_ref, vb_ref, o_ref):
    for mixer, (k_ref, v_ref) in enumerate(((ka_ref, va_ref), (kb_ref, vb_ref))):
        kvar = _pad_variants_t(k_ref[0])
        vvar = _pad_variants_t(v_ref[0])
        for t in range(4):
            tile = mixer * 4 + t
            kv = t // 2
            qp = q_ref[:, tile * 128:(tile + 1) * 128]
            acc = None
            for par in range(2):
                sink = sink_ref[2 * t + par] * LOG2E if mixer == 1 else None
                o = _head_attention_small(qp, kvar[(kv, par)], vvar[(kv, par)], sink)
                acc = o if acc is None else acc + o
            o_ref[:, tile * 128:(tile + 1) * 128] = acc.astype(o_ref.dtype)


def _attn_ctx(sink, q, ka, va, kb, vb, seq):
    n = q.shape[0]
    row = lambda b: (b, 0)
    kv_spec = pl.BlockSpec((1, 128, seq), lambda b: (b, 0, 0))
    return pl.pallas_call(
        _attn_ctx_kernel,
        grid=(n // seq,),
        in_specs=[pl.BlockSpec(memory_space=pltpu.SMEM),
                  pl.BlockSpec((seq, ATTN_OUT), row), kv_spec, kv_spec, kv_spec, kv_spec],
        out_specs=pl.BlockSpec((seq, ATTN_OUT), row),
        out_shape=jax.ShapeDtypeStruct((n, ATTN_OUT), BF16),
        compiler_params=_params("parallel"),
        name="attn_ctx",
    )(sink, q, ka, va, kb, vb)


def _attn_lat_kernel(sink_ref, q_ref, ka_ref, va_ref, kb_ref, vb_ref,
                     cka_ref, cva_ref, ckb_ref, cvb_ref, o_ref, *, tq, seq):
    qi = pl.program_id(1)
    ck = _pad_variants(cka_ref[0])
    cv = _pad_variants(cva_ref[0], ones=True)
    lk = _pad_variants(ka_ref[...].astype(F32))
    lv = _pad_variants(va_ref[...].astype(F32), ones=True)
    for t in range(4):
        kv = t // 2
        qp = q_ref[:, t * 128:(t + 1) * 128]
        acc = None
        for par in range(2):
            o = _head_attention(qp, [ck[(kv, par)], lk[(kv, par)]], [cv[(kv, par)], lv[(kv, par)]],
                                [None, None], None, par)
            acc = o if acc is None else acc + o
        o_ref[:, t * 128:(t + 1) * 128] = acc.astype(o_ref.dtype)
    span = BAND_TQ + 2 * WINDOW
    ck = _pad_variants(ckb_ref[0])
    cv = _pad_variants(cvb_ref[0], ones=True)
    for sub in range(tq // BAND_TQ):
        q0 = qi * tq + sub * BAND_TQ
        rows = slice(sub * BAND_TQ, (sub + 1) * BAND_TQ)
        lo = pl.multiple_of(jnp.clip(q0 - WINDOW, 0, seq - span), 128)
        qpos = q0 + lax.broadcasted_iota(I32, (BAND_TQ, span), 0)
        kpos = lo + lax.broadcasted_iota(I32, (BAND_TQ, span), 1)
        band = jnp.abs(qpos - kpos) <= WINDOW
        lk = _pad_variants(kb_ref[pl.ds(lo, span), :].astype(F32))
        lv = _pad_variants(vb_ref[pl.ds(lo, span), :].astype(F32), ones=True)
        for t in range(4):
            kv = t // 2
            tile = 4 + t
            qp = q_ref[rows, tile * 128:(tile + 1) * 128]
            acc = None
            for par in range(2):
                sink = sink_ref[2 * t + par] * LOG2E
                o = _head_attention(qp, [lk[(kv, par)], ck[(kv, par)]], [lv[(kv, par)], cv[(kv, par)]],
                                    [band, None], sink, par)
                acc = o if acc is None else acc + o
            o_ref[rows, tile * 128:(tile + 1) * 128] = acc.astype(o_ref.dtype)


def _attn_lat(sink, q, ka, va, kb, vb, cka, cva, ckb, cvb, seq, tq=512):
    n = q.shape[0]
    nb = n // seq
    nq = seq // tq
    qrow = lambda b, i: (b * nq + i, 0)
    brow = lambda b, i: (b, 0)
    kv_spec = pl.BlockSpec((seq, 128), brow)
    past = cka.shape[1]
    c_spec = pl.BlockSpec((1, past, 128), lambda b, i: (b, 0, 0))
    return pl.pallas_call(
        functools.partial(_attn_lat_kernel, tq=tq, seq=seq),
        grid=(nb, nq),
        in_specs=[pl.BlockSpec(memory_space=pltpu.SMEM),
                  pl.BlockSpec((tq, ATTN_OUT), qrow), kv_spec, kv_spec, kv_spec, kv_spec,
                  c_spec, c_spec, c_spec, c_spec],
        out_specs=pl.BlockSpec((tq, ATTN_OUT), qrow),
        out_shape=jax.ShapeDtypeStruct((n, ATTN_OUT), BF16),
        compiler_params=_params("parallel", "parallel"),
        name="attn_lat",
    )(sink, q, ka, va, kb, vb, cka, cva, ckb, cvb)


def _postmix_kernel(m_ref, x_ref, mod_ref, w_ref, gain_ref, rt_ref, x1_ref, h2_ref, aff_ref, *, project):
    if project:
        m = _dot(m_ref[...], w_ref[...])
    else:
        m = m_ref[...]
    x1 = x_ref[...] + mod_ref[0, 2:3, :] * m
    x1_ref[...] = x1
    h2 = _norm_mod(x1, gain_ref[...], mod_ref[0, 3:4, :], mod_ref[0, 4:5, :])
    h2_ref[...] = h2.astype(h2_ref.dtype)
    logits = _dot3(rt_ref[...], h2, NT)
    e = jnp.exp(logits - logits.max(axis=0, keepdims=True))
    aff_ref[...] = e / e.sum(axis=0, keepdims=True)


def _postmix(m, x, mod, w_bf, gain, router_t, *, rows_per_cond, project):
    n = x.shape[0]
    row = lambda i: (i, 0)
    const = lambda i: (0, 0)
    return pl.pallas_call(
        functools.partial(_postmix_kernel, project=project),
        grid=(n // TM,),
        in_specs=[pl.BlockSpec((TM, D_MODEL), row),
                  pl.BlockSpec((TM, D_MODEL), row),
                  _mod_spec(rows_per_cond),
                  pl.BlockSpec(w_bf.shape, const),
                  pl.BlockSpec((1, D_MODEL), const),
                  pl.BlockSpec((N_EXPERTS, D_MODEL), const)],
        out_specs=[pl.BlockSpec((TM, D_MODEL), row),
                   pl.BlockSpec((TM, D_MODEL), row),
                   pl.BlockSpec((N_EXPERTS, TM), lambda i: (0, i))],
        out_shape=[jax.ShapeDtypeStruct((n, D_MODEL), F32),
                   jax.ShapeDtypeStruct((n, D_MODEL), BF16),
                   jax.ShapeDtypeStruct((N_EXPERTS, n), F32)],
        compiler_params=_params("parallel"),
        name="postmix_proj" if project else "postmix",
    )(m, x, mod, w_bf, gain, router_t)


def _route_kernel(aff_ref, slot_ref, gate_ref, pos_ref, tcnt_ref, *, seq, cap, nseg, tt):
    aff = jnp.concatenate([aff_ref[:, s * seq:(s + 1) * seq] for s in range(nseg)], axis=0)
    rows = aff.shape[0]
    capf = jnp.float32(cap)
    thr_bits = jnp.zeros((rows, 1), I32)
    for bit in range(30, -1, -1):
        cand = thr_bits | (1 << bit)
        cnt = jnp.where(aff >= pltpu.bitcast(cand, F32), 1.0, 0.0).sum(axis=1, keepdims=True)
        thr_bits = jnp.where(cnt >= capf, cand, thr_bits)
    thr = pltpu.bitcast(thr_bits, F32)
    gt = aff > thr
    eq = aff == thr
    n_gt = jnp.where(gt, 1.0, 0.0).sum(axis=1, keepdims=True)
    pw = min(seq, 256)
    tri = jnp.where(lax.broadcasted_iota(I32, (pw, pw), 0) < lax.broadcasted_iota(I32, (pw, pw), 1),
                    1.0, 0.0).astype(BF16)

    def count_before(flag):
        ones = jnp.where(flag, 1.0, 0.0)
        parts = []
        run = jnp.zeros((rows, 1), F32)
        for c0 in range(0, seq, pw):
            blk = ones[:, c0:c0 + pw]
            parts.append(_dot(blk.astype(BF16), tri) + run)
            run = run + blk.sum(axis=1, keepdims=True)
        return jnp.concatenate(parts, axis=1) if len(parts) > 1 else parts[0]

    sel = gt | (eq & (count_before(eq) < capf - n_gt))
    rank = count_before(sel)
    expert = lax.broadcasted_iota(I32, (rows, seq), 0) & (N_EXPERTS - 1)
    slot = jnp.where(sel, expert * cap + rank.astype(I32), -1)
    gate = jnp.where(sel, aff, 0.0)
    pos = jnp.where(sel, rank, -1.0)
    nt = seq // tt
    tile_of = jnp.where((lax.broadcasted_iota(I32, (seq, nt), 0) // tt) == lax.broadcasted_iota(I32, (seq, nt), 1),
                        1.0, 0.0).astype(BF16)
    tcnt = _dot(jnp.where(sel, 1.0, 0.0).astype(BF16), tile_of)
    for s in range(nseg):
        rows_s = slice(s * N_EXPERTS, (s + 1) * N_EXPERTS)
        slot_ref[:, s * seq:(s + 1) * seq] = slot[rows_s, :]
        gate_ref[:, s * seq:(s + 1) * seq] = gate[rows_s, :]
        pos_ref[:, s * seq:(s + 1) * seq] = pos[rows_s, :]
        tcnt_ref[0, :, s * nt:(s + 1) * nt] = tcnt[rows_s, :]


def _route(aff_t, seq, cap, nseg, tt):
    n = aff_t.shape[1]
    nt = seq // tt
    steps = n // (nseg * seq)
    spec = pl.BlockSpec((N_EXPERTS, nseg * seq), lambda i: (0, i))
    slot, gate, pos, tcnt = pl.pallas_call(
        functools.partial(_route_kernel, seq=seq, cap=cap, nseg=nseg, tt=tt),
        grid=(steps,),
        in_specs=[spec],
        out_specs=[spec, spec, spec, pl.BlockSpec((1, N_EXPERTS, nseg * nt), lambda i: (i, 0, 0))],
        out_shape=[jax.ShapeDtypeStruct((N_EXPERTS, n), I32), jax.ShapeDtypeStruct((N_EXPERTS, n), F32),
                   jax.ShapeDtypeStruct((N_EXPERTS, n), F32),
                   jax.ShapeDtypeStruct((steps, N_EXPERTS, nseg * nt), F32)],
        compiler_params=_params("parallel"),
        name="route",
    )(aff_t)
    return slot, gate, pos, tcnt.transpose(1, 0, 2).reshape(N_EXPERTS, n // tt)


def _dispatch_kernel(slot_ref, h_ref, x_ref, *, cap, seq):
    m = N_EXPERTS * cap
    slot_id = lax.broadcasted_iota(I32, (m, seq), 0)
    for r in range(h_ref.shape[0] // seq):
        slots = slot_ref[:, r * seq:(r + 1) * seq]
        owner = jnp.broadcast_to(slots[:, None, :], (N_EXPERTS, cap, seq)).reshape(m, seq)
        sel = jnp.where(owner == slot_id, 1.0, 0.0).astype(BF16)
        x_ref[r * m:(r + 1) * m, :] = _dot(sel, h_ref[r * seq:(r + 1) * seq, :]).astype(x_ref.dtype)


def _dispatch(slot, h, seq, cap):
    n = h.shape[0]
    nb = n // seq
    rb = MOE_RB if nb % MOE_RB == 0 else 1
    return pl.pallas_call(
        functools.partial(_dispatch_kernel, cap=cap, seq=seq),
        grid=(nb // rb,),
        in_specs=[pl.BlockSpec((N_EXPERTS, rb * seq), lambda b: (0, b)),
                  pl.BlockSpec((rb * seq, D_MODEL), lambda b: (b, 0))],
        out_specs=pl.BlockSpec((rb * N_EXPERTS * cap, D_MODEL), lambda b: (b, 0)),
        out_shape=jax.ShapeDtypeStruct((nb * N_EXPERTS * cap, D_MODEL), BF16),
        compiler_params=_params("parallel"),
        name="moe_dispatch",
    )(slot, h)


def _window(cum_ref, base, e, cap, k=0):
    lo = ((cum_ref[base + e] >> 4) << 4) + k * MOE_W
    return lo, pl.multiple_of(jnp.minimum(lo, cap - MOE_W), 16)


def _extra_windows(cum_ref, base, e):
    lo = (cum_ref[base + e] >> 4) << 4
    return (cum_ref[base + N_EXPERTS + e] - lo + (MOE_W - 1)) >> 6


def _dispatch_win_kernel(cum_ref, slot_ref, h_ref, x_ref, *, cap, nt):
    b, i = pl.program_id(0), pl.program_id(1)
    tt = h_ref.shape[0]
    base = (b * (nt + 1) + i) * N_EXPERTS
    h = h_ref[...]
    row = lax.broadcasted_iota(I32, (MOE_W, tt), 0)

    @pl.when(i == 0)
    def _():
        x_ref[...] = jnp.zeros_like(x_ref)

    def hits(e, lo, ws):
        srow = slot_ref[e:e + 1, :]
        return (srow == row + (e * cap + ws)) & (srow >= e * cap + lo)

    for grp in range(N_EXPERTS // MOE_EG):
        wins = [(e,) + _window(cum_ref, base, e, cap) for e in range(grp * MOE_EG, (grp + 1) * MOE_EG)]
        sel = jnp.concatenate([hits(e, lo, ws) for e, lo, ws in wins], axis=0)
        x = _dot(jnp.where(sel, 1.0, 0.0).astype(BF16), h)
        for q, (e, lo, ws) in enumerate(wins):
            dst = pl.ds(e * cap + ws, MOE_W)
            x_ref[dst, :] += x[q * MOE_W:(q + 1) * MOE_W].astype(x_ref.dtype)

    for e in range(N_EXPERTS):
        def extra(k, carry, e=e):
            lo, ws = _window(cum_ref, base, e, cap, k)
            x = _dot(jnp.where(hits(e, lo, ws), 1.0, 0.0).astype(BF16), h)
            x_ref[pl.ds(e * cap + ws, MOE_W), :] += x.astype(x_ref.dtype)
            return carry
        lax.fori_loop(1, _extra_windows(cum_ref, base, e), extra, 0)


def _dispatch_win(cum, slot, h, seq, cap):
    n = h.shape[0]
    nb, nt = n // seq, seq // MOE_TT
    return pl.pallas_call(
        functools.partial(_dispatch_win_kernel, cap=cap, nt=nt),
        grid_spec=pltpu.PrefetchScalarGridSpec(
            num_scalar_prefetch=1,
            grid=(nb, nt),
            in_specs=[pl.BlockSpec((N_EXPERTS, MOE_TT), lambda b, i, c: (0, b * nt + i)),
                      pl.BlockSpec((MOE_TT, D_MODEL), lambda b, i, c: (b * nt + i, 0))],
            out_specs=pl.BlockSpec((N_EXPERTS * cap, D_MODEL), lambda b, i, c: (b, 0))),
        out_shape=jax.ShapeDtypeStruct((nb * N_EXPERTS * cap, D_MODEL), BF16),
        compiler_params=_params("parallel", "arbitrary"),
        name="moe_dispatch_win",
    )(cum, slot, h)


FFN_TF = 512
FFN_RC = 512


def _ffn_kernel(xa_ref, xb_ref, wg_ref, wu_ref, wd_ref, ya_ref, yb_ref, acc_ref):
    j = pl.program_id(1)

    @pl.when(j == 0)
    def _():
        acc_ref[...] = jnp.zeros_like(acc_ref)

    wg = wg_ref[0].astype(BF16)
    wu = wu_ref[0].astype(BF16)
    wd = wd_ref[0].astype(BF16)
    ra = xa_ref.shape[0] * xa_ref.shape[2]

    def row_chunks(ref, base):
        nb, _, cap, d = ref.shape
        rc = min(FFN_RC, nb * cap)
        for r0 in range(0, nb * cap, rc):
            if cap >= rc:
                b, c0 = divmod(r0, cap)
                yield base + r0, ref[b, 0, c0:c0 + rc, :]
            else:
                yield base + r0, ref[r0 // cap:(r0 + rc) // cap, 0, :, :].reshape(rc, d)

    for r0, x in itertools.chain(row_chunks(xa_ref, 0), row_chunks(xb_ref, ra)):
        rc = x.shape[0]
        g = _dot(x, wg)
        u = _dot(x, wu)
        mid = (g * _sigmoid(g) * u).astype(BF16)
        acc_ref[r0:r0 + rc, :] += _dot(mid, wd)

    @pl.when(j == pl.num_programs(1) - 1)
    def _():
        for ref, base in ((ya_ref, 0), (yb_ref, ra)):
            nb, _, cap, d = ref.shape
            ref[...] = acc_ref[base:base + nb * cap, :].reshape(nb, 1, cap, d).astype(ref.dtype)


def _ffn(xa, xb, w_gate, w_up, w_down):
    ba, _, ca, d = xa.shape
    bb, _, cb, _ = xb.shape
    nj = D_FF // FFN_TF
    xa_spec = pl.BlockSpec((ba, 1, ca, d), lambda e, j: (0, e, 0, 0))
    xb_spec = pl.BlockSpec((bb, 1, cb, d), lambda e, j: (0, e, 0, 0))
    return pl.pallas_call(
        _ffn_kernel,
        grid=(N_EXPERTS, nj),
        in_specs=[xa_spec, xb_spec,
                  pl.BlockSpec((1, d, FFN_TF), lambda e, j: (e, 0, j)),
                  pl.BlockSpec((1, d, FFN_TF), lambda e, j: (e, 0, j)),
                  pl.BlockSpec((1, FFN_TF, d), lambda e, j: (e, j, 0))],
        out_specs=[xa_spec, xb_spec],
        out_shape=[jax.ShapeDtypeStruct(xa.shape, BF16), jax.ShapeDtypeStruct(xb.shape, BF16)],
        scratch_shapes=[pltpu.VMEM((ba * ca + bb * cb, d), F32)],
        compiler_params=_params("parallel", "arbitrary"),
        name="moe_ffn",
    )(xa, xb, w_gate, w_up, w_down)


def _expand(vals_bf, first_expert, width, total):
    e_of_lane = first_expert + lax.broadcasted_iota(I32, (N_EXPERTS, total), 1) // width
    pick = jnp.where(lax.broadcasted_iota(I32, (N_EXPERTS, total), 0) == e_of_lane, 1.0, 0.0).astype(BF16)
    return _dot(vals_bf, pick)


def _combine_kernel(pos_ref, gate_ref, y_ref, x_ref, mod_ref, o_ref, *, cap, seq):
    m = N_EXPERTS * cap
    rank = (lax.broadcasted_iota(I32, (seq, m), 1) % cap).astype(F32)
    for r in range(x_ref.shape[0] // seq):
        rows = slice(r * seq, (r + 1) * seq)
        pos = _expand(pos_ref[rows, :].astype(BF16), 0, cap, m)
        gate = _expand(gate_ref[rows, :].astype(BF16), 0, cap, m)
        w = jnp.where(pos == rank, gate, 0.0).astype(BF16)
        o_ref[rows, :] = x_ref[rows, :] + mod_ref[0, 5:6, :] * _dot(w, y_ref[r * m:(r + 1) * m, :])


def _combine(pos_t, gate_t, y, x, mod, *, seq, cap):
    n = x.shape[0]
    nb = n // seq
    rb = MOE_RB if nb % MOE_RB == 0 else 1
    row = lambda b: (b, 0)
    return pl.pallas_call(
        functools.partial(_combine_kernel, cap=cap, seq=seq),
        grid=(nb // rb,),
        in_specs=[pl.BlockSpec((rb * seq, N_EXPERTS), row),
                  pl.BlockSpec((rb * seq, N_EXPERTS), row),
                  pl.BlockSpec((rb * N_EXPERTS * cap, D_MODEL), row),
                  pl.BlockSpec((rb * seq, D_MODEL), row),
                  pl.BlockSpec((1, 8, D_MODEL), lambda b: (0, 0, 0))],
        out_specs=pl.BlockSpec((rb * seq, D_MODEL), row),
        out_shape=jax.ShapeDtypeStruct((n, D_MODEL), F32),
        compiler_params=_params("parallel"),
        name="moe_combine",
    )(pos_t, gate_t, y, x, mod)


def _combine_win_kernel(cum_ref, pos_ref, gate_ref, y_ref, x_ref, mod_ref, o_ref, acc_ref, *, cap, nt):
    b, i = pl.program_id(0), pl.program_id(1)
    tt = x_ref.shape[0]
    base = (b * (nt + 1) + i) * N_EXPERTS
    width = MOE_EG * MOE_W
    posb = pos_ref[...].astype(BF16)
    gateb = gate_ref[...].astype(BF16)
    lane = lax.broadcasted_iota(I32, (1, width), 1)
    offset = (lane & (MOE_W - 1)).astype(F32)
    acc = jnp.zeros((tt, D_MODEL), F32)
    for grp in range(N_EXPERTS // MOE_EG):
        wins = [(e,) + _window(cum_ref, base, e, cap) for e in range(grp * MOE_EG, (grp + 1) * MOE_EG)]
        lo_l = jnp.zeros((1, width), F32)
        ws_l = jnp.zeros((1, width), F32)
        for q, (e, lo, ws) in enumerate(wins):
            mine = (lane >> 6) == q
            lo_l = jnp.where(mine, lo.astype(F32), lo_l)
            ws_l = jnp.where(mine, ws.astype(F32), ws_l)
        pos = _expand(posb, grp * MOE_EG, MOE_W, width)
        gate = _expand(gateb, grp * MOE_EG, MOE_W, width)
        w = jnp.where((pos - ws_l == offset) & (pos >= lo_l), gate, 0.0).astype(BF16)
        ywin = jnp.concatenate([y_ref[pl.ds(e * cap + ws, MOE_W), :] for e, lo, ws in wins], axis=0)
        acc = acc + _dot(w, ywin)
    acc_ref[...] = acc

    off64 = lax.broadcasted_iota(I32, (1, MOE_W), 1).astype(F32)
    for e in range(N_EXPERTS):
        def extra(k, carry, e=e):
            lo, ws = _window(cum_ref, base, e, cap, k)
            pos = pos_ref[:, e:e + 1]
            gate = gate_ref[:, e:e + 1].astype(BF16).astype(F32)
            w = jnp.where((pos - ws.astype(F32) == off64) & (pos >= lo.astype(F32)), gate, 0.0).astype(BF16)
            acc_ref[...] += _dot(w, y_ref[pl.ds(e * cap + ws, MOE_W), :])
            return carry
        lax.fori_loop(1, _extra_windows(cum_ref, base, e), extra, 0)

    o_ref[...] = x_ref[...] + mod_ref[0, 5:6, :] * acc_ref[...]


def _combine_win(cum, pos_t, gate_t, y, x, mod, *, seq, cap):
    n = x.shape[0]
    nb, nt = n // seq, seq // MOE_TT
    row = lambda b, i, c: (b * nt + i, 0)
    return pl.pallas_call(
        functools.partial(_combine_win_kernel, cap=cap, nt=nt),
        grid_spec=pltpu.PrefetchScalarGridSpec(
            num_scalar_prefetch=1,
            grid=(nb, nt),
            in_specs=[pl.BlockSpec((MOE_TT, N_EXPERTS), row),
                      pl.BlockSpec((MOE_TT, N_EXPERTS), row),
                      pl.BlockSpec((N_EXPERTS * cap, D_MODEL), lambda b, i, c: (b, 0)),
                      pl.BlockSpec((MOE_TT, D_MODEL), row),
                      pl.BlockSpec((1, 8, D_MODEL), lambda b, i, c: (1 + b, 0, 0))],
            out_specs=pl.BlockSpec((MOE_TT, D_MODEL), row),
            scratch_shapes=[pltpu.VMEM((MOE_TT, D_MODEL), F32)]),
        out_shape=jax.ShapeDtypeStruct((n, D_MODEL), F32),
        compiler_params=_params("parallel", "parallel"),
        name="moe_combine_win",
    )(cum, pos_t, gate_t, y, x, mod)


def _moe_pair(hp, affp, x1p, hs, affs, x1s, mod, w_gate, w_up, w_down, seq_p, seq_s):
    n_p, n_s = hp.shape[0], hs.shape[0]
    nb_p, nb_s = n_p // seq_p, n_s // seq_s
    cap_p = EC_FACTOR * seq_p // N_EXPERTS
    cap_s = EC_FACTOR * seq_s // N_EXPERTS
    assert N_EXPERTS * cap_p <= 512 and cap_s >= MOE_W and cap_s % 16 == 0 and seq_s % MOE_TT == 0
    slot_p, gate_p, pos_p, _ = _route(affp, seq_p, cap_p, nseg=min(8, nb_p), tt=seq_p)
    slot_s, gate_s, pos_s, tcnt = _route(affs, seq_s, cap_s, nseg=min(4, nb_s), tt=MOE_TT)
    nt = seq_s // MOE_TT
    counts = tcnt.T.reshape(nb_s, nt, N_EXPERTS).astype(I32)
    cum = jnp.concatenate([jnp.zeros((nb_s, 1, N_EXPERTS), I32), jnp.cumsum(counts, axis=1)], axis=1).reshape(-1)
    xp = _dispatch(slot_p, hp, seq_p, cap_p).reshape(nb_p, N_EXPERTS, cap_p, D_MODEL)
    xs = _dispatch_win(cum, slot_s, hs, seq_s, cap_s).reshape(nb_s, N_EXPERTS, cap_s, D_MODEL)
    ys, yp = _ffn(xs, xp, w_gate, w_up, w_down)
    outp = _combine(pos_p.T, gate_p.T, yp.reshape(-1, D_MODEL), x1p, mod, seq=seq_p, cap=cap_p)
    outs = _combine_win(cum, pos_s.T, gate_s.T, ys.reshape(-1, D_MODEL), x1s, mod, seq=seq_s, cap=cap_s)
    return outp, outs


def _ssm_in_kernel(x_ref, mod_ref, gain_ref, wt_ref, ut_ref, *, cols_per_cond):
    tc = x_ref.shape[1]
    if cols_per_cond is None:
        h = _norm_mod(x_ref[0], gain_ref[...], mod_ref[0, 0:1, :], mod_ref[0, 1:2, :]).astype(BF16)
    else:
        first = 1 + pl.program_id(1) * (tc // cols_per_cond)
        parts = []
        for s in range(tc // cols_per_cond):
            m = mod_ref[first + s]
            parts.append(_norm_mod(x_ref[0, s * cols_per_cond:(s + 1) * cols_per_cond, :], gain_ref[...],
                                   m[0:1, :], m[1:2, :]).astype(BF16))
        h = jnp.concatenate(parts, axis=0) if len(parts) > 1 else parts[0]
    ut_ref[0] = _dot(wt_ref[...], h, NT)


def _ssm_in(xperm, mod, gain, wt_bf, *, cols_per_cond, tc):
    l, bk, d = xperm.shape
    assert cols_per_cond is None or tc % cols_per_cond == 0
    return pl.pallas_call(
        functools.partial(_ssm_in_kernel, cols_per_cond=cols_per_cond),
        grid=(l, bk // tc),
        in_specs=[pl.BlockSpec((1, tc, d), lambda j, i: (j, i, 0)),
                  pl.BlockSpec(mod.shape, lambda j, i: (0, 0, 0)),
                  pl.BlockSpec((1, d), lambda j, i: (0, 0)),
                  pl.BlockSpec((d, d), lambda j, i: (0, 0))],
        out_specs=pl.BlockSpec((1, d, tc), lambda j, i: (j, 0, i)),
        out_shape=jax.ShapeDtypeStruct((l, d, bk), F32),
        compiler_params=_params("parallel", "parallel"),
        name="ssm_in",
    )(xperm, mod, gain, wt_bf)


def _ssm_core_kernel(utp_ref, uts_ref, lamp_ref, c_ref, bt_ref, dsk_ref, h0_ref, ytp_ref, yts_ref, fs_ref,
                     *, kp, ks, nbp, nbs):
    rows = SSM_ROWS
    p = SSM_STATE
    lc = SSM_CHUNK
    ri = lax.broadcasted_iota(I32, (rows, rows), 0)
    cj = lax.broadcasted_iota(I32, (rows, rows), 1)
    causal = (ri >> 4) >= (cj >> 4)
    anticausal = (cj >> 4) >= (ri >> 4)
    diag = ri == cj
    leftc = lax.broadcasted_iota(I32, (lc, 128), 1) < p
    nrow = lax.broadcasted_iota(I32, (lc, 128), 0).astype(F32)
    eye = lax.broadcasted_iota(I32, (p, 128), 0) == lax.broadcasted_iota(I32, (p, 128), 1)

    def cmul(ar, ai, xr, xi):
        return ar * xr - ai * xi, ar * xi + ai * xr

    def expand_rows(t):
        return jnp.broadcast_to(t[:, None, :], (lc, SSM_GROUP, 128)).reshape(rows, 128)

    def tile_rows(t):
        return jnp.broadcast_to(t[None, :, :], (lc, SSM_GROUP, 128)).reshape(rows, 128)

    def to_col(row):
        return jnp.where(eye, jnp.broadcast_to(row, (p, 128)), 0.0).sum(axis=1, keepdims=True)

    def operands(gg, d):
        lp = lamp_ref[gg, d]
        lre, lim = lp[0:1], lp[1:2]
        dt = jnp.exp(lp[2:3])
        a, th = lre * dt, lim * dt
        ang = nrow * th
        cs, sn = jnp.cos(ang), jnp.sin(ang)
        ep, em = jnp.exp(nrow * a), jnp.exp(-(nrow * a))
        pr, pi = ep * cs, ep * sn
        nr, ni = em * cs, -(em * sn)
        l1r, l1i = pr[1:2], pi[1:2]
        lmr, lmi = pr[lc - 1:lc], pi[lc - 1:lc]
        llr, lli = cmul(lmr, lmi, l1r, l1i)
        den = lre * lre + lim * lim
        cr = ((l1r - 1.0) * lre + l1i * lim) / den
        ci = (l1i * lre - (l1r - 1.0) * lim) / den
        btr, bti = bt_ref[gg, d, 0], bt_ref[gg, d, 1]
        bbr, bbi = cr * btr - ci * bti, cr * bti + ci * btr
        u1 = tile_rows(jnp.where(leftc, bbr, bbi))
        u2 = tile_rows(jnp.where(leftc, bbi, bbr))
        c1 = tile_rows(c_ref[gg, d, 0])
        c2 = tile_rows(c_ref[gg, d, 1])

        def left_form(xr, xi):
            return (c1 * expand_rows(jnp.where(leftc, xr, -xi))
                    + c2 * expand_rows(jnp.where(leftc, -xi, -xr)))

        def right_form(xr, xi):
            return u1 * expand_rows(xr) + u2 * expand_rows(jnp.where(leftc, -xi, xi))

        if d == 0:
            al = left_form(pr, pi)
            brt = right_form(nr, ni)
            rrt = right_form(*cmul(lmr, lmi, nr, ni))
            qq = left_form(*cmul(l1r, l1i, pr, pi))
            mat = jnp.where(causal, _dot3(al, brt, NT), 0.0)
        else:
            al = left_form(nr, ni)
            brt = right_form(pr, pi)
            rrt = brt
            qq = left_form(*cmul(llr, lli, nr, ni))
            mat = jnp.where(anticausal, _dot3(al, brt, NT), 0.0)
        return mat, rrt.T, qq, to_col(llr), to_col(lli)

    def scan(sr, si, lr, li, h0r, h0i, reverse, nchunk, nbatch, sel):
        bk = sr.shape[1]
        lane = lax.broadcasted_iota(I32, (p, bk), 1)
        kidx = lane & (nchunk - 1)
        edge = (nchunk - 1) if reverse else 0
        if h0r is not None:
            h0cr = jnp.zeros((p, bk), F32)
            h0ci = jnp.zeros((p, bk), F32)
            for b in range(nbatch):
                at = lane == (b * nchunk + edge)
                h0cr = jnp.where(at, h0r[:, b:b + 1], h0cr)
                h0ci = jnp.where(at, h0i[:, b:b + 1], h0ci)
            ar, ai = cmul(lr, li, h0cr, h0ci)
            er, ei = sr + ar, si + ai
        else:
            er, ei = sr, si
        ar, ai = lr, li
        s = 1
        while s < nchunk:
            if reverse:
                ok = kidx < nchunk - s
                tr, ti = pltpu.roll(er, bk - s, 1), pltpu.roll(ei, bk - s, 1)
            else:
                ok = kidx >= s
                tr, ti = pltpu.roll(er, s, 1), pltpu.roll(ei, s, 1)
            tr = jnp.where(ok, tr, 0.0)
            ti = jnp.where(ok, ti, 0.0)
            dr, di = cmul(ar, ai, tr, ti)
            er, ei = er + dr, ei + di
            ar, ai = cmul(ar, ai, ar, ai)
            s *= 2
        if reverse:
            inner = kidx < nchunk - 1
            hr, hi = pltpu.roll(er, bk - 1, 1), pltpu.roll(ei, bk - 1, 1)
        else:
            inner = kidx >= 1
            hr, hi = pltpu.roll(er, 1, 1), pltpu.roll(ei, 1, 1)
        hr = jnp.where(inner, hr, h0cr if h0r is not None else 0.0)
        hi = jnp.where(inner, hi, h0ci if h0r is not None else 0.0)
        fin = None if sel is None else (_dot_sel(er, sel), _dot_sel(ei, sel))
        return hr, hi, fin

    def final_selectors(bk, nchunk, nbatch):
        col = lax.broadcasted_iota(I32, (bk, nbatch), 0)
        bat = lax.broadcasted_iota(I32, (bk, nbatch), 1)
        last = jnp.where(col == bat * nchunk + (nchunk - 1), 1.0, 0.0).astype(BF16)
        first = jnp.where(col == bat * nchunk, 1.0, 0.0).astype(BF16)
        return last, first

    sel_last, sel_first = final_selectors(utp_ref.shape[2], kp, nbp)

    for gg in range(SSM_GB):
        mf, rf, qf, lfr, lfi = operands(gg, 0)
        mb, rb, qb, lbr, lbi = operands(gg, 1)
        skip = jnp.where(diag, jnp.broadcast_to(dsk_ref[gg], (rows, rows)), 0.0)
        stack = jnp.concatenate([mf + mb + skip, rf, rb], axis=0)
        qq = jnp.concatenate([qf, qb], axis=1)
        h0 = h0_ref[gg]
        for ut_ref, yt_ref, nchunk, nbatch, latent in ((utp_ref, ytp_ref, kp, nbp, False),
                                                       (uts_ref, yts_ref, ks, nbs, True)):
            bk = ut_ref.shape[2]
            x = ut_ref[:, gg * SSM_GROUP:(gg + 1) * SSM_GROUP, :].reshape(rows, bk)
            res = _mm(stack, x, SSM_PASSES)
            hfr, hfi, ff = scan(res[rows:rows + p], res[rows + p:rows + 2 * p], lfr, lfi,
                                h0[0] if latent else None, h0[1] if latent else None,
                                False, nchunk, nbatch, None if latent else sel_last)
            hbr, hbi, fb = scan(res[rows + 2 * p:rows + 3 * p], res[rows + 3 * p:rows + 4 * p], lbr, lbi,
                                h0[2] if latent else None, h0[3] if latent else None,
                                True, nchunk, nbatch, None if latent else sel_first)
            states = jnp.concatenate([hfr, hfi, hbr, hbi], axis=0)
            y = res[:rows] + _mm(qq, states, SSM_PASSES)
            yt_ref[:, gg * SSM_GROUP:(gg + 1) * SSM_GROUP, :] = y.reshape(lc, SSM_GROUP, bk)
            if not latent:
                fs_ref[gg, 0] = ff[0]
                fs_ref[gg, 1] = ff[1]
                fs_ref[gg, 2] = fb[0]
                fs_ref[gg, 3] = fb[1]


def _ssm_core(utp, uts, ops, h0, *, kp, ks, nbp, nbs):
    lamp, c2, bt2, dsk = ops
    l, d, bkp = utp.shape
    bks = uts.shape[2]
    g = SSM_GROUPS
    gb = SSM_GB
    lead4 = lambda i: (i, 0, 0, 0)
    lead5 = lambda i: (i, 0, 0, 0, 0)
    ut_spec = lambda bk: pl.BlockSpec((l, gb * SSM_GROUP, bk), lambda i: (0, i, 0))
    return pl.pallas_call(
        functools.partial(_ssm_core_kernel, kp=kp, ks=ks, nbp=nbp, nbs=nbs),
        grid=(g // gb,),
        in_specs=[ut_spec(bkp), ut_spec(bks),
                  pl.BlockSpec((gb, 2, 8, 128), lead4),
                  pl.BlockSpec((gb, 2, 2, SSM_GROUP, 128), lead5),
                  pl.BlockSpec((gb, 2, 2, SSM_GROUP, 128), lead5),
                  pl.BlockSpec((gb, 1, SSM_ROWS), lambda i: (i, 0, 0)),
                  pl.BlockSpec((gb, 4, SSM_STATE, nbs), lead4)],
        out_specs=[ut_spec(bkp), ut_spec(bks),
                   pl.BlockSpec((gb, 4, SSM_STATE, nbp), lead4)],
        out_shape=[jax.ShapeDtypeStruct((l, d, bkp), F32),
                   jax.ShapeDtypeStruct((l, d, bks), F32),
                   jax.ShapeDtypeStruct((g, 4, SSM_STATE, nbp), F32)],
        compiler_params=_params("parallel"),
        name="ssm_core",
    )(utp, uts, lamp, c2, bt2, dsk, h0)


def _ssm_out_kernel(yt_ref, w_ref, m_ref):
    y = yt_ref[0].T
    act = 0.5 * y * (1.0 + jnp.tanh(0.7978845608028654 * (y + 0.044715 * (y * y * y))))
    ag = _dot(act.astype(BF16), w_ref[...])
    d = m_ref.shape[2]
    m_ref[0] = ag[:, :d] * _sigmoid(ag[:, d:])


def _ssm_out(yt, w_bf, tc):
    l, d, bk = yt.shape
    return pl.pallas_call(
        _ssm_out_kernel,
        grid=(l, bk // tc),
        in_specs=[pl.BlockSpec((1, d, tc), lambda j, i: (j, 0, i)),
                  pl.BlockSpec((d, 2 * d), lambda j, i: (0, 0))],
        out_specs=pl.BlockSpec((1, tc, d), lambda j, i: (j, i, 0)),
        out_shape=jax.ShapeDtypeStruct((l, bk, d), F32),
        compiler_params=_params("parallel", "parallel"),
        name="ssm_out",
    )(yt, w_bf)


def _ssm_operand_params(lam_re, lam_im, b_re, b_im, c_re, c_im, log_dt, d_skip):
    g, l = SSM_GROUPS, SSM_CHUNK
    dup = lambda t: jnp.concatenate([t, t], axis=-1)
    lamp = jnp.stack([lam_re, lam_im, jnp.broadcast_to(log_dt[..., None], lam_re.shape)], axis=2)
    lamp = dup(jnp.pad(lamp, ((0, 0), (0, 0), (0, 5), (0, 0)))).transpose(1, 0, 2, 3)
    c2 = dup(jnp.stack([c_re, c_im], axis=2)).transpose(1, 0, 2, 3, 4)
    bt2 = dup(jnp.stack([jnp.swapaxes(b_re, -1, -2), jnp.swapaxes(b_im, -1, -2)], axis=2)).transpose(1, 0, 2, 3, 4)
    dsk = jnp.tile(d_skip.reshape(g, 1, SSM_GROUP), (1, 1, l))
    return lamp, c2, bt2, dsk


def _to_chunks(x, nb, seq):
    k = seq // SSM_CHUNK
    return x.reshape(nb, k, SSM_CHUNK, -1).transpose(2, 0, 1, 3).reshape(SSM_CHUNK, nb * k, -1)


def _from_chunks(x, nb, seq):
    k = seq // SSM_CHUNK
    return x.reshape(SSM_CHUNK, nb, k, -1).transpose(1, 2, 0, 3).reshape(nb * seq, -1)


def _ssm_mixers(xp, xs, mod, gain, wt_bf, ops, w_out_bf, h0, *, nbp, sp, nbs, ss):
    kp, ks = sp // SSM_CHUNK, ss // SSM_CHUNK
    tcp, tcs = min(SSM_TC, nbp * kp), min(SSM_TC, nbs * ks)
    utp = _ssm_in(_to_chunks(xp, nbp, sp), mod, gain, wt_bf, cols_per_cond=None, tc=tcp)
    uts = _ssm_in(_to_chunks(xs, nbs, ss), mod, gain, wt_bf, cols_per_cond=ks, tc=tcs)
    ytp, yts, fs = _ssm_core(utp, uts, ops, h0, kp=kp, ks=ks, nbp=nbp, nbs=nbs)
    mp = _from_chunks(_ssm_out(ytp, w_out_bf, tcp), nbp, sp)
    ms = _from_chunks(_ssm_out(yts, w_out_bf, tcs), nbs, ss)
    return mp, ms, fs


def _rope_tables(seq):
    t = jnp.arange(seq)
    row = (t // GRID_W).astype(F32)
    col = (t % GRID_W).astype(F32)
    n_freq = HEAD_DIM // 4
    inv_freq = ROPE_THETA ** (-jnp.arange(n_freq, dtype=F32) / n_freq)
    ang = jnp.concatenate([row[:, None] * inv_freq, col[:, None] * inv_freq], axis=-1)
    cos = jnp.repeat(jnp.cos(ang), 2, axis=-1)
    sin = jnp.repeat(jnp.sin(ang), 2, axis=-1)
    sign = jnp.tile(jnp.array([-1.0, 1.0], F32), HEAD_DIM // 2)
    return jnp.tile(cos, (1, 4)), jnp.tile(sin * sign, (1, 4))


def _head_gains(qn_a, kn_a, qn_b, kn_b):
    scale = HEAD_DIM ** -0.5 * LOG2E
    ones = jnp.ones((N_KV * HEAD_DIM,), F32)
    return jnp.concatenate([jnp.tile(qn_a, N_HEADS) * scale, jnp.tile(kn_a, N_KV), ones,
                            jnp.tile(qn_b, N_HEADS) * scale, jnp.tile(kn_b, N_KV), ones]).reshape(1, QKV_COLS)


def kernel(x_prompt, x_sample, c, cache_k_a_l0, cache_v_a_l0, cache_k_b_l0, cache_v_b_l0, state_ssm_re_l1, state_ssm_im_l1, c_ctx, mod_w_l0, mod_b_l0, norm_mix_l0, attn_w_in_l0, q_norm_a_l0, k_norm_a_l0, q_norm_b_l0, k_norm_b_l0, sink_b_l0, attn_w_out_l0, norm_ffn_l0, router_l0, moe_w_gate_l0, moe_w_up_l0, moe_w_down_l0, mod_w_l1, mod_b_l1, norm_mix_l1, ssm_w_in_l1, ssm_lambda_re_l1, ssm_lambda_im_l1, ssm_b_re_l1, ssm_b_im_l1, ssm_c_re_l1, ssm_c_im_l1, ssm_log_dt_l1, ssm_d_l1, ssm_w_out_l1, norm_ffn_l1, router_l1, moe_w_gate_l1, moe_w_up_l1, moe_w_down_l1):
    bp, sp, d = x_prompt.shape
    bs, ss, _ = x_sample.shape
    past = cache_k_a_l0.shape[1]
    assert d == D_MODEL and bs <= 7 and (bp * sp) % TM == 0 and TM % sp == 0 and ss % TM == 0
    xp = x_prompt.reshape(bp * sp, d)
    xs = x_sample.reshape(bs * ss, d)
    cond8 = jnp.concatenate([c_ctx[None], c, jnp.zeros((7 - bs, d), F32)], axis=0)
    row1 = lambda v: v.reshape(1, -1)

    mod0 = _mod_rows(cond8, mod_w_l0, mod_b_l0)
    w_in = attn_w_in_l0.astype(BF16)
    hgain = _head_gains(q_norm_a_l0, k_norm_a_l0, q_norm_b_l0, k_norm_b_l0)
    lane = np.arange(256)
    bd = jnp.asarray((lane[:, None] // HEAD_DIM == lane[None, :] // HEAD_DIM) / HEAD_DIM, BF16)
    cos_t, sin_t = _rope_tables(ss)
    qp, kap, vap, kbp, vbp = _qkv(xp, mod0, row1(norm_mix_l0), w_in, hgain, bd, cos_t, sin_t,
                                  rows_per_cond=None, seq=sp, rope=False, kv_dtype=F32, transposed_kv=True)
    qs, kas, vas, kbs, vbs = _qkv(xs, mod0, row1(norm_mix_l0), w_in, hgain, bd, cos_t, sin_t,
                                  rows_per_cond=ss, seq=ss, rope=True, kv_dtype=BF16, transposed_kv=False)
    op = _attn_ctx(sink_b_l0, qp, kap, vap, kbp, vbp, sp)
    cache = lambda t: t.reshape(bs, past, N_KV * HEAD_DIM)
    os_ = _attn_lat(sink_b_l0, qs, kas, vas, kbs, vbs, cache(cache_k_a_l0), cache(cache_v_a_l0),
                    cache(cache_k_b_l0), cache(cache_v_b_l0), ss)
    w_out = attn_w_out_l0.astype(BF16)
    x1p, hp, affp = _postmix(op, xp, mod0, w_out, row1(norm_ffn_l0), router_l0.T, rows_per_cond=None, project=True)
    x1s, hs, affs = _postmix(os_, xs, mod0, w_out, row1(norm_ffn_l0), router_l0.T, rows_per_cond=ss, project=True)
    xp, xs = _moe_pair(hp, affp, x1p, hs, affs, x1s, mod0, moe_w_gate_l0, moe_w_up_l0, moe_w_down_l0, sp, ss)

    mod1 = _mod_rows(cond8, mod_w_l1, mod_b_l1)
    ops = _ssm_operand_params(ssm_lambda_re_l1, ssm_lambda_im_l1, ssm_b_re_l1, ssm_b_im_l1, ssm_c_re_l1, ssm_c_im_l1,
                              ssm_log_dt_l1, ssm_d_l1)
    wt = ssm_w_in_l1.T.astype(BF16)
    w_so = ssm_w_out_l1.astype(BF16)
    h0 = jnp.stack([state_ssm_re_l1[:, 0], state_ssm_im_l1[:, 0], state_ssm_re_l1[:, 1], state_ssm_im_l1[:, 1]],
                   axis=0).transpose(2, 0, 3, 1)
    mp, ms, fsp = _ssm_mixers(xp, xs, mod1, row1(norm_mix_l1), wt, ops, w_so, h0, nbp=bp, sp=sp, nbs=bs, ss=ss)
    dummy_w = jnp.zeros((8, 128), BF16)
    x1p, hp, affp = _postmix(mp, xp, mod1, dummy_w, row1(norm_ffn_l1), router_l1.T, rows_per_cond=None, project=False)
    x1s, hs, affs = _postmix(ms, xs, mod1, dummy_w, row1(norm_ffn_l1), router_l1.T, rows_per_cond=ss, project=False)
    xp, xs = _moe_pair(hp, affp, x1p, hs, affs, x1s, mod1, moe_w_gate_l1, moe_w_up_l1, moe_w_down_l1, sp, ss)

    kv_out = lambda t: t.reshape(bp, N_KV, HEAD_DIM, sp).transpose(0, 3, 1, 2)
    fin = fsp.transpose(3, 1, 0, 2)
    ssm_re = jnp.stack([fin[:, 0], fin[:, 2]], axis=1)
    ssm_im = jnp.stack([fin[:, 1], fin[:, 3]], axis=1)
    return (xp.reshape(bp, sp, d), xs.reshape(bs, ss, d), kv_out(kap), kv_out(vap), kv_out(kbp), kv_out(vbp),
            ssm_re, ssm_im)
```

```python
import functools
import itertools

import jax
import jax.numpy as jnp
import numpy as np
from jax import lax
from jax.experimental import pallas as pl
from jax.experimental.pallas import tpu as pltpu

F32, BF16, I32 = jnp.float32, jnp.bfloat16, jnp.int32

D_MODEL = 1024
GRID_W = 64
HEAD_DIM = 64
N_HEADS = 8
N_KV = 2
WINDOW = 128
ROPE_THETA = 10000.0
SSM_GROUP = 16
SSM_GROUPS = D_MODEL // SSM_GROUP
SSM_STATE = 64
N_EXPERTS = 16
EC_FACTOR = 2
D_FF = 2 * D_MODEL
EPS = 1e-6
NEG_INF = -1e30
LOG2E = 1.4426950408889634
QKV_COLS = 2 * (N_HEADS + 2 * N_KV) * HEAD_DIM
ATTN_OUT = 2 * N_HEADS * HEAD_DIM

SSM_CHUNK = 16
SSM_ROWS = SSM_CHUNK * SSM_GROUP
SSM_GB = 4
SSM_PASSES = 1

SSM_TC = 512

TM = 512

BAND_TQ = 256

MOE_TT = 256
MOE_W = 64
MOE_EG = 4
MOE_RB = 4
VMEM_LIMIT = 56 * 1024 * 1024

NN = (((1,), (0,)), ((), ()))
NT = (((1,), (1,)), ((), ()))


def _dot(a, b, dims=NN):
    return lax.dot_general(a, b, dims, preferred_element_type=F32)


def _split2(x):
    hi = x.astype(BF16)
    lo = (x - hi.astype(F32)).astype(BF16)
    return hi, lo


def _split3(x):
    hi = x.astype(BF16)
    r = x - hi.astype(F32)
    mid = r.astype(BF16)
    lo = (r - mid.astype(F32)).astype(BF16)
    return hi, mid, lo


def _dot3(a, b, dims=NN):
    ah, al = _split2(a)
    bh, bl = _split2(b)
    return _dot(ah, bh, dims) + (_dot(ah, bl, dims) + _dot(al, bh, dims))


def _mm(a, b, passes):
    if passes == 1:
        return _dot(a.astype(BF16), b.astype(BF16))
    return _dot3(a, b)


def _sel_dot_t(sel, x):
    hi, mid, lo = _split3(x)
    return _dot(sel, hi, NT) + (_dot(sel, mid, NT) + _dot(sel, lo, NT))


def _sigmoid(x):
    return 1.0 / (1.0 + jnp.exp(-x))


def _norm_mod(x, gain, shift, scale):
    ms = jnp.mean(x * x, axis=-1, keepdims=True)
    y = x * lax.rsqrt(ms + EPS) * gain
    return y * (1.0 + scale) + shift


def _params(*sem):
    return pltpu.CompilerParams(dimension_semantics=sem, vmem_limit_bytes=VMEM_LIMIT)


def _adaln_kernel(c_ref, w_ref, b_ref, o_ref):
    c = c_ref[...]
    s = c * _sigmoid(c)
    o_ref[...] = _dot3(s, w_ref[...]) + b_ref[...]


def _adaln(cond8, w_mod, b_mod):
    d, e = w_mod.shape
    tn = 1536
    return pl.pallas_call(
        _adaln_kernel,
        grid=(e // tn,),
        in_specs=[pl.BlockSpec((8, d), lambda j: (0, 0)),
                  pl.BlockSpec((d, tn), lambda j: (0, j)),
                  pl.BlockSpec((1, tn), lambda j: (0, j))],
        out_specs=pl.BlockSpec((8, tn), lambda j: (0, j)),
        out_shape=jax.ShapeDtypeStruct((8, e), F32),
        compiler_params=_params("parallel"),
        name="adaln",
    )(cond8, w_mod, b_mod.reshape(1, e))


def _mod_rows(cond8, w_mod, b_mod):
    m = _adaln(cond8, w_mod, b_mod).reshape(8, 6, D_MODEL)
    return jnp.pad(m, ((0, 0), (0, 2), (0, 0)))


def _mod_spec(rows_per_cond):
    if rows_per_cond is None:
        return pl.BlockSpec((1, 8, D_MODEL), lambda i: (0, 0, 0))
    return pl.BlockSpec((1, 8, D_MODEL), lambda i: (1 + (i * TM) // rows_per_cond, 0, 0))


def _qkv_kernel(x_ref, mod_ref, gain_ref, w_ref, hg_ref, bd_ref, cos_ref, sin_ref,
                q_ref, ka_ref, va_ref, kb_ref, vb_ref, *, rope, transposed_kv):
    h = _norm_mod(x_ref[...], gain_ref[...], mod_ref[0, 0:1, :], mod_ref[0, 1:2, :])
    proj = _dot(h.astype(BF16), w_ref[...])
    bd = bd_ref[...]

    def head_norm(blk, g):
        ms = _dot((blk * blk).astype(BF16), bd)
        return blk * lax.rsqrt(ms + EPS) * g

    def rotary(blk):
        w = blk.shape[1]
        even = (lax.broadcasted_iota(I32, blk.shape, 1) & 1) == 0
        swapped = jnp.where(even, pltpu.roll(blk, w - 1, 1), pltpu.roll(blk, 1, 1))
        return blk * cos_ref[:, :w] + swapped * sin_ref[:, :w]

    def qk(c0):
        blk = head_norm(proj[:, c0:c0 + 256], hg_ref[:, c0:c0 + 256])
        return rotary(blk) if rope else blk

    q_ref[:, 0:256] = qk(0).astype(q_ref.dtype)
    q_ref[:, 256:512] = qk(256).astype(q_ref.dtype)
    q_ref[:, 512:768] = qk(768).astype(q_ref.dtype)
    q_ref[:, 768:1024] = qk(1024).astype(q_ref.dtype)
    kva = qk(512)
    kvb = qk(1280)
    outs = ((ka_ref, kva[:, :128]), (va_ref, proj[:, 640:768]), (kb_ref, kvb[:, :128]), (vb_ref, proj[:, 1408:1536]))
    for ref, val in outs:
        if transposed_kv:
            seq = ref.shape[2]
            for r in range(ref.shape[0]):
                ref[r] = val[r * seq:(r + 1) * seq].T.astype(ref.dtype)
        else:
            ref[...] = val.astype(ref.dtype)


def _qkv(x, mod, gain, w_bf, hgain, bd, cos_t, sin_t, *, rows_per_cond, seq, rope, kv_dtype, transposed_kv):
    n = x.shape[0]
    tiles_per_seq = max(1, seq // TM)
    row = lambda i: (i, 0)
    const = lambda i: (0, 0)
    pos = lambda i: (i % tiles_per_seq, 0)
    if transposed_kv:
        assert TM % seq == 0
        kv_shape = jax.ShapeDtypeStruct((n // seq, 128, seq), kv_dtype)
        kv_spec = pl.BlockSpec((TM // seq, 128, seq), lambda i: (i, 0, 0))
    else:
        kv_shape = jax.ShapeDtypeStruct((n, 128), kv_dtype)
        kv_spec = pl.BlockSpec((TM, 128), row)
    return pl.pallas_call(
        functools.partial(_qkv_kernel, rope=rope, transposed_kv=transposed_kv),
        grid=(n // TM,),
        in_specs=[pl.BlockSpec((TM, D_MODEL), row),
                  _mod_spec(rows_per_cond),
                  pl.BlockSpec((1, D_MODEL), const),
                  pl.BlockSpec((D_MODEL, QKV_COLS), const),
                  pl.BlockSpec((1, QKV_COLS), const),
                  pl.BlockSpec((256, 256), const),
                  pl.BlockSpec((TM, 256), pos),
                  pl.BlockSpec((TM, 256), pos)],
        out_specs=[pl.BlockSpec((TM, ATTN_OUT), row)] + [kv_spec] * 4,
        out_shape=[jax.ShapeDtypeStruct((n, ATTN_OUT), BF16)] + [kv_shape] * 4,
        compiler_params=_params("parallel"),
        name="qkv_rope" if rope else "qkv",
    )(x, mod, gain, w_bf, hgain, bd, cos_t, sin_t)


def _pad_variants(kk, ones=False):
    lane = lax.broadcasted_iota(I32, kk.shape, 1)
    left = lane < HEAD_DIM
    rolled = pltpu.roll(kk, HEAD_DIM, 1)
    fill_r = jnp.where(lane == HEAD_DIM, 1.0, 0.0) if ones else jnp.zeros_like(kk)
    fill_l = jnp.where(lane == 0, 1.0, 0.0) if ones else jnp.zeros_like(kk)
    return {(0, 0): jnp.where(left, kk, fill_r).astype(BF16),
            (0, 1): jnp.where(left, fill_l, rolled).astype(BF16),
            (1, 0): jnp.where(left, rolled, fill_r).astype(BF16),
            (1, 1): jnp.where(left, fill_l, kk).astype(BF16)}


def _pad_variants_t(kt):
    top = lax.broadcasted_iota(I32, kt.shape, 0) < HEAD_DIM
    zero = jnp.zeros((HEAD_DIM, kt.shape[1]), F32)
    return {(0, 0): jnp.where(top, kt, 0.0).astype(BF16),
            (0, 1): jnp.concatenate([zero, kt[:HEAD_DIM]], axis=0).astype(BF16),
            (1, 0): jnp.concatenate([kt[HEAD_DIM:], zero], axis=0).astype(BF16),
            (1, 1): jnp.where(top, 0.0, kt).astype(BF16)}


def _head_attention_small(qp, kblk, vblk, sink):
    s = _dot(qp, kblk, NN)
    m = s.max(axis=-1, keepdims=True)
    if sink is not None:
        m = jnp.maximum(m, sink)
    p = jnp.exp2(s - m)
    den = p.sum(axis=-1, keepdims=True)
    if sink is not None:
        den = den + jnp.exp2(sink - m)
    return _dot(p.astype(BF16), vblk, NT) / den


def _head_attention(qp, keys, vals, masks, sink, par):
    scores = []
    for kblk, mask in zip(keys, masks):
        s = _dot(qp, kblk, NT)
        if mask is not None:
            s = jnp.where(mask, s, NEG_INF)
        scores.append(s)
    m = scores[0].max(axis=-1, keepdims=True)
    for s in scores[1:]:
        m = jnp.maximum(m, s.max(axis=-1, keepdims=True))
    if sink is not None:
        m = jnp.maximum(m, sink)
    out = None
    for s, vblk in zip(scores, vals):
        o = _dot(jnp.exp2((s - m).astype(BF16)), vblk)
        out = o if out is None else out + o
    ones_lane = HEAD_DIM if par == 0 else 0
    den = out[:, ones_lane:ones_lane + 1]
    if sink is not None:
        den = den + jnp.exp2(sink - m)
    own = (lax.broadcasted_iota(I32, out.shape, 1) < HEAD_DIM) == (par == 0)
    return jnp.where(own, out / den, 0.0)


def _attn_ctx_kernel(sink_ref, q_ref, ka_ref, va_ref, kb_ref, vb_ref, o_ref):
    for mixer, (k_ref, v_ref) in enumerate(((ka_ref, va_ref), (kb_ref, vb_ref))):
        kvar = _pad_variants_t(k_ref[0])
        vvar = _pad_variants_t(v_ref[0])
        for t in range(4):
            tile = mixer * 4 + t
            kv = t // 2
            qp = q_ref[:, tile * 128:(tile + 1) * 128]
            acc = None
            for par in range(2):
                sink = sink_ref[2 * t + par] * LOG2E if mixer == 1 else None
                o = _head_attention_small(qp, kvar[(kv, par)], vvar[(kv, par)], sink)
                acc = o if acc is None else acc + o
            o_ref[:, tile * 128:(tile + 1) * 128] = acc.astype(o_ref.dtype)


def _attn_ctx(sink, q, ka, va, kb, vb, seq):
    n = q.shape[0]
    row = lambda b: (b, 0)
    kv_spec = pl.BlockSpec((1, 128, seq), lambda b: (b, 0, 0))
    return pl.pallas_call(
        _attn_ctx_kernel,
        grid=(n // seq,),
        in_specs=[pl.BlockSpec(memory_space=pltpu.SMEM),
                  pl.BlockSpec((seq, ATTN_OUT), row), kv_spec, kv_spec, kv_spec, kv_spec],
        out_specs=pl.BlockSpec((seq, ATTN_OUT), row),
        out_shape=jax.ShapeDtypeStruct((n, ATTN_OUT), BF16),
        compiler_params=_params("parallel"),
        name="attn_ctx",
    )(sink, q, ka, va, kb, vb)


def _attn_lat_kernel(sink_ref, q_ref, ka_ref, va_ref, kb_ref, vb_ref,
                     cka_ref, cva_ref, ckb_ref, cvb_ref, o_ref, *, tq, seq):
    qi = pl.program_id(1)
    ck = _pad_variants(cka_ref[0])
    cv = _pad_variants(cva_ref[0], ones=True)
    lk = _pad_variants(ka_ref[...].astype(F32))
    lv = _pad_variants(va_ref[...].astype(F32), ones=True)
    for t in range(4):
        kv = t // 2
        qp = q_ref[:, t * 128:(t + 1) * 128]
        acc = None
        for par in range(2):
            o = _head_attention(qp, [ck[(kv, par)], lk[(kv, par)]], [cv[(kv, par)], lv[(kv, par)]],
                                [None, None], None, par)
            acc = o if acc is None else acc + o
        o_ref[:, t * 128:(t + 1) * 128] = acc.astype(o_ref.dtype)
    span = BAND_TQ + 2 * WINDOW
    ck = _pad_variants(ckb_ref[0])
    cv = _pad_variants(cvb_ref[0], ones=True)
    for sub in range(tq // BAND_TQ):
        q0 = qi * tq + sub * BAND_TQ
        rows = slice(sub * BAND_TQ, (sub + 1) * BAND_TQ)
        lo = pl.multiple_of(jnp.clip(q0 - WINDOW, 0, seq - span), 128)
        qpos = q0 + lax.broadcasted_iota(I32, (BAND_TQ, span), 0)
        kpos = lo + lax.broadcasted_iota(I32, (BAND_TQ, span), 1)
        band = jnp.abs(qpos - kpos) <= WINDOW
        lk = _pad_variants(kb_ref[pl.ds(lo, span), :].astype(F32))
        lv = _pad_variants(vb_ref[pl.ds(lo, span), :].astype(F32), ones=True)
        for t in range(4):
            kv = t // 2
            tile = 4 + t
            qp = q_ref[rows, tile * 128:(tile + 1) * 128]
            acc = None
            for par in range(2):
                sink = sink_ref[2 * t + par] * LOG2E
                o = _head_attention(qp, [lk[(kv, par)], ck[(kv, par)]], [lv[(kv, par)], cv[(kv, par)]],
                                    [band, None], sink, par)
                acc = o if acc is None else acc + o
            o_ref[rows, tile * 128:(tile + 1) * 128] = acc.astype(o_ref.dtype)


def _attn_lat(sink, q, ka, va, kb, vb, cka, cva, ckb, cvb, seq, tq=512):
    n = q.shape[0]
    nb = n // seq
    nq = seq // tq
    qrow = lambda b, i: (b * nq + i, 0)
    brow = lambda b, i: (b, 0)
    kv_spec = pl.BlockSpec((seq, 128), brow)
    past = cka.shape[1]
    c_spec = pl.BlockSpec((1, past, 128), lambda b, i: (b, 0, 0))
    return pl.pallas_call(
        functools.partial(_attn_lat_kernel, tq=tq, seq=seq),
        grid=(nb, nq),
        in_specs=[pl.BlockSpec(memory_space=pltpu.SMEM),
                  pl.BlockSpec((tq, ATTN_OUT), qrow), kv_spec, kv_spec, kv_spec, kv_spec,
                  c_spec, c_spec, c_spec, c_spec],
        out_specs=pl.BlockSpec((tq, ATTN_OUT), qrow),
        out_shape=jax.ShapeDtypeStruct((n, ATTN_OUT), BF16),
        compiler_params=_params("parallel", "parallel"),
        name="attn_lat",
    )(sink, q, ka, va, kb, vb, cka, cva, ckb, cvb)


def _postmix_kernel(m_ref, x_ref, mod_ref, w_ref, gain_ref, rt_ref, x1_ref, h2_ref, aff_ref, *, project):
    if project:
        m = _dot(m_ref[...], w_ref[...])
    else:
        m = m_ref[...]
    x1 = x_ref[...] + mod_ref[0, 2:3, :] * m
    x1_ref[...] = x1
    h2 = _norm_mod(x1, gain_ref[...], mod_ref[0, 3:4, :], mod_ref[0, 4:5, :])
    h2_ref[...] = h2.astype(h2_ref.dtype)
    logits = _dot3(rt_ref[...], h2, NT)
    e = jnp.exp(logits - logits.max(axis=0, keepdims=True))
    aff_ref[...] = e / e.sum(axis=0, keepdims=True)


def _postmix(m, x, mod, w_bf, gain, router_t, *, rows_per_cond, project):
    n = x.shape[0]
    row = lambda i: (i, 0)
    const = lambda i: (0, 0)
    return pl.pallas_call(
        functools.partial(_postmix_kernel, project=project),
        grid=(n // TM,),
        in_specs=[pl.BlockSpec((TM, D_MODEL), row),
                  pl.BlockSpec((TM, D_MODEL), row),
                  _mod_spec(rows_per_cond),
                  pl.BlockSpec(w_bf.shape, const),
                  pl.BlockSpec((1, D_MODEL), const),
                  pl.BlockSpec((N_EXPERTS, D_MODEL), const)],
        out_specs=[pl.BlockSpec((TM, D_MODEL), row),
                   pl.BlockSpec((TM, D_MODEL), row),
                   pl.BlockSpec((N_EXPERTS, TM), lambda i: (0, i))],
        out_shape=[jax.ShapeDtypeStruct((n, D_MODEL), F32),
                   jax.ShapeDtypeStruct((n, D_MODEL), BF16),
                   jax.ShapeDtypeStruct((N_EXPERTS, n), F32)],
        compiler_params=_params("parallel"),
        name="postmix_proj" if project else "postmix",
    )(m, x, mod, w_bf, gain, router_t)


def _route_kernel(aff_ref, slot_ref, gate_ref, pos_ref, tcnt_ref, *, seq, cap, nseg, tt):
    aff = jnp.concatenate([aff_ref[:, s * seq:(s + 1) * seq] for s in range(nseg)], axis=0)
    rows = aff.shape[0]
    capf = jnp.float32(cap)
    thr_bits = jnp.zeros((rows, 1), I32)

    def enough(cand):
        cnt = jnp.where(aff >= pltpu.bitcast(cand, F32), 1.0, 0.0).sum(axis=1, keepdims=True)
        return cnt >= capf

    for bit in range(30, 0, -2):
        hi, lo = 1 << bit, 1 << (bit - 1)
        both, only_hi, only_lo = thr_bits | hi | lo, thr_bits | hi, thr_bits | lo
        thr_bits = jnp.where(enough(both), both,
                             jnp.where(enough(only_hi), only_hi, jnp.where(enough(only_lo), only_lo, thr_bits)))
    thr_bits = jnp.where(enough(thr_bits | 1), thr_bits | 1, thr_bits)
    thr = pltpu.bitcast(thr_bits, F32)
    gt = aff > thr
    eq = aff == thr
    n_gt = jnp.where(gt, 1.0, 0.0).sum(axis=1, keepdims=True)
    pw = min(seq, 256)
    tri = jnp.where(lax.broadcasted_iota(I32, (pw, pw), 0) < lax.broadcasted_iota(I32, (pw, pw), 1),
                    1.0, 0.0).astype(BF16)

    def count_before(flag):
        ones = jnp.where(flag, 1.0, 0.0)
        parts = []
        run = jnp.zeros((rows, 1), F32)
        for c0 in range(0, seq, pw):
            blk = ones[:, c0:c0 + pw]
            parts.append(_dot(blk.astype(BF16), tri) + run)
            run = run + blk.sum(axis=1, keepdims=True)
        return jnp.concatenate(parts, axis=1) if len(parts) > 1 else parts[0]

    sel = gt | (eq & (count_before(eq) < capf - n_gt))
    rank = count_before(sel)
    expert = lax.broadcasted_iota(I32, (rows, seq), 0) & (N_EXPERTS - 1)
    slot = jnp.where(sel, expert * cap + rank.astype(I32), -1)
    gate = jnp.where(sel, aff, 0.0)
    pos = jnp.where(sel, rank, -1.0)
    nt = seq // tt
    tile_of = jnp.where((lax.broadcasted_iota(I32, (seq, nt), 0) // tt) == lax.broadcasted_iota(I32, (seq, nt), 1),
                        1.0, 0.0).astype(BF16)
    tcnt = _dot(jnp.where(sel, 1.0, 0.0).astype(BF16), tile_of)
    for s in range(nseg):
        rows_s = slice(s * N_EXPERTS, (s + 1) * N_EXPERTS)
        slot_ref[:, s * seq:(s + 1) * seq] = slot[rows_s, :]
        gate_ref[:, s * seq:(s + 1) * seq] = gate[rows_s, :]
        pos_ref[:, s * seq:(s + 1) * seq] = pos[rows_s, :]
        tcnt_ref[0, :, s * nt:(s + 1) * nt] = tcnt[rows_s, :]


def _route(aff_t, seq, cap, nseg, tt):
    n = aff_t.shape[1]
    nt = seq // tt
    steps = n // (nseg * seq)
    spec = pl.BlockSpec((N_EXPERTS, nseg * seq), lambda i: (0, i))
    slot, gate, pos, tcnt = pl.pallas_call(
        functools.partial(_route_kernel, seq=seq, cap=cap, nseg=nseg, tt=tt),
        grid=(steps,),
        in_specs=[spec],
        out_specs=[spec, spec, spec, pl.BlockSpec((1, N_EXPERTS, nseg * nt), lambda i: (i, 0, 0))],
        out_shape=[jax.ShapeDtypeStruct((N_EXPERTS, n), I32), jax.ShapeDtypeStruct((N_EXPERTS, n), F32),
                   jax.ShapeDtypeStruct((N_EXPERTS, n), F32),
                   jax.ShapeDtypeStruct((steps, N_EXPERTS, nseg * nt), F32)],
        compiler_params=_params("parallel"),
        name="route",
    )(aff_t)
    return slot, gate, pos, tcnt.transpose(1, 0, 2).reshape(N_EXPERTS, n // tt)


def _dispatch_kernel(slot_ref, h_ref, x_ref, *, cap, seq):
    m = N_EXPERTS * cap
    slot_id = lax.broadcasted_iota(I32, (m, seq), 0)
    for r in range(h_ref.shape[0] // seq):
        slots = slot_ref[:, r * seq:(r + 1) * seq]
        owner = jnp.broadcast_to(slots[:, None, :], (N_EXPERTS, cap, seq)).reshape(m, seq)
        sel = jnp.where(owner == slot_id, 1.0, 0.0).astype(BF16)
        x_ref[r * m:(r + 1) * m, :] = _dot(sel, h_ref[r * seq:(r + 1) * seq, :]).astype(x_ref.dtype)


def _dispatch(slot, h, seq, cap):
    n = h.shape[0]
    nb = n // seq
    rb = MOE_RB if nb % MOE_RB == 0 else 1
    return pl.pallas_call(
        functools.partial(_dispatch_kernel, cap=cap, seq=seq),
        grid=(nb // rb,),
        in_specs=[pl.BlockSpec((N_EXPERTS, rb * seq), lambda b: (0, b)),
                  pl.BlockSpec((rb * seq, D_MODEL), lambda b: (b, 0))],
        out_specs=pl.BlockSpec((rb * N_EXPERTS * cap, D_MODEL), lambda b: (b, 0)),
        out_shape=jax.ShapeDtypeStruct((nb * N_EXPERTS * cap, D_MODEL), BF16),
        compiler_params=_params("parallel"),
        name="moe_dispatch",
    )(slot, h)


def _window(cum_ref, base, e, cap, k=0):
    lo = ((cum_ref[base + e] >> 4) << 4) + k * MOE_W
    return lo, pl.multiple_of(jnp.minimum(lo, cap - MOE_W), 16)


def _extra_windows(cum_ref, base, e):
    lo = (cum_ref[base + e] >> 4) << 4
    return (cum_ref[base + N_EXPERTS + e] - lo + (MOE_W - 1)) >> 6


def _dispatch_win_kernel(cum_ref, slot_ref, h_ref, x_ref, *, cap, nt):
    b, i = pl.program_id(0), pl.program_id(1)
    tt = h_ref.shape[0]
    base = (b * (nt + 1) + i) * N_EXPERTS
    h = h_ref[...]
    row = lax.broadcasted_iota(I32, (MOE_W, tt), 0)

    @pl.when(i == 0)
    def _():
        x_ref[...] = jnp.zeros_like(x_ref)

    def hits(e, lo, ws):
        srow = slot_ref[e:e + 1, :]
        return (srow == row + (e * cap + ws)) & (srow >= e * cap + lo)

    for grp in range(N_EXPERTS // MOE_EG):
        wins = [(e,) + _window(cum_ref, base, e, cap) for e in range(grp * MOE_EG, (grp + 1) * MOE_EG)]
        sel = jnp.concatenate([hits(e, lo, ws) for e, lo, ws in wins], axis=0)
        x = _dot(jnp.where(sel, 1.0, 0.0).astype(BF16), h)
        for q, (e, lo, ws) in enumerate(wins):
            dst = pl.ds(e * cap + ws, MOE_W)
            x_ref[dst, :] += x[q * MOE_W:(q + 1) * MOE_W].astype(x_ref.dtype)

    for e in range(N_EXPERTS):
        def extra(k, carry, e=e):
            lo, ws = _window(cum_ref, base, e, cap, k)
            x = _dot(jnp.where(hits(e, lo, ws), 1.0, 0.0).astype(BF16), h)
            x_ref[pl.ds(e * cap + ws, MOE_W), :] += x.astype(x_ref.dtype)
            return carry
        lax.fori_loop(1, _extra_windows(cum_ref, base, e), extra, 0)


def _dispatch_win(cum, slot, h, seq, cap):
    n = h.shape[0]
    nb, nt = n // seq, seq // MOE_TT
    return pl.pallas_call(
        functools.partial(_dispatch_win_kernel, cap=cap, nt=nt),
        grid_spec=pltpu.PrefetchScalarGridSpec(
            num_scalar_prefetch=1,
            grid=(nb, nt),
            in_specs=[pl.BlockSpec((N_EXPERTS, MOE_TT), lambda b, i, c: (0, b * nt + i)),
                      pl.BlockSpec((MOE_TT, D_MODEL), lambda b, i, c: (b * nt + i, 0))],
            out_specs=pl.BlockSpec((N_EXPERTS * cap, D_MODEL), lambda b, i, c: (b, 0))),
        out_shape=jax.ShapeDtypeStruct((nb * N_EXPERTS * cap, D_MODEL), BF16),
        compiler_params=_params("parallel", "arbitrary"),
        name="moe_dispatch_win",
    )(cum, slot, h)


FFN_TF = 512
FFN_RC = 512


def _ffn_kernel(xa_ref, xb_ref, wg_ref, wu_ref, wd_ref, ya_ref, yb_ref, acc_ref):
    j = pl.program_id(1)

    @pl.when(j == 0)
    def _():
        acc_ref[...] = jnp.zeros_like(acc_ref)

    wg = wg_ref[0].astype(BF16)
    wu = wu_ref[0].astype(BF16)
    wd = wd_ref[0].astype(BF16)
    ra = xa_ref.shape[0] * xa_ref.shape[2]

    def row_chunks(ref, base):
        nb, _, cap, d = ref.shape
        rc = min(FFN_RC, nb * cap)
        for r0 in range(0, nb * cap, rc):
            if cap >= rc:
                b, c0 = divmod(r0, cap)
                yield base + r0, ref[b, 0, c0:c0 + rc, :]
            else:
                yield base + r0, ref[r0 // cap:(r0 + rc) // cap, 0, :, :].reshape(rc, d)

    for r0, x in itertools.chain(row_chunks(xa_ref, 0), row_chunks(xb_ref, ra)):
        rc = x.shape[0]
        g = _dot(x, wg)
        u = _dot(x, wu)
        mid = (g * _sigmoid(g) * u).astype(BF16)
        acc_ref[r0:r0 + rc, :] += _dot(mid, wd)

    @pl.when(j == pl.num_programs(1) - 1)
    def _():
        for ref, base in ((ya_ref, 0), (yb_ref, ra)):
            nb, _, cap, d = ref.shape
            ref[...] = acc_ref[base:base + nb * cap, :].reshape(nb, 1, cap, d).astype(ref.dtype)


def _ffn(xa, xb, w_gate, w_up, w_down):
    ba, _, ca, d = xa.shape
    bb, _, cb, _ = xb.shape
    nj = D_FF // FFN_TF
    xa_spec = pl.BlockSpec((ba, 1, ca, d), lambda e, j: (0, e, 0, 0))
    xb_spec = pl.BlockSpec((bb, 1, cb, d), lambda e, j: (0, e, 0, 0))
    return pl.pallas_call(
        _ffn_kernel,
        grid=(N_EXPERTS, nj),
        in_specs=[xa_spec, xb_spec,
                  pl.BlockSpec((1, d, FFN_TF), lambda e, j: (e, 0, j)),
                  pl.BlockSpec((1, d, FFN_TF), lambda e, j: (e, 0, j)),
                  pl.BlockSpec((1, FFN_TF, d), lambda e, j: (e, j, 0))],
        out_specs=[xa_spec, xb_spec],
        out_shape=[jax.ShapeDtypeStruct(xa.shape, BF16), jax.ShapeDtypeStruct(xb.shape, BF16)],
        scratch_shapes=[pltpu.VMEM((ba * ca + bb * cb, d), F32)],
        compiler_params=_params("parallel", "arbitrary"),
        name="moe_ffn",
    )(xa, xb, w_gate, w_up, w_down)


def _expand(vals_bf, first_expert, width, total):
    e_of_lane = first_expert + lax.broadcasted_iota(I32, (N_EXPERTS, total), 1) // width
    pick = jnp.where(lax.broadcasted_iota(I32, (N_EXPERTS, total), 0) == e_of_lane, 1.0, 0.0).astype(BF16)
    return _dot(vals_bf, pick)


def _combine_kernel(pos_ref, gate_ref, y_ref, x_ref, mod_ref, o_ref, *, cap, seq):
    m = N_EXPERTS * cap
    rank = (lax.broadcasted_iota(I32, (seq, m), 1) % cap).astype(F32)
    for r in range(x_ref.shape[0] // seq):
        rows = slice(r * seq, (r + 1) * seq)
        pos = _expand(pos_ref[rows, :].astype(BF16), 0, cap, m)
        gate = _expand(gate_ref[rows, :].astype(BF16), 0, cap, m)
        w = jnp.where(pos == rank, gate, 0.0).astype(BF16)
        o_ref[rows, :] = x_ref[rows, :] + mod_ref[0, 5:6, :] * _dot(w, y_ref[r * m:(r + 1) * m, :])


def _combine(pos_t, gate_t, y, x, mod, *, seq, cap):
    n = x.shape[0]
    nb = n // seq
    rb = MOE_RB if nb % MOE_RB == 0 else 1
    row = lambda b: (b, 0)
    return pl.pallas_call(
        functools.partial(_combine_kernel, cap=cap, seq=seq),
        grid=(nb // rb,),
        in_specs=[pl.BlockSpec((rb * seq, N_EXPERTS), row),
                  pl.BlockSpec((rb * seq, N_EXPERTS), row),
                  pl.BlockSpec((rb * N_EXPERTS * cap, D_MODEL), row),
                  pl.BlockSpec((rb * seq, D_MODEL), row),
                  pl.BlockSpec((1, 8, D_MODEL), lambda b: (0, 0, 0))],
        out_specs=pl.BlockSpec((rb * seq, D_MODEL), row),
        out_shape=jax.ShapeDtypeStruct((n, D_MODEL), F32),
        compiler_params=_params("parallel"),
        name="moe_combine",
    )(pos_t, gate_t, y, x, mod)


def _combine_win_kernel(cum_ref, pos_ref, gate_ref, y_ref, x_ref, mod_ref, o_ref, acc_ref, *, cap, nt):
    b, i = pl.program_id(0), pl.program_id(1)
    tt = x_ref.shape[0]
    base = (b * (nt + 1) + i) * N_EXPERTS
    width = MOE_EG * MOE_W
    posb = pos_ref[...].astype(BF16)
    gateb = gate_ref[...].astype(BF16)
    lane = lax.broadcasted_iota(I32, (1, width), 1)
    offset = (lane & (MOE_W - 1)).astype(F32)
    acc = jnp.zeros((tt, D_MODEL), F32)
    for grp in range(N_EXPERTS // MOE_EG):
        wins = [(e,) + _window(cum_ref, base, e, cap) for e in range(grp * MOE_EG, (grp + 1) * MOE_EG)]
        lo_l = jnp.zeros((1, width), F32)
        ws_l = jnp.zeros((1, width), F32)
        for q, (e, lo, ws) in enumerate(wins):
            mine = (lane >> 6) == q
            lo_l = jnp.where(mine, lo.astype(F32), lo_l)
            ws_l = jnp.where(mine, ws.astype(F32), ws_l)
        pos = _expand(posb, grp * MOE_EG, MOE_W, width)
        gate = _expand(gateb, grp * MOE_EG, MOE_W, width)
        w = jnp.where((pos - ws_l == offset) & (pos >= lo_l), gate, 0.0).astype(BF16)
        ywin = jnp.concatenate([y_ref[pl.ds(e * cap + ws, MOE_W), :] for e, lo, ws in wins], axis=0)
        acc = acc + _dot(w, ywin)
    acc_ref[...] = acc

    off64 = lax.broadcasted_iota(I32, (1, MOE_W), 1).astype(F32)
    for e in range(N_EXPERTS):
        def extra(k, carry, e=e):
            lo, ws = _window(cum_ref, base, e, cap, k)
            pos = pos_ref[:, e:e + 1]
            gate = gate_ref[:, e:e + 1].astype(BF16).astype(F32)
            w = jnp.where((pos - ws.astype(F32) == off64) & (pos >= lo.astype(F32)), gate, 0.0).astype(BF16)
            acc_ref[...] += _dot(w, y_ref[pl.ds(e * cap + ws, MOE_W), :])
            return carry
        lax.fori_loop(1, _extra_windows(cum_ref, base, e), extra, 0)

    o_ref[...] = x_ref[...] + mod_ref[0, 5:6, :] * acc_ref[...]


def _combine_win(cum, pos_t, gate_t, y, x, mod, *, seq, cap):
    n = x.shape[0]
    nb, nt = n // seq, seq // MOE_TT
    row = lambda b, i, c: (b * nt + i, 0)
    return pl.pallas_call(
        functools.partial(_combine_win_kernel, cap=cap, nt=nt),
        grid_spec=pltpu.PrefetchScalarGridSpec(
            num_scalar_prefetch=1,
            grid=(nb, nt),
            in_specs=[pl.BlockSpec((MOE_TT, N_EXPERTS), row),
                      pl.BlockSpec((MOE_TT, N_EXPERTS), row),
                      pl.BlockSpec((N_EXPERTS * cap, D_MODEL), lambda b, i, c: (b, 0)),
                      pl.BlockSpec((MOE_TT, D_MODEL), row),
                      pl.BlockSpec((1, 8, D_MODEL), lambda b, i, c: (1 + b, 0, 0))],
            out_specs=pl.BlockSpec((MOE_TT, D_MODEL), row),
            scratch_shapes=[pltpu.VMEM((MOE_TT, D_MODEL), F32)]),
        out_shape=jax.ShapeDtypeStruct((n, D_MODEL), F32),
        compiler_params=_params("parallel", "parallel"),
        name="moe_combine_win",
    )(cum, pos_t, gate_t, y, x, mod)


def _moe_pair(hp, affp, x1p, hs, affs, x1s, mod, w_gate, w_up, w_down, seq_p, seq_s):
    n_p, n_s = hp.shape[0], hs.shape[0]
    nb_p, nb_s = n_p // seq_p, n_s // seq_s
    cap_p = EC_FACTOR * seq_p // N_EXPERTS
    cap_s = EC_FACTOR * seq_s // N_EXPERTS
    assert N_EXPERTS * cap_p <= 512 and cap_s >= MOE_W and cap_s % 16 == 0 and seq_s % MOE_TT == 0
    slot_p, gate_p, pos_p, _ = _route(affp, seq_p, cap_p, nseg=min(8, nb_p), tt=seq_p)
    slot_s, gate_s, pos_s, tcnt = _route(affs, seq_s, cap_s, nseg=min(4, nb_s), tt=MOE_TT)
    nt = seq_s // MOE_TT
    counts = tcnt.T.reshape(nb_s, nt, N_EXPERTS).astype(I32)
    cum = jnp.concatenate([jnp.zeros((nb_s, 1, N_EXPERTS), I32), jnp.cumsum(counts, axis=1)], axis=1).reshape(-1)
    xp = _dispatch(slot_p, hp, seq_p, cap_p).reshape(nb_p, N_EXPERTS, cap_p, D_MODEL)
    xs = _dispatch_win(cum, slot_s, hs, seq_s, cap_s).reshape(nb_s, N_EXPERTS, cap_s, D_MODEL)
    ys, yp = _ffn(xs, xp, w_gate, w_up, w_down)
    outp = _combine(pos_p.T, gate_p.T, yp.reshape(-1, D_MODEL), x1p, mod, seq=seq_p, cap=cap_p)
    outs = _combine_win(cum, pos_s.T, gate_s.T, ys.reshape(-1, D_MODEL), x1s, mod, seq=seq_s, cap=cap_s)
    return outp, outs


def _ssm_in_kernel(x_ref, mod_ref, gain_ref, wt_ref, ut_ref, *, cols_per_cond):
    tc = x_ref.shape[1]
    if cols_per_cond is None:
        h = _norm_mod(x_ref[0], gain_ref[...], mod_ref[0, 0:1, :], mod_ref[0, 1:2, :]).astype(BF16)
    else:
        first = 1 + pl.program_id(1) * (tc // cols_per_cond)
        parts = []
        for s in range(tc // cols_per_cond):
            m = mod_ref[first + s]
            parts.append(_norm_mod(x_ref[0, s * cols_per_cond:(s + 1) * cols_per_cond, :], gain_ref[...],
                                   m[0:1, :], m[1:2, :]).astype(BF16))
        h = jnp.concatenate(parts, axis=0) if len(parts) > 1 else parts[0]
    ut_ref[0] = _dot(wt_ref[...], h, NT)


def _ssm_in(xperm, mod, gain, wt_bf, *, cols_per_cond, tc):
    l, bk, d = xperm.shape
    assert cols_per_cond is None or tc % cols_per_cond == 0
    return pl.pallas_call(
        functools.partial(_ssm_in_kernel, cols_per_cond=cols_per_cond),
        grid=(l, bk // tc),
        in_specs=[pl.BlockSpec((1, tc, d), lambda j, i: (j, i, 0)),
                  pl.BlockSpec(mod.shape, lambda j, i: (0, 0, 0)),
                  pl.BlockSpec((1, d), lambda j, i: (0, 0)),
                  pl.BlockSpec((d, d), lambda j, i: (0, 0))],
        out_specs=pl.BlockSpec((1, d, tc), lambda j, i: (j, 0, i)),
        out_shape=jax.ShapeDtypeStruct((l, d, bk), F32),
        compiler_params=_params("parallel", "parallel"),
        name="ssm_in",
    )(xperm, mod, gain, wt_bf)


def _ssm_core_kernel(utp_ref, uts_ref, lamp_ref, c_ref, bt_ref, dsk_ref, h0_ref, ytp_ref, yts_ref, fs_ref,
                     *, kp, ks, nbp, nbs):
    rows = SSM_ROWS
    p = SSM_STATE
    lc = SSM_CHUNK
    ri = lax.broadcasted_iota(I32, (rows, rows), 0)
    cj = lax.broadcasted_iota(I32, (rows, rows), 1)
    causal = (ri >> 4) >= (cj >> 4)
    anticausal = (cj >> 4) >= (ri >> 4)
    diag = ri == cj
    leftc = lax.broadcasted_iota(I32, (lc, 128), 1) < p
    leftg = lax.broadcasted_iota(I32, (SSM_GROUP, 128), 1) < p
    nrow = lax.broadcasted_iota(I32, (lc, 128), 0).astype(F32)
    eye = lax.broadcasted_iota(I32, (p, 128), 0) == lax.broadcasted_iota(I32, (p, 128), 1)

    def cmul(ar, ai, xr, xi):
        return ar * xr - ai * xi, ar * xi + ai * xr

    def expand_rows(t):
        return jnp.broadcast_to(t[:, None, :], (lc, SSM_GROUP, 128)).reshape(rows, 128)

    def tile_rows(t):
        return jnp.broadcast_to(t[None, :, :], (lc, SSM_GROUP, 128)).reshape(rows, 128)

    def to_col(row):
        return jnp.where(eye, jnp.broadcast_to(row, (p, 128)), 0.0).sum(axis=1, keepdims=True)

    def operands(gg, d):
        lp = lamp_ref[gg, d]
        lre, lim = lp[0:1], lp[1:2]
        dt = jnp.exp(lp[2:3])
        a, th = lre * dt, lim * dt
        ang = nrow * th
        cs, sn = jnp.cos(ang), jnp.sin(ang)
        ep, em = jnp.exp(nrow * a), jnp.exp(-(nrow * a))
        pr, pi = ep * cs, ep * sn
        nr, ni = em * cs, -(em * sn)
        l1r, l1i = pr[1:2], pi[1:2]
        lmr, lmi = pr[lc - 1:lc], pi[lc - 1:lc]
        llr, lli = cmul(lmr, lmi, l1r, l1i)
        den = lre * lre + lim * lim
        cr = ((l1r - 1.0) * lre + l1i * lim) / den
        ci = (l1i * lre - (l1r - 1.0) * lim) / den
        btr, bti = bt_ref[gg, d, 0], bt_ref[gg, d, 1]
        bbr, bbi = cr * btr - ci * bti, cr * bti + ci * btr
        u1 = tile_rows(jnp.where(leftg, bbr, bbi))
        u2 = tile_rows(jnp.where(leftg, bbi, bbr))
        c1 = tile_rows(c_ref[gg, d, 0])
        c2 = tile_rows(c_ref[gg, d, 1])

        def left_form(xr, xi):
            return (c1 * expand_rows(jnp.where(leftc, xr, -xi))
                    + c2 * expand_rows(jnp.where(leftc, -xi, -xr)))

        def right_form(xr, xi):
            return u1 * expand_rows(xr) + u2 * expand_rows(jnp.where(leftc, -xi, xi))

        if d == 0:
            al = left_form(pr, pi)
            brt = right_form(nr, ni)
            rrt = right_form(*cmul(lmr, lmi, nr, ni))
            qq = left_form(*cmul(l1r, l1i, pr, pi))
            mat = jnp.where(causal, _dot3(al, brt, NT), 0.0)
        else:
            al = left_form(nr, ni)
            brt = right_form(pr, pi)
            rrt = brt
            qq = left_form(*cmul(llr, lli, nr, ni))
            mat = jnp.where(anticausal, _dot3(al, brt, NT), 0.0)
        return mat, rrt.T, qq, to_col(llr), to_col(lli)

    def scan(sr, si, lr, li, h0r, h0i, reverse, nchunk, nbatch, sel):
        bk = sr.shape[1]
        lane = lax.broadcasted_iota(I32, (p, bk), 1)
        kidx = lane & (nchunk - 1)
        edge = (nchunk - 1) if reverse else 0
        if h0r is not None:
            h0cr = jnp.zeros((p, bk), F32)
            h0ci = jnp.zeros((p, bk), F32)
            for b in range(nbatch):
                at = lane == (b * nchunk + edge)
                h0cr = jnp.where(at, h0r[:, b:b + 1], h0cr)
                h0ci = jnp.where(at, h0i[:, b:b + 1], h0ci)
            ar, ai = cmul(lr, li, h0cr, h0ci)
            er, ei = sr + ar, si + ai
        else:
            er, ei = sr, si
        ar, ai = lr, li
        s = 1
        while s < nchunk:
            if reverse:
                ok = kidx < nchunk - s
                tr, ti = pltpu.roll(er, bk - s, 1), pltpu.roll(ei, bk - s, 1)
            else:
                ok = kidx >= s
                tr, ti = pltpu.roll(er, s, 1), pltpu.roll(ei, s, 1)
            tr = jnp.where(ok, tr, 0.0)
            ti = jnp.where(ok, ti, 0.0)
            dr, di = cmul(ar, ai, tr, ti)
            er, ei = er + dr, ei + di
            ar, ai = cmul(ar, ai, ar, ai)
            s *= 2
        if reverse:
            inner = kidx < nchunk - 1
            hr, hi = pltpu.roll(er, bk - 1, 1), pltpu.roll(ei, bk - 1, 1)
        else:
            inner = kidx >= 1
            hr, hi = pltpu.roll(er, 1, 1), pltpu.roll(ei, 1, 1)
        hr = jnp.where(inner, hr, h0cr if h0r is not None else 0.0)
        hi = jnp.where(inner, hi, h0ci if h0r is not None else 0.0)
        fin = None if sel is None else (_sel_dot_t(sel, er), _sel_dot_t(sel, ei))
        return hr, hi, fin

    def final_selectors(bk, nchunk, nbatch):
        bat = lax.broadcasted_iota(I32, (nbatch, bk), 0)
        col = lax.broadcasted_iota(I32, (nbatch, bk), 1)
        last = jnp.where(col == bat * nchunk + (nchunk - 1), 1.0, 0.0).astype(BF16)
        first = jnp.where(col == bat * nchunk, 1.0, 0.0).astype(BF16)
        return last, first

    sel_last, sel_first = final_selectors(utp_ref.shape[2], kp, nbp)

    for gg in range(SSM_GB):
        mf, rf, qf, lfr, lfi = operands(gg, 0)
        mb, rb, qb, lbr, lbi = operands(gg, 1)
        skip = jnp.where(diag, jnp.broadcast_to(dsk_ref[gg], (rows, rows)), 0.0)
        stack = jnp.concatenate([mf + mb + skip, rf, rb], axis=0)
        qq = jnp.concatenate([qf, qb], axis=1)
        h0 = h0_ref[gg]
        for ut_ref, yt_ref, nchunk, nbatch, latent in ((utp_ref, ytp_ref, kp, nbp, False),
                                                       (uts_ref, yts_ref, ks, nbs, True)):
            bk = ut_ref.shape[2]
            x = ut_ref[:, gg * SSM_GROUP:(gg + 1) * SSM_GROUP, :].reshape(rows, bk)
            res = _mm(stack, x, SSM_PASSES)
            hfr, hfi, ff = scan(res[rows:rows + p], res[rows + p:rows + 2 * p], lfr, lfi,
                                h0[0] if latent else None, h0[1] if latent else None,
                                False, nchunk, nbatch, None if latent else sel_last)
            hbr, hbi, fb = scan(res[rows + 2 * p:rows + 3 * p], res[rows + 3 * p:rows + 4 * p], lbr, lbi,
                                h0[2] if latent else None, h0[3] if latent else None,
                                True, nchunk, nbatch, None if latent else sel_first)
            states = jnp.concatenate([hfr, hfi, hbr, hbi], axis=0)
            y = res[:rows] + _mm(qq, states, SSM_PASSES)
            yt_ref[:, gg * SSM_GROUP:(gg + 1) * SSM_GROUP, :] = y.reshape(lc, SSM_GROUP, bk)
            if not latent:
                fs_ref[gg, 0] = ff[0]
                fs_ref[gg, 1] = ff[1]
                fs_ref[gg, 2] = fb[0]
                fs_ref[gg, 3] = fb[1]


def _ssm_core(utp, uts, ops, h0, *, kp, ks, nbp, nbs):
    lamp, c2, bt2, dsk = ops
    l, d, bkp = utp.shape
    bks = uts.shape[2]
    g = SSM_GROUPS
    gb = SSM_GB
    lead4 = lambda i: (i, 0, 0, 0)
    lead5 = lambda i: (i, 0, 0, 0, 0)
    ut_spec = lambda bk: pl.BlockSpec((l, gb * SSM_GROUP, bk), lambda i: (0, i, 0))
    return pl.pallas_call(
        functools.partial(_ssm_core_kernel, kp=kp, ks=ks, nbp=nbp, nbs=nbs),
        grid=(g // gb,),
        in_specs=[ut_spec(bkp), ut_spec(bks),
                  pl.BlockSpec((gb, 2, 8, 128), lead4),
                  pl.BlockSpec((gb, 2, 2, SSM_GROUP, 128), lead5),
                  pl.BlockSpec((gb, 2, 2, SSM_GROUP, 128), lead5),
                  pl.BlockSpec((gb, 1, SSM_ROWS), lambda i: (i, 0, 0)),
                  pl.BlockSpec((gb, 4, SSM_STATE, nbs), lead4)],
        out_specs=[ut_spec(bkp), ut_spec(bks),
                   pl.BlockSpec((gb, 4, nbp, SSM_STATE), lead4)],
        out_shape=[jax.ShapeDtypeStruct((l, d, bkp), F32),
                   jax.ShapeDtypeStruct((l, d, bks), F32),
                   jax.ShapeDtypeStruct((g, 4, nbp, SSM_STATE), F32)],
        compiler_params=_params("parallel"),
        name="ssm_core",
    )(utp, uts, lamp, c2, bt2, dsk, h0)


def _ssm_out_kernel(yt_ref, w_ref, m_ref):
    y = yt_ref[0].T
    act = 0.5 * y * (1.0 + jnp.tanh(0.7978845608028654 * (y + 0.044715 * (y * y * y))))
    ag = _dot(act.astype(BF16), w_ref[...])
    d = m_ref.shape[2]
    m_ref[0] = ag[:, :d] * _sigmoid(ag[:, d:])


def _ssm_out(yt, w_bf, tc):
    l, d, bk = yt.shape
    return pl.pallas_call(
        _ssm_out_kernel,
        grid=(l, bk // tc),
        in_specs=[pl.BlockSpec((1, d, tc), lambda j, i: (j, 0, i)),
                  pl.BlockSpec((d, 2 * d), lambda j, i: (0, 0))],
        out_specs=pl.BlockSpec((1, tc, d), lambda j, i: (j, i, 0)),
        out_shape=jax.ShapeDtypeStruct((l, bk, d), F32),
        compiler_params=_params("parallel", "parallel"),
        name="ssm_out",
    )(yt, w_bf)


def _ssm_operand_params(lam_re, lam_im, b_re, b_im, c_re, c_im, log_dt, d_skip):
    g, l = SSM_GROUPS, SSM_CHUNK
    dup = lambda t: jnp.concatenate([t, t], axis=-1)
    lamp = jnp.stack([lam_re, lam_im, jnp.broadcast_to(log_dt[..., None], lam_re.shape)], axis=2)
    lamp = dup(jnp.pad(lamp, ((0, 0), (0, 0), (0, 5), (0, 0)))).transpose(1, 0, 2, 3)
    c2 = dup(jnp.stack([c_re, c_im], axis=2)).transpose(1, 0, 2, 3, 4)
    bt2 = dup(jnp.stack([jnp.swapaxes(b_re, -1, -2), jnp.swapaxes(b_im, -1, -2)], axis=2)).transpose(1, 0, 2, 3, 4)
    dsk = jnp.tile(d_skip.reshape(g, 1, SSM_GROUP), (1, 1, l))
    return lamp, c2, bt2, dsk


def _to_chunks(x, nb, seq):
    k = seq // SSM_CHUNK
    return x.reshape(nb, k, SSM_CHUNK, -1).transpose(2, 0, 1, 3).reshape(SSM_CHUNK, nb * k, -1)


def _from_chunks(x, nb, seq):
    k = seq // SSM_CHUNK
    return x.reshape(SSM_CHUNK, nb, k, -1).transpose(1, 2, 0, 3).reshape(nb * seq, -1)


def _ssm_mixers(xp, xs, mod, gain, wt_bf, ops, w_out_bf, h0, *, nbp, sp, nbs, ss):
    kp, ks = sp // SSM_CHUNK, ss // SSM_CHUNK
    tcp, tcs = min(SSM_TC, nbp * kp), min(SSM_TC, nbs * ks)
    utp = _ssm_in(_to_chunks(xp, nbp, sp), mod, gain, wt_bf, cols_per_cond=None, tc=tcp)
    uts = _ssm_in(_to_chunks(xs, nbs, ss), mod, gain, wt_bf, cols_per_cond=ks, tc=tcs)
    ytp, yts, fs = _ssm_core(utp, uts, ops, h0, kp=kp, ks=ks, nbp=nbp, nbs=nbs)
    mp = _from_chunks(_ssm_out(ytp, w_out_bf, tcp), nbp, sp)
    ms = _from_chunks(_ssm_out(yts, w_out_bf, tcs), nbs, ss)
    return mp, ms, fs


def _rope_tables(seq):
    t = jnp.arange(seq)
    row = (t // GRID_W).astype(F32)
    col = (t % GRID_W).astype(F32)
    n_freq = HEAD_DIM // 4
    inv_freq = ROPE_THETA ** (-jnp.arange(n_freq, dtype=F32) / n_freq)
    ang = jnp.concatenate([row[:, None] * inv_freq, col[:, None] * inv_freq], axis=-1)
    cos = jnp.repeat(jnp.cos(ang), 2, axis=-1)
    sin = jnp.repeat(jnp.sin(ang), 2, axis=-1)
    sign = jnp.tile(jnp.array([-1.0, 1.0], F32), HEAD_DIM // 2)
    return jnp.tile(cos, (1, 4)), jnp.tile(sin * sign, (1, 4))


def _head_gains(qn_a, kn_a, qn_b, kn_b):
    scale = HEAD_DIM ** -0.5 * LOG2E
    ones = jnp.ones((N_KV * HEAD_DIM,), F32)
    return jnp.concatenate([jnp.tile(qn_a, N_HEADS) * scale, jnp.tile(kn_a, N_KV), ones,
                            jnp.tile(qn_b, N_HEADS) * scale, jnp.tile(kn_b, N_KV), ones]).reshape(1, QKV_COLS)


def kernel(x_prompt, x_sample, c, cache_k_a_l0, cache_v_a_l0, cache_k_b_l0, cache_v_b_l0, state_ssm_re_l1, state_ssm_im_l1, c_ctx, mod_w_l0, mod_b_l0, norm_mix_l0, attn_w_in_l0, q_norm_a_l0, k_norm_a_l0, q_norm_b_l0, k_norm_b_l0, sink_b_l0, attn_w_out_l0, norm_ffn_l0, router_l0, moe_w_gate_l0, moe_w_up_l0, moe_w_down_l0, mod_w_l1, mod_b_l1, norm_mix_l1, ssm_w_in_l1, ssm_lambda_re_l1, ssm_lambda_im_l1, ssm_b_re_l1, ssm_b_im_l1, ssm_c_re_l1, ssm_c_im_l1, ssm_log_dt_l1, ssm_d_l1, ssm_w_out_l1, norm_ffn_l1, router_l1, moe_w_gate_l1, moe_w_up_l1, moe_w_down_l1):
    bp, sp, d = x_prompt.shape
    bs, ss, _ = x_sample.shape
    past = cache_k_a_l0.shape[1]
    assert d == D_MODEL and bs <= 7 and (bp * sp) % TM == 0 and TM % sp == 0 and ss % TM == 0
    xp = x_prompt.reshape(bp * sp, d)
    xs = x_sample.reshape(bs * ss, d)
    cond8 = jnp.concatenate([c_ctx[None], c, jnp.zeros((7 - bs, d), F32)], axis=0)
    row1 = lambda v: v.reshape(1, -1)

    mod0 = _mod_rows(cond8, mod_w_l0, mod_b_l0)
    w_in = attn_w_in_l0.astype(BF16)
    hgain = _head_gains(q_norm_a_l0, k_norm_a_l0, q_norm_b_l0, k_norm_b_l0)
    lane = np.arange(256)
    bd = jnp.asarray((lane[:, None] // HEAD_DIM == lane[None, :] // HEAD_DIM) / HEAD_DIM, BF16)
    cos_t, sin_t = _rope_tables(ss)
    qp, kap, vap, kbp, vbp = _qkv(xp, mod0, row1(norm_mix_l0), w_in, hgain, bd, cos_t, sin_t,
                                  rows_per_cond=None, seq=sp, rope=False, kv_dtype=F32, transposed_kv=True)
    qs, kas, vas, kbs, vbs = _qkv(xs, mod0, row1(norm_mix_l0), w_in, hgain, bd, cos_t, sin_t,
                                  rows_per_cond=ss, seq=ss, rope=True, kv_dtype=BF16, transposed_kv=False)
    op = _attn_ctx(sink_b_l0, qp, kap, vap, kbp, vbp, sp)
    cache = lambda t: t.reshape(bs, past, N_KV * HEAD_DIM)
    os_ = _attn_lat(sink_b_l0, qs, kas, vas, kbs, vbs, cache(cache_k_a_l0), cache(cache_v_a_l0),
                    cache(cache_k_b_l0), cache(cache_v_b_l0), ss)
    w_out = attn_w_out_l0.astype(BF16)
    x1p, hp, affp = _postmix(op, xp, mod0, w_out, row1(norm_ffn_l0), router_l0.T, rows_per_cond=None, project=True)
    x1s, hs, affs = _postmix(os_, xs, mod0, w_out, row1(norm_ffn_l0), router_l0.T, rows_per_cond=ss, project=True)
    xp, xs = _moe_pair(hp, affp, x1p, hs, affs, x1s, mod0, moe_w_gate_l0, moe_w_up_l0, moe_w_down_l0, sp, ss)

    mod1 = _mod_rows(cond8, mod_w_l1, mod_b_l1)
    ops = _ssm_operand_params(ssm_lambda_re_l1, ssm_lambda_im_l1, ssm_b_re_l1, ssm_b_im_l1, ssm_c_re_l1, ssm_c_im_l1,
                              ssm_log_dt_l1, ssm_d_l1)
    wt = ssm_w_in_l1.T.astype(BF16)
    w_so = ssm_w_out_l1.astype(BF16)
    h0 = jnp.stack([state_ssm_re_l1[:, 0], state_ssm_im_l1[:, 0], state_ssm_re_l1[:, 1], state_ssm_im_l1[:, 1]],
                   axis=0).transpose(2, 0, 3, 1)
    mp, ms, fsp = _ssm_mixers(xp, xs, mod1, row1(norm_mix_l1), wt, ops, w_so, h0, nbp=bp, sp=sp, nbs=bs, ss=ss)
    dummy_w = jnp.zeros((8, 128), BF16)
    x1p, hp, affp = _postmix(mp, xp, mod1, dummy_w, row1(norm_ffn_l1), router_l1.T, rows_per_cond=None, project=False)
    x1s, hs, affs = _postmix(ms, xs, mod1, dummy_w, row1(norm_ffn_l1), router_l1.T, rows_per_cond=ss, project=False)
    xp, xs = _moe_pair(hp, affp, x1p, hs, affs, x1s, mod1, moe_w_gate_l1, moe_w_up_l1, moe_w_down_l1, sp, ss)

    kv_out = lambda t: t.reshape(bp, N_KV, HEAD_DIM, sp).transpose(0, 3, 1, 2)
    fin = fsp.transpose(2, 1, 0, 3)
    ssm_re = jnp.stack([fin[:, 0], fin[:, 2]], axis=1)
    ssm_im = jnp.stack([fin[:, 1], fin[:, 3]], axis=1)
    return (xp.reshape(bp, sp, d), xs.reshape(bs, ss, d), kv_out(kap), kv_out(vap), kv_out(kbp), kv_out(vbp),
            ssm_re, ssm_im)
```

```python
import functools

import jax
import jax.numpy as jnp
import numpy as np
from jax import lax
from jax.experimental import pallas as pl
from jax.experimental.pallas import tpu as pltpu

F32, BF16, I32 = jnp.float32, jnp.bfloat16, jnp.int32

D_MODEL = 1024
GRID_W = 64
HEAD_DIM = 64
N_HEADS = 8
N_KV = 2
WINDOW = 128
ROPE_THETA = 10000.0
SSM_GROUP = 16
SSM_GROUPS = D_MODEL // SSM_GROUP
SSM_STATE = 64
N_EXPERTS = 16
EC_FACTOR = 2
D_FF = 2 * D_MODEL
EPS = 1e-6
NEG_INF = -1e30
LOG2E = 1.4426950408889634
QKV_COLS = 2 * (N_HEADS + 2 * N_KV) * HEAD_DIM
ATTN_OUT = 2 * N_HEADS * HEAD_DIM

SSM_CHUNK = 16
SSM_ROWS = SSM_CHUNK * SSM_GROUP
SSM_GB = 4
SSM_PASSES = 1

SSM_TC = 512

TM = 512

BAND_TQ = 256

MOE_TT = 256
BF16_ROWS = 16
MOE_W = 64
MOE_EG = 4
MOE_RB = 4
VMEM_LIMIT = 56 * 1024 * 1024

NN = (((1,), (0,)), ((), ()))
NT = (((1,), (1,)), ((), ()))


def _dot(a, b, dims=NN):
    return lax.dot_general(a, b, dims, preferred_element_type=F32)


def _split2(x):
    hi = x.astype(BF16)
    lo = (x - hi.astype(F32)).astype(BF16)
    return hi, lo


def _split3(x):
    hi = x.astype(BF16)
    r = x - hi.astype(F32)
    mid = r.astype(BF16)
    lo = (r - mid.astype(F32)).astype(BF16)
    return hi, mid, lo


def _dot3(a, b, dims=NN):
    ah, al = _split2(a)
    bh, bl = _split2(b)
    return _dot(ah, bh, dims) + (_dot(ah, bl, dims) + _dot(al, bh, dims))


def _mm(a, b, passes):
    if passes == 1:
        return _dot(a.astype(BF16), b.astype(BF16))
    return _dot3(a, b)


def _sel_dot_t(sel, x):
    hi, mid, lo = _split3(x)
    return _dot(sel, hi, NT) + (_dot(sel, mid, NT) + _dot(sel, lo, NT))


def _sigmoid(x):
    return 1.0 / (1.0 + jnp.exp(-x))


def _norm_mod(x, gain, shift, scale):
    ms = jnp.mean(x * x, axis=-1, keepdims=True)
    y = x * lax.rsqrt(ms + EPS) * gain
    return y * (1.0 + scale) + shift


def _params(*sem):
    return pltpu.CompilerParams(dimension_semantics=sem, vmem_limit_bytes=VMEM_LIMIT)


def _adaln_kernel(c_ref, w_ref, b_ref, o_ref):
    c = c_ref[...]
    s = c * _sigmoid(c)
    o_ref[...] = _dot3(s, w_ref[...]) + b_ref[...]


def _adaln(cond8, w_mod, b_mod):
    d, e = w_mod.shape
    tn = 1536
    return pl.pallas_call(
        _adaln_kernel,
        grid=(e // tn,),
        in_specs=[pl.BlockSpec((8, d), lambda j: (0, 0)),
                  pl.BlockSpec((d, tn), lambda j: (0, j)),
                  pl.BlockSpec((1, tn), lambda j: (0, j))],
        out_specs=pl.BlockSpec((8, tn), lambda j: (0, j)),
        out_shape=jax.ShapeDtypeStruct((8, e), F32),
        compiler_params=_params("parallel"),
        name="adaln",
    )(cond8, w_mod, b_mod.reshape(1, e))


def _mod_rows(cond8, w_mod, b_mod):
    m = _adaln(cond8, w_mod, b_mod).reshape(8, 6, D_MODEL)
    return jnp.pad(m, ((0, 0), (0, 2), (0, 0)))


def _mod_spec(rows_per_cond):
    if rows_per_cond is None:
        return pl.BlockSpec((1, 8, D_MODEL), lambda i: (0, 0, 0))
    return pl.BlockSpec((1, 8, D_MODEL), lambda i: (1 + (i * TM) // rows_per_cond, 0, 0))


def _qkv_kernel(x_ref, mod_ref, gain_ref, w_ref, hg_ref, bd_ref, cos_ref, sin_ref,
                q_ref, ka_ref, va_ref, kb_ref, vb_ref, *, rope, transposed_kv):
    h = _norm_mod(x_ref[...], gain_ref[...], mod_ref[0, 0:1, :], mod_ref[0, 1:2, :])
    proj = _dot(h.astype(BF16), w_ref[...])
    bd = bd_ref[...]

    def head_norm(blk, g):
        ms = _dot((blk * blk).astype(BF16), bd)
        return blk * lax.rsqrt(ms + EPS) * g

    def rotary(blk):
        w = blk.shape[1]
        even = (lax.broadcasted_iota(I32, blk.shape, 1) & 1) == 0
        swapped = jnp.where(even, pltpu.roll(blk, w - 1, 1), pltpu.roll(blk, 1, 1))
        return blk * cos_ref[:, :w] + swapped * sin_ref[:, :w]

    def qk(c0):
        blk = head_norm(proj[:, c0:c0 + 256], hg_ref[:, c0:c0 + 256])
        return rotary(blk) if rope else blk

    q_ref[:, 0:256] = qk(0).astype(q_ref.dtype)
    q_ref[:, 256:512] = qk(256).astype(q_ref.dtype)
    q_ref[:, 512:768] = qk(768).astype(q_ref.dtype)
    q_ref[:, 768:1024] = qk(1024).astype(q_ref.dtype)
    kva = qk(512)
    kvb = qk(1280)
    outs = ((ka_ref, kva[:, :128]), (va_ref, proj[:, 640:768]), (kb_ref, kvb[:, :128]), (vb_ref, proj[:, 1408:1536]))
    for ref, val in outs:
        if transposed_kv:
            seq = ref.shape[2]
            for r in range(ref.shape[0]):
                ref[r] = val[r * seq:(r + 1) * seq].T.astype(ref.dtype)
        else:
            ref[...] = val.astype(ref.dtype)


def _qkv(x, mod, gain, w_bf, hgain, bd, cos_t, sin_t, *, rows_per_cond, seq, rope, kv_dtype, transposed_kv):
    n = x.shape[0]
    tiles_per_seq = max(1, seq // TM)
    row = lambda i: (i, 0)
    const = lambda i: (0, 0)
    pos = lambda i: (i % tiles_per_seq, 0)
    if transposed_kv:
        assert TM % seq == 0
        kv_shape = jax.ShapeDtypeStruct((n // seq, 128, seq), kv_dtype)
        kv_spec = pl.BlockSpec((TM // seq, 128, seq), lambda i: (i, 0, 0))
    else:
        kv_shape = jax.ShapeDtypeStruct((n, 128), kv_dtype)
        kv_spec = pl.BlockSpec((TM, 128), row)
    return pl.pallas_call(
        functools.partial(_qkv_kernel, rope=rope, transposed_kv=transposed_kv),
        grid=(n // TM,),
        in_specs=[pl.BlockSpec((TM, D_MODEL), row),
                  _mod_spec(rows_per_cond),
                  pl.BlockSpec((1, D_MODEL), const),
                  pl.BlockSpec((D_MODEL, QKV_COLS), const),
                  pl.BlockSpec((1, QKV_COLS), const),
                  pl.BlockSpec((256, 256), const),
                  pl.BlockSpec((TM, 256), pos),
                  pl.BlockSpec((TM, 256), pos)],
        out_specs=[pl.BlockSpec((TM, ATTN_OUT), row)] + [kv_spec] * 4,
        out_shape=[jax.ShapeDtypeStruct((n, ATTN_OUT), BF16)] + [kv_shape] * 4,
        compiler_params=_params("parallel"),
        name="qkv_rope" if rope else "qkv",
    )(x, mod, gain, w_bf, hgain, bd, cos_t, sin_t)


def _pad_variants(kk, ones=False):
    lane = lax.broadcasted_iota(I32, kk.shape, 1)
    left = lane < HEAD_DIM
    rolled = pltpu.roll(kk, HEAD_DIM, 1)
    fill_r = jnp.where(lane == HEAD_DIM, 1.0, 0.0) if ones else jnp.zeros_like(kk)
    fill_l = jnp.where(lane == 0, 1.0, 0.0) if ones else jnp.zeros_like(kk)
    return {(0, 0): jnp.where(left, kk, fill_r).astype(BF16),
            (0, 1): jnp.where(left, fill_l, rolled).astype(BF16),
            (1, 0): jnp.where(left, rolled, fill_r).astype(BF16),
            (1, 1): jnp.where(left, fill_l, kk).astype(BF16)}


def _pad_variants_t(kt):
    top = lax.broadcasted_iota(I32, kt.shape, 0) < HEAD_DIM
    zero = jnp.zeros((HEAD_DIM, kt.shape[1]), F32)
    return {(0, 0): jnp.where(top, kt, 0.0).astype(BF16),
            (0, 1): jnp.concatenate([zero, kt[:HEAD_DIM]], axis=0).astype(BF16),
            (1, 0): jnp.concatenate([kt[HEAD_DIM:], zero], axis=0).astype(BF16),
            (1, 1): jnp.where(top, 0.0, kt).astype(BF16)}


def _head_attention_small(qp, kblk, vblk, sink):
    s = _dot(qp, kblk, NN)
    m = s.max(axis=-1, keepdims=True)
    if sink is not None:
        m = jnp.maximum(m, sink)
    p = jnp.exp2(s - m)
    den = p.sum(axis=-1, keepdims=True)
    if sink is not None:
        den = den + jnp.exp2(sink - m)
    return _dot(p.astype(BF16), vblk, NT) / den


def _head_attention(qp, keys, vals, masks, sink, par):
    scores = []
    for kblk, mask in zip(keys, masks):
        s = _dot(qp, kblk, NT)
        if mask is not None:
            s = jnp.where(mask, s, NEG_INF)
        scores.append(s)
    m = scores[0].max(axis=-1, keepdims=True)
    for s in scores[1:]:
        m = jnp.maximum(m, s.max(axis=-1, keepdims=True))
    if sink is not None:
        m = jnp.maximum(m, sink)
    out = None
    for s, vblk in zip(scores, vals):
        o = _dot(jnp.exp2((s - m).astype(BF16)), vblk)
        out = o if out is None else out + o
    ones_lane = HEAD_DIM if par == 0 else 0
    den = out[:, ones_lane:ones_lane + 1]
    if sink is not None:
        den = den + jnp.exp2(sink - m)
    own = (lax.broadcasted_iota(I32, out.shape, 1) < HEAD_DIM) == (par == 0)
    return jnp.where(own, out / den, 0.0)


def _attn_ctx_kernel(sink_ref, q_ref, ka_ref, va_ref, kb_ref, vb_ref, o_ref):
    for mixer, (k_ref, v_ref) in enumerate(((ka_ref, va_ref), (kb_ref, vb_ref))):
        kvar = _pad_variants_t(k_ref[0])
        vvar = _pad_variants_t(v_ref[0])
        for t in range(4):
            tile = mixer * 4 + t
            kv = t // 2
            qp = q_ref[:, tile * 128:(tile + 1) * 128]
            acc = None
            for par in range(2):
                sink = sink_ref[2 * t + par] * LOG2E if mixer == 1 else None
                o = _head_attention_small(qp, kvar[(kv, par)], vvar[(kv, par)], sink)
                acc = o if acc is None else acc + o
            o_ref[:, tile * 128:(tile + 1) * 128] = acc.astype(o_ref.dtype)


def _attn_ctx(sink, q, ka, va, kb, vb, seq):
    n = q.shape[0]
    row = lambda b: (b, 0)
    kv_spec = pl.BlockSpec((1, 128, seq), lambda b: (b, 0, 0))
    return pl.pallas_call(
        _attn_ctx_kernel,
        grid=(n // seq,),
        in_specs=[pl.BlockSpec(memory_space=pltpu.SMEM),
                  pl.BlockSpec((seq, ATTN_OUT), row), kv_spec, kv_spec, kv_spec, kv_spec],
        out_specs=pl.BlockSpec((seq, ATTN_OUT), row),
        out_shape=jax.ShapeDtypeStruct((n, ATTN_OUT), BF16),
        compiler_params=_params("parallel"),
        name="attn_ctx",
    )(sink, q, ka, va, kb, vb)


def _attn_lat_kernel(sink_ref, q_ref, ka_ref, va_ref, kb_ref, vb_ref,
                     cka_ref, cva_ref, ckb_ref, cvb_ref, o_ref, *, tq, seq):
    qi = pl.program_id(1)
    ck = _pad_variants(cka_ref[0])
    cv = _pad_variants(cva_ref[0], ones=True)
    lk = _pad_variants(ka_ref[...].astype(F32))
    lv = _pad_variants(va_ref[...].astype(F32), ones=True)
    for t in range(4):
        kv = t // 2
        qp = q_ref[:, t * 128:(t + 1) * 128]
        acc = None
        for par in range(2):
            o = _head_attention(qp, [ck[(kv, par)], lk[(kv, par)]], [cv[(kv, par)], lv[(kv, par)]],
                                [None, None], None, par)
            acc = o if acc is None else acc + o
        o_ref[:, t * 128:(t + 1) * 128] = acc.astype(o_ref.dtype)
    span = BAND_TQ + 2 * WINDOW
    ck = _pad_variants(ckb_ref[0])
    cv = _pad_variants(cvb_ref[0], ones=True)
    for sub in range(tq // BAND_TQ):
        q0 = qi * tq + sub * BAND_TQ
        rows = slice(sub * BAND_TQ, (sub + 1) * BAND_TQ)
        lo = pl.multiple_of(jnp.clip(q0 - WINDOW, 0, seq - span), 128)
        qpos = q0 + lax.broadcasted_iota(I32, (BAND_TQ, span), 0)
        kpos = lo + lax.broadcasted_iota(I32, (BAND_TQ, span), 1)
        band = jnp.abs(qpos - kpos) <= WINDOW
        lk = _pad_variants(kb_ref[pl.ds(lo, span), :].astype(F32))
        lv = _pad_variants(vb_ref[pl.ds(lo, span), :].astype(F32), ones=True)
        for t in range(4):
            kv = t // 2
            tile = 4 + t
            qp = q_ref[rows, tile * 128:(tile + 1) * 128]
            acc = None
            for par in range(2):
                sink = sink_ref[2 * t + par] * LOG2E
                o = _head_attention(qp, [lk[(kv, par)], ck[(kv, par)]], [lv[(kv, par)], cv[(kv, par)]],
                                    [band, None], sink, par)
                acc = o if acc is None else acc + o
            o_ref[rows, tile * 128:(tile + 1) * 128] = acc.astype(o_ref.dtype)


def _attn_lat(sink, q, ka, va, kb, vb, cka, cva, ckb, cvb, seq, tq=512):
    n = q.shape[0]
    nb = n // seq
    nq = seq // tq
    qrow = lambda b, i: (b * nq + i, 0)
    brow = lambda b, i: (b, 0)
    kv_spec = pl.BlockSpec((seq, 128), brow)
    past = cka.shape[1]
    c_spec = pl.BlockSpec((1, past, 128), lambda b, i: (b, 0, 0))
    return pl.pallas_call(
        functools.partial(_attn_lat_kernel, tq=tq, seq=seq),
        grid=(nb, nq),
        in_specs=[pl.BlockSpec(memory_space=pltpu.SMEM),
                  pl.BlockSpec((tq, ATTN_OUT), qrow), kv_spec, kv_spec, kv_spec, kv_spec,
                  c_spec, c_spec, c_spec, c_spec],
        out_specs=pl.BlockSpec((tq, ATTN_OUT), qrow),
        out_shape=jax.ShapeDtypeStruct((n, ATTN_OUT), BF16),
        compiler_params=_params("parallel", "parallel"),
        name="attn_lat",
    )(sink, q, ka, va, kb, vb, cka, cva, ckb, cvb)


def _postmix_kernel(m_ref, x_ref, mod_ref, w_ref, gain_ref, rt_ref, x1_ref, h2_ref, aff_ref, *, project):
    if project:
        m = _dot(m_ref[...], w_ref[...])
    else:
        m = m_ref[...]
    x1 = x_ref[...] + mod_ref[0, 2:3, :] * m
    x1_ref[...] = x1
    h2 = _norm_mod(x1, gain_ref[...], mod_ref[0, 3:4, :], mod_ref[0, 4:5, :])
    h2_ref[...] = h2.astype(h2_ref.dtype)
    logits = _dot3(rt_ref[...], h2, NT)
    e = jnp.exp(logits - logits.max(axis=0, keepdims=True))
    aff_ref[...] = e / e.sum(axis=0, keepdims=True)


def _postmix(m, x, mod, w_bf, gain, router_t, *, rows_per_cond, project):
    n = x.shape[0]
    row = lambda i: (i, 0)
    const = lambda i: (0, 0)
    return pl.pallas_call(
        functools.partial(_postmix_kernel, project=project),
        grid=(n // TM,),
        in_specs=[pl.BlockSpec((TM, D_MODEL), row),
                  pl.BlockSpec((TM, D_MODEL), row),
                  _mod_spec(rows_per_cond),
                  pl.BlockSpec(w_bf.shape, const),
                  pl.BlockSpec((1, D_MODEL), const),
                  pl.BlockSpec((N_EXPERTS, D_MODEL), const)],
        out_specs=[pl.BlockSpec((TM, D_MODEL), row),
                   pl.BlockSpec((TM, D_MODEL), row),
                   pl.BlockSpec((N_EXPERTS, TM), lambda i: (0, i))],
        out_shape=[jax.ShapeDtypeStruct((n, D_MODEL), F32),
                   jax.ShapeDtypeStruct((n, D_MODEL), BF16),
                   jax.ShapeDtypeStruct((N_EXPERTS, n), F32)],
        compiler_params=_params("parallel"),
        name="postmix_proj" if project else "postmix",
    )(m, x, mod, w_bf, gain, router_t)


def _route_kernel(aff_ref, slot_ref, gate_ref, pos_ref, tcnt_ref, *, seq, cap, nseg, tt):
    aff = jnp.concatenate([aff_ref[:, s * seq:(s + 1) * seq] for s in range(nseg)], axis=0)
    rows = aff.shape[0]
    capf = jnp.float32(cap)
    thr_bits = jnp.zeros((rows, 1), I32)

    def enough(cand):
        cnt = jnp.where(aff >= pltpu.bitcast(cand, F32), 1.0, 0.0).sum(axis=1, keepdims=True)
        return cnt >= capf

    for bit in range(30, 0, -2):
        hi, lo = 1 << bit, 1 << (bit - 1)
        both, only_hi, only_lo = thr_bits | hi | lo, thr_bits | hi, thr_bits | lo
        thr_bits = jnp.where(enough(both), both,
                             jnp.where(enough(only_hi), only_hi, jnp.where(enough(only_lo), only_lo, thr_bits)))
    thr_bits = jnp.where(enough(thr_bits | 1), thr_bits | 1, thr_bits)
    thr = pltpu.bitcast(thr_bits, F32)
    gt = aff > thr
    eq = aff == thr
    n_gt = jnp.where(gt, 1.0, 0.0).sum(axis=1, keepdims=True)
    pw = min(seq, 256)
    tri = jnp.where(lax.broadcasted_iota(I32, (pw, pw), 0) < lax.broadcasted_iota(I32, (pw, pw), 1),
                    1.0, 0.0).astype(BF16)

    def count_before(flag):
        ones = jnp.where(flag, 1.0, 0.0)
        parts = []
        run = jnp.zeros((rows, 1), F32)
        for c0 in range(0, seq, pw):
            blk = ones[:, c0:c0 + pw]
            parts.append(_dot(blk.astype(BF16), tri) + run)
            run = run + blk.sum(axis=1, keepdims=True)
        return jnp.concatenate(parts, axis=1) if len(parts) > 1 else parts[0]

    sel = gt | (eq & (count_before(eq) < capf - n_gt))
    rank = count_before(sel)
    expert = lax.broadcasted_iota(I32, (rows, seq), 0) & (N_EXPERTS - 1)
    slot = jnp.where(sel, expert * cap + rank.astype(I32), -1)
    gate = jnp.where(sel, aff, 0.0)
    pos = jnp.where(sel, rank, -1.0)
    nt = seq // tt
    tile_of = jnp.where((lax.broadcasted_iota(I32, (seq, nt), 0) // tt) == lax.broadcasted_iota(I32, (seq, nt), 1),
                        1.0, 0.0).astype(BF16)
    tcnt = _dot(jnp.where(sel, 1.0, 0.0).astype(BF16), tile_of)
    for s in range(nseg):
        rows_s = slice(s * N_EXPERTS, (s + 1) * N_EXPERTS)
        slot_ref[:, s * seq:(s + 1) * seq] = slot[rows_s, :]
        gate_ref[:, s * seq:(s + 1) * seq] = gate[rows_s, :]
        pos_ref[:, s * seq:(s + 1) * seq] = pos[rows_s, :]
        tcnt_ref[0, :, s * nt:(s + 1) * nt] = tcnt[rows_s, :]


def _route(aff_t, seq, cap, nseg, tt):
    n = aff_t.shape[1]
    nt = seq // tt
    steps = n // (nseg * seq)
    spec = pl.BlockSpec((N_EXPERTS, nseg * seq), lambda i: (0, i))
    slot, gate, pos, tcnt = pl.pallas_call(
        functools.partial(_route_kernel, seq=seq, cap=cap, nseg=nseg, tt=tt),
        grid=(steps,),
        in_specs=[spec],
        out_specs=[spec, spec, spec, pl.BlockSpec((1, N_EXPERTS, nseg * nt), lambda i: (i, 0, 0))],
        out_shape=[jax.ShapeDtypeStruct((N_EXPERTS, n), I32), jax.ShapeDtypeStruct((N_EXPERTS, n), F32),
                   jax.ShapeDtypeStruct((N_EXPERTS, n), F32),
                   jax.ShapeDtypeStruct((steps, N_EXPERTS, nseg * nt), F32)],
        compiler_params=_params("parallel"),
        name="route",
    )(aff_t)
    return slot, gate, pos, tcnt.transpose(1, 0, 2).reshape(N_EXPERTS, n // tt)


def _dispatch_kernel(slot_ref, h_ref, x_ref, *, cap, seq):
    m = N_EXPERTS * cap
    slot_id = lax.broadcasted_iota(I32, (m, seq), 0)
    for r in range(h_ref.shape[0] // seq):
        slots = slot_ref[:, r * seq:(r + 1) * seq]
        owner = jnp.broadcast_to(slots[:, None, :], (N_EXPERTS, cap, seq)).reshape(m, seq)
        sel = jnp.where(owner == slot_id, 1.0, 0.0).astype(BF16)
        x_ref[r * m:(r + 1) * m, :] = _dot(sel, h_ref[r * seq:(r + 1) * seq, :]).astype(x_ref.dtype)


def _dispatch(slot, h, seq, cap):
    n = h.shape[0]
    nb = n // seq
    rb = MOE_RB if nb % MOE_RB == 0 else 1
    return pl.pallas_call(
        functools.partial(_dispatch_kernel, cap=cap, seq=seq),
        grid=(nb // rb,),
        in_specs=[pl.BlockSpec((N_EXPERTS, rb * seq), lambda b: (0, b)),
                  pl.BlockSpec((rb * seq, D_MODEL), lambda b: (b, 0))],
        out_specs=pl.BlockSpec((rb * N_EXPERTS * cap, D_MODEL), lambda b: (b, 0)),
        out_shape=jax.ShapeDtypeStruct((nb * N_EXPERTS * cap, D_MODEL), BF16),
        compiler_params=_params("parallel"),
        name="moe_dispatch",
    )(slot, h)


def _window(cum_ref, base, e, cap, k=0):
    lo = _align_down(cum_ref[base + e]) + k * MOE_W
    return lo, pl.multiple_of(jnp.minimum(lo, cap - MOE_W), BF16_ROWS)


def _align_down(rank):
    return rank & ~(BF16_ROWS - 1)


def _extra_windows(cum_ref, base, e):
    lo = _align_down(cum_ref[base + e])
    return (cum_ref[base + N_EXPERTS + e] - lo + (MOE_W - 1)) // MOE_W


def _dispatch_win_kernel(cum_ref, slot_ref, h_ref, x_ref, *, cap, nt):
    b, i = pl.program_id(0), pl.program_id(1)
    tt = h_ref.shape[0]
    base = (b * (nt + 1) + i) * N_EXPERTS
    h = h_ref[...]
    row = lax.broadcasted_iota(I32, (MOE_W, tt), 0)

    @pl.when(i == 0)
    def _():
        x_ref[...] = jnp.zeros_like(x_ref)

    def hits(e, lo, ws):
        srow = slot_ref[e:e + 1, :]
        return (srow == row + (e * cap + ws)) & (srow >= e * cap + lo)

    for grp in range(N_EXPERTS // MOE_EG):
        wins = [(e,) + _window(cum_ref, base, e, cap) for e in range(grp * MOE_EG, (grp + 1) * MOE_EG)]
        sel = jnp.concatenate([hits(e, lo, ws) for e, lo, ws in wins], axis=0)
        x = _dot(jnp.where(sel, 1.0, 0.0).astype(BF16), h)
        for q, (e, lo, ws) in enumerate(wins):
            dst = pl.ds(e * cap + ws, MOE_W)
            x_ref[dst, :] += x[q * MOE_W:(q + 1) * MOE_W].astype(x_ref.dtype)

    for e in range(N_EXPERTS):
        def extra(k, carry, e=e):
            lo, ws = _window(cum_ref, base, e, cap, k)
            x = _dot(jnp.where(hits(e, lo, ws), 1.0, 0.0).astype(BF16), h)
            x_ref[pl.ds(e * cap + ws, MOE_W), :] += x.astype(x_ref.dtype)
            return carry
        lax.fori_loop(1, _extra_windows(cum_ref, base, e), extra, 0)


def _dispatch_win(cum, slot, h, seq, cap):
    n = h.shape[0]
    nb, nt = n // seq, seq // MOE_TT
    return pl.pallas_call(
        functools.partial(_dispatch_win_kernel, cap=cap, nt=nt),
        grid_spec=pltpu.PrefetchScalarGridSpec(
            num_scalar_prefetch=1,
            grid=(nb, nt),
            in_specs=[pl.BlockSpec((N_EXPERTS, MOE_TT), lambda b, i, c: (0, b * nt + i)),
                      pl.BlockSpec((MOE_TT, D_MODEL), lambda b, i, c: (b * nt + i, 0))],
            out_specs=pl.BlockSpec((N_EXPERTS * cap, D_MODEL), lambda b, i, c: (b, 0))),
        out_shape=jax.ShapeDtypeStruct((nb * N_EXPERTS * cap, D_MODEL), BF16),
        compiler_params=_params("parallel", "arbitrary"),
        name="moe_dispatch_win",
    )(cum, slot, h)


FFN_TF = 512
FFN_RC = 512


def _ffn_kernel(xa_ref, xb_ref, wg_ref, wu_ref, wd_ref, ya_ref, yb_ref, acc_ref):
    j = pl.program_id(1)
    last = pl.num_programs(1) - 1
    ra = xa_ref.shape[0] * xa_ref.shape[2]

    def row_chunks(ref):
        nb, _, cap, d = ref.shape
        rc = min(FFN_RC, nb * cap)
        for r0 in range(0, nb * cap, rc):
            if cap >= rc:
                b, c0 = divmod(r0, cap)
                yield r0, rc, (slice(b, b + 1), 0, slice(c0, c0 + rc), slice(None)), (1, rc, d)
            else:
                yield r0, rc, (slice(r0 // cap, (r0 + rc) // cap), 0, slice(None), slice(None)), (rc // cap, cap, d)

    def sweep(first_tile, last_tile):
        wg = wg_ref[0].astype(BF16)
        wu = wu_ref[0].astype(BF16)
        wd = wd_ref[0].astype(BF16)
        d = wd.shape[1]
        for x_ref, y_ref, base in ((xa_ref, ya_ref, 0), (xb_ref, yb_ref, ra)):
            for r0, rc, idx, shape in row_chunks(x_ref):
                x = x_ref[idx].reshape(rc, d)
                g = _dot(x, wg)
                u = _dot(x, wu)
                y = _dot((g * _sigmoid(g) * u).astype(BF16), wd)
                rows = slice(base + r0, base + r0 + rc)
                if not first_tile:
                    y = acc_ref[rows, :] + y
                if last_tile:
                    y_ref[idx] = y.astype(y_ref.dtype).reshape(shape)
                else:
                    acc_ref[rows, :] = y

    if D_FF == FFN_TF:
        sweep(True, True)
    else:
        pl.when(j == 0)(lambda: sweep(True, False))
        pl.when((j > 0) & (j < last))(lambda: sweep(False, False))
        pl.when(j == last)(lambda: sweep(False, True))


def _ffn(xa, xb, w_gate, w_up, w_down):
    ba, _, ca, d = xa.shape
    bb, _, cb, _ = xb.shape
    nj = D_FF // FFN_TF
    xa_spec = pl.BlockSpec((ba, 1, ca, d), lambda e, j: (0, e, 0, 0))
    xb_spec = pl.BlockSpec((bb, 1, cb, d), lambda e, j: (0, e, 0, 0))
    return pl.pallas_call(
        _ffn_kernel,
        grid=(N_EXPERTS, nj),
        in_specs=[xa_spec, xb_spec,
                  pl.BlockSpec((1, d, FFN_TF), lambda e, j: (e, 0, j)),
                  pl.BlockSpec((1, d, FFN_TF), lambda e, j: (e, 0, j)),
                  pl.BlockSpec((1, FFN_TF, d), lambda e, j: (e, j, 0))],
        out_specs=[xa_spec, xb_spec],
        out_shape=[jax.ShapeDtypeStruct(xa.shape, BF16), jax.ShapeDtypeStruct(xb.shape, BF16)],
        scratch_shapes=[pltpu.VMEM((ba * ca + bb * cb, d), F32)],
        compiler_params=_params("parallel", "arbitrary"),
        name="moe_ffn",
    )(xa, xb, w_gate, w_up, w_down)


def _expand(vals_bf, first_expert, width, total):
    e_of_lane = first_expert + lax.broadcasted_iota(I32, (N_EXPERTS, total), 1) // width
    pick = jnp.where(lax.broadcasted_iota(I32, (N_EXPERTS, total), 0) == e_of_lane, 1.0, 0.0).astype(BF16)
    return _dot(vals_bf, pick)


def _combine_kernel(pos_ref, gate_ref, y_ref, x_ref, mod_ref, o_ref, *, cap, seq):
    m = N_EXPERTS * cap
    rank = (lax.broadcasted_iota(I32, (seq, m), 1) % cap).astype(F32)
    for r in range(x_ref.shape[0] // seq):
        rows = slice(r * seq, (r + 1) * seq)
        pos = _expand(pos_ref[rows, :].astype(BF16), 0, cap, m)
        gate = _expand(gate_ref[rows, :].astype(BF16), 0, cap, m)
        w = jnp.where(pos == rank, gate, 0.0).astype(BF16)
        o_ref[rows, :] = x_ref[rows, :] + mod_ref[0, 5:6, :] * _dot(w, y_ref[r * m:(r + 1) * m, :])


def _combine(pos_t, gate_t, y, x, mod, *, seq, cap):
    n = x.shape[0]
    nb = n // seq
    rb = MOE_RB if nb % MOE_RB == 0 else 1
    row = lambda b: (b, 0)
    return pl.pallas_call(
        functools.partial(_combine_kernel, cap=cap, seq=seq),
        grid=(nb // rb,),
        in_specs=[pl.BlockSpec((rb * seq, N_EXPERTS), row),
                  pl.BlockSpec((rb * seq, N_EXPERTS), row),
                  pl.BlockSpec((rb * N_EXPERTS * cap, D_MODEL), row),
                  pl.BlockSpec((rb * seq, D_MODEL), row),
                  pl.BlockSpec((1, 8, D_MODEL), lambda b: (0, 0, 0))],
        out_specs=pl.BlockSpec((rb * seq, D_MODEL), row),
        out_shape=jax.ShapeDtypeStruct((n, D_MODEL), F32),
        compiler_params=_params("parallel"),
        name="moe_combine",
    )(pos_t, gate_t, y, x, mod)


def _combine_win_kernel(cum_ref, pos_ref, gate_ref, y_ref, x_ref, mod_ref, o_ref, acc_ref, *, cap, nt):
    b, i = pl.program_id(0), pl.program_id(1)
    tt = x_ref.shape[0]
    base = (b * (nt + 1) + i) * N_EXPERTS
    width = MOE_EG * MOE_W
    posb = pos_ref[...].astype(BF16)
    gateb = gate_ref[...].astype(BF16)
    lane = lax.broadcasted_iota(I32, (1, width), 1)
    offset = (lane % MOE_W).astype(F32)
    acc = jnp.zeros((tt, D_MODEL), F32)
    for grp in range(N_EXPERTS // MOE_EG):
        wins = [(e,) + _window(cum_ref, base, e, cap) for e in range(grp * MOE_EG, (grp + 1) * MOE_EG)]
        lo_l = jnp.zeros((1, width), F32)
        ws_l = jnp.zeros((1, width), F32)
        for q, (e, lo, ws) in enumerate(wins):
            mine = lane // MOE_W == q
            lo_l = jnp.where(mine, lo.astype(F32), lo_l)
            ws_l = jnp.where(mine, ws.astype(F32), ws_l)
        pos = _expand(posb, grp * MOE_EG, MOE_W, width)
        gate = _expand(gateb, grp * MOE_EG, MOE_W, width)
        w = jnp.where((pos - ws_l == offset) & (pos >= lo_l), gate, 0.0).astype(BF16)
        ywin = jnp.concatenate([y_ref[pl.ds(e * cap + ws, MOE_W), :] for e, lo, ws in wins], axis=0)
        acc = acc + _dot(w, ywin)
    acc_ref[...] = acc

    off64 = lax.broadcasted_iota(I32, (1, MOE_W), 1).astype(F32)
    for e in range(N_EXPERTS):
        def extra(k, carry, e=e):
            lo, ws = _window(cum_ref, base, e, cap, k)
            pos = pos_ref[:, e:e + 1]
            gate = gate_ref[:, e:e + 1].astype(BF16).astype(F32)
            w = jnp.where((pos - ws.astype(F32) == off64) & (pos >= lo.astype(F32)), gate, 0.0).astype(BF16)
            acc_ref[...] += _dot(w, y_ref[pl.ds(e * cap + ws, MOE_W), :])
            return carry
        lax.fori_loop(1, _extra_windows(cum_ref, base, e), extra, 0)

    o_ref[...] = x_ref[...] + mod_ref[0, 5:6, :] * acc_ref[...]


def _combine_win(cum, pos_t, gate_t, y, x, mod, *, seq, cap):
    n = x.shape[0]
    nb, nt = n // seq, seq // MOE_TT
    row = lambda b, i, c: (b * nt + i, 0)
    return pl.pallas_call(
        functools.partial(_combine_win_kernel, cap=cap, nt=nt),
        grid_spec=pltpu.PrefetchScalarGridSpec(
            num_scalar_prefetch=1,
            grid=(nb, nt),
            in_specs=[pl.BlockSpec((MOE_TT, N_EXPERTS), row),
                      pl.BlockSpec((MOE_TT, N_EXPERTS), row),
                      pl.BlockSpec((N_EXPERTS * cap, D_MODEL), lambda b, i, c: (b, 0)),
                      pl.BlockSpec((MOE_TT, D_MODEL), row),
                      pl.BlockSpec((1, 8, D_MODEL), lambda b, i, c: (1 + b, 0, 0))],
            out_specs=pl.BlockSpec((MOE_TT, D_MODEL), row),
            scratch_shapes=[pltpu.VMEM((MOE_TT, D_MODEL), F32)]),
        out_shape=jax.ShapeDtypeStruct((n, D_MODEL), F32),
        compiler_params=_params("parallel", "parallel"),
        name="moe_combine_win",
    )(cum, pos_t, gate_t, y, x, mod)


def _moe_pair(hp, affp, x1p, hs, affs, x1s, mod, w_gate, w_up, w_down, seq_p, seq_s):
    n_p, n_s = hp.shape[0], hs.shape[0]
    nb_p, nb_s = n_p // seq_p, n_s // seq_s
    cap_p = EC_FACTOR * seq_p // N_EXPERTS
    cap_s = EC_FACTOR * seq_s // N_EXPERTS
    assert N_EXPERTS * cap_p <= 512 and cap_s >= MOE_W and cap_s % BF16_ROWS == 0 and seq_s % MOE_TT == 0
    slot_p, gate_p, pos_p, _ = _route(affp, seq_p, cap_p, nseg=min(8, nb_p), tt=seq_p)
    slot_s, gate_s, pos_s, tcnt = _route(affs, seq_s, cap_s, nseg=min(4, nb_s), tt=MOE_TT)
    nt = seq_s // MOE_TT
    counts = tcnt.T.reshape(nb_s, nt, N_EXPERTS).astype(I32)
    cum = jnp.concatenate([jnp.zeros((nb_s, 1, N_EXPERTS), I32), jnp.cumsum(counts, axis=1)], axis=1).reshape(-1)
    xp = _dispatch(slot_p, hp, seq_p, cap_p).reshape(nb_p, N_EXPERTS, cap_p, D_MODEL)
    xs = _dispatch_win(cum, slot_s, hs, seq_s, cap_s).reshape(nb_s, N_EXPERTS, cap_s, D_MODEL)
    ys, yp = _ffn(xs, xp, w_gate, w_up, w_down)
    outp = _combine(pos_p.T, gate_p.T, yp.reshape(-1, D_MODEL), x1p, mod, seq=seq_p, cap=cap_p)
    outs = _combine_win(cum, pos_s.T, gate_s.T, ys.reshape(-1, D_MODEL), x1s, mod, seq=seq_s, cap=cap_s)
    return outp, outs


def _ssm_in_kernel(x_ref, mod_ref, gain_ref, wt_ref, ut_ref, *, cols_per_cond):
    tc = x_ref.shape[1]
    if cols_per_cond is None:
        h = _norm_mod(x_ref[0], gain_ref[...], mod_ref[0, 0:1, :], mod_ref[0, 1:2, :]).astype(BF16)
    else:
        first = 1 + pl.program_id(1) * (tc // cols_per_cond)
        parts = []
        for s in range(tc // cols_per_cond):
            m = mod_ref[first + s]
            parts.append(_norm_mod(x_ref[0, s * cols_per_cond:(s + 1) * cols_per_cond, :], gain_ref[...],
                                   m[0:1, :], m[1:2, :]).astype(BF16))
        h = jnp.concatenate(parts, axis=0) if len(parts) > 1 else parts[0]
    ut_ref[0] = _dot(wt_ref[...], h, NT)


def _ssm_in(xperm, mod, gain, wt_bf, *, cols_per_cond, tc):
    l, bk, d = xperm.shape
    assert cols_per_cond is None or tc % cols_per_cond == 0
    return pl.pallas_call(
        functools.partial(_ssm_in_kernel, cols_per_cond=cols_per_cond),
        grid=(l, bk // tc),
        in_specs=[pl.BlockSpec((1, tc, d), lambda j, i: (j, i, 0)),
                  pl.BlockSpec(mod.shape, lambda j, i: (0, 0, 0)),
                  pl.BlockSpec((1, d), lambda j, i: (0, 0)),
                  pl.BlockSpec((d, d), lambda j, i: (0, 0))],
        out_specs=pl.BlockSpec((1, d, tc), lambda j, i: (j, 0, i)),
        out_shape=jax.ShapeDtypeStruct((l, d, bk), F32),
        compiler_params=_params("parallel", "parallel"),
        name="ssm_in",
    )(xperm, mod, gain, wt_bf)


def _ssm_core_kernel(utp_ref, uts_ref, lamp_ref, c_ref, bt_ref, dsk_ref, h0_ref, ytp_ref, yts_ref, fs_ref,
                     *, kp, ks, nbp, nbs):
    rows = SSM_ROWS
    p = SSM_STATE
    lc = SSM_CHUNK
    ri = lax.broadcasted_iota(I32, (rows, rows), 0)
    cj = lax.broadcasted_iota(I32, (rows, rows), 1)
    causal = ri // SSM_GROUP >= cj // SSM_GROUP
    anticausal = cj // SSM_GROUP >= ri // SSM_GROUP
    diag = ri == cj
    leftc = lax.broadcasted_iota(I32, (lc, 128), 1) < p
    leftg = lax.broadcasted_iota(I32, (SSM_GROUP, 128), 1) < p
    nrow = lax.broadcasted_iota(I32, (lc, 128), 0).astype(F32)
    eye = lax.broadcasted_iota(I32, (p, 128), 0) == lax.broadcasted_iota(I32, (p, 128), 1)

    def cmul(ar, ai, xr, xi):
        return ar * xr - ai * xi, ar * xi + ai * xr

    def expand_rows(t):
        return jnp.broadcast_to(t[:, None, :], (lc, SSM_GROUP, 128)).reshape(rows, 128)

    def tile_rows(t):
        return jnp.broadcast_to(t[None, :, :], (lc, SSM_GROUP, 128)).reshape(rows, 128)

    def to_col(row):
        return jnp.where(eye, jnp.broadcast_to(row, (p, 128)), 0.0).sum(axis=1, keepdims=True)

    def operands(gg, d):
        lp = lamp_ref[gg, d]
        lre, lim = lp[0:1], lp[1:2]
        dt = jnp.exp(lp[2:3])
        a, th = lre * dt, lim * dt
        ang = nrow * th
        cs, sn = jnp.cos(ang), jnp.sin(ang)
        ep, em = jnp.exp(nrow * a), jnp.exp(-(nrow * a))
        pr, pi = ep * cs, ep * sn
        nr, ni = em * cs, -(em * sn)
        l1r, l1i = pr[1:2], pi[1:2]
        lmr, lmi = pr[lc - 1:lc], pi[lc - 1:lc]
        llr, lli = cmul(lmr, lmi, l1r, l1i)
        den = lre * lre + lim * lim
        cr = ((l1r - 1.0) * lre + l1i * lim) / den
        ci = (l1i * lre - (l1r - 1.0) * lim) / den
        btr, bti = bt_ref[gg, d, 0], bt_ref[gg, d, 1]
        bbr, bbi = cr * btr - ci * bti, cr * bti + ci * btr
        u1 = tile_rows(jnp.where(leftg, bbr, bbi))
        u2 = tile_rows(jnp.where(leftg, bbi, bbr))
        c1 = tile_rows(c_ref[gg, d, 0])
        c2 = tile_rows(c_ref[gg, d, 1])

        def left_form(xr, xi):
            return (c1 * expand_rows(jnp.where(leftc, xr, -xi))
                    + c2 * expand_rows(jnp.where(leftc, -xi, -xr)))

        def right_form(xr, xi):
            return u1 * expand_rows(xr) + u2 * expand_rows(jnp.where(leftc, -xi, xi))

        if d == 0:
            al = left_form(pr, pi)
            brt = right_form(nr, ni)
            rrt = right_form(*cmul(lmr, lmi, nr, ni))
            qq = left_form(*cmul(l1r, l1i, pr, pi))
            mat = jnp.where(causal, _dot3(al, brt, NT), 0.0)
        else:
            al = left_form(nr, ni)
            brt = right_form(pr, pi)
            rrt = brt
            qq = left_form(*cmul(llr, lli, nr, ni))
            mat = jnp.where(anticausal, _dot3(al, brt, NT), 0.0)
        return mat, rrt.T, qq, to_col(llr), to_col(lli)

    def scan(sr, si, lr, li, h0r, h0i, reverse, nchunk, nbatch, sel):
        bk = sr.shape[1]
        lane = lax.broadcasted_iota(I32, (p, bk), 1)
        kidx = lane & (nchunk - 1)
        edge = (nchunk - 1) if reverse else 0
        if h0r is not None:
            h0cr = jnp.zeros((p, bk), F32)
            h0ci = jnp.zeros((p, bk), F32)
            for b in range(nbatch):
                at = lane == (b * nchunk + edge)
                h0cr = jnp.where(at, h0r[:, b:b + 1], h0cr)
                h0ci = jnp.where(at, h0i[:, b:b + 1], h0ci)
            ar, ai = cmul(lr, li, h0cr, h0ci)
            er, ei = sr + ar, si + ai
        else:
            er, ei = sr, si
        ar, ai = lr, li
        s = 1
        while s < nchunk:
            if reverse:
                ok = kidx < nchunk - s
                tr, ti = pltpu.roll(er, bk - s, 1), pltpu.roll(ei, bk - s, 1)
            else:
                ok = kidx >= s
                tr, ti = pltpu.roll(er, s, 1), pltpu.roll(ei, s, 1)
            tr = jnp.where(ok, tr, 0.0)
            ti = jnp.where(ok, ti, 0.0)
            dr, di = cmul(ar, ai, tr, ti)
            er, ei = er + dr, ei + di
            ar, ai = cmul(ar, ai, ar, ai)
            s *= 2
        if reverse:
            inner = kidx < nchunk - 1
            hr, hi = pltpu.roll(er, bk - 1, 1), pltpu.roll(ei, bk - 1, 1)
        else:
            inner = kidx >= 1
            hr, hi = pltpu.roll(er, 1, 1), pltpu.roll(ei, 1, 1)
        hr = jnp.where(inner, hr, h0cr if h0r is not None else 0.0)
        hi = jnp.where(inner, hi, h0ci if h0r is not None else 0.0)
        fin = None if sel is None else (_sel_dot_t(sel, er), _sel_dot_t(sel, ei))
        return hr, hi, fin

    def final_selectors(bk, nchunk, nbatch):
        bat = lax.broadcasted_iota(I32, (nbatch, bk), 0)
        col = lax.broadcasted_iota(I32, (nbatch, bk), 1)
        last = jnp.where(col == bat * nchunk + (nchunk - 1), 1.0, 0.0).astype(BF16)
        first = jnp.where(col == bat * nchunk, 1.0, 0.0).astype(BF16)
        return last, first

    sel_last, sel_first = final_selectors(utp_ref.shape[2], kp, nbp)

    for gg in range(SSM_GB):
        mf, rf, qf, lfr, lfi = operands(gg, 0)
        mb, rb, qb, lbr, lbi = operands(gg, 1)
        skip = jnp.where(diag, jnp.broadcast_to(dsk_ref[gg], (rows, rows)), 0.0)
        stack = jnp.concatenate([mf + mb + skip, rf, rb], axis=0)
        qq = jnp.concatenate([qf, qb], axis=1)
        h0 = h0_ref[gg]
        for ut_ref, yt_ref, nchunk, nbatch, latent in ((utp_ref, ytp_ref, kp, nbp, False),
                                                       (uts_ref, yts_ref, ks, nbs, True)):
            bk = ut_ref.shape[2]
            x = ut_ref[:, gg * SSM_GROUP:(gg + 1) * SSM_GROUP, :].reshape(rows, bk)
            res = _mm(stack, x, SSM_PASSES)
            hfr, hfi, ff = scan(res[rows:rows + p], res[rows + p:rows + 2 * p], lfr, lfi,
                                h0[0] if latent else None, h0[1] if latent else None,
                                False, nchunk, nbatch, None if latent else sel_last)
            hbr, hbi, fb = scan(res[rows + 2 * p:rows + 3 * p], res[rows + 3 * p:rows + 4 * p], lbr, lbi,
                                h0[2] if latent else None, h0[3] if latent else None,
                                True, nchunk, nbatch, None if latent else sel_first)
            states = jnp.concatenate([hfr, hfi, hbr, hbi], axis=0)
            y = res[:rows] + _mm(qq, states, SSM_PASSES)
            yt_ref[:, gg * SSM_GROUP:(gg + 1) * SSM_GROUP, :] = y.reshape(lc, SSM_GROUP, bk)
            if not latent:
                fs_ref[gg, 0] = ff[0]
                fs_ref[gg, 1] = ff[1]
                fs_ref[gg, 2] = fb[0]
                fs_ref[gg, 3] = fb[1]


def _ssm_core(utp, uts, ops, h0, *, kp, ks, nbp, nbs):
    lamp, c2, bt2, dsk = ops
    l, d, bkp = utp.shape
    bks = uts.shape[2]
    g = SSM_GROUPS
    gb = SSM_GB
    lead4 = lambda i: (i, 0, 0, 0)
    lead5 = lambda i: (i, 0, 0, 0, 0)
    ut_spec = lambda bk: pl.BlockSpec((l, gb * SSM_GROUP, bk), lambda i: (0, i, 0))
    return pl.pallas_call(
        functools.partial(_ssm_core_kernel, kp=kp, ks=ks, nbp=nbp, nbs=nbs),
        grid=(g // gb,),
        in_specs=[ut_spec(bkp), ut_spec(bks),
                  pl.BlockSpec((gb, 2, 8, 128), lead4),
                  pl.BlockSpec((gb, 2, 2, SSM_GROUP, 128), lead5),
                  pl.BlockSpec((gb, 2, 2, SSM_GROUP, 128), lead5),
                  pl.BlockSpec((gb, 1, SSM_ROWS), lambda i: (i, 0, 0)),
                  pl.BlockSpec((gb, 4, SSM_STATE, nbs), lead4)],
        out_specs=[ut_spec(bkp), ut_spec(bks),
                   pl.BlockSpec((gb, 4, nbp, SSM_STATE), lead4)],
        out_shape=[jax.ShapeDtypeStruct((l, d, bkp), F32),
                   jax.ShapeDtypeStruct((l, d, bks), F32),
                   jax.ShapeDtypeStruct((g, 4, nbp, SSM_STATE), F32)],
        compiler_params=_params("parallel"),
        name="ssm_core",
    )(utp, uts, lamp, c2, bt2, dsk, h0)


def _ssm_out_kernel(yt_ref, w_ref, m_ref):
    y = yt_ref[0].T
    act = 0.5 * y * (1.0 + jnp.tanh(0.7978845608028654 * (y + 0.044715 * (y * y * y))))
    ag = _dot(act.astype(BF16), w_ref[...])
    d = m_ref.shape[2]
    m_ref[0] = ag[:, :d] * _sigmoid(ag[:, d:])


def _ssm_out(yt, w_bf, tc):
    l, d, bk = yt.shape
    return pl.pallas_call(
        _ssm_out_kernel,
        grid=(l, bk // tc),
        in_specs=[pl.BlockSpec((1, d, tc), lambda j, i: (j, 0, i)),
                  pl.BlockSpec((d, 2 * d), lambda j, i: (0, 0))],
        out_specs=pl.BlockSpec((1, tc, d), lambda j, i: (j, i, 0)),
        out_shape=jax.ShapeDtypeStruct((l, bk, d), F32),
        compiler_params=_params("parallel", "parallel"),
        name="ssm_out",
    )(yt, w_bf)


def _ssm_operand_params(lam_re, lam_im, b_re, b_im, c_re, c_im, log_dt, d_skip):
    g, l = SSM_GROUPS, SSM_CHUNK
    dup = lambda t: jnp.concatenate([t, t], axis=-1)
    lamp = jnp.stack([lam_re, lam_im, jnp.broadcast_to(log_dt[..., None], lam_re.shape)], axis=2)
    lamp = dup(jnp.pad(lamp, ((0, 0), (0, 0), (0, 5), (0, 0)))).transpose(1, 0, 2, 3)
    c2 = dup(jnp.stack([c_re, c_im], axis=2)).transpose(1, 0, 2, 3, 4)
    bt2 = dup(jnp.stack([jnp.swapaxes(b_re, -1, -2), jnp.swapaxes(b_im, -1, -2)], axis=2)).transpose(1, 0, 2, 3, 4)
    dsk = jnp.tile(d_skip.reshape(g, 1, SSM_GROUP), (1, 1, l))
    return lamp, c2, bt2, dsk


def _to_chunks(x, nb, seq):
    k = seq // SSM_CHUNK
    return x.reshape(nb, k, SSM_CHUNK, -1).transpose(2, 0, 1, 3).reshape(SSM_CHUNK, nb * k, -1)


def _from_chunks(x, nb, seq):
    k = seq // SSM_CHUNK
    return x.reshape(SSM_CHUNK, nb, k, -1).transpose(1, 2, 0, 3).reshape(nb * seq, -1)


def _ssm_mixers(xp, xs, mod, gain, wt_bf, ops, w_out_bf, h0, *, nbp, sp, nbs, ss):
    kp, ks = sp // SSM_CHUNK, ss // SSM_CHUNK
    tcp, tcs = min(SSM_TC, nbp * kp), min(SSM_TC, nbs * ks)
    utp = _ssm_in(_to_chunks(xp, nbp, sp), mod, gain, wt_bf, cols_per_cond=None, tc=tcp)
    uts = _ssm_in(_to_chunks(xs, nbs, ss), mod, gain, wt_bf, cols_per_cond=ks, tc=tcs)
    ytp, yts, fs = _ssm_core(utp, uts, ops, h0, kp=kp, ks=ks, nbp=nbp, nbs=nbs)
    mp = _from_chunks(_ssm_out(ytp, w_out_bf, tcp), nbp, sp)
    ms = _from_chunks(_ssm_out(yts, w_out_bf, tcs), nbs, ss)
    return mp, ms, fs


def _rope_tables(seq):
    t = jnp.arange(seq)
    row = (t // GRID_W).astype(F32)
    col = (t % GRID_W).astype(F32)
    n_freq = HEAD_DIM // 4
    inv_freq = ROPE_THETA ** (-jnp.arange(n_freq, dtype=F32) / n_freq)
    ang = jnp.concatenate([row[:, None] * inv_freq, col[:, None] * inv_freq], axis=-1)
    cos = jnp.repeat(jnp.cos(ang), 2, axis=-1)
    sin = jnp.repeat(jnp.sin(ang), 2, axis=-1)
    sign = jnp.tile(jnp.array([-1.0, 1.0], F32), HEAD_DIM // 2)
    return jnp.tile(cos, (1, 4)), jnp.tile(sin * sign, (1, 4))


def _head_gains(qn_a, kn_a, qn_b, kn_b):
    scale = HEAD_DIM ** -0.5 * LOG2E
    ones = jnp.ones((N_KV * HEAD_DIM,), F32)
    return jnp.concatenate([jnp.tile(qn_a, N_HEADS) * scale, jnp.tile(kn_a, N_KV), ones,
                            jnp.tile(qn_b, N_HEADS) * scale, jnp.tile(kn_b, N_KV), ones]).reshape(1, QKV_COLS)


def kernel(x_prompt, x_sample, c, cache_k_a_l0, cache_v_a_l0, cache_k_b_l0, cache_v_b_l0, state_ssm_re_l1, state_ssm_im_l1, c_ctx, mod_w_l0, mod_b_l0, norm_mix_l0, attn_w_in_l0, q_norm_a_l0, k_norm_a_l0, q_norm_b_l0, k_norm_b_l0, sink_b_l0, attn_w_out_l0, norm_ffn_l0, router_l0, moe_w_gate_l0, moe_w_up_l0, moe_w_down_l0, mod_w_l1, mod_b_l1, norm_mix_l1, ssm_w_in_l1, ssm_lambda_re_l1, ssm_lambda_im_l1, ssm_b_re_l1, ssm_b_im_l1, ssm_c_re_l1, ssm_c_im_l1, ssm_log_dt_l1, ssm_d_l1, ssm_w_out_l1, norm_ffn_l1, router_l1, moe_w_gate_l1, moe_w_up_l1, moe_w_down_l1):
    bp, sp, d = x_prompt.shape
    bs, ss, _ = x_sample.shape
    past = cache_k_a_l0.shape[1]
    assert d == D_MODEL and bs <= 7 and (bp * sp) % TM == 0 and TM % sp == 0 and ss % TM == 0
    xp = x_prompt.reshape(bp * sp, d)
    xs = x_sample.reshape(bs * ss, d)
    cond8 = jnp.concatenate([c_ctx[None], c, jnp.zeros((7 - bs, d), F32)], axis=0)
    row1 = lambda v: v.reshape(1, -1)

    mod0 = _mod_rows(cond8, mod_w_l0, mod_b_l0)
    w_in = attn_w_in_l0.astype(BF16)
    hgain = _head_gains(q_norm_a_l0, k_norm_a_l0, q_norm_b_l0, k_norm_b_l0)
    lane = np.arange(256)
    bd = jnp.asarray((lane[:, None] // HEAD_DIM == lane[None, :] // HEAD_DIM) / HEAD_DIM, BF16)
    cos_t, sin_t = _rope_tables(ss)
    qp, kap, vap, kbp, vbp = _qkv(xp, mod0, row1(norm_mix_l0), w_in, hgain, bd, cos_t, sin_t,
                                  rows_per_cond=None, seq=sp, rope=False, kv_dtype=F32, transposed_kv=True)
    qs, kas, vas, kbs, vbs = _qkv(xs, mod0, row1(norm_mix_l0), w_in, hgain, bd, cos_t, sin_t,
                                  rows_per_cond=ss, seq=ss, rope=True, kv_dtype=BF16, transposed_kv=False)
    op = _attn_ctx(sink_b_l0, qp, kap, vap, kbp, vbp, sp)
    cache = lambda t: t.reshape(bs, past, N_KV * HEAD_DIM)
    os_ = _attn_lat(sink_b_l0, qs, kas, vas, kbs, vbs, cache(cache_k_a_l0), cache(cache_v_a_l0),
                    cache(cache_k_b_l0), cache(cache_v_b_l0), ss)
    w_out = attn_w_out_l0.astype(BF16)
    x1p, hp, affp = _postmix(op, xp, mod0, w_out, row1(norm_ffn_l0), router_l0.T, rows_per_cond=None, project=True)
    x1s, hs, affs = _postmix(os_, xs, mod0, w_out, row1(norm_ffn_l0), router_l0.T, rows_per_cond=ss, project=True)
    xp, xs = _moe_pair(hp, affp, x1p, hs, affs, x1s, mod0, moe_w_gate_l0, moe_w_up_l0, moe_w_down_l0, sp, ss)

    mod1 = _mod_rows(cond8, mod_w_l1, mod_b_l1)
    ops = _ssm_operand_params(ssm_lambda_re_l1, ssm_lambda_im_l1, ssm_b_re_l1, ssm_b_im_l1, ssm_c_re_l1, ssm_c_im_l1,
                              ssm_log_dt_l1, ssm_d_l1)
    wt = ssm_w_in_l1.T.astype(BF16)
    w_so = ssm_w_out_l1.astype(BF16)
    h0 = jnp.stack([state_ssm_re_l1[:, 0], state_ssm_im_l1[:, 0], state_ssm_re_l1[:, 1], state_ssm_im_l1[:, 1]],
                   axis=0).transpose(2, 0, 3, 1)
    mp, ms, fsp = _ssm_mixers(xp, xs, mod1, row1(norm_mix_l1), wt, ops, w_so, h0, nbp=bp, sp=sp, nbs=bs, ss=ss)
    dummy_w = jnp.zeros((8, 128), BF16)
    x1p, hp, affp = _postmix(mp, xp, mod1, dummy_w, row1(norm_ffn_l1), router_l1.T, rows_per_cond=None, project=False)
    x1s, hs, affs = _postmix(ms, xs, mod1, dummy_w, row1(norm_ffn_l1), router_l1.T, rows_per_cond=ss, project=False)
    xp, xs = _moe_pair(hp, affp, x1p, hs, affs, x1s, mod1, moe_w_gate_l1, moe_w_up_l1, moe_w_down_l1, sp, ss)

    kv_out = lambda t: t.reshape(bp, N_KV, HEAD_DIM, sp).transpose(0, 3, 1, 2)
    fin = fsp.transpose(2, 1, 0, 3)
    ssm_re = jnp.stack([fin[:, 0], fin[:, 2]], axis=1)
    ssm_im = jnp.stack([fin[:, 1], fin[:, 3]], axis=1)
    return (xp.reshape(bp, sp, d), xs.reshape(bs, ss, d), kv_out(kap), kv_out(vap), kv_out(kbp), kv_out(vbp),
            ssm_re, ssm_im)
```

```python
import functools

import jax
import jax.numpy as jnp
import numpy as np
from jax import lax
from jax.experimental import pallas as pl
from jax.experimental.pallas import tpu as pltpu

F32, BF16, I32 = jnp.float32, jnp.bfloat16, jnp.int32

D_MODEL = 1024
GRID_W = 64
HEAD_DIM = 64
N_HEADS = 8
N_KV = 2
WINDOW = 128
ROPE_THETA = 10000.0
SSM_GROUP = 16
SSM_GROUPS = D_MODEL // SSM_GROUP
SSM_STATE = 64
N_EXPERTS = 16
EC_FACTOR = 2
D_FF = 2 * D_MODEL
EPS = 1e-6
NEG_INF = -1e30
LOG2E = 1.4426950408889634
QKV_COLS = 2 * (N_HEADS + 2 * N_KV) * HEAD_DIM
ATTN_OUT = 2 * N_HEADS * HEAD_DIM

SSM_CHUNK = 16
SSM_ROWS = SSM_CHUNK * SSM_GROUP
SSM_GB = 4
SSM_PASSES = 1

SSM_TC = 512

TM = 1024

BAND_TQ = 256

MOE_TT = 256
BF16_ROWS = 16
MOE_W = 64
MOE_EG = 4
MOE_RB = 4
VMEM_LIMIT = 56 * 1024 * 1024

NN = (((1,), (0,)), ((), ()))
NT = (((1,), (1,)), ((), ()))
TN = (((0,), (0,)), ((), ()))


def _dot(a, b, dims=NN):
    return lax.dot_general(a, b, dims, preferred_element_type=F32)


def _split2(x):
    hi = x.astype(BF16)
    lo = (x - hi.astype(F32)).astype(BF16)
    return hi, lo


def _split3(x):
    hi = x.astype(BF16)
    r = x - hi.astype(F32)
    mid = r.astype(BF16)
    lo = (r - mid.astype(F32)).astype(BF16)
    return hi, mid, lo


def _dot3(a, b, dims=NN):
    ah, al = _split2(a)
    bh, bl = _split2(b)
    return _dot(ah, bh, dims) + (_dot(ah, bl, dims) + _dot(al, bh, dims))


def _mm(a, b, passes):
    if passes == 1:
        return _dot(a.astype(BF16), b.astype(BF16))
    return _dot3(a, b)


def _sel_dot_t(sel, x):
    hi, mid, lo = _split3(x)
    return _dot(sel, hi, NT) + (_dot(sel, mid, NT) + _dot(sel, lo, NT))


def _sigmoid(x):
    return 1.0 / (1.0 + jnp.exp(-x))


def _norm_mod(x, gain, shift, scale):
    ms = jnp.mean(x * x, axis=-1, keepdims=True)
    y = x * lax.rsqrt(ms + EPS) * gain
    return y * (1.0 + scale) + shift


def _params(*sem):
    return pltpu.CompilerParams(dimension_semantics=sem, vmem_limit_bytes=VMEM_LIMIT)


def _adaln_kernel(c_ref, w_ref, b_ref, o_ref):
    c = c_ref[...]
    s = c * _sigmoid(c)
    o_ref[...] = _dot3(s, w_ref[...]) + b_ref[...]


def _adaln(cond8, w_mod, b_mod):
    d, e = w_mod.shape
    tn = 1536
    return pl.pallas_call(
        _adaln_kernel,
        grid=(e // tn,),
        in_specs=[pl.BlockSpec((8, d), lambda j: (0, 0)),
                  pl.BlockSpec((d, tn), lambda j: (0, j)),
                  pl.BlockSpec((1, tn), lambda j: (0, j))],
        out_specs=pl.BlockSpec((8, tn), lambda j: (0, j)),
        out_shape=jax.ShapeDtypeStruct((8, e), F32),
        compiler_params=_params("parallel"),
        name="adaln",
    )(cond8, w_mod, b_mod.reshape(1, e))


def _mod_rows(cond8, w_mod, b_mod):
    m = _adaln(cond8, w_mod, b_mod).reshape(8, 6, D_MODEL)
    return jnp.pad(m, ((0, 0), (0, 2), (0, 0)))


def _mod_spec(rows_per_cond):
    if rows_per_cond is None:
        return pl.BlockSpec((1, 8, D_MODEL), lambda i: (0, 0, 0))
    return pl.BlockSpec((1, 8, D_MODEL), lambda i: (1 + (i * TM) // rows_per_cond, 0, 0))


def _qkv_kernel(x_ref, mod_ref, gain_ref, w_ref, hg_ref, bd_ref, cos_ref, sin_ref,
                q_ref, ka_ref, va_ref, kb_ref, vb_ref, *, rope, transposed_kv):
    h = _norm_mod(x_ref[...], gain_ref[...], mod_ref[0, 0:1, :], mod_ref[0, 1:2, :])
    proj = _dot(h.astype(BF16), w_ref[...])
    bd = bd_ref[...]

    def head_norm(blk, g):
        ms = _dot((blk * blk).astype(BF16), bd)
        return blk * lax.rsqrt(ms + EPS) * g

    def rotary(blk):
        w = blk.shape[1]
        even = (lax.broadcasted_iota(I32, blk.shape, 1) & 1) == 0
        swapped = jnp.where(even, pltpu.roll(blk, w - 1, 1), pltpu.roll(blk, 1, 1))
        return blk * cos_ref[:, :w] + swapped * sin_ref[:, :w]

    def qk(c0):
        blk = head_norm(proj[:, c0:c0 + 256], hg_ref[:, c0:c0 + 256])
        return rotary(blk) if rope else blk

    q_ref[:, 0:256] = qk(0).astype(q_ref.dtype)
    q_ref[:, 256:512] = qk(256).astype(q_ref.dtype)
    q_ref[:, 512:768] = qk(768).astype(q_ref.dtype)
    q_ref[:, 768:1024] = qk(1024).astype(q_ref.dtype)
    kva = qk(512)
    kvb = qk(1280)
    outs = ((ka_ref, kva[:, :128]), (va_ref, proj[:, 640:768]), (kb_ref, kvb[:, :128]), (vb_ref, proj[:, 1408:1536]))
    for ref, val in outs:
        if transposed_kv:
            seq = ref.shape[2]
            for r in range(ref.shape[0]):
                ref[r] = val[r * seq:(r + 1) * seq].T.astype(ref.dtype)
        else:
            ref[...] = val.astype(ref.dtype)


def _qkv(x, mod, gain, w_bf, hgain, bd, cos_t, sin_t, *, rows_per_cond, seq, rope, kv_dtype, transposed_kv):
    n = x.shape[0]
    tiles_per_seq = max(1, seq // TM)
    row = lambda i: (i, 0)
    const = lambda i: (0, 0)
    pos = lambda i: (i % tiles_per_seq, 0)
    if transposed_kv:
        assert TM % seq == 0
        kv_shape = jax.ShapeDtypeStruct((n // seq, 128, seq), kv_dtype)
        kv_spec = pl.BlockSpec((TM // seq, 128, seq), lambda i: (i, 0, 0))
    else:
        kv_shape = jax.ShapeDtypeStruct((n, 128), kv_dtype)
        kv_spec = pl.BlockSpec((TM, 128), row)
    return pl.pallas_call(
        functools.partial(_qkv_kernel, rope=rope, transposed_kv=transposed_kv),
        grid=(n // TM,),
        in_specs=[pl.BlockSpec((TM, D_MODEL), row),
                  _mod_spec(rows_per_cond),
                  pl.BlockSpec((1, D_MODEL), const),
                  pl.BlockSpec((D_MODEL, QKV_COLS), const),
                  pl.BlockSpec((1, QKV_COLS), const),
                  pl.BlockSpec((256, 256), const),
                  pl.BlockSpec((TM, 256), pos),
                  pl.BlockSpec((TM, 256), pos)],
        out_specs=[pl.BlockSpec((TM, ATTN_OUT), row)] + [kv_spec] * 4,
        out_shape=[jax.ShapeDtypeStruct((n, ATTN_OUT), BF16)] + [kv_shape] * 4,
        compiler_params=_params("parallel"),
        name="qkv_rope" if rope else "qkv",
    )(x, mod, gain, w_bf, hgain, bd, cos_t, sin_t)


def _pad_variants(kk, ones=False):
    lane = lax.broadcasted_iota(I32, kk.shape, 1)
    left = lane < HEAD_DIM
    rolled = pltpu.roll(kk, HEAD_DIM, 1)
    fill_r = jnp.where(lane == HEAD_DIM, 1.0, 0.0) if ones else jnp.zeros_like(kk)
    fill_l = jnp.where(lane == 0, 1.0, 0.0) if ones else jnp.zeros_like(kk)
    return {(0, 0): jnp.where(left, kk, fill_r).astype(BF16),
            (0, 1): jnp.where(left, fill_l, rolled).astype(BF16),
            (1, 0): jnp.where(left, rolled, fill_r).astype(BF16),
            (1, 1): jnp.where(left, fill_l, kk).astype(BF16)}


def _pad_variants_t(kt):
    top = lax.broadcasted_iota(I32, kt.shape, 0) < HEAD_DIM
    zero = jnp.zeros((HEAD_DIM, kt.shape[1]), F32)
    return {(0, 0): jnp.where(top, kt, 0.0).astype(BF16),
            (0, 1): jnp.concatenate([zero, kt[:HEAD_DIM]], axis=0).astype(BF16),
            (1, 0): jnp.concatenate([kt[HEAD_DIM:], zero], axis=0).astype(BF16),
            (1, 1): jnp.where(top, 0.0, kt).astype(BF16)}


def _head_attention_small(qp, kblk, vblk, sink):
    s = _dot(qp, kblk, NN)
    m = s.max(axis=-1, keepdims=True)
    if sink is not None:
        m = jnp.maximum(m, sink)
    p = jnp.exp2(s - m)
    den = p.sum(axis=-1, keepdims=True)
    if sink is not None:
        den = den + jnp.exp2(sink - m)
    return _dot(p.astype(BF16), vblk, NT) / den


def _head_attention(qp, keys, vals, masks, sink, par):
    scores = []
    for kblk, mask in zip(keys, masks):
        s = _dot(qp, kblk, NT)
        if mask is not None:
            s = jnp.where(mask, s, NEG_INF)
        scores.append(s)
    m = scores[0].max(axis=-1, keepdims=True)
    for s in scores[1:]:
        m = jnp.maximum(m, s.max(axis=-1, keepdims=True))
    if sink is not None:
        m = jnp.maximum(m, sink)
    out = None
    for s, vblk in zip(scores, vals):
        o = _dot(jnp.exp2((s - m).astype(BF16)), vblk)
        out = o if out is None else out + o
    ones_lane = HEAD_DIM if par == 0 else 0
    den = out[:, ones_lane:ones_lane + 1]
    if sink is not None:
        den = den + jnp.exp2(sink - m)
    own = (lax.broadcasted_iota(I32, out.shape, 1) < HEAD_DIM) == (par == 0)
    return jnp.where(own, out / den, 0.0)


def _attn_ctx_kernel(sink_ref, q_ref, ka_ref, va_ref, kb_ref, vb_ref, o_ref):
    for mixer, (k_ref, v_ref) in enumerate(((ka_ref, va_ref), (kb_ref, vb_ref))):
        kvar = _pad_variants_t(k_ref[0])
        vvar = _pad_variants_t(v_ref[0])
        for t in range(4):
            tile = mixer * 4 + t
            kv = t // 2
            qp = q_ref[:, tile * 128:(tile + 1) * 128]
            acc = None
            for par in range(2):
                sink = sink_ref[2 * t + par] * LOG2E if mixer == 1 else None
                o = _head_attention_small(qp, kvar[(kv, par)], vvar[(kv, par)], sink)
                acc = o if acc is None else acc + o
            o_ref[:, tile * 128:(tile + 1) * 128] = acc.astype(o_ref.dtype)


def _attn_ctx(sink, q, ka, va, kb, vb, seq):
    n = q.shape[0]
    row = lambda b: (b, 0)
    kv_spec = pl.BlockSpec((1, 128, seq), lambda b: (b, 0, 0))
    return pl.pallas_call(
        _attn_ctx_kernel,
        grid=(n // seq,),
        in_specs=[pl.BlockSpec(memory_space=pltpu.SMEM),
                  pl.BlockSpec((seq, ATTN_OUT), row), kv_spec, kv_spec, kv_spec, kv_spec],
        out_specs=pl.BlockSpec((seq, ATTN_OUT), row),
        out_shape=jax.ShapeDtypeStruct((n, ATTN_OUT), BF16),
        compiler_params=_params("parallel"),
        name="attn_ctx",
    )(sink, q, ka, va, kb, vb)


def _attn_lat_kernel(sink_ref, q_ref, ka_ref, va_ref, kb_ref, vb_ref,
                     cka_ref, cva_ref, ckb_ref, cvb_ref, o_ref, *, tq, seq):
    qi = pl.program_id(1)
    ck = _pad_variants(cka_ref[0])
    cv = _pad_variants(cva_ref[0], ones=True)
    lk = _pad_variants(ka_ref[...].astype(F32))
    lv = _pad_variants(va_ref[...].astype(F32), ones=True)
    for t in range(4):
        kv = t // 2
        qp = q_ref[:, t * 128:(t + 1) * 128]
        acc = None
        for par in range(2):
            o = _head_attention(qp, [ck[(kv, par)], lk[(kv, par)]], [cv[(kv, par)], lv[(kv, par)]],
                                [None, None], None, par)
            acc = o if acc is None else acc + o
        o_ref[:, t * 128:(t + 1) * 128] = acc.astype(o_ref.dtype)
    span = BAND_TQ + 2 * WINDOW
    ck = _pad_variants(ckb_ref[0])
    cv = _pad_variants(cvb_ref[0], ones=True)
    for sub in range(tq // BAND_TQ):
        q0 = qi * tq + sub * BAND_TQ
        rows = slice(sub * BAND_TQ, (sub + 1) * BAND_TQ)
        lo = pl.multiple_of(jnp.clip(q0 - WINDOW, 0, seq - span), 128)
        qpos = q0 + lax.broadcasted_iota(I32, (BAND_TQ, span), 0)
        kpos = lo + lax.broadcasted_iota(I32, (BAND_TQ, span), 1)
        band = jnp.abs(qpos - kpos) <= WINDOW
        lk = _pad_variants(kb_ref[pl.ds(lo, span), :].astype(F32))
        lv = _pad_variants(vb_ref[pl.ds(lo, span), :].astype(F32), ones=True)
        for t in range(4):
            kv = t // 2
            tile = 4 + t
            qp = q_ref[rows, tile * 128:(tile + 1) * 128]
            acc = None
            for par in range(2):
                sink = sink_ref[2 * t + par] * LOG2E
                o = _head_attention(qp, [lk[(kv, par)], ck[(kv, par)]], [lv[(kv, par)], cv[(kv, par)]],
                                    [band, None], sink, par)
                acc = o if acc is None else acc + o
            o_ref[rows, tile * 128:(tile + 1) * 128] = acc.astype(o_ref.dtype)


def _attn_lat(sink, q, ka, va, kb, vb, cka, cva, ckb, cvb, seq, tq=512):
    n = q.shape[0]
    nb = n // seq
    nq = seq // tq
    qrow = lambda b, i: (b * nq + i, 0)
    brow = lambda b, i: (b, 0)
    kv_spec = pl.BlockSpec((seq, 128), brow)
    past = cka.shape[1]
    c_spec = pl.BlockSpec((1, past, 128), lambda b, i: (b, 0, 0))
    return pl.pallas_call(
        functools.partial(_attn_lat_kernel, tq=tq, seq=seq),
        grid=(nb, nq),
        in_specs=[pl.BlockSpec(memory_space=pltpu.SMEM),
                  pl.BlockSpec((tq, ATTN_OUT), qrow), kv_spec, kv_spec, kv_spec, kv_spec,
                  c_spec, c_spec, c_spec, c_spec],
        out_specs=pl.BlockSpec((tq, ATTN_OUT), qrow),
        out_shape=jax.ShapeDtypeStruct((n, ATTN_OUT), BF16),
        compiler_params=_params("parallel", "parallel"),
        name="attn_lat",
    )(sink, q, ka, va, kb, vb, cka, cva, ckb, cvb)


def _postmix_kernel(m_ref, x_ref, mod_ref, w_ref, gain_ref, rt_ref, x1_ref, h2_ref, aff_ref, *, project):
    if project:
        m = _dot(m_ref[...], w_ref[...])
    else:
        m = m_ref[...]
    x1 = x_ref[...] + mod_ref[0, 2:3, :] * m
    x1_ref[...] = x1
    h2 = _norm_mod(x1, gain_ref[...], mod_ref[0, 3:4, :], mod_ref[0, 4:5, :])
    h2_ref[...] = h2.astype(h2_ref.dtype)
    logits = _dot3(rt_ref[...], h2, NT)
    e = jnp.exp(logits - logits.max(axis=0, keepdims=True))
    aff_ref[...] = e / e.sum(axis=0, keepdims=True)


def _postmix(m, x, mod, w_bf, gain, router_t, *, rows_per_cond, project):
    n = x.shape[0]
    row = lambda i: (i, 0)
    const = lambda i: (0, 0)
    return pl.pallas_call(
        functools.partial(_postmix_kernel, project=project),
        grid=(n // TM,),
        in_specs=[pl.BlockSpec((TM, D_MODEL), row),
                  pl.BlockSpec((TM, D_MODEL), row),
                  _mod_spec(rows_per_cond),
                  pl.BlockSpec(w_bf.shape, const),
                  pl.BlockSpec((1, D_MODEL), const),
                  pl.BlockSpec((N_EXPERTS, D_MODEL), const)],
        out_specs=[pl.BlockSpec((TM, D_MODEL), row),
                   pl.BlockSpec((TM, D_MODEL), row),
                   pl.BlockSpec((N_EXPERTS, TM), lambda i: (0, i))],
        out_shape=[jax.ShapeDtypeStruct((n, D_MODEL), F32),
                   jax.ShapeDtypeStruct((n, D_MODEL), BF16),
                   jax.ShapeDtypeStruct((N_EXPERTS, n), F32)],
        compiler_params=_params("parallel"),
        name="postmix_proj" if project else "postmix",
    )(m, x, mod, w_bf, gain, router_t)


def _route_kernel(aff_ref, slot_ref, gate_ref, pos_ref, tcnt_ref, *, seq, cap, nseg, tt):
    aff = jnp.concatenate([aff_ref[:, s * seq:(s + 1) * seq] for s in range(nseg)], axis=0)
    rows = aff.shape[0]
    capf = jnp.float32(cap)
    thr_bits = jnp.zeros((rows, 1), I32)

    def enough(cand):
        cnt = jnp.where(aff >= pltpu.bitcast(cand, F32), 1.0, 0.0).sum(axis=1, keepdims=True)
        return cnt >= capf

    for bit in range(30, 0, -2):
        hi, lo = 1 << bit, 1 << (bit - 1)
        both, only_hi, only_lo = thr_bits | hi | lo, thr_bits | hi, thr_bits | lo
        thr_bits = jnp.where(enough(both), both,
                             jnp.where(enough(only_hi), only_hi, jnp.where(enough(only_lo), only_lo, thr_bits)))
    thr_bits = jnp.where(enough(thr_bits | 1), thr_bits | 1, thr_bits)
    thr = pltpu.bitcast(thr_bits, F32)
    gt = aff > thr
    eq = aff == thr
    n_gt = jnp.where(gt, 1.0, 0.0).sum(axis=1, keepdims=True)
    pw = min(seq, 256)
    tri = jnp.where(lax.broadcasted_iota(I32, (pw, pw), 0) < lax.broadcasted_iota(I32, (pw, pw), 1),
                    1.0, 0.0).astype(BF16)

    def count_before(flag):
        ones = jnp.where(flag, 1.0, 0.0)
        parts = []
        run = jnp.zeros((rows, 1), F32)
        for c0 in range(0, seq, pw):
            blk = ones[:, c0:c0 + pw]
            parts.append(_dot(blk.astype(BF16), tri) + run)
            run = run + blk.sum(axis=1, keepdims=True)
        return jnp.concatenate(parts, axis=1) if len(parts) > 1 else parts[0]

    sel = gt | (eq & (count_before(eq) < capf - n_gt))
    rank = count_before(sel)
    expert = lax.broadcasted_iota(I32, (rows, seq), 0) & (N_EXPERTS - 1)
    slot = jnp.where(sel, expert * cap + rank.astype(I32), -1)
    gate = jnp.where(sel, aff, 0.0)
    pos = jnp.where(sel, rank, -1.0)
    nt = seq // tt
    tile_of = jnp.where((lax.broadcasted_iota(I32, (seq, nt), 0) // tt) == lax.broadcasted_iota(I32, (seq, nt), 1),
                        1.0, 0.0).astype(BF16)
    tcnt = _dot(jnp.where(sel, 1.0, 0.0).astype(BF16), tile_of)
    for s in range(nseg):
        rows_s = slice(s * N_EXPERTS, (s + 1) * N_EXPERTS)
        slot_ref[:, s * seq:(s + 1) * seq] = slot[rows_s, :]
        gate_ref[:, s * seq:(s + 1) * seq] = gate[rows_s, :]
        pos_ref[:, s * seq:(s + 1) * seq] = pos[rows_s, :]
        tcnt_ref[0, :, s * nt:(s + 1) * nt] = tcnt[rows_s, :]


def _route(aff_t, seq, cap, nseg, tt):
    n = aff_t.shape[1]
    nt = seq // tt
    steps = n // (nseg * seq)
    spec = pl.BlockSpec((N_EXPERTS, nseg * seq), lambda i: (0, i))
    slot, gate, pos, tcnt = pl.pallas_call(
        functools.partial(_route_kernel, seq=seq, cap=cap, nseg=nseg, tt=tt),
        grid=(steps,),
        in_specs=[spec],
        out_specs=[spec, spec, spec, pl.BlockSpec((1, N_EXPERTS, nseg * nt), lambda i: (i, 0, 0))],
        out_shape=[jax.ShapeDtypeStruct((N_EXPERTS, n), I32), jax.ShapeDtypeStruct((N_EXPERTS, n), F32),
                   jax.ShapeDtypeStruct((N_EXPERTS, n), F32),
                   jax.ShapeDtypeStruct((steps, N_EXPERTS, nseg * nt), F32)],
        compiler_params=_params("parallel"),
        name="route",
    )(aff_t)
    return slot, gate, pos, tcnt.transpose(1, 0, 2).reshape(N_EXPERTS, n // tt)


def _dispatch_kernel(slot_ref, h_ref, x_ref, *, cap, seq):
    m = N_EXPERTS * cap
    slot_id = lax.broadcasted_iota(I32, (m, seq), 0)
    for r in range(h_ref.shape[0] // seq):
        slots = slot_ref[:, r * seq:(r + 1) * seq]
        owner = jnp.broadcast_to(slots[:, None, :], (N_EXPERTS, cap, seq)).reshape(m, seq)
        sel = jnp.where(owner == slot_id, 1.0, 0.0).astype(BF16)
        x_ref[r * m:(r + 1) * m, :] = _dot(sel, h_ref[r * seq:(r + 1) * seq, :]).astype(x_ref.dtype)


def _dispatch(slot, h, seq, cap):
    n = h.shape[0]
    nb = n // seq
    rb = MOE_RB if nb % MOE_RB == 0 else 1
    return pl.pallas_call(
        functools.partial(_dispatch_kernel, cap=cap, seq=seq),
        grid=(nb // rb,),
        in_specs=[pl.BlockSpec((N_EXPERTS, rb * seq), lambda b: (0, b)),
                  pl.BlockSpec((rb * seq, D_MODEL), lambda b: (b, 0))],
        out_specs=pl.BlockSpec((rb * N_EXPERTS * cap, D_MODEL), lambda b: (b, 0)),
        out_shape=jax.ShapeDtypeStruct((nb * N_EXPERTS * cap, D_MODEL), BF16),
        compiler_params=_params("parallel"),
        name="moe_dispatch",
    )(slot, h)


def _window(cum_ref, base, e, cap, k=0):
    lo = _align_down(cum_ref[base + e]) + k * MOE_W
    return lo, pl.multiple_of(jnp.minimum(lo, cap - MOE_W), BF16_ROWS)


def _align_down(rank):
    return rank & ~(BF16_ROWS - 1)


def _extra_windows(cum_ref, base, e):
    lo = _align_down(cum_ref[base + e])
    return (cum_ref[base + N_EXPERTS + e] - lo + (MOE_W - 1)) // MOE_W


def _dispatch_win_kernel(cum_ref, slot_ref, h_ref, x_ref, *, cap, nt):
    b, i = pl.program_id(0), pl.program_id(1)
    tt = h_ref.shape[0]
    base = (b * (nt + 1) + i) * N_EXPERTS
    h = h_ref[...]
    row = lax.broadcasted_iota(I32, (MOE_W, tt), 0)

    @pl.when(i == 0)
    def _():
        x_ref[...] = jnp.zeros_like(x_ref)

    def hits(e, lo, ws):
        srow = slot_ref[e:e + 1, :]
        return (srow == row + (e * cap + ws)) & (srow >= e * cap + lo)

    for grp in range(N_EXPERTS // MOE_EG):
        wins = [(e,) + _window(cum_ref, base, e, cap) for e in range(grp * MOE_EG, (grp + 1) * MOE_EG)]
        sel = jnp.concatenate([hits(e, lo, ws) for e, lo, ws in wins], axis=0)
        x = _dot(jnp.where(sel, 1.0, 0.0).astype(BF16), h)
        for q, (e, lo, ws) in enumerate(wins):
            dst = pl.ds(e * cap + ws, MOE_W)
            x_ref[dst, :] += x[q * MOE_W:(q + 1) * MOE_W].astype(x_ref.dtype)

    for e in range(N_EXPERTS):
        def extra(k, carry, e=e):
            lo, ws = _window(cum_ref, base, e, cap, k)
            x = _dot(jnp.where(hits(e, lo, ws), 1.0, 0.0).astype(BF16), h)
            x_ref[pl.ds(e * cap + ws, MOE_W), :] += x.astype(x_ref.dtype)
            return carry
        lax.fori_loop(1, _extra_windows(cum_ref, base, e), extra, 0)


def _dispatch_win(cum, slot, h, seq, cap):
    n = h.shape[0]
    nb, nt = n // seq, seq // MOE_TT
    return pl.pallas_call(
        functools.partial(_dispatch_win_kernel, cap=cap, nt=nt),
        grid_spec=pltpu.PrefetchScalarGridSpec(
            num_scalar_prefetch=1,
            grid=(nb, nt),
            in_specs=[pl.BlockSpec((N_EXPERTS, MOE_TT), lambda b, i, c: (0, b * nt + i)),
                      pl.BlockSpec((MOE_TT, D_MODEL), lambda b, i, c: (b * nt + i, 0))],
            out_specs=pl.BlockSpec((N_EXPERTS * cap, D_MODEL), lambda b, i, c: (b, 0))),
        out_shape=jax.ShapeDtypeStruct((nb * N_EXPERTS * cap, D_MODEL), BF16),
        compiler_params=_params("parallel", "arbitrary"),
        name="moe_dispatch_win",
    )(cum, slot, h)


FFN_TF = 512
FFN_RC = 512


def _ffn_kernel(xa_ref, xb_ref, wg_ref, wu_ref, wd_ref, ya_ref, yb_ref, acc_ref):
    j = pl.program_id(1)
    last = pl.num_programs(1) - 1
    ra = xa_ref.shape[0] * xa_ref.shape[2]

    def row_chunks(ref):
        nb, _, cap, d = ref.shape
        rc = min(FFN_RC, nb * cap)
        for r0 in range(0, nb * cap, rc):
            if cap >= rc:
                b, c0 = divmod(r0, cap)
                yield r0, rc, (slice(b, b + 1), 0, slice(c0, c0 + rc), slice(None)), (1, rc, d)
            else:
                yield r0, rc, (slice(r0 // cap, (r0 + rc) // cap), 0, slice(None), slice(None)), (rc // cap, cap, d)

    def sweep(first_tile, last_tile):
        wg = wg_ref[0].astype(BF16)
        wu = wu_ref[0].astype(BF16)
        wd = wd_ref[0].astype(BF16)
        d = wd.shape[1]
        for x_ref, y_ref, base in ((xa_ref, ya_ref, 0), (xb_ref, yb_ref, ra)):
            for r0, rc, idx, shape in row_chunks(x_ref):
                x = x_ref[idx].reshape(rc, d)
                g = _dot(x, wg)
                u = _dot(x, wu)
                y = _dot((g * _sigmoid(g) * u).astype(BF16), wd)
                rows = slice(base + r0, base + r0 + rc)
                if not first_tile:
                    y = acc_ref[rows, :] + y
                if last_tile:
                    y_ref[idx] = y.astype(y_ref.dtype).reshape(shape)
                else:
                    acc_ref[rows, :] = y

    if D_FF == FFN_TF:
        sweep(True, True)
    else:
        pl.when(j == 0)(lambda: sweep(True, False))
        pl.when((j > 0) & (j < last))(lambda: sweep(False, False))
        pl.when(j == last)(lambda: sweep(False, True))


def _ffn(xa, xb, w_gate, w_up, w_down):
    ba, _, ca, d = xa.shape
    bb, _, cb, _ = xb.shape
    nj = D_FF // FFN_TF
    xa_spec = pl.BlockSpec((ba, 1, ca, d), lambda e, j: (0, e, 0, 0))
    xb_spec = pl.BlockSpec((bb, 1, cb, d), lambda e, j: (0, e, 0, 0))
    return pl.pallas_call(
        _ffn_kernel,
        grid=(N_EXPERTS, nj),
        in_specs=[xa_spec, xb_spec,
                  pl.BlockSpec((1, d, FFN_TF), lambda e, j: (e, 0, j)),
                  pl.BlockSpec((1, d, FFN_TF), lambda e, j: (e, 0, j)),
                  pl.BlockSpec((1, FFN_TF, d), lambda e, j: (e, j, 0))],
        out_specs=[xa_spec, xb_spec],
        out_shape=[jax.ShapeDtypeStruct(xa.shape, BF16), jax.ShapeDtypeStruct(xb.shape, BF16)],
        scratch_shapes=[pltpu.VMEM((ba * ca + bb * cb, d), F32)],
        compiler_params=_params("parallel", "arbitrary"),
        name="moe_ffn",
    )(xa, xb, w_gate, w_up, w_down)


def _expand(vals_bf, first_expert, width, total):
    e_of_lane = first_expert + lax.broadcasted_iota(I32, (N_EXPERTS, total), 1) // width
    pick = jnp.where(lax.broadcasted_iota(I32, (N_EXPERTS, total), 0) == e_of_lane, 1.0, 0.0).astype(BF16)
    return _dot(vals_bf, pick, TN)


def _combine_kernel(pos_ref, gate_ref, y_ref, x_ref, mod_ref, o_ref, *, cap, seq):
    m = N_EXPERTS * cap
    rank = (lax.broadcasted_iota(I32, (seq, m), 1) % cap).astype(F32)
    for r in range(x_ref.shape[0] // seq):
        rows = slice(r * seq, (r + 1) * seq)
        pos = _expand(pos_ref[:, rows].astype(BF16), 0, cap, m)
        gate = _expand(gate_ref[:, rows].astype(BF16), 0, cap, m)
        w = jnp.where(pos == rank, gate, 0.0).astype(BF16)
        o_ref[rows, :] = x_ref[rows, :] + mod_ref[0, 5:6, :] * _dot(w, y_ref[r * m:(r + 1) * m, :])


def _combine(pos, gate, y, x, mod, *, seq, cap):
    n = x.shape[0]
    nb = n // seq
    rb = MOE_RB if nb % MOE_RB == 0 else 1
    row = lambda b: (b, 0)
    return pl.pallas_call(
        functools.partial(_combine_kernel, cap=cap, seq=seq),
        grid=(nb // rb,),
        in_specs=[pl.BlockSpec((N_EXPERTS, rb * seq), lambda b: (0, b)),
                  pl.BlockSpec((N_EXPERTS, rb * seq), lambda b: (0, b)),
                  pl.BlockSpec((rb * N_EXPERTS * cap, D_MODEL), row),
                  pl.BlockSpec((rb * seq, D_MODEL), row),
                  pl.BlockSpec((1, 8, D_MODEL), lambda b: (0, 0, 0))],
        out_specs=pl.BlockSpec((rb * seq, D_MODEL), row),
        out_shape=jax.ShapeDtypeStruct((n, D_MODEL), F32),
        compiler_params=_params("parallel"),
        name="moe_combine",
    )(pos, gate, y, x, mod)


def _combine_win_kernel(cum_ref, pos_ref, gate_ref, y_ref, x_ref, mod_ref, o_ref, acc_ref, *, cap, nt):
    b, i = pl.program_id(0), pl.program_id(1)
    tt = x_ref.shape[0]
    base = (b * (nt + 1) + i) * N_EXPERTS
    width = MOE_EG * MOE_W
    posb = pos_ref[...].astype(BF16)
    gateb = gate_ref[...].astype(BF16)
    lane = lax.broadcasted_iota(I32, (1, width), 1)
    offset = (lane % MOE_W).astype(F32)
    acc = jnp.zeros((tt, D_MODEL), F32)
    for grp in range(N_EXPERTS // MOE_EG):
        wins = [(e,) + _window(cum_ref, base, e, cap) for e in range(grp * MOE_EG, (grp + 1) * MOE_EG)]
        lo_l = jnp.zeros((1, width), F32)
        ws_l = jnp.zeros((1, width), F32)
        for q, (e, lo, ws) in enumerate(wins):
            mine = lane // MOE_W == q
            lo_l = jnp.where(mine, lo.astype(F32), lo_l)
            ws_l = jnp.where(mine, ws.astype(F32), ws_l)
        pos = _expand(posb, grp * MOE_EG, MOE_W, width)
        gate = _expand(gateb, grp * MOE_EG, MOE_W, width)
        w = jnp.where((pos - ws_l == offset) & (pos >= lo_l), gate, 0.0).astype(BF16)
        ywin = jnp.concatenate([y_ref[pl.ds(e * cap + ws, MOE_W), :] for e, lo, ws in wins], axis=0)
        acc = acc + _dot(w, ywin)
    acc_ref[...] = acc

    offset_rows = lax.broadcasted_iota(I32, (MOE_W, tt), 0).astype(F32)
    for e in range(N_EXPERTS):
        def extra(k, carry, e=e):
            lo, ws = _window(cum_ref, base, e, cap, k)
            pos = pos_ref[e:e + 1, :]
            gate = gate_ref[e:e + 1, :].astype(BF16).astype(F32)
            w_t = jnp.where((pos - ws.astype(F32) == offset_rows) & (pos >= lo.astype(F32)), gate, 0.0).astype(BF16)
            acc_ref[...] += _dot(w_t, y_ref[pl.ds(e * cap + ws, MOE_W), :], TN)
            return carry
        lax.fori_loop(1, _extra_windows(cum_ref, base, e), extra, 0)

    o_ref[...] = x_ref[...] + mod_ref[0, 5:6, :] * acc_ref[...]


def _combine_win(cum, pos, gate, y, x, mod, *, seq, cap):
    n = x.shape[0]
    nb, nt = n // seq, seq // MOE_TT
    row = lambda b, i, c: (b * nt + i, 0)
    return pl.pallas_call(
        functools.partial(_combine_win_kernel, cap=cap, nt=nt),
        grid_spec=pltpu.PrefetchScalarGridSpec(
            num_scalar_prefetch=1,
            grid=(nb, nt),
            in_specs=[pl.BlockSpec((N_EXPERTS, MOE_TT), lambda b, i, c: (0, b * nt + i)),
                      pl.BlockSpec((N_EXPERTS, MOE_TT), lambda b, i, c: (0, b * nt + i)),
                      pl.BlockSpec((N_EXPERTS * cap, D_MODEL), lambda b, i, c: (b, 0)),
                      pl.BlockSpec((MOE_TT, D_MODEL), row),
                      pl.BlockSpec((1, 8, D_MODEL), lambda b, i, c: (1 + b, 0, 0))],
            out_specs=pl.BlockSpec((MOE_TT, D_MODEL), row),
            scratch_shapes=[pltpu.VMEM((MOE_TT, D_MODEL), F32)]),
        out_shape=jax.ShapeDtypeStruct((n, D_MODEL), F32),
        compiler_params=_params("parallel", "parallel"),
        name="moe_combine_win",
    )(cum, pos, gate, y, x, mod)


def _moe_pair(hp, affp, x1p, hs, affs, x1s, mod, w_gate, w_up, w_down, seq_p, seq_s):
    n_p, n_s = hp.shape[0], hs.shape[0]
    nb_p, nb_s = n_p // seq_p, n_s // seq_s
    cap_p = EC_FACTOR * seq_p // N_EXPERTS
    cap_s = EC_FACTOR * seq_s // N_EXPERTS
    assert N_EXPERTS * cap_p <= 512 and cap_s >= MOE_W and cap_s % BF16_ROWS == 0 and seq_s % MOE_TT == 0
    slot_p, gate_p, pos_p, _ = _route(affp, seq_p, cap_p, nseg=min(8, nb_p), tt=seq_p)
    slot_s, gate_s, pos_s, tcnt = _route(affs, seq_s, cap_s, nseg=min(4, nb_s), tt=MOE_TT)
    nt = seq_s // MOE_TT
    counts = tcnt.T.reshape(nb_s, nt, N_EXPERTS).astype(I32)
    cum = jnp.concatenate([jnp.zeros((nb_s, 1, N_EXPERTS), I32), jnp.cumsum(counts, axis=1)], axis=1).reshape(-1)
    xp = _dispatch(slot_p, hp, seq_p, cap_p).reshape(nb_p, N_EXPERTS, cap_p, D_MODEL)
    xs = _dispatch_win(cum, slot_s, hs, seq_s, cap_s).reshape(nb_s, N_EXPERTS, cap_s, D_MODEL)
    ys, yp = _ffn(xs, xp, w_gate, w_up, w_down)
    outp = _combine(pos_p, gate_p, yp.reshape(-1, D_MODEL), x1p, mod, seq=seq_p, cap=cap_p)
    outs = _combine_win(cum, pos_s, gate_s, ys.reshape(-1, D_MODEL), x1s, mod, seq=seq_s, cap=cap_s)
    return outp, outs


def _ssm_in_kernel(x_ref, mod_ref, gain_ref, wt_ref, ut_ref, *, cols_per_cond):
    tc = x_ref.shape[1]
    if cols_per_cond is None:
        h = _norm_mod(x_ref[0], gain_ref[...], mod_ref[0, 0:1, :], mod_ref[0, 1:2, :]).astype(BF16)
    else:
        first = 1 + pl.program_id(1) * (tc // cols_per_cond)
        parts = []
        for s in range(tc // cols_per_cond):
            m = mod_ref[first + s]
            parts.append(_norm_mod(x_ref[0, s * cols_per_cond:(s + 1) * cols_per_cond, :], gain_ref[...],
                                   m[0:1, :], m[1:2, :]).astype(BF16))
        h = jnp.concatenate(parts, axis=0) if len(parts) > 1 else parts[0]
    ut_ref[0] = _dot(wt_ref[...], h, NT)


def _ssm_in(xperm, mod, gain, wt_bf, *, cols_per_cond, tc):
    l, bk, d = xperm.shape
    assert cols_per_cond is None or tc % cols_per_cond == 0
    return pl.pallas_call(
        functools.partial(_ssm_in_kernel, cols_per_cond=cols_per_cond),
        grid=(l, bk // tc),
        in_specs=[pl.BlockSpec((1, tc, d), lambda j, i: (j, i, 0)),
                  pl.BlockSpec(mod.shape, lambda j, i: (0, 0, 0)),
                  pl.BlockSpec((1, d), lambda j, i: (0, 0)),
                  pl.BlockSpec((d, d), lambda j, i: (0, 0))],
        out_specs=pl.BlockSpec((1, d, tc), lambda j, i: (j, 0, i)),
        out_shape=jax.ShapeDtypeStruct((l, d, bk), F32),
        compiler_params=_params("parallel", "parallel"),
        name="ssm_in",
    )(xperm, mod, gain, wt_bf)


def _ssm_core_kernel(utp_ref, uts_ref, lamp_ref, c_ref, bt_ref, dsk_ref, h0_ref, ytp_ref, yts_ref, fs_ref,
                     *, kp, ks, nbp, nbs):
    rows = SSM_ROWS
    p = SSM_STATE
    lc = SSM_CHUNK
    ri = lax.broadcasted_iota(I32, (rows, rows), 0)
    cj = lax.broadcasted_iota(I32, (rows, rows), 1)
    causal = ri // SSM_GROUP >= cj // SSM_GROUP
    anticausal = cj // SSM_GROUP >= ri // SSM_GROUP
    diag = ri == cj
    leftc = lax.broadcasted_iota(I32, (lc, 128), 1) < p
    leftg = lax.broadcasted_iota(I32, (SSM_GROUP, 128), 1) < p
    nrow = lax.broadcasted_iota(I32, (lc, 128), 0).astype(F32)
    eye = lax.broadcasted_iota(I32, (p, 128), 0) == lax.broadcasted_iota(I32, (p, 128), 1)

    def cmul(ar, ai, xr, xi):
        return ar * xr - ai * xi, ar * xi + ai * xr

    def expand_rows(t):
        return jnp.broadcast_to(t[:, None, :], (lc, SSM_GROUP, 128)).reshape(rows, 128)

    def tile_rows(t):
        return jnp.broadcast_to(t[None, :, :], (lc, SSM_GROUP, 128)).reshape(rows, 128)

    def to_col(row):
        return jnp.where(eye, jnp.broadcast_to(row, (p, 128)), 0.0).sum(axis=1, keepdims=True)

    def operands(gg, d):
        lp = lamp_ref[gg, d]
        lre, lim = lp[0:1], lp[1:2]
        dt = jnp.exp(lp[2:3])
        a, th = lre * dt, lim * dt
        ang = nrow * th
        cs, sn = jnp.cos(ang), jnp.sin(ang)
        ep, em = jnp.exp(nrow * a), jnp.exp(-(nrow * a))
        pr, pi = ep * cs, ep * sn
        nr, ni = em * cs, -(em * sn)
        l1r, l1i = pr[1:2], pi[1:2]
        lmr, lmi = pr[lc - 1:lc], pi[lc - 1:lc]
        llr, lli = cmul(lmr, lmi, l1r, l1i)
        den = lre * lre + lim * lim
        cr = ((l1r - 1.0) * lre + l1i * lim) / den
        ci = (l1i * lre - (l1r - 1.0) * lim) / den
        btr, bti = bt_ref[gg, d, 0], bt_ref[gg, d, 1]
        bbr, bbi = cr * btr - ci * bti, cr * bti + ci * btr
        u1 = tile_rows(jnp.where(leftg, bbr, bbi))
        u2 = tile_rows(jnp.where(leftg, bbi, bbr))
        c1 = tile_rows(c_ref[gg, d, 0])
        c2 = tile_rows(c_ref[gg, d, 1])

        def left_form(xr, xi):
            return (c1 * expand_rows(jnp.where(leftc, xr, -xi))
                    + c2 * expand_rows(jnp.where(leftc, -xi, -xr)))

        def right_form(xr, xi):
            return u1 * expand_rows(xr) + u2 * expand_rows(jnp.where(leftc, -xi, xi))

        if d == 0:
            al = left_form(pr, pi)
            brt = right_form(nr, ni)
            rrt = right_form(*cmul(lmr, lmi, nr, ni))
            qq = left_form(*cmul(l1r, l1i, pr, pi))
            mat = jnp.where(causal, _dot3(al, brt, NT), 0.0)
        else:
            al = left_form(nr, ni)
            brt = right_form(pr, pi)
            rrt = brt
            qq = left_form(*cmul(llr, lli, nr, ni))
            mat = jnp.where(anticausal, _dot3(al, brt, NT), 0.0)
        return mat, rrt.T, qq, to_col(llr), to_col(lli)

    def scan(sr, si, lr, li, h0r, h0i, reverse, nchunk, nbatch, sel):
        bk = sr.shape[1]
        lane = lax.broadcasted_iota(I32, (p, bk), 1)
        kidx = lane & (nchunk - 1)
        edge = (nchunk - 1) if reverse else 0
        if h0r is not None:
            h0cr = jnp.zeros((p, bk), F32)
            h0ci = jnp.zeros((p, bk), F32)
            for b in range(nbatch):
                at = lane == (b * nchunk + edge)
                h0cr = jnp.where(at, h0r[:, b:b + 1], h0cr)
                h0ci = jnp.where(at, h0i[:, b:b + 1], h0ci)
            ar, ai = cmul(lr, li, h0cr, h0ci)
            er, ei = sr + ar, si + ai
        else:
            er, ei = sr, si
        ar, ai = lr, li
        s = 1
        while s < nchunk:
            if reverse:
                ok = kidx < nchunk - s
                tr, ti = pltpu.roll(er, bk - s, 1), pltpu.roll(ei, bk - s, 1)
            else:
                ok = kidx >= s
                tr, ti = pltpu.roll(er, s, 1), pltpu.roll(ei, s, 1)
            tr = jnp.where(ok, tr, 0.0)
            ti = jnp.where(ok, ti, 0.0)
            dr, di = cmul(ar, ai, tr, ti)
            er, ei = er + dr, ei + di
            ar, ai = cmul(ar, ai, ar, ai)
            s *= 2
        if reverse:
            inner = kidx < nchunk - 1
            hr, hi = pltpu.roll(er, bk - 1, 1), pltpu.roll(ei, bk - 1, 1)
        else:
            inner = kidx >= 1
            hr, hi = pltpu.roll(er, 1, 1), pltpu.roll(ei, 1, 1)
        hr = jnp.where(inner, hr, h0cr if h0r is not None else 0.0)
        hi = jnp.where(inner, hi, h0ci if h0r is not None else 0.0)
        fin = None if sel is None else (_sel_dot_t(sel, er), _sel_dot_t(sel, ei))
        return hr, hi, fin

    def final_selectors(bk, nchunk, nbatch):
        bat = lax.broadcasted_iota(I32, (nbatch, bk), 0)
        col = lax.broadcasted_iota(I32, (nbatch, bk), 1)
        last = jnp.where(col == bat * nchunk + (nchunk - 1), 1.0, 0.0).astype(BF16)
        first = jnp.where(col == bat * nchunk, 1.0, 0.0).astype(BF16)
        return last, first

    sel_last, sel_first = final_selectors(utp_ref.shape[2], kp, nbp)

    for gg in range(SSM_GB):
        mf, rf, qf, lfr, lfi = operands(gg, 0)
        mb, rb, qb, lbr, lbi = operands(gg, 1)
        skip = jnp.where(diag, jnp.broadcast_to(dsk_ref[gg], (rows, rows)), 0.0)
        stack = jnp.concatenate([mf + mb + skip, rf, rb], axis=0)
        qq = jnp.concatenate([qf, qb], axis=1)
        h0 = h0_ref[gg]
        for ut_ref, yt_ref, nchunk, nbatch, latent in ((utp_ref, ytp_ref, kp, nbp, False),
                                                       (uts_ref, yts_ref, ks, nbs, True)):
            bk = ut_ref.shape[2]
            x = ut_ref[:, gg * SSM_GROUP:(gg + 1) * SSM_GROUP, :].reshape(rows, bk)
            res = _mm(stack, x, SSM_PASSES)
            hfr, hfi, ff = scan(res[rows:rows + p], res[rows + p:rows + 2 * p], lfr, lfi,
                                h0[0] if latent else None, h0[1] if latent else None,
                                False, nchunk, nbatch, None if latent else sel_last)
            hbr, hbi, fb = scan(res[rows + 2 * p:rows + 3 * p], res[rows + 3 * p:rows + 4 * p], lbr, lbi,
                                h0[2] if latent else None, h0[3] if latent else None,
                                True, nchunk, nbatch, None if latent else sel_first)
            states = jnp.concatenate([hfr, hfi, hbr, hbi], axis=0)
            y = res[:rows] + _mm(qq, states, SSM_PASSES)
            yt_ref[:, gg * SSM_GROUP:(gg + 1) * SSM_GROUP, :] = y.reshape(lc, SSM_GROUP, bk)
            if not latent:
                fs_ref[gg, 0] = ff[0]
                fs_ref[gg, 1] = ff[1]
                fs_ref[gg, 2] = fb[0]
                fs_ref[gg, 3] = fb[1]


def _ssm_core(utp, uts, ops, h0, *, kp, ks, nbp, nbs):
    lamp, c2, bt2, dsk = ops
    l, d, bkp = utp.shape
    bks = uts.shape[2]
    g = SSM_GROUPS
    gb = SSM_GB
    lead4 = lambda i: (i, 0, 0, 0)
    lead5 = lambda i: (i, 0, 0, 0, 0)
    ut_spec = lambda bk: pl.BlockSpec((l, gb * SSM_GROUP, bk), lambda i: (0, i, 0))
    return pl.pallas_call(
        functools.partial(_ssm_core_kernel, kp=kp, ks=ks, nbp=nbp, nbs=nbs),
        grid=(g // gb,),
        in_specs=[ut_spec(bkp), ut_spec(bks),
                  pl.BlockSpec((gb, 2, 8, 128), lead4),
                  pl.BlockSpec((gb, 2, 2, SSM_GROUP, 128), lead5),
                  pl.BlockSpec((gb, 2, 2, SSM_GROUP, 128), lead5),
                  pl.BlockSpec((gb, 1, SSM_ROWS), lambda i: (i, 0, 0)),
                  pl.BlockSpec((gb, 4, SSM_STATE, nbs), lead4)],
        out_specs=[ut_spec(bkp), ut_spec(bks),
                   pl.BlockSpec((gb, 4, nbp, SSM_STATE), lead4)],
        out_shape=[jax.ShapeDtypeStruct((l, d, bkp), F32),
                   jax.ShapeDtypeStruct((l, d, bks), F32),
                   jax.ShapeDtypeStruct((g, 4, nbp, SSM_STATE), F32)],
        compiler_params=_params("parallel"),
        name="ssm_core",
    )(utp, uts, lamp, c2, bt2, dsk, h0)


def _ssm_out_kernel(yt_ref, w_ref, m_ref):
    y = yt_ref[0].T
    act = 0.5 * y * (1.0 + jnp.tanh(0.7978845608028654 * (y + 0.044715 * (y * y * y))))
    ag = _dot(act.astype(BF16), w_ref[...])
    d = m_ref.shape[2]
    m_ref[0] = ag[:, :d] * _sigmoid(ag[:, d:])


def _ssm_out(yt, w_bf, tc):
    l, d, bk = yt.shape
    return pl.pallas_call(
        _ssm_out_kernel,
        grid=(l, bk // tc),
        in_specs=[pl.BlockSpec((1, d, tc), lambda j, i: (j, 0, i)),
                  pl.BlockSpec((d, 2 * d), lambda j, i: (0, 0))],
        out_specs=pl.BlockSpec((1, tc, d), lambda j, i: (j, i, 0)),
        out_shape=jax.ShapeDtypeStruct((l, bk, d), F32),
        compiler_params=_params("parallel", "parallel"),
        name="ssm_out",
    )(yt, w_bf)


def _ssm_operand_params(lam_re, lam_im, b_re, b_im, c_re, c_im, log_dt, d_skip):
    g, l = SSM_GROUPS, SSM_CHUNK
    dup = lambda t: jnp.concatenate([t, t], axis=-1)
    lamp = jnp.stack([lam_re, lam_im, jnp.broadcast_to(log_dt[..., None], lam_re.shape)], axis=2)
    lamp = dup(jnp.pad(lamp, ((0, 0), (0, 0), (0, 5), (0, 0)))).transpose(1, 0, 2, 3)
    c2 = dup(jnp.stack([c_re, c_im], axis=2)).transpose(1, 0, 2, 3, 4)
    bt2 = dup(jnp.stack([jnp.swapaxes(b_re, -1, -2), jnp.swapaxes(b_im, -1, -2)], axis=2)).transpose(1, 0, 2, 3, 4)
    dsk = jnp.tile(d_skip.reshape(g, 1, SSM_GROUP), (1, 1, l))
    return lamp, c2, bt2, dsk


def _to_chunks(x, nb, seq):
    k = seq // SSM_CHUNK
    return x.reshape(nb, k, SSM_CHUNK, -1).transpose(2, 0, 1, 3).reshape(SSM_CHUNK, nb * k, -1)


def _from_chunks(x, nb, seq):
    k = seq // SSM_CHUNK
    return x.reshape(SSM_CHUNK, nb, k, -1).transpose(1, 2, 0, 3).reshape(nb * seq, -1)


def _ssm_mixers(xp, xs, mod, gain, wt_bf, ops, w_out_bf, h0, *, nbp, sp, nbs, ss):
    kp, ks = sp // SSM_CHUNK, ss // SSM_CHUNK
    tcp, tcs = min(SSM_TC, nbp * kp), min(SSM_TC, nbs * ks)
    utp = _ssm_in(_to_chunks(xp, nbp, sp), mod, gain, wt_bf, cols_per_cond=None, tc=tcp)
    uts = _ssm_in(_to_chunks(xs, nbs, ss), mod, gain, wt_bf, cols_per_cond=ks, tc=tcs)
    ytp, yts, fs = _ssm_core(utp, uts, ops, h0, kp=kp, ks=ks, nbp=nbp, nbs=nbs)
    mp = _from_chunks(_ssm_out(ytp, w_out_bf, tcp), nbp, sp)
    ms = _from_chunks(_ssm_out(yts, w_out_bf, tcs), nbs, ss)
    return mp, ms, fs


def _rope_tables(seq):
    t = jnp.arange(seq)
    row = (t // GRID_W).astype(F32)
    col = (t % GRID_W).astype(F32)
    n_freq = HEAD_DIM // 4
    inv_freq = ROPE_THETA ** (-jnp.arange(n_freq, dtype=F32) / n_freq)
    ang = jnp.concatenate([row[:, None] * inv_freq, col[:, None] * inv_freq], axis=-1)
    cos = jnp.repeat(jnp.cos(ang), 2, axis=-1)
    sin = jnp.repeat(jnp.sin(ang), 2, axis=-1)
    sign = jnp.tile(jnp.array([-1.0, 1.0], F32), HEAD_DIM // 2)
    return jnp.tile(cos, (1, 4)), jnp.tile(sin * sign, (1, 4))


def _head_gains(qn_a, kn_a, qn_b, kn_b):
    scale = HEAD_DIM ** -0.5 * LOG2E
    ones = jnp.ones((N_KV * HEAD_DIM,), F32)
    return jnp.concatenate([jnp.tile(qn_a, N_HEADS) * scale, jnp.tile(kn_a, N_KV), ones,
                            jnp.tile(qn_b, N_HEADS) * scale, jnp.tile(kn_b, N_KV), ones]).reshape(1, QKV_COLS)


def kernel(x_prompt, x_sample, c, cache_k_a_l0, cache_v_a_l0, cache_k_b_l0, cache_v_b_l0, state_ssm_re_l1, state_ssm_im_l1, c_ctx, mod_w_l0, mod_b_l0, norm_mix_l0, attn_w_in_l0, q_norm_a_l0, k_norm_a_l0, q_norm_b_l0, k_norm_b_l0, sink_b_l0, attn_w_out_l0, norm_ffn_l0, router_l0, moe_w_gate_l0, moe_w_up_l0, moe_w_down_l0, mod_w_l1, mod_b_l1, norm_mix_l1, ssm_w_in_l1, ssm_lambda_re_l1, ssm_lambda_im_l1, ssm_b_re_l1, ssm_b_im_l1, ssm_c_re_l1, ssm_c_im_l1, ssm_log_dt_l1, ssm_d_l1, ssm_w_out_l1, norm_ffn_l1, router_l1, moe_w_gate_l1, moe_w_up_l1, moe_w_down_l1):
    bp, sp, d = x_prompt.shape
    bs, ss, _ = x_sample.shape
    past = cache_k_a_l0.shape[1]
    assert d == D_MODEL and bs <= 7 and (bp * sp) % TM == 0 and TM % sp == 0 and ss % TM == 0
    xp = x_prompt.reshape(bp * sp, d)
    xs = x_sample.reshape(bs * ss, d)
    cond8 = jnp.concatenate([c_ctx[None], c, jnp.zeros((7 - bs, d), F32)], axis=0)
    row1 = lambda v: v.reshape(1, -1)

    mod0 = _mod_rows(cond8, mod_w_l0, mod_b_l0)
    w_in = attn_w_in_l0.astype(BF16)
    hgain = _head_gains(q_norm_a_l0, k_norm_a_l0, q_norm_b_l0, k_norm_b_l0)
    lane = np.arange(256)
    bd = jnp.asarray((lane[:, None] // HEAD_DIM == lane[None, :] // HEAD_DIM) / HEAD_DIM, BF16)
    cos_t, sin_t = _rope_tables(ss)
    qp, kap, vap, kbp, vbp = _qkv(xp, mod0, row1(norm_mix_l0), w_in, hgain, bd, cos_t, sin_t,
                                  rows_per_cond=None, seq=sp, rope=False, kv_dtype=F32, transposed_kv=True)
    qs, kas, vas, kbs, vbs = _qkv(xs, mod0, row1(norm_mix_l0), w_in, hgain, bd, cos_t, sin_t,
                                  rows_per_cond=ss, seq=ss, rope=True, kv_dtype=BF16, transposed_kv=False)
    op = _attn_ctx(sink_b_l0, qp, kap, vap, kbp, vbp, sp)
    cache = lambda t: t.reshape(bs, past, N_KV * HEAD_DIM)
    os_ = _attn_lat(sink_b_l0, qs, kas, vas, kbs, vbs, cache(cache_k_a_l0), cache(cache_v_a_l0),
                    cache(cache_k_b_l0), cache(cache_v_b_l0), ss)
    w_out = attn_w_out_l0.astype(BF16)
    x1p, hp, affp = _postmix(op, xp, mod0, w_out, row1(norm_ffn_l0), router_l0.T, rows_per_cond=None, project=True)
    x1s, hs, affs = _postmix(os_, xs, mod0, w_out, row1(norm_ffn_l0), router_l0.T, rows_per_cond=ss, project=True)
    xp, xs = _moe_pair(hp, affp, x1p, hs, affs, x1s, mod0, moe_w_gate_l0, moe_w_up_l0, moe_w_down_l0, sp, ss)

    mod1 = _mod_rows(cond8, mod_w_l1, mod_b_l1)
    ops = _ssm_operand_params(ssm_lambda_re_l1, ssm_lambda_im_l1, ssm_b_re_l1, ssm_b_im_l1, ssm_c_re_l1, ssm_c_im_l1,
                              ssm_log_dt_l1, ssm_d_l1)
    wt = ssm_w_in_l1.T.astype(BF16)
    w_so = ssm_w_out_l1.astype(BF16)
    h0 = jnp.stack([state_ssm_re_l1[:, 0], state_ssm_im_l1[:, 0], state_ssm_re_l1[:, 1], state_ssm_im_l1[:, 1]],
                   axis=0).transpose(2, 0, 3, 1)
    mp, ms, fsp = _ssm_mixers(xp, xs, mod1, row1(norm_mix_l1), wt, ops, w_so, h0, nbp=bp, sp=sp, nbs=bs, ss=ss)
    dummy_w = jnp.zeros((8, 128), BF16)
    x1p, hp, affp = _postmix(mp, xp, mod1, dummy_w, row1(norm_ffn_l1), router_l1.T, rows_per_cond=None, project=False)
    x1s, hs, affs = _postmix(ms, xs, mod1, dummy_w, row1(norm_ffn_l1), router_l1.T, rows_per_cond=ss, project=False)
    xp, xs = _moe_pair(hp, affp, x1p, hs, affs, x1s, mod1, moe_w_gate_l1, moe_w_up_l1, moe_w_down_l1, sp, ss)

    kv_out = lambda t: t.reshape(bp, N_KV, HEAD_DIM, sp).transpose(0, 3, 1, 2)
    fin = fsp.transpose(2, 1, 0, 3)
    ssm_re = jnp.stack([fin[:, 0], fin[:, 2]], axis=1)
    ssm_im = jnp.stack([fin[:, 1], fin[:, 3]], axis=1)
    return (xp.reshape(bp, sp, d), xs.reshape(bs, ss, d), kv_out(kap), kv_out(vap), kv_out(kbp), kv_out(vbp),
            ssm_re, ssm_im)
```

```python
import functools

import jax
import jax.numpy as jnp
import numpy as np
from jax import lax
from jax.experimental import pallas as pl
from jax.experimental.pallas import tpu as pltpu

F32, BF16, I32 = jnp.float32, jnp.bfloat16, jnp.int32

D_MODEL = 1024
GRID_W = 64
HEAD_DIM = 64
N_HEADS = 8
N_KV = 2
WINDOW = 128
ROPE_THETA = 10000.0
SSM_GROUP = 16
SSM_GROUPS = D_MODEL // SSM_GROUP
SSM_STATE = 64
N_EXPERTS = 16
EC_FACTOR = 2
D_FF = 2 * D_MODEL
EPS = 1e-6
NEG_INF = -1e30
LOG2E = 1.4426950408889634
QKV_COLS = 2 * (N_HEADS + 2 * N_KV) * HEAD_DIM
ATTN_OUT = 2 * N_HEADS * HEAD_DIM

SSM_CHUNK = 16
SSM_ROWS = SSM_CHUNK * SSM_GROUP
SSM_GB = 4
SSM_PASSES = 1

SSM_TC = 512

TM = 1024

BAND_TQ = 256

MOE_TT = 256
BF16_ROWS = 16
MOE_W = 64
MOE_EG = 4
MOE_RB = 4
VMEM_LIMIT = 56 * 1024 * 1024

NN = (((1,), (0,)), ((), ()))
NT = (((1,), (1,)), ((), ()))
TN = (((0,), (0,)), ((), ()))


def _dot(a, b, dims=NN):
    return lax.dot_general(a, b, dims, preferred_element_type=F32)


def _split2(x):
    hi = x.astype(BF16)
    lo = (x - hi.astype(F32)).astype(BF16)
    return hi, lo


def _split3(x):
    hi = x.astype(BF16)
    r = x - hi.astype(F32)
    mid = r.astype(BF16)
    lo = (r - mid.astype(F32)).astype(BF16)
    return hi, mid, lo


def _dot3(a, b, dims=NN):
    ah, al = _split2(a)
    bh, bl = _split2(b)
    return _dot(ah, bh, dims) + (_dot(ah, bl, dims) + _dot(al, bh, dims))


def _mm(a, b, passes):
    if passes == 1:
        return _dot(a.astype(BF16), b.astype(BF16))
    return _dot3(a, b)


def _sel_dot_t(sel, x):
    hi, mid, lo = _split3(x)
    return _dot(sel, hi, NT) + (_dot(sel, mid, NT) + _dot(sel, lo, NT))


def _sigmoid(x):
    return 1.0 / (1.0 + jnp.exp(-x))


def _norm_mod(x, gain, shift, scale):
    ms = jnp.mean(x * x, axis=-1, keepdims=True)
    y = x * lax.rsqrt(ms + EPS) * gain
    return y * (1.0 + scale) + shift


def _params(*sem):
    return pltpu.CompilerParams(dimension_semantics=sem, vmem_limit_bytes=VMEM_LIMIT)


def _adaln_kernel(c_ref, w_ref, b_ref, o_ref):
    c = c_ref[...]
    s = c * _sigmoid(c)
    o_ref[...] = _dot3(s, w_ref[...]) + b_ref[...]


def _adaln(cond8, w_mod, b_mod):
    d, e = w_mod.shape
    tn = 1536
    return pl.pallas_call(
        _adaln_kernel,
        grid=(e // tn,),
        in_specs=[pl.BlockSpec((8, d), lambda j: (0, 0)),
                  pl.BlockSpec((d, tn), lambda j: (0, j)),
                  pl.BlockSpec((1, tn), lambda j: (0, j))],
        out_specs=pl.BlockSpec((8, tn), lambda j: (0, j)),
        out_shape=jax.ShapeDtypeStruct((8, e), F32),
        compiler_params=_params("parallel"),
        name="adaln",
    )(cond8, w_mod, b_mod.reshape(1, e))


def _mod_rows(cond8, w_mod, b_mod):
    m = _adaln(cond8, w_mod, b_mod).reshape(8, 6, D_MODEL)
    return jnp.pad(m, ((0, 0), (0, 2), (0, 0)))


def _mod_spec(rows_per_cond):
    if rows_per_cond is None:
        return pl.BlockSpec((1, 8, D_MODEL), lambda i: (0, 0, 0))
    return pl.BlockSpec((1, 8, D_MODEL), lambda i: (1 + (i * TM) // rows_per_cond, 0, 0))


def _qkv_kernel(x_ref, mod_ref, gain_ref, w_ref, hg_ref, bd_ref, cos_ref, sin_ref,
                q_ref, ka_ref, va_ref, kb_ref, vb_ref, *, rope, transposed_kv):
    h = _norm_mod(x_ref[...], gain_ref[...], mod_ref[0, 0:1, :], mod_ref[0, 1:2, :])
    proj = _dot(h.astype(BF16), w_ref[...])
    bd = bd_ref[...]

    def head_norm(blk, g):
        ms = _dot((blk * blk).astype(BF16), bd)
        return blk * lax.rsqrt(ms + EPS) * g

    def rotary(blk):
        w = blk.shape[1]
        even = (lax.broadcasted_iota(I32, blk.shape, 1) & 1) == 0
        swapped = jnp.where(even, pltpu.roll(blk, w - 1, 1), pltpu.roll(blk, 1, 1))
        return blk * cos_ref[:, :w] + swapped * sin_ref[:, :w]

    def qk(c0):
        blk = head_norm(proj[:, c0:c0 + 256], hg_ref[:, c0:c0 + 256])
        return rotary(blk) if rope else blk

    q_ref[:, 0:256] = qk(0).astype(q_ref.dtype)
    q_ref[:, 256:512] = qk(256).astype(q_ref.dtype)
    q_ref[:, 512:768] = qk(768).astype(q_ref.dtype)
    q_ref[:, 768:1024] = qk(1024).astype(q_ref.dtype)
    kva = qk(512)
    kvb = qk(1280)
    outs = ((ka_ref, kva[:, :128]), (va_ref, proj[:, 640:768]), (kb_ref, kvb[:, :128]), (vb_ref, proj[:, 1408:1536]))
    for ref, val in outs:
        if transposed_kv:
            seq = ref.shape[2]
            for r in range(ref.shape[0]):
                ref[r] = val[r * seq:(r + 1) * seq].T.astype(ref.dtype)
        else:
            ref[...] = val.astype(ref.dtype)


def _qkv(x, mod, gain, w_bf, hgain, bd, cos_t, sin_t, *, rows_per_cond, seq, rope, kv_dtype, transposed_kv):
    n = x.shape[0]
    tiles_per_seq = max(1, seq // TM)
    row = lambda i: (i, 0)
    const = lambda i: (0, 0)
    pos = lambda i: (i % tiles_per_seq, 0)
    if transposed_kv:
        assert TM % seq == 0
        kv_shape = jax.ShapeDtypeStruct((n // seq, 128, seq), kv_dtype)
        kv_spec = pl.BlockSpec((TM // seq, 128, seq), lambda i: (i, 0, 0))
    else:
        kv_shape = jax.ShapeDtypeStruct((n, 128), kv_dtype)
        kv_spec = pl.BlockSpec((TM, 128), row)
    return pl.pallas_call(
        functools.partial(_qkv_kernel, rope=rope, transposed_kv=transposed_kv),
        grid=(n // TM,),
        in_specs=[pl.BlockSpec((TM, D_MODEL), row),
                  _mod_spec(rows_per_cond),
                  pl.BlockSpec((1, D_MODEL), const),
                  pl.BlockSpec((D_MODEL, QKV_COLS), const),
                  pl.BlockSpec((1, QKV_COLS), const),
                  pl.BlockSpec((256, 256), const),
                  pl.BlockSpec((TM, 256), pos),
                  pl.BlockSpec((TM, 256), pos)],
        out_specs=[pl.BlockSpec((TM, ATTN_OUT), row)] + [kv_spec] * 4,
        out_shape=[jax.ShapeDtypeStruct((n, ATTN_OUT), BF16)] + [kv_shape] * 4,
        compiler_params=_params("parallel"),
        name="qkv_rope" if rope else "qkv",
    )(x, mod, gain, w_bf, hgain, bd, cos_t, sin_t)


def _pad_variants(kk, ones=False):
    lane = lax.broadcasted_iota(I32, kk.shape, 1)
    left = lane < HEAD_DIM
    rolled = pltpu.roll(kk, HEAD_DIM, 1)
    fill_r = jnp.where(lane == HEAD_DIM, 1.0, 0.0) if ones else jnp.zeros_like(kk)
    fill_l = jnp.where(lane == 0, 1.0, 0.0) if ones else jnp.zeros_like(kk)
    return {(0, 0): jnp.where(left, kk, fill_r).astype(BF16),
            (0, 1): jnp.where(left, fill_l, rolled).astype(BF16),
            (1, 0): jnp.where(left, rolled, fill_r).astype(BF16),
            (1, 1): jnp.where(left, fill_l, kk).astype(BF16)}


def _pad_variants_t(kt):
    top = lax.broadcasted_iota(I32, kt.shape, 0) < HEAD_DIM
    zero = jnp.zeros((HEAD_DIM, kt.shape[1]), F32)
    return {(0, 0): jnp.where(top, kt, 0.0).astype(BF16),
            (0, 1): jnp.concatenate([zero, kt[:HEAD_DIM]], axis=0).astype(BF16),
            (1, 0): jnp.concatenate([kt[HEAD_DIM:], zero], axis=0).astype(BF16),
            (1, 1): jnp.where(top, 0.0, kt).astype(BF16)}


def _head_attention_small(qp, kblk, vblk, sink):
    s = _dot(qp, kblk, NN)
    m = s.max(axis=-1, keepdims=True)
    if sink is not None:
        m = jnp.maximum(m, sink)
    p = jnp.exp2(s - m)
    den = p.sum(axis=-1, keepdims=True)
    if sink is not None:
        den = den + jnp.exp2(sink - m)
    return _dot(p.astype(BF16), vblk, NT) / den


def _head_attention(qp, keys, vals, masks, sink, par):
    scores = []
    for kblk, mask in zip(keys, masks):
        s = _dot(qp, kblk, NT)
        if mask is not None:
            s = jnp.where(mask, s, NEG_INF)
        scores.append(s)
    m = scores[0].max(axis=-1, keepdims=True)
    for s in scores[1:]:
        m = jnp.maximum(m, s.max(axis=-1, keepdims=True))
    if sink is not None:
        m = jnp.maximum(m, sink)
    out = None
    for s, vblk in zip(scores, vals):
        o = _dot(jnp.exp2((s - m).astype(BF16)), vblk)
        out = o if out is None else out + o
    ones_lane = HEAD_DIM if par == 0 else 0
    den = out[:, ones_lane:ones_lane + 1]
    if sink is not None:
        den = den + jnp.exp2(sink - m)
    own = (lax.broadcasted_iota(I32, out.shape, 1) < HEAD_DIM) == (par == 0)
    return jnp.where(own, out / den, 0.0)


def _attn_ctx_kernel(sink_ref, q_ref, ka_ref, va_ref, kb_ref, vb_ref, o_ref):
    for mixer, (k_ref, v_ref) in enumerate(((ka_ref, va_ref), (kb_ref, vb_ref))):
        kvar = _pad_variants_t(k_ref[0])
        vvar = _pad_variants_t(v_ref[0])
        for t in range(4):
            tile = mixer * 4 + t
            kv = t // 2
            qp = q_ref[:, tile * 128:(tile + 1) * 128]
            acc = None
            for par in range(2):
                sink = sink_ref[2 * t + par] * LOG2E if mixer == 1 else None
                o = _head_attention_small(qp, kvar[(kv, par)], vvar[(kv, par)], sink)
                acc = o if acc is None else acc + o
            o_ref[:, tile * 128:(tile + 1) * 128] = acc.astype(o_ref.dtype)


def _attn_ctx(sink, q, ka, va, kb, vb, seq):
    n = q.shape[0]
    row = lambda b: (b, 0)
    kv_spec = pl.BlockSpec((1, 128, seq), lambda b: (b, 0, 0))
    return pl.pallas_call(
        _attn_ctx_kernel,
        grid=(n // seq,),
        in_specs=[pl.BlockSpec(memory_space=pltpu.SMEM),
                  pl.BlockSpec((seq, ATTN_OUT), row), kv_spec, kv_spec, kv_spec, kv_spec],
        out_specs=pl.BlockSpec((seq, ATTN_OUT), row),
        out_shape=jax.ShapeDtypeStruct((n, ATTN_OUT), BF16),
        compiler_params=_params("parallel"),
        name="attn_ctx",
    )(sink, q, ka, va, kb, vb)


def _attn_lat_kernel(sink_ref, q_ref, ka_ref, va_ref, kb_ref, vb_ref,
                     cka_ref, cva_ref, ckb_ref, cvb_ref, o_ref, *, tq, seq):
    qi = pl.program_id(1)
    ck = _pad_variants(cka_ref[0])
    cv = _pad_variants(cva_ref[0], ones=True)
    lk = _pad_variants(ka_ref[...].astype(F32))
    lv = _pad_variants(va_ref[...].astype(F32), ones=True)
    for t in range(4):
        kv = t // 2
        qp = q_ref[:, t * 128:(t + 1) * 128]
        acc = None
        for par in range(2):
            o = _head_attention(qp, [ck[(kv, par)], lk[(kv, par)]], [cv[(kv, par)], lv[(kv, par)]],
                                [None, None], None, par)
            acc = o if acc is None else acc + o
        o_ref[:, t * 128:(t + 1) * 128] = acc.astype(o_ref.dtype)
    span = BAND_TQ + 2 * WINDOW
    ck = _pad_variants(ckb_ref[0])
    cv = _pad_variants(cvb_ref[0], ones=True)
    for sub in range(tq // BAND_TQ):
        q0 = qi * tq + sub * BAND_TQ
        rows = slice(sub * BAND_TQ, (sub + 1) * BAND_TQ)
        lo = pl.multiple_of(jnp.clip(q0 - WINDOW, 0, seq - span), 128)
        qpos = q0 + lax.broadcasted_iota(I32, (BAND_TQ, span), 0)
        kpos = lo + lax.broadcasted_iota(I32, (BAND_TQ, span), 1)
        band = jnp.abs(qpos - kpos) <= WINDOW
        lk = _pad_variants(kb_ref[pl.ds(lo, span), :].astype(F32))
        lv = _pad_variants(vb_ref[pl.ds(lo, span), :].astype(F32), ones=True)
        for t in range(4):
            kv = t // 2
            tile = 4 + t
            qp = q_ref[rows, tile * 128:(tile + 1) * 128]
            acc = None
            for par in range(2):
                sink = sink_ref[2 * t + par] * LOG2E
                o = _head_attention(qp, [lk[(kv, par)], ck[(kv, par)]], [lv[(kv, par)], cv[(kv, par)]],
                                    [band, None], sink, par)
                acc = o if acc is None else acc + o
            o_ref[rows, tile * 128:(tile + 1) * 128] = acc.astype(o_ref.dtype)


def _attn_lat(sink, q, ka, va, kb, vb, cka, cva, ckb, cvb, seq, tq=512):
    n = q.shape[0]
    nb = n // seq
    nq = seq // tq
    qrow = lambda b, i: (b * nq + i, 0)
    brow = lambda b, i: (b, 0)
    kv_spec = pl.BlockSpec((seq, 128), brow)
    past = cka.shape[1]
    c_spec = pl.BlockSpec((1, past, 128), lambda b, i: (b, 0, 0))
    return pl.pallas_call(
        functools.partial(_attn_lat_kernel, tq=tq, seq=seq),
        grid=(nb, nq),
        in_specs=[pl.BlockSpec(memory_space=pltpu.SMEM),
                  pl.BlockSpec((tq, ATTN_OUT), qrow), kv_spec, kv_spec, kv_spec, kv_spec,
                  c_spec, c_spec, c_spec, c_spec],
        out_specs=pl.BlockSpec((tq, ATTN_OUT), qrow),
        out_shape=jax.ShapeDtypeStruct((n, ATTN_OUT), BF16),
        compiler_params=_params("parallel", "parallel"),
        name="attn_lat",
    )(sink, q, ka, va, kb, vb, cka, cva, ckb, cvb)


def _ffn_pre(x1, mod_ref, gain_ref, rt_ref, h2_ref, aff_ref):
    h2 = _norm_mod(x1, gain_ref[...], mod_ref[0, 3:4, :], mod_ref[0, 4:5, :])
    h2_ref[...] = h2.astype(h2_ref.dtype)
    logits = _dot3(rt_ref[...], h2, NT)
    e = jnp.exp(logits - logits.max(axis=0, keepdims=True))
    aff_ref[...] = e / e.sum(axis=0, keepdims=True)


def _postmix_kernel(o_ref, x_ref, mod_ref, w_ref, gain_ref, rt_ref, x1_ref, h2_ref, aff_ref):
    x1 = x_ref[...] + mod_ref[0, 2:3, :] * _dot(o_ref[...], w_ref[...])
    x1_ref[...] = x1
    _ffn_pre(x1, mod_ref, gain_ref, rt_ref, h2_ref, aff_ref)


def _postmix(o, x, mod, w_bf, gain, router_t, *, rows_per_cond):
    n = x.shape[0]
    row = lambda i: (i, 0)
    const = lambda i: (0, 0)
    return pl.pallas_call(
        _postmix_kernel,
        grid=(n // TM,),
        in_specs=[pl.BlockSpec((TM, D_MODEL), row),
                  pl.BlockSpec((TM, D_MODEL), row),
                  _mod_spec(rows_per_cond),
                  pl.BlockSpec(w_bf.shape, const),
                  pl.BlockSpec((1, D_MODEL), const),
                  pl.BlockSpec((N_EXPERTS, D_MODEL), const)],
        out_specs=[pl.BlockSpec((TM, D_MODEL), row),
                   pl.BlockSpec((TM, D_MODEL), row),
                   pl.BlockSpec((N_EXPERTS, TM), lambda i: (0, i))],
        out_shape=[jax.ShapeDtypeStruct((n, D_MODEL), F32),
                   jax.ShapeDtypeStruct((n, D_MODEL), BF16),
                   jax.ShapeDtypeStruct((N_EXPERTS, n), F32)],
        compiler_params=_params("parallel"),
        name="postmix_proj",
    )(o, x, mod, w_bf, gain, router_t)


def _ffn_pre_kernel(x1_ref, mod_ref, gain_ref, rt_ref, h2_ref, aff_ref):
    _ffn_pre(x1_ref[...], mod_ref, gain_ref, rt_ref, h2_ref, aff_ref)


def _ffn_pre_call(x1, mod, gain, router_t, *, rows_per_cond):
    n = x1.shape[0]
    row = lambda i: (i, 0)
    const = lambda i: (0, 0)
    return pl.pallas_call(
        _ffn_pre_kernel,
        grid=(n // TM,),
        in_specs=[pl.BlockSpec((TM, D_MODEL), row),
                  _mod_spec(rows_per_cond),
                  pl.BlockSpec((1, D_MODEL), const),
                  pl.BlockSpec((N_EXPERTS, D_MODEL), const)],
        out_specs=[pl.BlockSpec((TM, D_MODEL), row),
                   pl.BlockSpec((N_EXPERTS, TM), lambda i: (0, i))],
        out_shape=[jax.ShapeDtypeStruct((n, D_MODEL), BF16),
                   jax.ShapeDtypeStruct((N_EXPERTS, n), F32)],
        compiler_params=_params("parallel"),
        name="ffn_pre",
    )(x1, mod, gain, router_t)


def _route_kernel(aff_ref, slot_ref, gate_ref, pos_ref, tcnt_ref, *, seq, cap, nseg, tt):
    aff = jnp.concatenate([aff_ref[:, s * seq:(s + 1) * seq] for s in range(nseg)], axis=0)
    rows = aff.shape[0]
    capf = jnp.float32(cap)
    thr_bits = jnp.zeros((rows, 1), I32)

    def enough(cand):
        cnt = jnp.where(aff >= pltpu.bitcast(cand, F32), 1.0, 0.0).sum(axis=1, keepdims=True)
        return cnt >= capf

    for bit in range(30, 0, -2):
        hi, lo = 1 << bit, 1 << (bit - 1)
        both, only_hi, only_lo = thr_bits | hi | lo, thr_bits | hi, thr_bits | lo
        thr_bits = jnp.where(enough(both), both,
                             jnp.where(enough(only_hi), only_hi, jnp.where(enough(only_lo), only_lo, thr_bits)))
    thr_bits = jnp.where(enough(thr_bits | 1), thr_bits | 1, thr_bits)
    thr = pltpu.bitcast(thr_bits, F32)
    gt = aff > thr
    eq = aff == thr
    n_gt = jnp.where(gt, 1.0, 0.0).sum(axis=1, keepdims=True)
    pw = min(seq, 256)
    tri = jnp.where(lax.broadcasted_iota(I32, (pw, pw), 0) < lax.broadcasted_iota(I32, (pw, pw), 1),
                    1.0, 0.0).astype(BF16)

    def count_before(flag):
        ones = jnp.where(flag, 1.0, 0.0)
        parts = []
        run = jnp.zeros((rows, 1), F32)
        for c0 in range(0, seq, pw):
            blk = ones[:, c0:c0 + pw]
            parts.append(_dot(blk.astype(BF16), tri) + run)
            run = run + blk.sum(axis=1, keepdims=True)
        return jnp.concatenate(parts, axis=1) if len(parts) > 1 else parts[0]

    sel = gt | (eq & (count_before(eq) < capf - n_gt))
    rank = count_before(sel)
    expert = lax.broadcasted_iota(I32, (rows, seq), 0) & (N_EXPERTS - 1)
    slot = jnp.where(sel, expert * cap + rank.astype(I32), -1)
    gate = jnp.where(sel, aff, 0.0)
    pos = jnp.where(sel, rank, -1.0)
    nt = seq // tt
    tile_of = jnp.where((lax.broadcasted_iota(I32, (seq, nt), 0) // tt) == lax.broadcasted_iota(I32, (seq, nt), 1),
                        1.0, 0.0).astype(BF16)
    tcnt = _dot(jnp.where(sel, 1.0, 0.0).astype(BF16), tile_of)
    for s in range(nseg):
        rows_s = slice(s * N_EXPERTS, (s + 1) * N_EXPERTS)
        slot_ref[:, s * seq:(s + 1) * seq] = slot[rows_s, :]
        gate_ref[:, s * seq:(s + 1) * seq] = gate[rows_s, :]
        pos_ref[:, s * seq:(s + 1) * seq] = pos[rows_s, :]
        tcnt_ref[0, :, s * nt:(s + 1) * nt] = tcnt[rows_s, :]


def _route(aff_t, seq, cap, nseg, tt):
    n = aff_t.shape[1]
    nt = seq // tt
    steps = n // (nseg * seq)
    spec = pl.BlockSpec((N_EXPERTS, nseg * seq), lambda i: (0, i))
    slot, gate, pos, tcnt = pl.pallas_call(
        functools.partial(_route_kernel, seq=seq, cap=cap, nseg=nseg, tt=tt),
        grid=(steps,),
        in_specs=[spec],
        out_specs=[spec, spec, spec, pl.BlockSpec((1, N_EXPERTS, nseg * nt), lambda i: (i, 0, 0))],
        out_shape=[jax.ShapeDtypeStruct((N_EXPERTS, n), I32), jax.ShapeDtypeStruct((N_EXPERTS, n), F32),
                   jax.ShapeDtypeStruct((N_EXPERTS, n), F32),
                   jax.ShapeDtypeStruct((steps, N_EXPERTS, nseg * nt), F32)],
        compiler_params=_params("parallel"),
        name="route",
    )(aff_t)
    return slot, gate, pos, tcnt.transpose(1, 0, 2).reshape(N_EXPERTS, n // tt)


def _dispatch_kernel(slot_ref, h_ref, x_ref, *, cap, seq):
    m = N_EXPERTS * cap
    slot_id = lax.broadcasted_iota(I32, (m, seq), 0)
    for r in range(h_ref.shape[0] // seq):
        slots = slot_ref[:, r * seq:(r + 1) * seq]
        owner = jnp.broadcast_to(slots[:, None, :], (N_EXPERTS, cap, seq)).reshape(m, seq)
        sel = jnp.where(owner == slot_id, 1.0, 0.0).astype(BF16)
        x_ref[r * m:(r + 1) * m, :] = _dot(sel, h_ref[r * seq:(r + 1) * seq, :]).astype(x_ref.dtype)


def _dispatch(slot, h, seq, cap):
    n = h.shape[0]
    nb = n // seq
    rb = MOE_RB if nb % MOE_RB == 0 else 1
    return pl.pallas_call(
        functools.partial(_dispatch_kernel, cap=cap, seq=seq),
        grid=(nb // rb,),
        in_specs=[pl.BlockSpec((N_EXPERTS, rb * seq), lambda b: (0, b)),
                  pl.BlockSpec((rb * seq, D_MODEL), lambda b: (b, 0))],
        out_specs=pl.BlockSpec((rb * N_EXPERTS * cap, D_MODEL), lambda b: (b, 0)),
        out_shape=jax.ShapeDtypeStruct((nb * N_EXPERTS * cap, D_MODEL), BF16),
        compiler_params=_params("parallel"),
        name="moe_dispatch",
    )(slot, h)


def _window(cum_ref, base, e, cap, k=0):
    lo = _align_down(cum_ref[base + e]) + k * MOE_W
    return lo, pl.multiple_of(jnp.minimum(lo, cap - MOE_W), BF16_ROWS)


def _align_down(rank):
    return rank & ~(BF16_ROWS - 1)


def _extra_windows(cum_ref, base, e):
    lo = _align_down(cum_ref[base + e])
    return (cum_ref[base + N_EXPERTS + e] - lo + (MOE_W - 1)) // MOE_W


def _dispatch_win_kernel(cum_ref, slot_ref, h_ref, x_ref, *, cap, nt):
    b, i = pl.program_id(0), pl.program_id(1)
    tt = h_ref.shape[0]
    base = (b * (nt + 1) + i) * N_EXPERTS
    h = h_ref[...]
    row = lax.broadcasted_iota(I32, (MOE_W, tt), 0)

    @pl.when(i == 0)
    def _():
        x_ref[...] = jnp.zeros_like(x_ref)

    def hits(e, lo, ws):
        srow = slot_ref[e:e + 1, :]
        return (srow == row + (e * cap + ws)) & (srow >= e * cap + lo)

    for grp in range(N_EXPERTS // MOE_EG):
        wins = [(e,) + _window(cum_ref, base, e, cap) for e in range(grp * MOE_EG, (grp + 1) * MOE_EG)]
        sel = jnp.concatenate([hits(e, lo, ws) for e, lo, ws in wins], axis=0)
        x = _dot(jnp.where(sel, 1.0, 0.0).astype(BF16), h)
        for q, (e, lo, ws) in enumerate(wins):
            dst = pl.ds(e * cap + ws, MOE_W)
            x_ref[dst, :] += x[q * MOE_W:(q + 1) * MOE_W].astype(x_ref.dtype)

    for e in range(N_EXPERTS):
        def extra(k, carry, e=e):
            lo, ws = _window(cum_ref, base, e, cap, k)
            x = _dot(jnp.where(hits(e, lo, ws), 1.0, 0.0).astype(BF16), h)
            x_ref[pl.ds(e * cap + ws, MOE_W), :] += x.astype(x_ref.dtype)
            return carry
        lax.fori_loop(1, _extra_windows(cum_ref, base, e), extra, 0)


def _dispatch_win(cum, slot, h, seq, cap):
    n = h.shape[0]
    nb, nt = n // seq, seq // MOE_TT
    return pl.pallas_call(
        functools.partial(_dispatch_win_kernel, cap=cap, nt=nt),
        grid_spec=pltpu.PrefetchScalarGridSpec(
            num_scalar_prefetch=1,
            grid=(nb, nt),
            in_specs=[pl.BlockSpec((N_EXPERTS, MOE_TT), lambda b, i, c: (0, b * nt + i)),
                      pl.BlockSpec((MOE_TT, D_MODEL), lambda b, i, c: (b * nt + i, 0))],
            out_specs=pl.BlockSpec((N_EXPERTS * cap, D_MODEL), lambda b, i, c: (b, 0))),
        out_shape=jax.ShapeDtypeStruct((nb * N_EXPERTS * cap, D_MODEL), BF16),
        compiler_params=_params("parallel", "arbitrary"),
        name="moe_dispatch_win",
    )(cum, slot, h)


FFN_TF = 512
FFN_RC = 512


def _ffn_kernel(xa_ref, xb_ref, wg_ref, wu_ref, wd_ref, ya_ref, yb_ref, acc_ref):
    j = pl.program_id(1)
    last = pl.num_programs(1) - 1
    ra = xa_ref.shape[0] * xa_ref.shape[2]

    def row_chunks(ref):
        nb, _, cap, d = ref.shape
        rc = min(FFN_RC, nb * cap)
        for r0 in range(0, nb * cap, rc):
            if cap >= rc:
                b, c0 = divmod(r0, cap)
                yield r0, rc, (slice(b, b + 1), 0, slice(c0, c0 + rc), slice(None)), (1, rc, d)
            else:
                yield r0, rc, (slice(r0 // cap, (r0 + rc) // cap), 0, slice(None), slice(None)), (rc // cap, cap, d)

    def sweep(first_tile, last_tile):
        wg = wg_ref[0].astype(BF16)
        wu = wu_ref[0].astype(BF16)
        wd = wd_ref[0].astype(BF16)
        d = wd.shape[1]
        for x_ref, y_ref, base in ((xa_ref, ya_ref, 0), (xb_ref, yb_ref, ra)):
            for r0, rc, idx, shape in row_chunks(x_ref):
                x = x_ref[idx].reshape(rc, d)
                g = _dot(x, wg)
                u = _dot(x, wu)
                y = _dot((g * _sigmoid(g) * u).astype(BF16), wd)
                rows = slice(base + r0, base + r0 + rc)
                if not first_tile:
                    y = acc_ref[rows, :] + y
                if last_tile:
                    y_ref[idx] = y.astype(y_ref.dtype).reshape(shape)
                else:
                    acc_ref[rows, :] = y

    if D_FF == FFN_TF:
        sweep(True, True)
    else:
        pl.when(j == 0)(lambda: sweep(True, False))
        pl.when((j > 0) & (j < last))(lambda: sweep(False, False))
        pl.when(j == last)(lambda: sweep(False, True))


def _ffn(xa, xb, w_gate, w_up, w_down):
    ba, _, ca, d = xa.shape
    bb, _, cb, _ = xb.shape
    nj = D_FF // FFN_TF
    xa_spec = pl.BlockSpec((ba, 1, ca, d), lambda e, j: (0, e, 0, 0))
    xb_spec = pl.BlockSpec((bb, 1, cb, d), lambda e, j: (0, e, 0, 0))
    return pl.pallas_call(
        _ffn_kernel,
        grid=(N_EXPERTS, nj),
        in_specs=[xa_spec, xb_spec,
                  pl.BlockSpec((1, d, FFN_TF), lambda e, j: (e, 0, j)),
                  pl.BlockSpec((1, d, FFN_TF), lambda e, j: (e, 0, j)),
                  pl.BlockSpec((1, FFN_TF, d), lambda e, j: (e, j, 0))],
        out_specs=[xa_spec, xb_spec],
        out_shape=[jax.ShapeDtypeStruct(xa.shape, BF16), jax.ShapeDtypeStruct(xb.shape, BF16)],
        scratch_shapes=[pltpu.VMEM((ba * ca + bb * cb, d), F32)],
        compiler_params=_params("parallel", "arbitrary"),
        name="moe_ffn",
    )(xa, xb, w_gate, w_up, w_down)


def _expand(vals_bf, first_expert, width, total):
    e_of_lane = first_expert + lax.broadcasted_iota(I32, (N_EXPERTS, total), 1) // width
    pick = jnp.where(lax.broadcasted_iota(I32, (N_EXPERTS, total), 0) == e_of_lane, 1.0, 0.0).astype(BF16)
    return _dot(vals_bf, pick, TN)


def _combine_kernel(pos_ref, gate_ref, y_ref, x_ref, mod_ref, o_ref, *, cap, seq):
    m = N_EXPERTS * cap
    rank = (lax.broadcasted_iota(I32, (seq, m), 1) % cap).astype(F32)
    for r in range(x_ref.shape[0] // seq):
        rows = slice(r * seq, (r + 1) * seq)
        pos = _expand(pos_ref[:, rows].astype(BF16), 0, cap, m)
        gate = _expand(gate_ref[:, rows].astype(BF16), 0, cap, m)
        w = jnp.where(pos == rank, gate, 0.0).astype(BF16)
        o_ref[rows, :] = x_ref[rows, :] + mod_ref[0, 5:6, :] * _dot(w, y_ref[r * m:(r + 1) * m, :])


def _combine(pos, gate, y, x, mod, *, seq, cap):
    n = x.shape[0]
    nb = n // seq
    rb = MOE_RB if nb % MOE_RB == 0 else 1
    row = lambda b: (b, 0)
    return pl.pallas_call(
        functools.partial(_combine_kernel, cap=cap, seq=seq),
        grid=(nb // rb,),
        in_specs=[pl.BlockSpec((N_EXPERTS, rb * seq), lambda b: (0, b)),
                  pl.BlockSpec((N_EXPERTS, rb * seq), lambda b: (0, b)),
                  pl.BlockSpec((rb * N_EXPERTS * cap, D_MODEL), row),
                  pl.BlockSpec((rb * seq, D_MODEL), row),
                  pl.BlockSpec((1, 8, D_MODEL), lambda b: (0, 0, 0))],
        out_specs=pl.BlockSpec((rb * seq, D_MODEL), row),
        out_shape=jax.ShapeDtypeStruct((n, D_MODEL), F32),
        compiler_params=_params("parallel"),
        name="moe_combine",
    )(pos, gate, y, x, mod)


def _combine_win_kernel(cum_ref, pos_ref, gate_ref, y_ref, x_ref, mod_ref, o_ref, acc_ref, *, cap, nt):
    b, i = pl.program_id(0), pl.program_id(1)
    tt = x_ref.shape[0]
    base = (b * (nt + 1) + i) * N_EXPERTS
    width = MOE_EG * MOE_W
    posb = pos_ref[...].astype(BF16)
    gateb = gate_ref[...].astype(BF16)
    lane = lax.broadcasted_iota(I32, (1, width), 1)
    offset = (lane % MOE_W).astype(F32)
    acc = jnp.zeros((tt, D_MODEL), F32)
    for grp in range(N_EXPERTS // MOE_EG):
        wins = [(e,) + _window(cum_ref, base, e, cap) for e in range(grp * MOE_EG, (grp + 1) * MOE_EG)]
        lo_l = jnp.zeros((1, width), F32)
        ws_l = jnp.zeros((1, width), F32)
        for q, (e, lo, ws) in enumerate(wins):
            mine = lane // MOE_W == q
            lo_l = jnp.where(mine, lo.astype(F32), lo_l)
            ws_l = jnp.where(mine, ws.astype(F32), ws_l)
        pos = _expand(posb, grp * MOE_EG, MOE_W, width)
        gate = _expand(gateb, grp * MOE_EG, MOE_W, width)
        w = jnp.where((pos - ws_l == offset) & (pos >= lo_l), gate, 0.0).astype(BF16)
        ywin = jnp.concatenate([y_ref[pl.ds(e * cap + ws, MOE_W), :] for e, lo, ws in wins], axis=0)
        acc = acc + _dot(w, ywin)
    acc_ref[...] = acc

    offset_rows = lax.broadcasted_iota(I32, (MOE_W, tt), 0).astype(F32)
    for e in range(N_EXPERTS):
        def extra(k, carry, e=e):
            lo, ws = _window(cum_ref, base, e, cap, k)
            pos = pos_ref[e:e + 1, :]
            gate = gate_ref[e:e + 1, :].astype(BF16).astype(F32)
            w_t = jnp.where((pos - ws.astype(F32) == offset_rows) & (pos >= lo.astype(F32)), gate, 0.0).astype(BF16)
            acc_ref[...] += _dot(w_t, y_ref[pl.ds(e * cap + ws, MOE_W), :], TN)
            return carry
        lax.fori_loop(1, _extra_windows(cum_ref, base, e), extra, 0)

    o_ref[...] = x_ref[...] + mod_ref[0, 5:6, :] * acc_ref[...]


def _combine_win(cum, pos, gate, y, x, mod, *, seq, cap):
    n = x.shape[0]
    nb, nt = n // seq, seq // MOE_TT
    row = lambda b, i, c: (b * nt + i, 0)
    return pl.pallas_call(
        functools.partial(_combine_win_kernel, cap=cap, nt=nt),
        grid_spec=pltpu.PrefetchScalarGridSpec(
            num_scalar_prefetch=1,
            grid=(nb, nt),
            in_specs=[pl.BlockSpec((N_EXPERTS, MOE_TT), lambda b, i, c: (0, b * nt + i)),
                      pl.BlockSpec((N_EXPERTS, MOE_TT), lambda b, i, c: (0, b * nt + i)),
                      pl.BlockSpec((N_EXPERTS * cap, D_MODEL), lambda b, i, c: (b, 0)),
                      pl.BlockSpec((MOE_TT, D_MODEL), row),
                      pl.BlockSpec((1, 8, D_MODEL), lambda b, i, c: (1 + b, 0, 0))],
            out_specs=pl.BlockSpec((MOE_TT, D_MODEL), row),
            scratch_shapes=[pltpu.VMEM((MOE_TT, D_MODEL), F32)]),
        out_shape=jax.ShapeDtypeStruct((n, D_MODEL), F32),
        compiler_params=_params("parallel", "parallel"),
        name="moe_combine_win",
    )(cum, pos, gate, y, x, mod)


def _moe_pair(hp, affp, x1p, hs, affs, x1s, mod, w_gate, w_up, w_down, seq_p, seq_s):
    n_p, n_s = hp.shape[0], hs.shape[0]
    nb_p, nb_s = n_p // seq_p, n_s // seq_s
    cap_p = EC_FACTOR * seq_p // N_EXPERTS
    cap_s = EC_FACTOR * seq_s // N_EXPERTS
    assert N_EXPERTS * cap_p <= 512 and cap_s >= MOE_W and cap_s % BF16_ROWS == 0 and seq_s % MOE_TT == 0
    slot_p, gate_p, pos_p, _ = _route(affp, seq_p, cap_p, nseg=min(8, nb_p), tt=seq_p)
    slot_s, gate_s, pos_s, tcnt = _route(affs, seq_s, cap_s, nseg=min(4, nb_s), tt=MOE_TT)
    nt = seq_s // MOE_TT
    counts = tcnt.T.reshape(nb_s, nt, N_EXPERTS).astype(I32)
    cum = jnp.concatenate([jnp.zeros((nb_s, 1, N_EXPERTS), I32), jnp.cumsum(counts, axis=1)], axis=1).reshape(-1)
    xp = _dispatch(slot_p, hp, seq_p, cap_p).reshape(nb_p, N_EXPERTS, cap_p, D_MODEL)
    xs = _dispatch_win(cum, slot_s, hs, seq_s, cap_s).reshape(nb_s, N_EXPERTS, cap_s, D_MODEL)
    ys, yp = _ffn(xs, xp, w_gate, w_up, w_down)
    outp = _combine(pos_p, gate_p, yp.reshape(-1, D_MODEL), x1p, mod, seq=seq_p, cap=cap_p)
    outs = _combine_win(cum, pos_s, gate_s, ys.reshape(-1, D_MODEL), x1s, mod, seq=seq_s, cap=cap_s)
    return outp, outs


def _cond_blocks(mod_ref, tc, cols_per_cond):
    if cols_per_cond is None:
        return [(slice(0, tc), mod_ref[0])]
    first = 1 + pl.program_id(1) * (tc // cols_per_cond)
    return [(slice(s * cols_per_cond, (s + 1) * cols_per_cond), mod_ref[first + s]) for s in range(tc // cols_per_cond)]


def _ssm_in_kernel(x_ref, mod_ref, gain_ref, wt_ref, ut_ref, *, cols_per_cond):
    parts = [_norm_mod(x_ref[0, rows, :], gain_ref[...], m[0:1, :], m[1:2, :]).astype(BF16)
             for rows, m in _cond_blocks(mod_ref, x_ref.shape[1], cols_per_cond)]
    h = jnp.concatenate(parts, axis=0) if len(parts) > 1 else parts[0]
    ut_ref[0] = _dot(wt_ref[...], h, NT)


def _ssm_in(xperm, mod, gain, wt_bf, *, cols_per_cond, tc):
    l, bk, d = xperm.shape
    assert cols_per_cond is None or tc % cols_per_cond == 0
    return pl.pallas_call(
        functools.partial(_ssm_in_kernel, cols_per_cond=cols_per_cond),
        grid=(l, bk // tc),
        in_specs=[pl.BlockSpec((1, tc, d), lambda j, i: (j, i, 0)),
                  pl.BlockSpec(mod.shape, lambda j, i: (0, 0, 0)),
                  pl.BlockSpec((1, d), lambda j, i: (0, 0)),
                  pl.BlockSpec((d, d), lambda j, i: (0, 0))],
        out_specs=pl.BlockSpec((1, d, tc), lambda j, i: (j, 0, i)),
        out_shape=jax.ShapeDtypeStruct((l, d, bk), F32),
        compiler_params=_params("parallel", "parallel"),
        name="ssm_in",
    )(xperm, mod, gain, wt_bf)


def _ssm_core_kernel(utp_ref, uts_ref, lamp_ref, c_ref, bt_ref, dsk_ref, h0_ref, ytp_ref, yts_ref, fs_ref,
                     *, kp, ks, nbp, nbs):
    rows = SSM_ROWS
    p = SSM_STATE
    lc = SSM_CHUNK
    ri = lax.broadcasted_iota(I32, (rows, rows), 0)
    cj = lax.broadcasted_iota(I32, (rows, rows), 1)
    causal = ri // SSM_GROUP >= cj // SSM_GROUP
    anticausal = cj // SSM_GROUP >= ri // SSM_GROUP
    diag = ri == cj
    leftc = lax.broadcasted_iota(I32, (lc, 128), 1) < p
    leftg = lax.broadcasted_iota(I32, (SSM_GROUP, 128), 1) < p
    nrow = lax.broadcasted_iota(I32, (lc, 128), 0).astype(F32)
    eye = lax.broadcasted_iota(I32, (p, 128), 0) == lax.broadcasted_iota(I32, (p, 128), 1)

    def cmul(ar, ai, xr, xi):
        return ar * xr - ai * xi, ar * xi + ai * xr

    def expand_rows(t):
        return jnp.broadcast_to(t[:, None, :], (lc, SSM_GROUP, 128)).reshape(rows, 128)

    def tile_rows(t):
        return jnp.broadcast_to(t[None, :, :], (lc, SSM_GROUP, 128)).reshape(rows, 128)

    def to_col(row):
        return jnp.where(eye, jnp.broadcast_to(row, (p, 128)), 0.0).sum(axis=1, keepdims=True)

    def operands(gg, d):
        lp = lamp_ref[gg, d]
        lre, lim = lp[0:1], lp[1:2]
        dt = jnp.exp(lp[2:3])
        a, th = lre * dt, lim * dt
        ang = nrow * th
        cs, sn = jnp.cos(ang), jnp.sin(ang)
        ep, em = jnp.exp(nrow * a), jnp.exp(-(nrow * a))
        pr, pi = ep * cs, ep * sn
        nr, ni = em * cs, -(em * sn)
        l1r, l1i = pr[1:2], pi[1:2]
        lmr, lmi = pr[lc - 1:lc], pi[lc - 1:lc]
        llr, lli = cmul(lmr, lmi, l1r, l1i)
        den = lre * lre + lim * lim
        cr = ((l1r - 1.0) * lre + l1i * lim) / den
        ci = (l1i * lre - (l1r - 1.0) * lim) / den
        btr, bti = bt_ref[gg, d, 0], bt_ref[gg, d, 1]
        bbr, bbi = cr * btr - ci * bti, cr * bti + ci * btr
        u1 = tile_rows(jnp.where(leftg, bbr, bbi))
        u2 = tile_rows(jnp.where(leftg, bbi, bbr))
        c1 = tile_rows(c_ref[gg, d, 0])
        c2 = tile_rows(c_ref[gg, d, 1])

        def left_form(xr, xi):
            return (c1 * expand_rows(jnp.where(leftc, xr, -xi))
                    + c2 * expand_rows(jnp.where(leftc, -xi, -xr)))

        def right_form(xr, xi):
            return u1 * expand_rows(xr) + u2 * expand_rows(jnp.where(leftc, -xi, xi))

        if d == 0:
            al = left_form(pr, pi)
            brt = right_form(nr, ni)
            rrt = right_form(*cmul(lmr, lmi, nr, ni))
            qq = left_form(*cmul(l1r, l1i, pr, pi))
            mat = jnp.where(causal, _dot3(al, brt, NT), 0.0)
        else:
            al = left_form(nr, ni)
            brt = right_form(pr, pi)
            rrt = brt
            qq = left_form(*cmul(llr, lli, nr, ni))
            mat = jnp.where(anticausal, _dot3(al, brt, NT), 0.0)
        return mat, rrt.T, qq, to_col(llr), to_col(lli)

    def scan(sr, si, lr, li, h0r, h0i, reverse, nchunk, nbatch, sel):
        bk = sr.shape[1]
        lane = lax.broadcasted_iota(I32, (p, bk), 1)
        kidx = lane & (nchunk - 1)
        edge = (nchunk - 1) if reverse else 0
        if h0r is not None:
            h0cr = jnp.zeros((p, bk), F32)
            h0ci = jnp.zeros((p, bk), F32)
            for b in range(nbatch):
                at = lane == (b * nchunk + edge)
                h0cr = jnp.where(at, h0r[:, b:b + 1], h0cr)
                h0ci = jnp.where(at, h0i[:, b:b + 1], h0ci)
            ar, ai = cmul(lr, li, h0cr, h0ci)
            er, ei = sr + ar, si + ai
        else:
            er, ei = sr, si
        ar, ai = lr, li
        s = 1
        while s < nchunk:
            if reverse:
                ok = kidx < nchunk - s
                tr, ti = pltpu.roll(er, bk - s, 1), pltpu.roll(ei, bk - s, 1)
            else:
                ok = kidx >= s
                tr, ti = pltpu.roll(er, s, 1), pltpu.roll(ei, s, 1)
            tr = jnp.where(ok, tr, 0.0)
            ti = jnp.where(ok, ti, 0.0)
            dr, di = cmul(ar, ai, tr, ti)
            er, ei = er + dr, ei + di
            ar, ai = cmul(ar, ai, ar, ai)
            s *= 2
        if reverse:
            inner = kidx < nchunk - 1
            hr, hi = pltpu.roll(er, bk - 1, 1), pltpu.roll(ei, bk - 1, 1)
        else:
            inner = kidx >= 1
            hr, hi = pltpu.roll(er, 1, 1), pltpu.roll(ei, 1, 1)
        hr = jnp.where(inner, hr, h0cr if h0r is not None else 0.0)
        hi = jnp.where(inner, hi, h0ci if h0r is not None else 0.0)
        fin = None if sel is None else (_sel_dot_t(sel, er), _sel_dot_t(sel, ei))
        return hr, hi, fin

    def final_selectors(bk, nchunk, nbatch):
        bat = lax.broadcasted_iota(I32, (nbatch, bk), 0)
        col = lax.broadcasted_iota(I32, (nbatch, bk), 1)
        last = jnp.where(col == bat * nchunk + (nchunk - 1), 1.0, 0.0).astype(BF16)
        first = jnp.where(col == bat * nchunk, 1.0, 0.0).astype(BF16)
        return last, first

    sel_last, sel_first = final_selectors(utp_ref.shape[2], kp, nbp)

    for gg in range(SSM_GB):
        mf, rf, qf, lfr, lfi = operands(gg, 0)
        mb, rb, qb, lbr, lbi = operands(gg, 1)
        skip = jnp.where(diag, jnp.broadcast_to(dsk_ref[gg], (rows, rows)), 0.0)
        stack = jnp.concatenate([mf + mb + skip, rf, rb], axis=0)
        qq = jnp.concatenate([qf, qb], axis=1)
        h0 = h0_ref[gg]
        for ut_ref, yt_ref, nchunk, nbatch, latent in ((utp_ref, ytp_ref, kp, nbp, False),
                                                       (uts_ref, yts_ref, ks, nbs, True)):
            bk = ut_ref.shape[2]
            x = ut_ref[:, gg * SSM_GROUP:(gg + 1) * SSM_GROUP, :].reshape(rows, bk)
            res = _mm(stack, x, SSM_PASSES)
            hfr, hfi, ff = scan(res[rows:rows + p], res[rows + p:rows + 2 * p], lfr, lfi,
                                h0[0] if latent else None, h0[1] if latent else None,
                                False, nchunk, nbatch, None if latent else sel_last)
            hbr, hbi, fb = scan(res[rows + 2 * p:rows + 3 * p], res[rows + 3 * p:rows + 4 * p], lbr, lbi,
                                h0[2] if latent else None, h0[3] if latent else None,
                                True, nchunk, nbatch, None if latent else sel_first)
            states = jnp.concatenate([hfr, hfi, hbr, hbi], axis=0)
            y = res[:rows] + _mm(qq, states, SSM_PASSES)
            yt_ref[:, gg * SSM_GROUP:(gg + 1) * SSM_GROUP, :] = y.reshape(lc, SSM_GROUP, bk)
            if not latent:
                fs_ref[gg, 0] = ff[0]
                fs_ref[gg, 1] = ff[1]
                fs_ref[gg, 2] = fb[0]
                fs_ref[gg, 3] = fb[1]


def _ssm_core(utp, uts, ops, h0, *, kp, ks, nbp, nbs):
    lamp, c2, bt2, dsk = ops
    l, d, bkp = utp.shape
    bks = uts.shape[2]
    g = SSM_GROUPS
    gb = SSM_GB
    lead4 = lambda i: (i, 0, 0, 0)
    lead5 = lambda i: (i, 0, 0, 0, 0)
    ut_spec = lambda bk: pl.BlockSpec((l, gb * SSM_GROUP, bk), lambda i: (0, i, 0))
    return pl.pallas_call(
        functools.partial(_ssm_core_kernel, kp=kp, ks=ks, nbp=nbp, nbs=nbs),
        grid=(g // gb,),
        in_specs=[ut_spec(bkp), ut_spec(bks),
                  pl.BlockSpec((gb, 2, 8, 128), lead4),
                  pl.BlockSpec((gb, 2, 2, SSM_GROUP, 128), lead5),
                  pl.BlockSpec((gb, 2, 2, SSM_GROUP, 128), lead5),
                  pl.BlockSpec((gb, 1, SSM_ROWS), lambda i: (i, 0, 0)),
                  pl.BlockSpec((gb, 4, SSM_STATE, nbs), lead4)],
        out_specs=[ut_spec(bkp), ut_spec(bks),
                   pl.BlockSpec((gb, 4, nbp, SSM_STATE), lead4)],
        out_shape=[jax.ShapeDtypeStruct((l, d, bkp), F32),
                   jax.ShapeDtypeStruct((l, d, bks), F32),
                   jax.ShapeDtypeStruct((g, 4, nbp, SSM_STATE), F32)],
        compiler_params=_params("parallel"),
        name="ssm_core",
    )(utp, uts, lamp, c2, bt2, dsk, h0)


def _ssm_out_kernel(yt_ref, x_ref, mod_ref, w_ref, x1_ref, *, cols_per_cond):
    y = yt_ref[0].T
    act = 0.5 * y * (1.0 + jnp.tanh(0.7978845608028654 * (y + 0.044715 * (y * y * y))))
    ag = _dot(act.astype(BF16), w_ref[...])
    d = x1_ref.shape[2]
    mix = ag[:, :d] * _sigmoid(ag[:, d:])
    for rows, m in _cond_blocks(mod_ref, x_ref.shape[1], cols_per_cond):
        x1_ref[0, rows, :] = x_ref[0, rows, :] + m[2:3, :] * mix[rows, :]


def _ssm_out(yt, xperm, mod, w_bf, *, cols_per_cond, tc):
    l, d, bk = yt.shape
    assert cols_per_cond is None or tc % cols_per_cond == 0
    return pl.pallas_call(
        functools.partial(_ssm_out_kernel, cols_per_cond=cols_per_cond),
        grid=(l, bk // tc),
        in_specs=[pl.BlockSpec((1, d, tc), lambda j, i: (j, 0, i)),
                  pl.BlockSpec((1, tc, d), lambda j, i: (j, i, 0)),
                  pl.BlockSpec(mod.shape, lambda j, i: (0, 0, 0)),
                  pl.BlockSpec((d, 2 * d), lambda j, i: (0, 0))],
        out_specs=pl.BlockSpec((1, tc, d), lambda j, i: (j, i, 0)),
        out_shape=jax.ShapeDtypeStruct((l, bk, d), F32),
        compiler_params=_params("parallel", "parallel"),
        name="ssm_out",
    )(yt, xperm, mod, w_bf)


def _ssm_operand_params(lam_re, lam_im, b_re, b_im, c_re, c_im, log_dt, d_skip):
    g, l = SSM_GROUPS, SSM_CHUNK
    dup = lambda t: jnp.concatenate([t, t], axis=-1)
    lamp = jnp.stack([lam_re, lam_im, jnp.broadcast_to(log_dt[..., None], lam_re.shape)], axis=2)
    lamp = dup(jnp.pad(lamp, ((0, 0), (0, 0), (0, 5), (0, 0)))).transpose(1, 0, 2, 3)
    c2 = dup(jnp.stack([c_re, c_im], axis=2)).transpose(1, 0, 2, 3, 4)
    bt2 = dup(jnp.stack([jnp.swapaxes(b_re, -1, -2), jnp.swapaxes(b_im, -1, -2)], axis=2)).transpose(1, 0, 2, 3, 4)
    dsk = jnp.tile(d_skip.reshape(g, 1, SSM_GROUP), (1, 1, l))
    return lamp, c2, bt2, dsk


def _to_chunks(x, nb, seq):
    k = seq // SSM_CHUNK
    return x.reshape(nb, k, SSM_CHUNK, -1).transpose(2, 0, 1, 3).reshape(SSM_CHUNK, nb * k, -1)


def _from_chunks(x, nb, seq):
    k = seq // SSM_CHUNK
    return x.reshape(SSM_CHUNK, nb, k, -1).transpose(1, 2, 0, 3).reshape(nb * seq, -1)


def _ssm_layer(xp, xs, mod, gain, wt_bf, ops, w_out_bf, h0, *, nbp, sp, nbs, ss):
    kp, ks = sp // SSM_CHUNK, ss // SSM_CHUNK
    tcp, tcs = min(SSM_TC, nbp * kp), min(SSM_TC, nbs * ks)
    xcp, xcs = _to_chunks(xp, nbp, sp), _to_chunks(xs, nbs, ss)
    utp = _ssm_in(xcp, mod, gain, wt_bf, cols_per_cond=None, tc=tcp)
    uts = _ssm_in(xcs, mod, gain, wt_bf, cols_per_cond=ks, tc=tcs)
    ytp, yts, fs = _ssm_core(utp, uts, ops, h0, kp=kp, ks=ks, nbp=nbp, nbs=nbs)
    x1p = _from_chunks(_ssm_out(ytp, xcp, mod, w_out_bf, cols_per_cond=None, tc=tcp), nbp, sp)
    x1s = _from_chunks(_ssm_out(yts, xcs, mod, w_out_bf, cols_per_cond=ks, tc=tcs), nbs, ss)
    return x1p, x1s, fs


def _rope_tables(seq):
    t = jnp.arange(seq)
    row = (t // GRID_W).astype(F32)
    col = (t % GRID_W).astype(F32)
    n_freq = HEAD_DIM // 4
    inv_freq = ROPE_THETA ** (-jnp.arange(n_freq, dtype=F32) / n_freq)
    ang = jnp.concatenate([row[:, None] * inv_freq, col[:, None] * inv_freq], axis=-1)
    cos = jnp.repeat(jnp.cos(ang), 2, axis=-1)
    sin = jnp.repeat(jnp.sin(ang), 2, axis=-1)
    sign = jnp.tile(jnp.array([-1.0, 1.0], F32), HEAD_DIM // 2)
    return jnp.tile(cos, (1, 4)), jnp.tile(sin * sign, (1, 4))


def _head_gains(qn_a, kn_a, qn_b, kn_b):
    scale = HEAD_DIM ** -0.5 * LOG2E
    ones = jnp.ones((N_KV * HEAD_DIM,), F32)
    return jnp.concatenate([jnp.tile(qn_a, N_HEADS) * scale, jnp.tile(kn_a, N_KV), ones,
                            jnp.tile(qn_b, N_HEADS) * scale, jnp.tile(kn_b, N_KV), ones]).reshape(1, QKV_COLS)


def kernel(x_prompt, x_sample, c, cache_k_a_l0, cache_v_a_l0, cache_k_b_l0, cache_v_b_l0, state_ssm_re_l1, state_ssm_im_l1, c_ctx, mod_w_l0, mod_b_l0, norm_mix_l0, attn_w_in_l0, q_norm_a_l0, k_norm_a_l0, q_norm_b_l0, k_norm_b_l0, sink_b_l0, attn_w_out_l0, norm_ffn_l0, router_l0, moe_w_gate_l0, moe_w_up_l0, moe_w_down_l0, mod_w_l1, mod_b_l1, norm_mix_l1, ssm_w_in_l1, ssm_lambda_re_l1, ssm_lambda_im_l1, ssm_b_re_l1, ssm_b_im_l1, ssm_c_re_l1, ssm_c_im_l1, ssm_log_dt_l1, ssm_d_l1, ssm_w_out_l1, norm_ffn_l1, router_l1, moe_w_gate_l1, moe_w_up_l1, moe_w_down_l1):
    bp, sp, d = x_prompt.shape
    bs, ss, _ = x_sample.shape
    past = cache_k_a_l0.shape[1]
    assert d == D_MODEL and bs <= 7 and (bp * sp) % TM == 0 and TM % sp == 0 and ss % TM == 0
    xp = x_prompt.reshape(bp * sp, d)
    xs = x_sample.reshape(bs * ss, d)
    cond8 = jnp.concatenate([c_ctx[None], c, jnp.zeros((7 - bs, d), F32)], axis=0)
    row1 = lambda v: v.reshape(1, -1)

    mod0 = _mod_rows(cond8, mod_w_l0, mod_b_l0)
    w_in = attn_w_in_l0.astype(BF16)
    hgain = _head_gains(q_norm_a_l0, k_norm_a_l0, q_norm_b_l0, k_norm_b_l0)
    lane = np.arange(256)
    bd = jnp.asarray((lane[:, None] // HEAD_DIM == lane[None, :] // HEAD_DIM) / HEAD_DIM, BF16)
    cos_t, sin_t = _rope_tables(ss)
    qp, kap, vap, kbp, vbp = _qkv(xp, mod0, row1(norm_mix_l0), w_in, hgain, bd, cos_t, sin_t,
                                  rows_per_cond=None, seq=sp, rope=False, kv_dtype=F32, transposed_kv=True)
    qs, kas, vas, kbs, vbs = _qkv(xs, mod0, row1(norm_mix_l0), w_in, hgain, bd, cos_t, sin_t,
                                  rows_per_cond=ss, seq=ss, rope=True, kv_dtype=BF16, transposed_kv=False)
    op = _attn_ctx(sink_b_l0, qp, kap, vap, kbp, vbp, sp)
    cache = lambda t: t.reshape(bs, past, N_KV * HEAD_DIM)
    os_ = _attn_lat(sink_b_l0, qs, kas, vas, kbs, vbs, cache(cache_k_a_l0), cache(cache_v_a_l0),
                    cache(cache_k_b_l0), cache(cache_v_b_l0), ss)
    w_out = attn_w_out_l0.astype(BF16)
    x1p, hp, affp = _postmix(op, xp, mod0, w_out, row1(norm_ffn_l0), router_l0.T, rows_per_cond=None)
    x1s, hs, affs = _postmix(os_, xs, mod0, w_out, row1(norm_ffn_l0), router_l0.T, rows_per_cond=ss)
    xp, xs = _moe_pair(hp, affp, x1p, hs, affs, x1s, mod0, moe_w_gate_l0, moe_w_up_l0, moe_w_down_l0, sp, ss)

    mod1 = _mod_rows(cond8, mod_w_l1, mod_b_l1)
    ops = _ssm_operand_params(ssm_lambda_re_l1, ssm_lambda_im_l1, ssm_b_re_l1, ssm_b_im_l1, ssm_c_re_l1, ssm_c_im_l1,
                              ssm_log_dt_l1, ssm_d_l1)
    wt = ssm_w_in_l1.T.astype(BF16)
    w_so = ssm_w_out_l1.astype(BF16)
    h0 = jnp.stack([state_ssm_re_l1[:, 0], state_ssm_im_l1[:, 0], state_ssm_re_l1[:, 1], state_ssm_im_l1[:, 1]],
                   axis=0).transpose(2, 0, 3, 1)
    x1p, x1s, fsp = _ssm_layer(xp, xs, mod1, row1(norm_mix_l1), wt, ops, w_so, h0, nbp=bp, sp=sp, nbs=bs, ss=ss)
    hp, affp = _ffn_pre_call(x1p, mod1, row1(norm_ffn_l1), router_l1.T, rows_per_cond=None)
    hs, affs = _ffn_pre_call(x1s, mod1, row1(norm_ffn_l1), router_l1.T, rows_per_cond=ss)
    xp, xs = _moe_pair(hp, affp, x1p, hs, affs, x1s, mod1, moe_w_gate_l1, moe_w_up_l1, moe_w_down_l1, sp, ss)

    kv_out = lambda t: t.reshape(bp, N_KV, HEAD_DIM, sp).transpose(0, 3, 1, 2)
    fin = fsp.transpose(2, 1, 0, 3)
    ssm_re = jnp.stack([fin[:, 0], fin[:, 2]], axis=1)
    ssm_im = jnp.stack([fin[:, 1], fin[:, 3]], axis=1)
    return (xp.reshape(bp, sp, d), xs.reshape(bs, ss, d), kv_out(kap), kv_out(vap), kv_out(kbp), kv_out(vbp),
            ssm_re, ssm_im)
```

```python
import functools

import jax
import jax.numpy as jnp
import numpy as np
from jax import lax
from jax.experimental import pallas as pl
from jax.experimental.pallas import tpu as pltpu

F32, BF16, I32 = jnp.float32, jnp.bfloat16, jnp.int32

D_MODEL = 1024
GRID_W = 64
HEAD_DIM = 64
N_HEADS = 8
N_KV = 2
WINDOW = 128
ROPE_THETA = 10000.0
SSM_GROUP = 16
SSM_GROUPS = D_MODEL // SSM_GROUP
SSM_STATE = 64
N_EXPERTS = 16
EC_FACTOR = 2
D_FF = 2 * D_MODEL
EPS = 1e-6
NEG_INF = -1e30
LOG2E = 1.4426950408889634
QKV_COLS = 2 * (N_HEADS + 2 * N_KV) * HEAD_DIM
ATTN_OUT = 2 * N_HEADS * HEAD_DIM

SSM_CHUNK = 16
SSM_ROWS = SSM_CHUNK * SSM_GROUP
SSM_GB = 4
SSM_PASSES = 1

SSM_TC = 512

TM = 1024

BAND_TQ = 256

MOE_TT = 256
BF16_ROWS = 16
MOE_W = 64
MOE_EG = 4
MOE_RB = 4
VMEM_LIMIT = 56 * 1024 * 1024

NN = (((1,), (0,)), ((), ()))
NT = (((1,), (1,)), ((), ()))
TN = (((0,), (0,)), ((), ()))


def _dot(a, b, dims=NN):
    return lax.dot_general(a, b, dims, preferred_element_type=F32)


def _split2(x):
    hi = x.astype(BF16)
    lo = (x - hi.astype(F32)).astype(BF16)
    return hi, lo


def _split3(x):
    hi = x.astype(BF16)
    r = x - hi.astype(F32)
    mid = r.astype(BF16)
    lo = (r - mid.astype(F32)).astype(BF16)
    return hi, mid, lo


def _dot3(a, b, dims=NN):
    ah, al = _split2(a)
    bh, bl = _split2(b)
    return _dot(ah, bh, dims) + (_dot(ah, bl, dims) + _dot(al, bh, dims))


def _mm(a, b, passes):
    if passes == 1:
        return _dot(a.astype(BF16), b.astype(BF16))
    return _dot3(a, b)


def _sel_dot_t(sel, x):
    hi, mid, lo = _split3(x)
    return _dot(sel, hi, NT) + (_dot(sel, mid, NT) + _dot(sel, lo, NT))


def _sigmoid(x):
    return 1.0 / (1.0 + jnp.exp(-x))


def _norm_mod(x, gain, shift, scale):
    ms = jnp.mean(x * x, axis=-1, keepdims=True)
    y = x * lax.rsqrt(ms + EPS) * gain
    return y * (1.0 + scale) + shift


def _params(*sem):
    return pltpu.CompilerParams(dimension_semantics=sem, vmem_limit_bytes=VMEM_LIMIT)


def _adaln_kernel(c_ref, w0_ref, b0_ref, w1_ref, b1_ref, o0_ref, o1_ref):
    c = c_ref[...]
    s = c * _sigmoid(c)
    o0_ref[...] = _dot3(s, w0_ref[...]) + b0_ref[...]
    o1_ref[...] = _dot3(s, w1_ref[...]) + b1_ref[...]


def _adaln(cond8, w_mod0, b_mod0, w_mod1, b_mod1):
    d, e = w_mod0.shape
    tn = 1536
    col = lambda j: (0, j)
    w_spec, v_spec = pl.BlockSpec((d, tn), col), pl.BlockSpec((1, tn), col)
    out = jax.ShapeDtypeStruct((8, e), F32)
    return pl.pallas_call(
        _adaln_kernel,
        grid=(e // tn,),
        in_specs=[pl.BlockSpec((8, d), lambda j: (0, 0)), w_spec, v_spec, w_spec, v_spec],
        out_specs=[pl.BlockSpec((8, tn), col)] * 2,
        out_shape=[out, out],
        compiler_params=_params("parallel"),
        name="adaln",
    )(cond8, w_mod0, b_mod0.reshape(1, e), w_mod1, b_mod1.reshape(1, e))


def _mod_rows(m):
    return jnp.pad(m.reshape(8, 6, D_MODEL), ((0, 0), (0, 2), (0, 0)))


def _mod_spec(rows_per_cond):
    if rows_per_cond is None:
        return pl.BlockSpec((1, 8, D_MODEL), lambda i: (0, 0, 0))
    return pl.BlockSpec((1, 8, D_MODEL), lambda i: (1 + (i * TM) // rows_per_cond, 0, 0))


def _qkv_kernel(x_ref, mod_ref, gain_ref, w_ref, hg_ref, bd_ref, cos_ref, sin_ref,
                q_ref, ka_ref, va_ref, kb_ref, vb_ref, *, rope, transposed_kv):
    h = _norm_mod(x_ref[...], gain_ref[...], mod_ref[0, 0:1, :], mod_ref[0, 1:2, :])
    proj = _dot(h.astype(BF16), w_ref[...])
    bd = bd_ref[...]

    def head_norm(blk, g):
        ms = _dot((blk * blk).astype(BF16), bd)
        return blk * lax.rsqrt(ms + EPS) * g

    def rotary(blk):
        w = blk.shape[1]
        even = (lax.broadcasted_iota(I32, blk.shape, 1) & 1) == 0
        swapped = jnp.where(even, pltpu.roll(blk, w - 1, 1), pltpu.roll(blk, 1, 1))
        return blk * cos_ref[:, :w] + swapped * sin_ref[:, :w]

    def qk(c0):
        blk = head_norm(proj[:, c0:c0 + 256], hg_ref[:, c0:c0 + 256])
        return rotary(blk) if rope else blk

    q_ref[:, 0:256] = qk(0).astype(q_ref.dtype)
    q_ref[:, 256:512] = qk(256).astype(q_ref.dtype)
    q_ref[:, 512:768] = qk(768).astype(q_ref.dtype)
    q_ref[:, 768:1024] = qk(1024).astype(q_ref.dtype)
    kva = qk(512)
    kvb = qk(1280)
    outs = ((ka_ref, kva[:, :128]), (va_ref, proj[:, 640:768]), (kb_ref, kvb[:, :128]), (vb_ref, proj[:, 1408:1536]))
    for ref, val in outs:
        if transposed_kv:
            seq = ref.shape[2]
            for r in range(ref.shape[0]):
                ref[r] = val[r * seq:(r + 1) * seq].T.astype(ref.dtype)
        else:
            ref[...] = val.astype(ref.dtype)


def _qkv(x, mod, gain, w_bf, hgain, bd, cos_t, sin_t, *, rows_per_cond, seq, rope, kv_dtype, transposed_kv):
    n = x.shape[0]
    tiles_per_seq = max(1, seq // TM)
    row = lambda i: (i, 0)
    const = lambda i: (0, 0)
    pos = lambda i: (i % tiles_per_seq, 0)
    if transposed_kv:
        assert TM % seq == 0
        kv_shape = jax.ShapeDtypeStruct((n // seq, 128, seq), kv_dtype)
        kv_spec = pl.BlockSpec((TM // seq, 128, seq), lambda i: (i, 0, 0))
    else:
        kv_shape = jax.ShapeDtypeStruct((n, 128), kv_dtype)
        kv_spec = pl.BlockSpec((TM, 128), row)
    return pl.pallas_call(
        functools.partial(_qkv_kernel, rope=rope, transposed_kv=transposed_kv),
        grid=(n // TM,),
        in_specs=[pl.BlockSpec((TM, D_MODEL), row),
                  _mod_spec(rows_per_cond),
                  pl.BlockSpec((1, D_MODEL), const),
                  pl.BlockSpec((D_MODEL, QKV_COLS), const),
                  pl.BlockSpec((1, QKV_COLS), const),
                  pl.BlockSpec((256, 256), const),
                  pl.BlockSpec((TM, 256), pos),
                  pl.BlockSpec((TM, 256), pos)],
        out_specs=[pl.BlockSpec((TM, ATTN_OUT), row)] + [kv_spec] * 4,
        out_shape=[jax.ShapeDtypeStruct((n, ATTN_OUT), BF16)] + [kv_shape] * 4,
        compiler_params=_params("parallel"),
        name="qkv_rope" if rope else "qkv",
    )(x, mod, gain, w_bf, hgain, bd, cos_t, sin_t)


def _pad_variants(kk, ones=False):
    lane = lax.broadcasted_iota(I32, kk.shape, 1)
    left = lane < HEAD_DIM
    rolled = pltpu.roll(kk, HEAD_DIM, 1)
    fill_r = jnp.where(lane == HEAD_DIM, 1.0, 0.0) if ones else jnp.zeros_like(kk)
    fill_l = jnp.where(lane == 0, 1.0, 0.0) if ones else jnp.zeros_like(kk)
    return {(0, 0): jnp.where(left, kk, fill_r).astype(BF16),
            (0, 1): jnp.where(left, fill_l, rolled).astype(BF16),
            (1, 0): jnp.where(left, rolled, fill_r).astype(BF16),
            (1, 1): jnp.where(left, fill_l, kk).astype(BF16)}


def _pad_variants_t(kt):
    top = lax.broadcasted_iota(I32, kt.shape, 0) < HEAD_DIM
    zero = jnp.zeros((HEAD_DIM, kt.shape[1]), F32)
    return {(0, 0): jnp.where(top, kt, 0.0).astype(BF16),
            (0, 1): jnp.concatenate([zero, kt[:HEAD_DIM]], axis=0).astype(BF16),
            (1, 0): jnp.concatenate([kt[HEAD_DIM:], zero], axis=0).astype(BF16),
            (1, 1): jnp.where(top, 0.0, kt).astype(BF16)}


def _head_attention_small(qp, kblk, vblk, sink):
    s = _dot(qp, kblk, NN)
    m = s.max(axis=-1, keepdims=True)
    if sink is not None:
        m = jnp.maximum(m, sink)
    p = jnp.exp2(s - m)
    den = p.sum(axis=-1, keepdims=True)
    if sink is not None:
        den = den + jnp.exp2(sink - m)
    return _dot(p.astype(BF16), vblk, NT) / den


def _head_attention(qp, keys, vals, masks, sink, par):
    scores = []
    for kblk, mask in zip(keys, masks):
        s = _dot(qp, kblk, NT)
        if mask is not None:
            s = jnp.where(mask, s, NEG_INF)
        scores.append(s)
    m = scores[0].max(axis=-1, keepdims=True)
    for s in scores[1:]:
        m = jnp.maximum(m, s.max(axis=-1, keepdims=True))
    if sink is not None:
        m = jnp.maximum(m, sink)
    out = None
    for s, vblk in zip(scores, vals):
        o = _dot(jnp.exp2((s - m).astype(BF16)), vblk)
        out = o if out is None else out + o
    ones_lane = HEAD_DIM if par == 0 else 0
    den = out[:, ones_lane:ones_lane + 1]
    if sink is not None:
        den = den + jnp.exp2(sink - m)
    own = (lax.broadcasted_iota(I32, out.shape, 1) < HEAD_DIM) == (par == 0)
    return jnp.where(own, out / den, 0.0)


def _attn_ctx_kernel(sink_ref, q_ref, ka_ref, va_ref, kb_ref, vb_ref, o_ref):
    for mixer, (k_ref, v_ref) in enumerate(((ka_ref, va_ref), (kb_ref, vb_ref))):
        kvar = _pad_variants_t(k_ref[0])
        vvar = _pad_variants_t(v_ref[0])
        for t in range(4):
            tile = mixer * 4 + t
            kv = t // 2
            qp = q_ref[:, tile * 128:(tile + 1) * 128]
            acc = None
            for par in range(2):
                sink = sink_ref[2 * t + par] * LOG2E if mixer == 1 else None
                o = _head_attention_small(qp, kvar[(kv, par)], vvar[(kv, par)], sink)
                acc = o if acc is None else acc + o
            o_ref[:, tile * 128:(tile + 1) * 128] = acc.astype(o_ref.dtype)


def _attn_ctx(sink, q, ka, va, kb, vb, seq):
    n = q.shape[0]
    row = lambda b: (b, 0)
    kv_spec = pl.BlockSpec((1, 128, seq), lambda b: (b, 0, 0))
    return pl.pallas_call(
        _attn_ctx_kernel,
        grid=(n // seq,),
        in_specs=[pl.BlockSpec(memory_space=pltpu.SMEM),
                  pl.BlockSpec((seq, ATTN_OUT), row), kv_spec, kv_spec, kv_spec, kv_spec],
        out_specs=pl.BlockSpec((seq, ATTN_OUT), row),
        out_shape=jax.ShapeDtypeStruct((n, ATTN_OUT), BF16),
        compiler_params=_params("parallel"),
        name="attn_ctx",
    )(sink, q, ka, va, kb, vb)


def _attn_lat_kernel(sink_ref, q_ref, ka_ref, va_ref, kb_ref, vb_ref,
                     cka_ref, cva_ref, ckb_ref, cvb_ref, o_ref, *, tq, seq):
    qi = pl.program_id(1)
    ck = _pad_variants(cka_ref[0])
    cv = _pad_variants(cva_ref[0], ones=True)
    lk = _pad_variants(ka_ref[...].astype(F32))
    lv = _pad_variants(va_ref[...].astype(F32), ones=True)
    for t in range(4):
        kv = t // 2
        qp = q_ref[:, t * 128:(t + 1) * 128]
        acc = None
        for par in range(2):
            o = _head_attention(qp, [ck[(kv, par)], lk[(kv, par)]], [cv[(kv, par)], lv[(kv, par)]],
                                [None, None], None, par)
            acc = o if acc is None else acc + o
        o_ref[:, t * 128:(t + 1) * 128] = acc.astype(o_ref.dtype)
    span = BAND_TQ + 2 * WINDOW
    ck = _pad_variants(ckb_ref[0])
    cv = _pad_variants(cvb_ref[0], ones=True)
    for sub in range(tq // BAND_TQ):
        q0 = qi * tq + sub * BAND_TQ
        rows = slice(sub * BAND_TQ, (sub + 1) * BAND_TQ)
        lo = pl.multiple_of(jnp.clip(q0 - WINDOW, 0, seq - span), 128)
        qpos = q0 + lax.broadcasted_iota(I32, (BAND_TQ, span), 0)
        kpos = lo + lax.broadcasted_iota(I32, (BAND_TQ, span), 1)
        band = jnp.abs(qpos - kpos) <= WINDOW
        lk = _pad_variants(kb_ref[pl.ds(lo, span), :].astype(F32))
        lv = _pad_variants(vb_ref[pl.ds(lo, span), :].astype(F32), ones=True)
        for t in range(4):
            kv = t // 2
            tile = 4 + t
            qp = q_ref[rows, tile * 128:(tile + 1) * 128]
            acc = None
            for par in range(2):
                sink = sink_ref[2 * t + par] * LOG2E
                o = _head_attention(qp, [lk[(kv, par)], ck[(kv, par)]], [lv[(kv, par)], cv[(kv, par)]],
                                    [band, None], sink, par)
                acc = o if acc is None else acc + o
            o_ref[rows, tile * 128:(tile + 1) * 128] = acc.astype(o_ref.dtype)


def _attn_lat(sink, q, ka, va, kb, vb, cka, cva, ckb, cvb, seq, tq=512):
    n = q.shape[0]
    nb = n // seq
    nq = seq // tq
    qrow = lambda b, i: (b * nq + i, 0)
    brow = lambda b, i: (b, 0)
    kv_spec = pl.BlockSpec((seq, 128), brow)
    past = cka.shape[1]
    c_spec = pl.BlockSpec((1, past, 128), lambda b, i: (b, 0, 0))
    return pl.pallas_call(
        functools.partial(_attn_lat_kernel, tq=tq, seq=seq),
        grid=(nb, nq),
        in_specs=[pl.BlockSpec(memory_space=pltpu.SMEM),
                  pl.BlockSpec((tq, ATTN_OUT), qrow), kv_spec, kv_spec, kv_spec, kv_spec,
                  c_spec, c_spec, c_spec, c_spec],
        out_specs=pl.BlockSpec((tq, ATTN_OUT), qrow),
        out_shape=jax.ShapeDtypeStruct((n, ATTN_OUT), BF16),
        compiler_params=_params("parallel", "parallel"),
        name="attn_lat",
    )(sink, q, ka, va, kb, vb, cka, cva, ckb, cvb)


def _ffn_pre(x1, mod_ref, gain_ref, rt_ref, h2_ref, aff_ref):
    h2 = _norm_mod(x1, gain_ref[...], mod_ref[0, 3:4, :], mod_ref[0, 4:5, :])
    h2_ref[...] = h2.astype(h2_ref.dtype)
    logits = _dot3(rt_ref[...], h2, NT)
    e = jnp.exp(logits - logits.max(axis=0, keepdims=True))
    aff_ref[...] = e / e.sum(axis=0, keepdims=True)


def _postmix_kernel(o_ref, x_ref, mod_ref, w_ref, gain_ref, rt_ref, x1_ref, h2_ref, aff_ref):
    x1 = x_ref[...] + mod_ref[0, 2:3, :] * _dot(o_ref[...], w_ref[...])
    x1_ref[...] = x1
    _ffn_pre(x1, mod_ref, gain_ref, rt_ref, h2_ref, aff_ref)


def _postmix(o, x, mod, w_bf, gain, router_t, *, rows_per_cond):
    n = x.shape[0]
    row = lambda i: (i, 0)
    const = lambda i: (0, 0)
    return pl.pallas_call(
        _postmix_kernel,
        grid=(n // TM,),
        in_specs=[pl.BlockSpec((TM, D_MODEL), row),
                  pl.BlockSpec((TM, D_MODEL), row),
                  _mod_spec(rows_per_cond),
                  pl.BlockSpec(w_bf.shape, const),
                  pl.BlockSpec((1, D_MODEL), const),
                  pl.BlockSpec((N_EXPERTS, D_MODEL), const)],
        out_specs=[pl.BlockSpec((TM, D_MODEL), row),
                   pl.BlockSpec((TM, D_MODEL), row),
                   pl.BlockSpec((N_EXPERTS, TM), lambda i: (0, i))],
        out_shape=[jax.ShapeDtypeStruct((n, D_MODEL), F32),
                   jax.ShapeDtypeStruct((n, D_MODEL), BF16),
                   jax.ShapeDtypeStruct((N_EXPERTS, n), F32)],
        compiler_params=_params("parallel"),
        name="postmix_proj",
    )(o, x, mod, w_bf, gain, router_t)


def _ffn_pre_kernel(x1_ref, mod_ref, gain_ref, rt_ref, h2_ref, aff_ref):
    _ffn_pre(x1_ref[...], mod_ref, gain_ref, rt_ref, h2_ref, aff_ref)


def _ffn_pre_call(x1, mod, gain, router_t, *, rows_per_cond):
    n = x1.shape[0]
    row = lambda i: (i, 0)
    const = lambda i: (0, 0)
    return pl.pallas_call(
        _ffn_pre_kernel,
        grid=(n // TM,),
        in_specs=[pl.BlockSpec((TM, D_MODEL), row),
                  _mod_spec(rows_per_cond),
                  pl.BlockSpec((1, D_MODEL), const),
                  pl.BlockSpec((N_EXPERTS, D_MODEL), const)],
        out_specs=[pl.BlockSpec((TM, D_MODEL), row),
                   pl.BlockSpec((N_EXPERTS, TM), lambda i: (0, i))],
        out_shape=[jax.ShapeDtypeStruct((n, D_MODEL), BF16),
                   jax.ShapeDtypeStruct((N_EXPERTS, n), F32)],
        compiler_params=_params("parallel"),
        name="ffn_pre",
    )(x1, mod, gain, router_t)


def _route_kernel(aff_ref, slot_ref, gate_ref, pos_ref, tcnt_ref, *, seq, cap, nseg, tt):
    aff = jnp.concatenate([aff_ref[:, s * seq:(s + 1) * seq] for s in range(nseg)], axis=0)
    rows = aff.shape[0]
    capf = jnp.float32(cap)
    thr_bits = jnp.zeros((rows, 1), I32)

    def enough(cand):
        cnt = jnp.where(aff >= pltpu.bitcast(cand, F32), 1.0, 0.0).sum(axis=1, keepdims=True)
        return cnt >= capf

    for bit in range(30, 0, -2):
        hi, lo = 1 << bit, 1 << (bit - 1)
        both, only_hi, only_lo = thr_bits | hi | lo, thr_bits | hi, thr_bits | lo
        thr_bits = jnp.where(enough(both), both,
                             jnp.where(enough(only_hi), only_hi, jnp.where(enough(only_lo), only_lo, thr_bits)))
    thr_bits = jnp.where(enough(thr_bits | 1), thr_bits | 1, thr_bits)
    thr = pltpu.bitcast(thr_bits, F32)
    gt = aff > thr
    eq = aff == thr
    n_gt = jnp.where(gt, 1.0, 0.0).sum(axis=1, keepdims=True)
    pw = min(seq, 256)
    tri = jnp.where(lax.broadcasted_iota(I32, (pw, pw), 0) < lax.broadcasted_iota(I32, (pw, pw), 1),
                    1.0, 0.0).astype(BF16)

    def count_before(flag):
        ones = jnp.where(flag, 1.0, 0.0)
        parts = []
        run = jnp.zeros((rows, 1), F32)
        for c0 in range(0, seq, pw):
            blk = ones[:, c0:c0 + pw]
            parts.append(_dot(blk.astype(BF16), tri) + run)
            run = run + blk.sum(axis=1, keepdims=True)
        return jnp.concatenate(parts, axis=1) if len(parts) > 1 else parts[0]

    sel = gt | (eq & (count_before(eq) < capf - n_gt))
    rank = count_before(sel)
    expert = lax.broadcasted_iota(I32, (rows, seq), 0) & (N_EXPERTS - 1)
    slot = jnp.where(sel, expert * cap + rank.astype(I32), -1)
    gate = jnp.where(sel, aff, 0.0)
    pos = jnp.where(sel, rank, -1.0)
    nt = seq // tt
    tile_of = jnp.where((lax.broadcasted_iota(I32, (seq, nt), 0) // tt) == lax.broadcasted_iota(I32, (seq, nt), 1),
                        1.0, 0.0).astype(BF16)
    tcnt = _dot(jnp.where(sel, 1.0, 0.0).astype(BF16), tile_of)
    for s in range(nseg):
        rows_s = slice(s * N_EXPERTS, (s + 1) * N_EXPERTS)
        slot_ref[:, s * seq:(s + 1) * seq] = slot[rows_s, :]
        gate_ref[:, s * seq:(s + 1) * seq] = gate[rows_s, :]
        pos_ref[:, s * seq:(s + 1) * seq] = pos[rows_s, :]
        tcnt_ref[0, :, s * nt:(s + 1) * nt] = tcnt[rows_s, :]


def _route(aff_t, seq, cap, nseg, tt):
    n = aff_t.shape[1]
    nt = seq // tt
    steps = n // (nseg * seq)
    spec = pl.BlockSpec((N_EXPERTS, nseg * seq), lambda i: (0, i))
    slot, gate, pos, tcnt = pl.pallas_call(
        functools.partial(_route_kernel, seq=seq, cap=cap, nseg=nseg, tt=tt),
        grid=(steps,),
        in_specs=[spec],
        out_specs=[spec, spec, spec, pl.BlockSpec((1, N_EXPERTS, nseg * nt), lambda i: (i, 0, 0))],
        out_shape=[jax.ShapeDtypeStruct((N_EXPERTS, n), I32), jax.ShapeDtypeStruct((N_EXPERTS, n), F32),
                   jax.ShapeDtypeStruct((N_EXPERTS, n), F32),
                   jax.ShapeDtypeStruct((steps, N_EXPERTS, nseg * nt), F32)],
        compiler_params=_params("parallel"),
        name="route",
    )(aff_t)
    return slot, gate, pos, tcnt.transpose(1, 0, 2).reshape(N_EXPERTS, n // tt)


def _dispatch_kernel(slot_ref, h_ref, x_ref, *, cap, seq):
    m = N_EXPERTS * cap
    slot_id = lax.broadcasted_iota(I32, (m, seq), 0)
    for r in range(h_ref.shape[0] // seq):
        slots = slot_ref[:, r * seq:(r + 1) * seq]
        owner = jnp.broadcast_to(slots[:, None, :], (N_EXPERTS, cap, seq)).reshape(m, seq)
        sel = jnp.where(owner == slot_id, 1.0, 0.0).astype(BF16)
        x_ref[r * m:(r + 1) * m, :] = _dot(sel, h_ref[r * seq:(r + 1) * seq, :]).astype(x_ref.dtype)


def _dispatch(slot, h, seq, cap):
    n = h.shape[0]
    nb = n // seq
    rb = MOE_RB if nb % MOE_RB == 0 else 1
    return pl.pallas_call(
        functools.partial(_dispatch_kernel, cap=cap, seq=seq),
        grid=(nb // rb,),
        in_specs=[pl.BlockSpec((N_EXPERTS, rb * seq), lambda b: (0, b)),
                  pl.BlockSpec((rb * seq, D_MODEL), lambda b: (b, 0))],
        out_specs=pl.BlockSpec((rb * N_EXPERTS * cap, D_MODEL), lambda b: (b, 0)),
        out_shape=jax.ShapeDtypeStruct((nb * N_EXPERTS * cap, D_MODEL), BF16),
        compiler_params=_params("parallel"),
        name="moe_dispatch",
    )(slot, h)


def _window(cum_ref, base, e, cap, k=0):
    lo = _align_down(cum_ref[base + e]) + k * MOE_W
    return lo, pl.multiple_of(jnp.minimum(lo, cap - MOE_W), BF16_ROWS)


def _align_down(rank):
    return rank & ~(BF16_ROWS - 1)


def _extra_windows(cum_ref, base, e):
    lo = _align_down(cum_ref[base + e])
    return (cum_ref[base + N_EXPERTS + e] - lo + (MOE_W - 1)) // MOE_W


def _dispatch_win_kernel(cum_ref, slot_ref, h_ref, x_ref, *, cap, nt):
    b, i = pl.program_id(0), pl.program_id(1)
    tt = h_ref.shape[0]
    base = (b * (nt + 1) + i) * N_EXPERTS
    h = h_ref[...]
    row = lax.broadcasted_iota(I32, (MOE_W, tt), 0)

    @pl.when(i == 0)
    def _():
        x_ref[...] = jnp.zeros_like(x_ref)

    def hits(e, lo, ws):
        srow = slot_ref[e:e + 1, :]
        return (srow == row + (e * cap + ws)) & (srow >= e * cap + lo)

    for grp in range(N_EXPERTS // MOE_EG):
        wins = [(e,) + _window(cum_ref, base, e, cap) for e in range(grp * MOE_EG, (grp + 1) * MOE_EG)]
        sel = jnp.concatenate([hits(e, lo, ws) for e, lo, ws in wins], axis=0)
        x = _dot(jnp.where(sel, 1.0, 0.0).astype(BF16), h)
        for q, (e, lo, ws) in enumerate(wins):
            dst = pl.ds(e * cap + ws, MOE_W)
            x_ref[dst, :] += x[q * MOE_W:(q + 1) * MOE_W].astype(x_ref.dtype)

    for e in range(N_EXPERTS):
        def extra(k, carry, e=e):
            lo, ws = _window(cum_ref, base, e, cap, k)
            x = _dot(jnp.where(hits(e, lo, ws), 1.0, 0.0).astype(BF16), h)
            x_ref[pl.ds(e * cap + ws, MOE_W), :] += x.astype(x_ref.dtype)
            return carry
        lax.fori_loop(1, _extra_windows(cum_ref, base, e), extra, 0)


def _dispatch_win(cum, slot, h, seq, cap):
    n = h.shape[0]
    nb, nt = n // seq, seq // MOE_TT
    return pl.pallas_call(
        functools.partial(_dispatch_win_kernel, cap=cap, nt=nt),
        grid_spec=pltpu.PrefetchScalarGridSpec(
            num_scalar_prefetch=1,
            grid=(nb, nt),
            in_specs=[pl.BlockSpec((N_EXPERTS, MOE_TT), lambda b, i, c: (0, b * nt + i)),
                      pl.BlockSpec((MOE_TT, D_MODEL), lambda b, i, c: (b * nt + i, 0))],
            out_specs=pl.BlockSpec((N_EXPERTS * cap, D_MODEL), lambda b, i, c: (b, 0))),
        out_shape=jax.ShapeDtypeStruct((nb * N_EXPERTS * cap, D_MODEL), BF16),
        compiler_params=_params("parallel", "arbitrary"),
        name="moe_dispatch_win",
    )(cum, slot, h)


FFN_TF = 512
FFN_RC = 512


def _ffn_kernel(xa_ref, xb_ref, wg_ref, wu_ref, wd_ref, ya_ref, yb_ref, acc_ref):
    j = pl.program_id(1)
    last = pl.num_programs(1) - 1
    ra = xa_ref.shape[0] * xa_ref.shape[2]

    def row_chunks(ref):
        nb, _, cap, d = ref.shape
        rc = min(FFN_RC, nb * cap)
        for r0 in range(0, nb * cap, rc):
            if cap >= rc:
                b, c0 = divmod(r0, cap)
                yield r0, rc, (slice(b, b + 1), 0, slice(c0, c0 + rc), slice(None)), (1, rc, d)
            else:
                yield r0, rc, (slice(r0 // cap, (r0 + rc) // cap), 0, slice(None), slice(None)), (rc // cap, cap, d)

    def sweep(first_tile, last_tile):
        wg = wg_ref[0].astype(BF16)
        wu = wu_ref[0].astype(BF16)
        wd = wd_ref[0].astype(BF16)
        d = wd.shape[1]
        for x_ref, y_ref, base in ((xa_ref, ya_ref, 0), (xb_ref, yb_ref, ra)):
            for r0, rc, idx, shape in row_chunks(x_ref):
                x = x_ref[idx].reshape(rc, d)
                g = _dot(x, wg)
                u = _dot(x, wu)
                y = _dot((g * _sigmoid(g) * u).astype(BF16), wd)
                rows = slice(base + r0, base + r0 + rc)
                if not first_tile:
                    y = acc_ref[rows, :] + y
                if last_tile:
                    y_ref[idx] = y.astype(y_ref.dtype).reshape(shape)
                else:
                    acc_ref[rows, :] = y

    if D_FF == FFN_TF:
        sweep(True, True)
    else:
        pl.when(j == 0)(lambda: sweep(True, False))
        pl.when((j > 0) & (j < last))(lambda: sweep(False, False))
        pl.when(j == last)(lambda: sweep(False, True))


def _ffn(xa, xb, w_gate, w_up, w_down):
    ba, _, ca, d = xa.shape
    bb, _, cb, _ = xb.shape
    nj = D_FF // FFN_TF
    xa_spec = pl.BlockSpec((ba, 1, ca, d), lambda e, j: (0, e, 0, 0))
    xb_spec = pl.BlockSpec((bb, 1, cb, d), lambda e, j: (0, e, 0, 0))
    return pl.pallas_call(
        _ffn_kernel,
        grid=(N_EXPERTS, nj),
        in_specs=[xa_spec, xb_spec,
                  pl.BlockSpec((1, d, FFN_TF), lambda e, j: (e, 0, j)),
                  pl.BlockSpec((1, d, FFN_TF), lambda e, j: (e, 0, j)),
                  pl.BlockSpec((1, FFN_TF, d), lambda e, j: (e, j, 0))],
        out_specs=[xa_spec, xb_spec],
        out_shape=[jax.ShapeDtypeStruct(xa.shape, BF16), jax.ShapeDtypeStruct(xb.shape, BF16)],
        scratch_shapes=[pltpu.VMEM((ba * ca + bb * cb, d), F32)],
        compiler_params=_params("parallel", "arbitrary"),
        name="moe_ffn",
    )(xa, xb, w_gate, w_up, w_down)


def _expand(vals_bf, first_expert, width, total):
    e_of_lane = first_expert + lax.broadcasted_iota(I32, (N_EXPERTS, total), 1) // width
    pick = jnp.where(lax.broadcasted_iota(I32, (N_EXPERTS, total), 0) == e_of_lane, 1.0, 0.0).astype(BF16)
    return _dot(vals_bf, pick, TN)


def _combine_kernel(pos_ref, gate_ref, y_ref, x_ref, mod_ref, o_ref, *, cap, seq):
    m = N_EXPERTS * cap
    rank = (lax.broadcasted_iota(I32, (seq, m), 1) % cap).astype(F32)
    for r in range(x_ref.shape[0] // seq):
        rows = slice(r * seq, (r + 1) * seq)
        pos = _expand(pos_ref[:, rows].astype(BF16), 0, cap, m)
        gate = _expand(gate_ref[:, rows].astype(BF16), 0, cap, m)
        w = jnp.where(pos == rank, gate, 0.0).astype(BF16)
        o_ref[rows, :] = x_ref[rows, :] + mod_ref[0, 5:6, :] * _dot(w, y_ref[r * m:(r + 1) * m, :])


def _combine(pos, gate, y, x, mod, *, seq, cap):
    n = x.shape[0]
    nb = n // seq
    rb = MOE_RB if nb % MOE_RB == 0 else 1
    row = lambda b: (b, 0)
    return pl.pallas_call(
        functools.partial(_combine_kernel, cap=cap, seq=seq),
        grid=(nb // rb,),
        in_specs=[pl.BlockSpec((N_EXPERTS, rb * seq), lambda b: (0, b)),
                  pl.BlockSpec((N_EXPERTS, rb * seq), lambda b: (0, b)),
                  pl.BlockSpec((rb * N_EXPERTS * cap, D_MODEL), row),
                  pl.BlockSpec((rb * seq, D_MODEL), row),
                  pl.BlockSpec((1, 8, D_MODEL), lambda b: (0, 0, 0))],
        out_specs=pl.BlockSpec((rb * seq, D_MODEL), row),
        out_shape=jax.ShapeDtypeStruct((n, D_MODEL), F32),
        compiler_params=_params("parallel"),
        name="moe_combine",
    )(pos, gate, y, x, mod)


def _combine_win_kernel(cum_ref, pos_ref, gate_ref, y_ref, x_ref, mod_ref, o_ref, acc_ref, *, cap, nt):
    b, i = pl.program_id(0), pl.program_id(1)
    tt = x_ref.shape[0]
    base = (b * (nt + 1) + i) * N_EXPERTS
    width = MOE_EG * MOE_W
    posb = pos_ref[...].astype(BF16)
    gateb = gate_ref[...].astype(BF16)
    lane = lax.broadcasted_iota(I32, (1, width), 1)
    offset = (lane % MOE_W).astype(F32)
    acc = jnp.zeros((tt, D_MODEL), F32)
    for grp in range(N_EXPERTS // MOE_EG):
        wins = [(e,) + _window(cum_ref, base, e, cap) for e in range(grp * MOE_EG, (grp + 1) * MOE_EG)]
        lo_l = jnp.zeros((1, width), F32)
        ws_l = jnp.zeros((1, width), F32)
        for q, (e, lo, ws) in enumerate(wins):
            mine = lane // MOE_W == q
            lo_l = jnp.where(mine, lo.astype(F32), lo_l)
            ws_l = jnp.where(mine, ws.astype(F32), ws_l)
        pos = _expand(posb, grp * MOE_EG, MOE_W, width)
        gate = _expand(gateb, grp * MOE_EG, MOE_W, width)
        w = jnp.where((pos - ws_l == offset) & (pos >= lo_l), gate, 0.0).astype(BF16)
        ywin = jnp.concatenate([y_ref[pl.ds(e * cap + ws, MOE_W), :] for e, lo, ws in wins], axis=0)
        acc = acc + _dot(w, ywin)
    acc_ref[...] = acc

    offset_rows = lax.broadcasted_iota(I32, (MOE_W, tt), 0).astype(F32)
    for e in range(N_EXPERTS):
        def extra(k, carry, e=e):
            lo, ws = _window(cum_ref, base, e, cap, k)
            pos = pos_ref[e:e + 1, :]
            gate = gate_ref[e:e + 1, :].astype(BF16).astype(F32)
            w_t = jnp.where((pos - ws.astype(F32) == offset_rows) & (pos >= lo.astype(F32)), gate, 0.0).astype(BF16)
            acc_ref[...] += _dot(w_t, y_ref[pl.ds(e * cap + ws, MOE_W), :], TN)
            return carry
        lax.fori_loop(1, _extra_windows(cum_ref, base, e), extra, 0)

    o_ref[...] = x_ref[...] + mod_ref[0, 5:6, :] * acc_ref[...]


def _combine_win(cum, pos, gate, y, x, mod, *, seq, cap):
    n = x.shape[0]
    nb, nt = n // seq, seq // MOE_TT
    row = lambda b, i, c: (b * nt + i, 0)
    return pl.pallas_call(
        functools.partial(_combine_win_kernel, cap=cap, nt=nt),
        grid_spec=pltpu.PrefetchScalarGridSpec(
            num_scalar_prefetch=1,
            grid=(nb, nt),
            in_specs=[pl.BlockSpec((N_EXPERTS, MOE_TT), lambda b, i, c: (0, b * nt + i)),
                      pl.BlockSpec((N_EXPERTS, MOE_TT), lambda b, i, c: (0, b * nt + i)),
                      pl.BlockSpec((N_EXPERTS * cap, D_MODEL), lambda b, i, c: (b, 0)),
                      pl.BlockSpec((MOE_TT, D_MODEL), row),
                      pl.BlockSpec((1, 8, D_MODEL), lambda b, i, c: (1 + b, 0, 0))],
            out_specs=pl.BlockSpec((MOE_TT, D_MODEL), row),
            scratch_shapes=[pltpu.VMEM((MOE_TT, D_MODEL), F32)]),
        out_shape=jax.ShapeDtypeStruct((n, D_MODEL), F32),
        compiler_params=_params("parallel", "parallel"),
        name="moe_combine_win",
    )(cum, pos, gate, y, x, mod)


def _moe_pair(hp, affp, x1p, hs, affs, x1s, mod, w_gate, w_up, w_down, seq_p, seq_s):
    n_p, n_s = hp.shape[0], hs.shape[0]
    nb_p, nb_s = n_p // seq_p, n_s // seq_s
    cap_p = EC_FACTOR * seq_p // N_EXPERTS
    cap_s = EC_FACTOR * seq_s // N_EXPERTS
    assert N_EXPERTS * cap_p <= 512 and cap_s >= MOE_W and cap_s % BF16_ROWS == 0 and seq_s % MOE_TT == 0
    slot_p, gate_p, pos_p, _ = _route(affp, seq_p, cap_p, nseg=min(8, nb_p), tt=seq_p)
    slot_s, gate_s, pos_s, tcnt = _route(affs, seq_s, cap_s, nseg=min(4, nb_s), tt=MOE_TT)
    nt = seq_s // MOE_TT
    counts = tcnt.T.reshape(nb_s, nt, N_EXPERTS).astype(I32)
    cum = jnp.concatenate([jnp.zeros((nb_s, 1, N_EXPERTS), I32), jnp.cumsum(counts, axis=1)], axis=1).reshape(-1)
    xp = _dispatch(slot_p, hp, seq_p, cap_p).reshape(nb_p, N_EXPERTS, cap_p, D_MODEL)
    xs = _dispatch_win(cum, slot_s, hs, seq_s, cap_s).reshape(nb_s, N_EXPERTS, cap_s, D_MODEL)
    ys, yp = _ffn(xs, xp, w_gate, w_up, w_down)
    outp = _combine(pos_p, gate_p, yp.reshape(-1, D_MODEL), x1p, mod, seq=seq_p, cap=cap_p)
    outs = _combine_win(cum, pos_s, gate_s, ys.reshape(-1, D_MODEL), x1s, mod, seq=seq_s, cap=cap_s)
    return outp, outs


def _cond_blocks(mod_ref, tc, cols_per_cond):
    if cols_per_cond is None:
        return [(slice(0, tc), mod_ref[0])]
    first = 1 + pl.program_id(1) * (tc // cols_per_cond)
    return [(slice(s * cols_per_cond, (s + 1) * cols_per_cond), mod_ref[first + s]) for s in range(tc // cols_per_cond)]


def _ssm_in_kernel(x_ref, mod_ref, gain_ref, wt_ref, ut_ref, *, cols_per_cond):
    parts = [_norm_mod(x_ref[0, rows, :], gain_ref[...], m[0:1, :], m[1:2, :]).astype(BF16)
             for rows, m in _cond_blocks(mod_ref, x_ref.shape[1], cols_per_cond)]
    h = jnp.concatenate(parts, axis=0) if len(parts) > 1 else parts[0]
    ut_ref[0] = _dot(wt_ref[...], h, NT)


def _ssm_in(xperm, mod, gain, wt_bf, *, cols_per_cond, tc):
    l, bk, d = xperm.shape
    assert cols_per_cond is None or tc % cols_per_cond == 0
    return pl.pallas_call(
        functools.partial(_ssm_in_kernel, cols_per_cond=cols_per_cond),
        grid=(l, bk // tc),
        in_specs=[pl.BlockSpec((1, tc, d), lambda j, i: (j, i, 0)),
                  pl.BlockSpec(mod.shape, lambda j, i: (0, 0, 0)),
                  pl.BlockSpec((1, d), lambda j, i: (0, 0)),
                  pl.BlockSpec((d, d), lambda j, i: (0, 0))],
        out_specs=pl.BlockSpec((1, d, tc), lambda j, i: (j, 0, i)),
        out_shape=jax.ShapeDtypeStruct((l, d, bk), F32),
        compiler_params=_params("parallel", "parallel"),
        name="ssm_in",
    )(xperm, mod, gain, wt_bf)


def _ssm_core_kernel(utp_ref, uts_ref, lamp_ref, c_ref, bt_ref, dsk_ref, h0_ref, ytp_ref, yts_ref, fs_ref,
                     *, kp, ks, nbp, nbs):
    rows = SSM_ROWS
    p = SSM_STATE
    lc = SSM_CHUNK
    ri = lax.broadcasted_iota(I32, (rows, rows), 0)
    cj = lax.broadcasted_iota(I32, (rows, rows), 1)
    causal = ri // SSM_GROUP >= cj // SSM_GROUP
    anticausal = cj // SSM_GROUP >= ri // SSM_GROUP
    diag = ri == cj
    leftc = lax.broadcasted_iota(I32, (lc, 128), 1) < p
    leftg = lax.broadcasted_iota(I32, (SSM_GROUP, 128), 1) < p
    nrow = lax.broadcasted_iota(I32, (lc, 128), 0).astype(F32)
    eye = lax.broadcasted_iota(I32, (p, 128), 0) == lax.broadcasted_iota(I32, (p, 128), 1)

    def cmul(ar, ai, xr, xi):
        return ar * xr - ai * xi, ar * xi + ai * xr

    def expand_rows(t):
        return jnp.broadcast_to(t[:, None, :], (lc, SSM_GROUP, 128)).reshape(rows, 128)

    def tile_rows(t):
        return jnp.broadcast_to(t[None, :, :], (lc, SSM_GROUP, 128)).reshape(rows, 128)

    def to_col(row):
        return jnp.where(eye, jnp.broadcast_to(row, (p, 128)), 0.0).sum(axis=1, keepdims=True)

    def operands(gg, d):
        lp = lamp_ref[gg, d]
        lre, lim = lp[0:1], lp[1:2]
        dt = jnp.exp(lp[2:3])
        a, th = lre * dt, lim * dt
        ang = nrow * th
        cs, sn = jnp.cos(ang), jnp.sin(ang)
        ep, em = jnp.exp(nrow * a), jnp.exp(-(nrow * a))
        pr, pi = ep * cs, ep * sn
        nr, ni = em * cs, -(em * sn)
        l1r, l1i = pr[1:2], pi[1:2]
        lmr, lmi = pr[lc - 1:lc], pi[lc - 1:lc]
        llr, lli = cmul(lmr, lmi, l1r, l1i)
        den = lre * lre + lim * lim
        cr = ((l1r - 1.0) * lre + l1i * lim) / den
        ci = (l1i * lre - (l1r - 1.0) * lim) / den
        btr, bti = bt_ref[gg, d, 0], bt_ref[gg, d, 1]
        bbr, bbi = cr * btr - ci * bti, cr * bti + ci * btr
        u1 = tile_rows(jnp.where(leftg, bbr, bbi))
        u2 = tile_rows(jnp.where(leftg, bbi, bbr))
        c1 = tile_rows(c_ref[gg, d, 0])
        c2 = tile_rows(c_ref[gg, d, 1])

        def left_form(xr, xi):
            return (c1 * expand_rows(jnp.where(leftc, xr, -xi))
                    + c2 * expand_rows(jnp.where(leftc, -xi, -xr)))

        def right_form(xr, xi):
            return u1 * expand_rows(xr) + u2 * expand_rows(jnp.where(leftc, -xi, xi))

        if d == 0:
            al = left_form(pr, pi)
            brt = right_form(nr, ni)
            rrt = right_form(*cmul(lmr, lmi, nr, ni))
            qq = left_form(*cmul(l1r, l1i, pr, pi))
            mat = jnp.where(causal, _dot3(al, brt, NT), 0.0)
        else:
            al = left_form(nr, ni)
            brt = right_form(pr, pi)
            rrt = brt
            qq = left_form(*cmul(llr, lli, nr, ni))
            mat = jnp.where(anticausal, _dot3(al, brt, NT), 0.0)
        return mat, rrt.T, qq, to_col(llr), to_col(lli)

    def scan(sr, si, lr, li, h0r, h0i, reverse, nchunk, nbatch, sel):
        bk = sr.shape[1]
        lane = lax.broadcasted_iota(I32, (p, bk), 1)
        kidx = lane & (nchunk - 1)
        edge = (nchunk - 1) if reverse else 0
        if h0r is not None:
            h0cr = jnp.zeros((p, bk), F32)
            h0ci = jnp.zeros((p, bk), F32)
            for b in range(nbatch):
                at = lane == (b * nchunk + edge)
                h0cr = jnp.where(at, h0r[:, b:b + 1], h0cr)
                h0ci = jnp.where(at, h0i[:, b:b + 1], h0ci)
            ar, ai = cmul(lr, li, h0cr, h0ci)
            er, ei = sr + ar, si + ai
        else:
            er, ei = sr, si
        ar, ai = lr, li
        s = 1
        while s < nchunk:
            if reverse:
                ok = kidx < nchunk - s
                tr, ti = pltpu.roll(er, bk - s, 1), pltpu.roll(ei, bk - s, 1)
            else:
                ok = kidx >= s
                tr, ti = pltpu.roll(er, s, 1), pltpu.roll(ei, s, 1)
            tr = jnp.where(ok, tr, 0.0)
            ti = jnp.where(ok, ti, 0.0)
            dr, di = cmul(ar, ai, tr, ti)
            er, ei = er + dr, ei + di
            ar, ai = cmul(ar, ai, ar, ai)
            s *= 2
        if reverse:
            inner = kidx < nchunk - 1
            hr, hi = pltpu.roll(er, bk - 1, 1), pltpu.roll(ei, bk - 1, 1)
        else:
            inner = kidx >= 1
            hr, hi = pltpu.roll(er, 1, 1), pltpu.roll(ei, 1, 1)
        hr = jnp.where(inner, hr, h0cr if h0r is not None else 0.0)
        hi = jnp.where(inner, hi, h0ci if h0r is not None else 0.0)
        fin = None if sel is None else (_sel_dot_t(sel, er), _sel_dot_t(sel, ei))
        return hr, hi, fin

    def final_selectors(bk, nchunk, nbatch):
        bat = lax.broadcasted_iota(I32, (nbatch, bk), 0)
        col = lax.broadcasted_iota(I32, (nbatch, bk), 1)
        last = jnp.where(col == bat * nchunk + (nchunk - 1), 1.0, 0.0).astype(BF16)
        first = jnp.where(col == bat * nchunk, 1.0, 0.0).astype(BF16)
        return last, first

    sel_last, sel_first = final_selectors(utp_ref.shape[2], kp, nbp)

    for gg in range(SSM_GB):
        mf, rf, qf, lfr, lfi = operands(gg, 0)
        mb, rb, qb, lbr, lbi = operands(gg, 1)
        skip = jnp.where(diag, jnp.broadcast_to(dsk_ref[gg], (rows, rows)), 0.0)
        stack = jnp.concatenate([mf + mb + skip, rf, rb], axis=0)
        qq = jnp.concatenate([qf, qb], axis=1)
        h0 = h0_ref[gg]
        for ut_ref, yt_ref, nchunk, nbatch, latent in ((utp_ref, ytp_ref, kp, nbp, False),
                                                       (uts_ref, yts_ref, ks, nbs, True)):
            bk = ut_ref.shape[2]
            x = ut_ref[:, gg * SSM_GROUP:(gg + 1) * SSM_GROUP, :].reshape(rows, bk)
            res = _mm(stack, x, SSM_PASSES)
            hfr, hfi, ff = scan(res[rows:rows + p], res[rows + p:rows + 2 * p], lfr, lfi,
                                h0[0] if latent else None, h0[1] if latent else None,
                                False, nchunk, nbatch, None if latent else sel_last)
            hbr, hbi, fb = scan(res[rows + 2 * p:rows + 3 * p], res[rows + 3 * p:rows + 4 * p], lbr, lbi,
                                h0[2] if latent else None, h0[3] if latent else None,
                                True, nchunk, nbatch, None if latent else sel_first)
            states = jnp.concatenate([hfr, hfi, hbr, hbi], axis=0)
            y = res[:rows] + _mm(qq, states, SSM_PASSES)
            yt_ref[:, gg * SSM_GROUP:(gg + 1) * SSM_GROUP, :] = y.reshape(lc, SSM_GROUP, bk)
            if not latent:
                fs_ref[gg, 0] = ff[0]
                fs_ref[gg, 1] = ff[1]
                fs_ref[gg, 2] = fb[0]
                fs_ref[gg, 3] = fb[1]


def _ssm_core(utp, uts, ops, h0, *, kp, ks, nbp, nbs):
    lamp, c2, bt2, dsk = ops
    l, d, bkp = utp.shape
    bks = uts.shape[2]
    g = SSM_GROUPS
    gb = SSM_GB
    lead4 = lambda i: (i, 0, 0, 0)
    lead5 = lambda i: (i, 0, 0, 0, 0)
    ut_spec = lambda bk: pl.BlockSpec((l, gb * SSM_GROUP, bk), lambda i: (0, i, 0))
    return pl.pallas_call(
        functools.partial(_ssm_core_kernel, kp=kp, ks=ks, nbp=nbp, nbs=nbs),
        grid=(g // gb,),
        in_specs=[ut_spec(bkp), ut_spec(bks),
                  pl.BlockSpec((gb, 2, 8, 128), lead4),
                  pl.BlockSpec((gb, 2, 2, SSM_GROUP, 128), lead5),
                  pl.BlockSpec((gb, 2, 2, SSM_GROUP, 128), lead5),
                  pl.BlockSpec((gb, 1, SSM_ROWS), lambda i: (i, 0, 0)),
                  pl.BlockSpec((gb, 4, SSM_STATE, nbs), lead4)],
        out_specs=[ut_spec(bkp), ut_spec(bks),
                   pl.BlockSpec((gb, 4, nbp, SSM_STATE), lead4)],
        out_shape=[jax.ShapeDtypeStruct((l, d, bkp), F32),
                   jax.ShapeDtypeStruct((l, d, bks), F32),
                   jax.ShapeDtypeStruct((g, 4, nbp, SSM_STATE), F32)],
        compiler_params=_params("parallel"),
        name="ssm_core",
    )(utp, uts, lamp, c2, bt2, dsk, h0)


def _ssm_out_kernel(yt_ref, x_ref, mod_ref, w_ref, x1_ref, *, cols_per_cond):
    y = yt_ref[0].T
    act = 0.5 * y * (1.0 + jnp.tanh(0.7978845608028654 * (y + 0.044715 * (y * y * y))))
    ag = _dot(act.astype(BF16), w_ref[...])
    d = x1_ref.shape[2]
    mix = ag[:, :d] * _sigmoid(ag[:, d:])
    for rows, m in _cond_blocks(mod_ref, x_ref.shape[1], cols_per_cond):
        x1_ref[0, rows, :] = x_ref[0, rows, :] + m[2:3, :] * mix[rows, :]


def _ssm_out(yt, xperm, mod, w_bf, *, cols_per_cond, tc):
    l, d, bk = yt.shape
    assert cols_per_cond is None or tc % cols_per_cond == 0
    return pl.pallas_call(
        functools.partial(_ssm_out_kernel, cols_per_cond=cols_per_cond),
        grid=(l, bk // tc),
        in_specs=[pl.BlockSpec((1, d, tc), lambda j, i: (j, 0, i)),
                  pl.BlockSpec((1, tc, d), lambda j, i: (j, i, 0)),
                  pl.BlockSpec(mod.shape, lambda j, i: (0, 0, 0)),
                  pl.BlockSpec((d, 2 * d), lambda j, i: (0, 0))],
        out_specs=pl.BlockSpec((1, tc, d), lambda j, i: (j, i, 0)),
        out_shape=jax.ShapeDtypeStruct((l, bk, d), F32),
        compiler_params=_params("parallel", "parallel"),
        name="ssm_out",
    )(yt, xperm, mod, w_bf)


def _ssm_operand_params(lam_re, lam_im, b_re, b_im, c_re, c_im, log_dt, d_skip):
    g, l = SSM_GROUPS, SSM_CHUNK
    dup = lambda t: jnp.concatenate([t, t], axis=-1)
    lamp = jnp.stack([lam_re, lam_im, jnp.broadcast_to(log_dt[..., None], lam_re.shape)], axis=2)
    lamp = dup(jnp.pad(lamp, ((0, 0), (0, 0), (0, 5), (0, 0)))).transpose(1, 0, 2, 3)
    c2 = dup(jnp.stack([c_re, c_im], axis=2)).transpose(1, 0, 2, 3, 4)
    bt2 = dup(jnp.stack([jnp.swapaxes(b_re, -1, -2), jnp.swapaxes(b_im, -1, -2)], axis=2)).transpose(1, 0, 2, 3, 4)
    dsk = jnp.tile(d_skip.reshape(g, 1, SSM_GROUP), (1, 1, l))
    return lamp, c2, bt2, dsk


def _to_chunks(x, nb, seq):
    k = seq // SSM_CHUNK
    return x.reshape(nb, k, SSM_CHUNK, -1).transpose(2, 0, 1, 3).reshape(SSM_CHUNK, nb * k, -1)


def _from_chunks(x, nb, seq):
    k = seq // SSM_CHUNK
    return x.reshape(SSM_CHUNK, nb, k, -1).transpose(1, 2, 0, 3).reshape(nb * seq, -1)


def _ssm_layer(xp, xs, mod, gain, wt_bf, ops, w_out_bf, h0, *, nbp, sp, nbs, ss):
    kp, ks = sp // SSM_CHUNK, ss // SSM_CHUNK
    tcp, tcs = min(SSM_TC, nbp * kp), min(SSM_TC, nbs * ks)
    xcp, xcs = _to_chunks(xp, nbp, sp), _to_chunks(xs, nbs, ss)
    utp = _ssm_in(xcp, mod, gain, wt_bf, cols_per_cond=None, tc=tcp)
    uts = _ssm_in(xcs, mod, gain, wt_bf, cols_per_cond=ks, tc=tcs)
    ytp, yts, fs = _ssm_core(utp, uts, ops, h0, kp=kp, ks=ks, nbp=nbp, nbs=nbs)
    x1p = _from_chunks(_ssm_out(ytp, xcp, mod, w_out_bf, cols_per_cond=None, tc=tcp), nbp, sp)
    x1s = _from_chunks(_ssm_out(yts, xcs, mod, w_out_bf, cols_per_cond=ks, tc=tcs), nbs, ss)
    return x1p, x1s, fs


def _rope_tables(seq):
    t = jnp.arange(seq)
    row = (t // GRID_W).astype(F32)
    col = (t % GRID_W).astype(F32)
    n_freq = HEAD_DIM // 4
    inv_freq = ROPE_THETA ** (-jnp.arange(n_freq, dtype=F32) / n_freq)
    ang = jnp.concatenate([row[:, None] * inv_freq, col[:, None] * inv_freq], axis=-1)
    cos = jnp.repeat(jnp.cos(ang), 2, axis=-1)
    sin = jnp.repeat(jnp.sin(ang), 2, axis=-1)
    sign = jnp.tile(jnp.array([-1.0, 1.0], F32), HEAD_DIM // 2)
    return jnp.tile(cos, (1, 4)), jnp.tile(sin * sign, (1, 4))


def _head_gains(qn_a, kn_a, qn_b, kn_b):
    scale = HEAD_DIM ** -0.5 * LOG2E
    ones = jnp.ones((N_KV * HEAD_DIM,), F32)
    return jnp.concatenate([jnp.tile(qn_a, N_HEADS) * scale, jnp.tile(kn_a, N_KV), ones,
                            jnp.tile(qn_b, N_HEADS) * scale, jnp.tile(kn_b, N_KV), ones]).reshape(1, QKV_COLS)


def kernel(x_prompt, x_sample, c, cache_k_a_l0, cache_v_a_l0, cache_k_b_l0, cache_v_b_l0, state_ssm_re_l1, state_ssm_im_l1, c_ctx, mod_w_l0, mod_b_l0, norm_mix_l0, attn_w_in_l0, q_norm_a_l0, k_norm_a_l0, q_norm_b_l0, k_norm_b_l0, sink_b_l0, attn_w_out_l0, norm_ffn_l0, router_l0, moe_w_gate_l0, moe_w_up_l0, moe_w_down_l0, mod_w_l1, mod_b_l1, norm_mix_l1, ssm_w_in_l1, ssm_lambda_re_l1, ssm_lambda_im_l1, ssm_b_re_l1, ssm_b_im_l1, ssm_c_re_l1, ssm_c_im_l1, ssm_log_dt_l1, ssm_d_l1, ssm_w_out_l1, norm_ffn_l1, router_l1, moe_w_gate_l1, moe_w_up_l1, moe_w_down_l1):
    bp, sp, d = x_prompt.shape
    bs, ss, _ = x_sample.shape
    past = cache_k_a_l0.shape[1]
    assert d == D_MODEL and bs <= 7 and (bp * sp) % TM == 0 and TM % sp == 0 and ss % TM == 0
    xp = x_prompt.reshape(bp * sp, d)
    xs = x_sample.reshape(bs * ss, d)
    cond8 = jnp.concatenate([c_ctx[None], c, jnp.zeros((7 - bs, d), F32)], axis=0)
    row1 = lambda v: v.reshape(1, -1)

    m0, m1 = _adaln(cond8, mod_w_l0, mod_b_l0, mod_w_l1, mod_b_l1)
    mod0, mod1 = _mod_rows(m0), _mod_rows(m1)
    w_in = attn_w_in_l0.astype(BF16)
    hgain = _head_gains(q_norm_a_l0, k_norm_a_l0, q_norm_b_l0, k_norm_b_l0)
    lane = np.arange(256)
    bd = jnp.asarray((lane[:, None] // HEAD_DIM == lane[None, :] // HEAD_DIM) / HEAD_DIM, BF16)
    cos_t, sin_t = _rope_tables(ss)
    qp, kap, vap, kbp, vbp = _qkv(xp, mod0, row1(norm_mix_l0), w_in, hgain, bd, cos_t, sin_t,
                                  rows_per_cond=None, seq=sp, rope=False, kv_dtype=F32, transposed_kv=True)
    qs, kas, vas, kbs, vbs = _qkv(xs, mod0, row1(norm_mix_l0), w_in, hgain, bd, cos_t, sin_t,
                                  rows_per_cond=ss, seq=ss, rope=True, kv_dtype=BF16, transposed_kv=False)
    op = _attn_ctx(sink_b_l0, qp, kap, vap, kbp, vbp, sp)
    cache = lambda t: t.reshape(bs, past, N_KV * HEAD_DIM)
    os_ = _attn_lat(sink_b_l0, qs, kas, vas, kbs, vbs, cache(cache_k_a_l0), cache(cache_v_a_l0),
                    cache(cache_k_b_l0), cache(cache_v_b_l0), ss)
    w_out = attn_w_out_l0.astype(BF16)
    x1p, hp, affp = _postmix(op, xp, mod0, w_out, row1(norm_ffn_l0), router_l0.T, rows_per_cond=None)
    x1s, hs, affs = _postmix(os_, xs, mod0, w_out, row1(norm_ffn_l0), router_l0.T, rows_per_cond=ss)
    xp, xs = _moe_pair(hp, affp, x1p, hs, affs, x1s, mod0, moe_w_gate_l0, moe_w_up_l0, moe_w_down_l0, sp, ss)

    ops = _ssm_operand_params(ssm_lambda_re_l1, ssm_lambda_im_l1, ssm_b_re_l1, ssm_b_im_l1, ssm_c_re_l1, ssm_c_im_l1,
                              ssm_log_dt_l1, ssm_d_l1)
    wt = ssm_w_in_l1.T.astype(BF16)
    w_so = ssm_w_out_l1.astype(BF16)
    h0 = jnp.stack([state_ssm_re_l1[:, 0], state_ssm_im_l1[:, 0], state_ssm_re_l1[:, 1], state_ssm_im_l1[:, 1]],
                   axis=0).transpose(2, 0, 3, 1)
    x1p, x1s, fsp = _ssm_layer(xp, xs, mod1, row1(norm_mix_l1), wt, ops, w_so, h0, nbp=bp, sp=sp, nbs=bs, ss=ss)
    hp, affp = _ffn_pre_call(x1p, mod1, row1(norm_ffn_l1), router_l1.T, rows_per_cond=None)
    hs, affs = _ffn_pre_call(x1s, mod1, row1(norm_ffn_l1), router_l1.T, rows_per_cond=ss)
    xp, xs = _moe_pair(hp, affp, x1p, hs, affs, x1s, mod1, moe_w_gate_l1, moe_w_up_l1, moe_w_down_l1, sp, ss)

    kv_out = lambda t: t.reshape(bp, N_KV, HEAD_DIM, sp).transpose(0, 3, 1, 2)
    fin = fsp.transpose(2, 1, 0, 3)
    ssm_re = jnp.stack([fin[:, 0], fin[:, 2]], axis=1)
    ssm_im = jnp.stack([fin[:, 1], fin[:, 3]], axis=1)
    return (xp.reshape(bp, sp, d), xs.reshape(bs, ss, d), kv_out(kap), kv_out(vap), kv_out(kbp), kv_out(vbp),
            ssm_re, ssm_im)
```

```python
import functools

import jax
import jax.numpy as jnp
import numpy as np
from jax import lax
from jax.experimental import pallas as pl
from jax.experimental.pallas import tpu as pltpu

F32, BF16, I32 = jnp.float32, jnp.bfloat16, jnp.int32

D_MODEL = 1024
GRID_W = 64
HEAD_DIM = 64
N_HEADS = 8
N_KV = 2
WINDOW = 128
ROPE_THETA = 10000.0
SSM_GROUP = 16
SSM_GROUPS = D_MODEL // SSM_GROUP
SSM_STATE = 64
N_EXPERTS = 16
EC_FACTOR = 2
D_FF = 2 * D_MODEL
EPS = 1e-6
NEG_INF = -1e30
LOG2E = 1.4426950408889634
QKV_COLS = 2 * (N_HEADS + 2 * N_KV) * HEAD_DIM
ATTN_OUT = 2 * N_HEADS * HEAD_DIM

SSM_CHUNK = 16
SSM_ROWS = SSM_CHUNK * SSM_GROUP
SSM_GB = 4
SSM_PASSES = 1

SSM_TC = 512

TM = 1024

BAND_TQ = 256
CTX_RB = 4

MOE_TT = 256
BF16_ROWS = 16
MOE_W = 64
MOE_EG = 4
MOE_RB = 4
VMEM_LIMIT = 56 * 1024 * 1024

NN = (((1,), (0,)), ((), ()))
NT = (((1,), (1,)), ((), ()))
TN = (((0,), (0,)), ((), ()))


def _dot(a, b, dims=NN):
    return lax.dot_general(a, b, dims, preferred_element_type=F32)


def _split2(x):
    hi = x.astype(BF16)
    lo = (x - hi.astype(F32)).astype(BF16)
    return hi, lo


def _split3(x):
    hi = x.astype(BF16)
    r = x - hi.astype(F32)
    mid = r.astype(BF16)
    lo = (r - mid.astype(F32)).astype(BF16)
    return hi, mid, lo


def _dot3(a, b, dims=NN):
    ah, al = _split2(a)
    bh, bl = _split2(b)
    return _dot(ah, bh, dims) + (_dot(ah, bl, dims) + _dot(al, bh, dims))


def _mm(a, b, passes):
    if passes == 1:
        return _dot(a.astype(BF16), b.astype(BF16))
    return _dot3(a, b)


def _sel_dot_t(sel, x):
    hi, mid, lo = _split3(x)
    return _dot(sel, hi, NT) + (_dot(sel, mid, NT) + _dot(sel, lo, NT))


def _sigmoid(x):
    return 1.0 / (1.0 + jnp.exp(-x))


def _norm_mod(x, gain, shift, scale):
    ms = jnp.mean(x * x, axis=-1, keepdims=True)
    y = x * lax.rsqrt(ms + EPS) * gain
    return y * (1.0 + scale) + shift


def _params(*sem):
    return pltpu.CompilerParams(dimension_semantics=sem, vmem_limit_bytes=VMEM_LIMIT)


def _adaln_kernel(c_ref, w_ref, b_ref, o_ref):
    c = c_ref[...]
    s = c * _sigmoid(c)
    o_ref[...] = _dot3(s, w_ref[...]) + b_ref[...]


def _adaln(cond8, w_mod, b_mod):
    d, e = w_mod.shape
    tn = 1536
    return pl.pallas_call(
        _adaln_kernel,
        grid=(e // tn,),
        in_specs=[pl.BlockSpec((8, d), lambda j: (0, 0)),
                  pl.BlockSpec((d, tn), lambda j: (0, j)),
                  pl.BlockSpec((1, tn), lambda j: (0, j))],
        out_specs=pl.BlockSpec((8, tn), lambda j: (0, j)),
        out_shape=jax.ShapeDtypeStruct((8, e), F32),
        compiler_params=_params("parallel"),
        name="adaln",
    )(cond8, w_mod, b_mod.reshape(1, e))


def _mod_rows(cond8, w_mod, b_mod):
    m = _adaln(cond8, w_mod, b_mod).reshape(8, 6, D_MODEL)
    return jnp.pad(m, ((0, 0), (0, 2), (0, 0)))


def _mod_spec(rows_per_cond):
    if rows_per_cond is None:
        return pl.BlockSpec((1, 8, D_MODEL), lambda i: (0, 0, 0))
    return pl.BlockSpec((1, 8, D_MODEL), lambda i: (1 + (i * TM) // rows_per_cond, 0, 0))


def _qkv_kernel(x_ref, mod_ref, gain_ref, w_ref, hg_ref, bd_ref, cos_ref, sin_ref,
                q_ref, ka_ref, va_ref, kb_ref, vb_ref, *, rope, transposed_kv):
    h = _norm_mod(x_ref[...], gain_ref[...], mod_ref[0, 0:1, :], mod_ref[0, 1:2, :])
    proj = _dot(h.astype(BF16), w_ref[...])
    bd = bd_ref[...]

    def head_norm(blk, g):
        ms = _dot((blk * blk).astype(BF16), bd)
        return blk * lax.rsqrt(ms + EPS) * g

    def rotary(blk):
        w = blk.shape[1]
        even = (lax.broadcasted_iota(I32, blk.shape, 1) & 1) == 0
        swapped = jnp.where(even, pltpu.roll(blk, w - 1, 1), pltpu.roll(blk, 1, 1))
        return blk * cos_ref[:, :w] + swapped * sin_ref[:, :w]

    def qk(c0):
        blk = head_norm(proj[:, c0:c0 + 256], hg_ref[:, c0:c0 + 256])
        return rotary(blk) if rope else blk

    q_ref[:, 0:256] = qk(0).astype(q_ref.dtype)
    q_ref[:, 256:512] = qk(256).astype(q_ref.dtype)
    q_ref[:, 512:768] = qk(768).astype(q_ref.dtype)
    q_ref[:, 768:1024] = qk(1024).astype(q_ref.dtype)
    kva = qk(512)
    kvb = qk(1280)
    outs = ((ka_ref, kva[:, :128]), (va_ref, proj[:, 640:768]), (kb_ref, kvb[:, :128]), (vb_ref, proj[:, 1408:1536]))
    for ref, val in outs:
        if transposed_kv:
            seq = ref.shape[2]
            for r in range(ref.shape[0]):
                ref[r] = val[r * seq:(r + 1) * seq].T.astype(ref.dtype)
        else:
            ref[...] = val.astype(ref.dtype)


def _qkv(x, mod, gain, w_bf, hgain, bd, cos_t, sin_t, *, rows_per_cond, seq, rope, kv_dtype, transposed_kv):
    n = x.shape[0]
    tiles_per_seq = max(1, seq // TM)
    row = lambda i: (i, 0)
    const = lambda i: (0, 0)
    pos = lambda i: (i % tiles_per_seq, 0)
    if transposed_kv:
        assert TM % seq == 0
        kv_shape = jax.ShapeDtypeStruct((n // seq, 128, seq), kv_dtype)
        kv_spec = pl.BlockSpec((TM // seq, 128, seq), lambda i: (i, 0, 0))
    else:
        kv_shape = jax.ShapeDtypeStruct((n, 128), kv_dtype)
        kv_spec = pl.BlockSpec((TM, 128), row)
    return pl.pallas_call(
        functools.partial(_qkv_kernel, rope=rope, transposed_kv=transposed_kv),
        grid=(n // TM,),
        in_specs=[pl.BlockSpec((TM, D_MODEL), row),
                  _mod_spec(rows_per_cond),
                  pl.BlockSpec((1, D_MODEL), const),
                  pl.BlockSpec((D_MODEL, QKV_COLS), const),
                  pl.BlockSpec((1, QKV_COLS), const),
                  pl.BlockSpec((256, 256), const),
                  pl.BlockSpec((TM, 256), pos),
                  pl.BlockSpec((TM, 256), pos)],
        out_specs=[pl.BlockSpec((TM, ATTN_OUT), row)] + [kv_spec] * 4,
        out_shape=[jax.ShapeDtypeStruct((n, ATTN_OUT), BF16)] + [kv_shape] * 4,
        compiler_params=_params("parallel"),
        name="qkv_rope" if rope else "qkv",
    )(x, mod, gain, w_bf, hgain, bd, cos_t, sin_t)


def _pad_variants(kk, ones=False):
    lane = lax.broadcasted_iota(I32, kk.shape, 1)
    left = lane < HEAD_DIM
    rolled = pltpu.roll(kk, HEAD_DIM, 1)
    fill_r = jnp.where(lane == HEAD_DIM, 1.0, 0.0) if ones else jnp.zeros_like(kk)
    fill_l = jnp.where(lane == 0, 1.0, 0.0) if ones else jnp.zeros_like(kk)
    return {(0, 0): jnp.where(left, kk, fill_r).astype(BF16),
            (0, 1): jnp.where(left, fill_l, rolled).astype(BF16),
            (1, 0): jnp.where(left, rolled, fill_r).astype(BF16),
            (1, 1): jnp.where(left, fill_l, kk).astype(BF16)}


def _pad_variants_t(kt):
    top = lax.broadcasted_iota(I32, kt.shape, 0) < HEAD_DIM
    zero = jnp.zeros((HEAD_DIM, kt.shape[1]), F32)
    return {(0, 0): jnp.where(top, kt, 0.0).astype(BF16),
            (0, 1): jnp.concatenate([zero, kt[:HEAD_DIM]], axis=0).astype(BF16),
            (1, 0): jnp.concatenate([kt[HEAD_DIM:], zero], axis=0).astype(BF16),
            (1, 1): jnp.where(top, 0.0, kt).astype(BF16)}


def _head_attention_small(qp, kblk, vblk, sink):
    s = _dot(qp, kblk, NN)
    m = s.max(axis=-1, keepdims=True)
    if sink is not None:
        m = jnp.maximum(m, sink)
    p = jnp.exp2(s - m)
    den = p.sum(axis=-1, keepdims=True)
    if sink is not None:
        den = den + jnp.exp2(sink - m)
    return _dot(p.astype(BF16), vblk, NT) / den


def _head_attention(qp, keys, vals, masks, sink, par):
    scores = []
    for kblk, mask in zip(keys, masks):
        s = _dot(qp, kblk, NT)
        if mask is not None:
            s = jnp.where(mask, s, NEG_INF)
        scores.append(s)
    m = scores[0].max(axis=-1, keepdims=True)
    for s in scores[1:]:
        m = jnp.maximum(m, s.max(axis=-1, keepdims=True))
    if sink is not None:
        m = jnp.maximum(m, sink)
    out = None
    for s, vblk in zip(scores, vals):
        o = _dot(jnp.exp2((s - m).astype(BF16)), vblk)
        out = o if out is None else out + o
    ones_lane = HEAD_DIM if par == 0 else 0
    den = out[:, ones_lane:ones_lane + 1]
    if sink is not None:
        den = den + jnp.exp2(sink - m)
    own = (lax.broadcasted_iota(I32, out.shape, 1) < HEAD_DIM) == (par == 0)
    return jnp.where(own, out / den, 0.0)


def _attn_ctx_kernel(sink_ref, q_ref, ka_ref, va_ref, kb_ref, vb_ref, o_ref):
    seq = ka_ref.shape[2]
    for r in range(ka_ref.shape[0]):
        rows = slice(r * seq, (r + 1) * seq)
        for mixer, (k_ref, v_ref) in enumerate(((ka_ref, va_ref), (kb_ref, vb_ref))):
            kvar = _pad_variants_t(k_ref[r])
            vvar = _pad_variants_t(v_ref[r])
            for t in range(4):
                tile = mixer * 4 + t
                kv = t // 2
                qp = q_ref[rows, tile * 128:(tile + 1) * 128]
                acc = None
                for par in range(2):
                    sink = sink_ref[2 * t + par] * LOG2E if mixer == 1 else None
                    o = _head_attention_small(qp, kvar[(kv, par)], vvar[(kv, par)], sink)
                    acc = o if acc is None else acc + o
                o_ref[rows, tile * 128:(tile + 1) * 128] = acc.astype(o_ref.dtype)


def _attn_ctx(sink, q, ka, va, kb, vb, seq):
    n = q.shape[0]
    nb = n // seq
    rb = CTX_RB if nb % CTX_RB == 0 else 1
    row = lambda b: (b, 0)
    kv_spec = pl.BlockSpec((rb, 128, seq), lambda b: (b, 0, 0))
    return pl.pallas_call(
        _attn_ctx_kernel,
        grid=(nb // rb,),
        in_specs=[pl.BlockSpec(memory_space=pltpu.SMEM),
                  pl.BlockSpec((rb * seq, ATTN_OUT), row), kv_spec, kv_spec, kv_spec, kv_spec],
        out_specs=pl.BlockSpec((rb * seq, ATTN_OUT), row),
        out_shape=jax.ShapeDtypeStruct((n, ATTN_OUT), BF16),
        compiler_params=_params("parallel"),
        name="attn_ctx",
    )(sink, q, ka, va, kb, vb)


def _attn_lat_kernel(sink_ref, q_ref, ka_ref, va_ref, kb_ref, vb_ref,
                     cka_ref, cva_ref, ckb_ref, cvb_ref, o_ref, *, tq, seq):
    qi = pl.program_id(1)
    ck = _pad_variants(cka_ref[0])
    cv = _pad_variants(cva_ref[0], ones=True)
    lk = _pad_variants(ka_ref[...].astype(F32))
    lv = _pad_variants(va_ref[...].astype(F32), ones=True)
    for t in range(4):
        kv = t // 2
        qp = q_ref[:, t * 128:(t + 1) * 128]
        acc = None
        for par in range(2):
            o = _head_attention(qp, [ck[(kv, par)], lk[(kv, par)]], [cv[(kv, par)], lv[(kv, par)]],
                                [None, None], None, par)
            acc = o if acc is None else acc + o
        o_ref[:, t * 128:(t + 1) * 128] = acc.astype(o_ref.dtype)
    span = BAND_TQ + 2 * WINDOW
    ck = _pad_variants(ckb_ref[0])
    cv = _pad_variants(cvb_ref[0], ones=True)
    for sub in range(tq // BAND_TQ):
        q0 = qi * tq + sub * BAND_TQ
        rows = slice(sub * BAND_TQ, (sub + 1) * BAND_TQ)
        lo = pl.multiple_of(jnp.clip(q0 - WINDOW, 0, seq - span), 128)
        qpos = q0 + lax.broadcasted_iota(I32, (BAND_TQ, span), 0)
        kpos = lo + lax.broadcasted_iota(I32, (BAND_TQ, span), 1)
        band = jnp.abs(qpos - kpos) <= WINDOW
        lk = _pad_variants(kb_ref[pl.ds(lo, span), :].astype(F32))
        lv = _pad_variants(vb_ref[pl.ds(lo, span), :].astype(F32), ones=True)
        for t in range(4):
            kv = t // 2
            tile = 4 + t
            qp = q_ref[rows, tile * 128:(tile + 1) * 128]
            acc = None
            for par in range(2):
                sink = sink_ref[2 * t + par] * LOG2E
                o = _head_attention(qp, [lk[(kv, par)], ck[(kv, par)]], [lv[(kv, par)], cv[(kv, par)]],
                                    [band, None], sink, par)
                acc = o if acc is None else acc + o
            o_ref[rows, tile * 128:(tile + 1) * 128] = acc.astype(o_ref.dtype)


def _attn_lat(sink, q, ka, va, kb, vb, cka, cva, ckb, cvb, seq, tq=512):
    n = q.shape[0]
    nb = n // seq
    nq = seq // tq
    qrow = lambda b, i: (b * nq + i, 0)
    brow = lambda b, i: (b, 0)
    kv_spec = pl.BlockSpec((seq, 128), brow)
    past = cka.shape[1]
    c_spec = pl.BlockSpec((1, past, 128), lambda b, i: (b, 0, 0))
    return pl.pallas_call(
        functools.partial(_attn_lat_kernel, tq=tq, seq=seq),
        grid=(nb, nq),
        in_specs=[pl.BlockSpec(memory_space=pltpu.SMEM),
                  pl.BlockSpec((tq, ATTN_OUT), qrow), kv_spec, kv_spec, kv_spec, kv_spec,
                  c_spec, c_spec, c_spec, c_spec],
        out_specs=pl.BlockSpec((tq, ATTN_OUT), qrow),
        out_shape=jax.ShapeDtypeStruct((n, ATTN_OUT), BF16),
        compiler_params=_params("parallel", "parallel"),
        name="attn_lat",
    )(sink, q, ka, va, kb, vb, cka, cva, ckb, cvb)


def _ffn_pre(x1, mod_ref, gain_ref, rt_ref, h2_ref, aff_ref):
    h2 = _norm_mod(x1, gain_ref[...], mod_ref[0, 3:4, :], mod_ref[0, 4:5, :])
    h2_ref[...] = h2.astype(h2_ref.dtype)
    logits = _dot3(rt_ref[...], h2, NT)
    e = jnp.exp(logits - logits.max(axis=0, keepdims=True))
    aff_ref[...] = e / e.sum(axis=0, keepdims=True)


def _postmix_kernel(o_ref, x_ref, mod_ref, w_ref, gain_ref, rt_ref, x1_ref, h2_ref, aff_ref):
    x1 = x_ref[...] + mod_ref[0, 2:3, :] * _dot(o_ref[...], w_ref[...])
    x1_ref[...] = x1
    _ffn_pre(x1, mod_ref, gain_ref, rt_ref, h2_ref, aff_ref)


def _postmix(o, x, mod, w_bf, gain, router_t, *, rows_per_cond):
    n = x.shape[0]
    row = lambda i: (i, 0)
    const = lambda i: (0, 0)
    return pl.pallas_call(
        _postmix_kernel,
        grid=(n // TM,),
        in_specs=[pl.BlockSpec((TM, D_MODEL), row),
                  pl.BlockSpec((TM, D_MODEL), row),
                  _mod_spec(rows_per_cond),
                  pl.BlockSpec(w_bf.shape, const),
                  pl.BlockSpec((1, D_MODEL), const),
                  pl.BlockSpec((N_EXPERTS, D_MODEL), const)],
        out_specs=[pl.BlockSpec((TM, D_MODEL), row),
                   pl.BlockSpec((TM, D_MODEL), row),
                   pl.BlockSpec((N_EXPERTS, TM), lambda i: (0, i))],
        out_shape=[jax.ShapeDtypeStruct((n, D_MODEL), F32),
                   jax.ShapeDtypeStruct((n, D_MODEL), BF16),
                   jax.ShapeDtypeStruct((N_EXPERTS, n), F32)],
        compiler_params=_params("parallel"),
        name="postmix_proj",
    )(o, x, mod, w_bf, gain, router_t)


def _ffn_pre_kernel(x1_ref, mod_ref, gain_ref, rt_ref, h2_ref, aff_ref):
    _ffn_pre(x1_ref[...], mod_ref, gain_ref, rt_ref, h2_ref, aff_ref)


def _ffn_pre_call(x1, mod, gain, router_t, *, rows_per_cond):
    n = x1.shape[0]
    row = lambda i: (i, 0)
    const = lambda i: (0, 0)
    return pl.pallas_call(
        _ffn_pre_kernel,
        grid=(n // TM,),
        in_specs=[pl.BlockSpec((TM, D_MODEL), row),
                  _mod_spec(rows_per_cond),
                  pl.BlockSpec((1, D_MODEL), const),
                  pl.BlockSpec((N_EXPERTS, D_MODEL), const)],
        out_specs=[pl.BlockSpec((TM, D_MODEL), row),
                   pl.BlockSpec((N_EXPERTS, TM), lambda i: (0, i))],
        out_shape=[jax.ShapeDtypeStruct((n, D_MODEL), BF16),
                   jax.ShapeDtypeStruct((N_EXPERTS, n), F32)],
        compiler_params=_params("parallel"),
        name="ffn_pre",
    )(x1, mod, gain, router_t)


def _route_kernel(aff_ref, slot_ref, gate_ref, pos_ref, tcnt_ref, *, seq, cap, nseg, tt):
    aff = jnp.concatenate([aff_ref[:, s * seq:(s + 1) * seq] for s in range(nseg)], axis=0)
    rows = aff.shape[0]
    capf = jnp.float32(cap)
    thr_bits = jnp.zeros((rows, 1), I32)

    def enough(cand):
        cnt = jnp.where(aff >= pltpu.bitcast(cand, F32), 1.0, 0.0).sum(axis=1, keepdims=True)
        return cnt >= capf

    for bit in range(30, 0, -2):
        hi, lo = 1 << bit, 1 << (bit - 1)
        both, only_hi, only_lo = thr_bits | hi | lo, thr_bits | hi, thr_bits | lo
        thr_bits = jnp.where(enough(both), both,
                             jnp.where(enough(only_hi), only_hi, jnp.where(enough(only_lo), only_lo, thr_bits)))
    thr_bits = jnp.where(enough(thr_bits | 1), thr_bits | 1, thr_bits)
    thr = pltpu.bitcast(thr_bits, F32)
    gt = aff > thr
    eq = aff == thr
    n_gt = jnp.where(gt, 1.0, 0.0).sum(axis=1, keepdims=True)
    pw = min(seq, 256)
    tri = jnp.where(lax.broadcasted_iota(I32, (pw, pw), 0) < lax.broadcasted_iota(I32, (pw, pw), 1),
                    1.0, 0.0).astype(BF16)

    def count_before(flag):
        ones = jnp.where(flag, 1.0, 0.0)
        parts = []
        run = jnp.zeros((rows, 1), F32)
        for c0 in range(0, seq, pw):
            blk = ones[:, c0:c0 + pw]
            parts.append(_dot(blk.astype(BF16), tri) + run)
            run = run + blk.sum(axis=1, keepdims=True)
        return jnp.concatenate(parts, axis=1) if len(parts) > 1 else parts[0]

    sel = gt | (eq & (count_before(eq) < capf - n_gt))
    rank = count_before(sel)
    expert = lax.broadcasted_iota(I32, (rows, seq), 0) & (N_EXPERTS - 1)
    slot = jnp.where(sel, expert * cap + rank.astype(I32), -1)
    gate = jnp.where(sel, aff, 0.0)
    pos = jnp.where(sel, rank, -1.0)
    nt = seq // tt
    tile_of = jnp.where((lax.broadcasted_iota(I32, (seq, nt), 0) // tt) == lax.broadcasted_iota(I32, (seq, nt), 1),
                        1.0, 0.0).astype(BF16)
    tcnt = _dot(jnp.where(sel, 1.0, 0.0).astype(BF16), tile_of)
    for s in range(nseg):
        rows_s = slice(s * N_EXPERTS, (s + 1) * N_EXPERTS)
        slot_ref[:, s * seq:(s + 1) * seq] = slot[rows_s, :]
        gate_ref[:, s * seq:(s + 1) * seq] = gate[rows_s, :]
        pos_ref[:, s * seq:(s + 1) * seq] = pos[rows_s, :]
        tcnt_ref[0, :, s * nt:(s + 1) * nt] = tcnt[rows_s, :]


def _route(aff_t, seq, cap, nseg, tt):
    n = aff_t.shape[1]
    nt = seq // tt
    steps = n // (nseg * seq)
    spec = pl.BlockSpec((N_EXPERTS, nseg * seq), lambda i: (0, i))
    slot, gate, pos, tcnt = pl.pallas_call(
        functools.partial(_route_kernel, seq=seq, cap=cap, nseg=nseg, tt=tt),
        grid=(steps,),
        in_specs=[spec],
        out_specs=[spec, spec, spec, pl.BlockSpec((1, N_EXPERTS, nseg * nt), lambda i: (i, 0, 0))],
        out_shape=[jax.ShapeDtypeStruct((N_EXPERTS, n), I32), jax.ShapeDtypeStruct((N_EXPERTS, n), F32),
                   jax.ShapeDtypeStruct((N_EXPERTS, n), F32),
                   jax.ShapeDtypeStruct((steps, N_EXPERTS, nseg * nt), F32)],
        compiler_params=_params("parallel"),
        name="route",
    )(aff_t)
    return slot, gate, pos, tcnt.transpose(1, 0, 2).reshape(N_EXPERTS, n // tt)


def _dispatch_kernel(slot_ref, h_ref, x_ref, *, cap, seq):
    m = N_EXPERTS * cap
    slot_id = lax.broadcasted_iota(I32, (m, seq), 0)
    for r in range(h_ref.shape[0] // seq):
        slots = slot_ref[:, r * seq:(r + 1) * seq]
        owner = jnp.broadcast_to(slots[:, None, :], (N_EXPERTS, cap, seq)).reshape(m, seq)
        sel = jnp.where(owner == slot_id, 1.0, 0.0).astype(BF16)
        x_ref[r * m:(r + 1) * m, :] = _dot(sel, h_ref[r * seq:(r + 1) * seq, :]).astype(x_ref.dtype)


def _dispatch(slot, h, seq, cap):
    n = h.shape[0]
    nb = n // seq
    rb = MOE_RB if nb % MOE_RB == 0 else 1
    return pl.pallas_call(
        functools.partial(_dispatch_kernel, cap=cap, seq=seq),
        grid=(nb // rb,),
        in_specs=[pl.BlockSpec((N_EXPERTS, rb * seq), lambda b: (0, b)),
                  pl.BlockSpec((rb * seq, D_MODEL), lambda b: (b, 0))],
        out_specs=pl.BlockSpec((rb * N_EXPERTS * cap, D_MODEL), lambda b: (b, 0)),
        out_shape=jax.ShapeDtypeStruct((nb * N_EXPERTS * cap, D_MODEL), BF16),
        compiler_params=_params("parallel"),
        name="moe_dispatch",
    )(slot, h)


def _window(cum_ref, base, e, cap, k=0):
    lo = _align_down(cum_ref[base + e]) + k * MOE_W
    return lo, pl.multiple_of(jnp.minimum(lo, cap - MOE_W), BF16_ROWS)


def _align_down(rank):
    return rank & ~(BF16_ROWS - 1)


def _extra_windows(cum_ref, base, e):
    lo = _align_down(cum_ref[base + e])
    return (cum_ref[base + N_EXPERTS + e] - lo + (MOE_W - 1)) // MOE_W


def _dispatch_win_kernel(cum_ref, slot_ref, h_ref, x_ref, *, cap, nt):
    b, i = pl.program_id(0), pl.program_id(1)
    tt = h_ref.shape[0]
    base = (b * (nt + 1) + i) * N_EXPERTS
    h = h_ref[...]
    row = lax.broadcasted_iota(I32, (MOE_W, tt), 0)

    @pl.when(i == 0)
    def _():
        x_ref[...] = jnp.zeros_like(x_ref)

    def hits(e, lo, ws):
        srow = slot_ref[e:e + 1, :]
        return (srow == row + (e * cap + ws)) & (srow >= e * cap + lo)

    for grp in range(N_EXPERTS // MOE_EG):
        wins = [(e,) + _window(cum_ref, base, e, cap) for e in range(grp * MOE_EG, (grp + 1) * MOE_EG)]
        sel = jnp.concatenate([hits(e, lo, ws) for e, lo, ws in wins], axis=0)
        x = _dot(jnp.where(sel, 1.0, 0.0).astype(BF16), h)
        for q, (e, lo, ws) in enumerate(wins):
            dst = pl.ds(e * cap + ws, MOE_W)
            x_ref[dst, :] += x[q * MOE_W:(q + 1) * MOE_W].astype(x_ref.dtype)

    for e in range(N_EXPERTS):
        def extra(k, carry, e=e):
            lo, ws = _window(cum_ref, base, e, cap, k)
            x = _dot(jnp.where(hits(e, lo, ws), 1.0, 0.0).astype(BF16), h)
            x_ref[pl.ds(e * cap + ws, MOE_W), :] += x.astype(x_ref.dtype)
            return carry
        lax.fori_loop(1, _extra_windows(cum_ref, base, e), extra, 0)


def _dispatch_win(cum, slot, h, seq, cap):
    n = h.shape[0]
    nb, nt = n // seq, seq // MOE_TT
    return pl.pallas_call(
        functools.partial(_dispatch_win_kernel, cap=cap, nt=nt),
        grid_spec=pltpu.PrefetchScalarGridSpec(
            num_scalar_prefetch=1,
            grid=(nb, nt),
            in_specs=[pl.BlockSpec((N_EXPERTS, MOE_TT), lambda b, i, c: (0, b * nt + i)),
                      pl.BlockSpec((MOE_TT, D_MODEL), lambda b, i, c: (b * nt + i, 0))],
            out_specs=pl.BlockSpec((N_EXPERTS * cap, D_MODEL), lambda b, i, c: (b, 0))),
        out_shape=jax.ShapeDtypeStruct((nb * N_EXPERTS * cap, D_MODEL), BF16),
        compiler_params=_params("parallel", "arbitrary"),
        name="moe_dispatch_win",
    )(cum, slot, h)


FFN_TF = 512
FFN_RC = 512


def _ffn_kernel(xa_ref, xb_ref, wg_ref, wu_ref, wd_ref, ya_ref, yb_ref, acc_ref):
    j = pl.program_id(1)
    last = pl.num_programs(1) - 1
    ra = xa_ref.shape[0] * xa_ref.shape[2]

    def row_chunks(ref):
        nb, _, cap, d = ref.shape
        rc = min(FFN_RC, nb * cap)
        for r0 in range(0, nb * cap, rc):
            if cap >= rc:
                b, c0 = divmod(r0, cap)
                yield r0, rc, (slice(b, b + 1), 0, slice(c0, c0 + rc), slice(None)), (1, rc, d)
            else:
                yield r0, rc, (slice(r0 // cap, (r0 + rc) // cap), 0, slice(None), slice(None)), (rc // cap, cap, d)

    def sweep(first_tile, last_tile):
        wg = wg_ref[0].astype(BF16)
        wu = wu_ref[0].astype(BF16)
        wd = wd_ref[0].astype(BF16)
        d = wd.shape[1]
        for x_ref, y_ref, base in ((xa_ref, ya_ref, 0), (xb_ref, yb_ref, ra)):
            for r0, rc, idx, shape in row_chunks(x_ref):
                x = x_ref[idx].reshape(rc, d)
                g = _dot(x, wg)
                u = _dot(x, wu)
                y = _dot((g * _sigmoid(g) * u).astype(BF16), wd)
                rows = slice(base + r0, base + r0 + rc)
                if not first_tile:
                    y = acc_ref[rows, :] + y
                if last_tile:
                    y_ref[idx] = y.astype(y_ref.dtype).reshape(shape)
                else:
                    acc_ref[rows, :] = y

    if D_FF == FFN_TF:
        sweep(True, True)
    else:
        pl.when(j == 0)(lambda: sweep(True, False))
        pl.when((j > 0) & (j < last))(lambda: sweep(False, False))
        pl.when(j == last)(lambda: sweep(False, True))


def _ffn(xa, xb, w_gate, w_up, w_down):
    ba, _, ca, d = xa.shape
    bb, _, cb, _ = xb.shape
    nj = D_FF // FFN_TF
    xa_spec = pl.BlockSpec((ba, 1, ca, d), lambda e, j: (0, e, 0, 0))
    xb_spec = pl.BlockSpec((bb, 1, cb, d), lambda e, j: (0, e, 0, 0))
    return pl.pallas_call(
        _ffn_kernel,
        grid=(N_EXPERTS, nj),
        in_specs=[xa_spec, xb_spec,
                  pl.BlockSpec((1, d, FFN_TF), lambda e, j: (e, 0, j)),
                  pl.BlockSpec((1, d, FFN_TF), lambda e, j: (e, 0, j)),
                  pl.BlockSpec((1, FFN_TF, d), lambda e, j: (e, j, 0))],
        out_specs=[xa_spec, xb_spec],
        out_shape=[jax.ShapeDtypeStruct(xa.shape, BF16), jax.ShapeDtypeStruct(xb.shape, BF16)],
        scratch_shapes=[pltpu.VMEM((ba * ca + bb * cb, d), F32)],
        compiler_params=_params("parallel", "arbitrary"),
        name="moe_ffn",
    )(xa, xb, w_gate, w_up, w_down)


def _expand(vals_bf, first_expert, width, total):
    e_of_lane = first_expert + lax.broadcasted_iota(I32, (N_EXPERTS, total), 1) // width
    pick = jnp.where(lax.broadcasted_iota(I32, (N_EXPERTS, total), 0) == e_of_lane, 1.0, 0.0).astype(BF16)
    return _dot(vals_bf, pick, TN)


def _combine_kernel(pos_ref, gate_ref, y_ref, x_ref, mod_ref, o_ref, *, cap, seq):
    m = N_EXPERTS * cap
    rank = (lax.broadcasted_iota(I32, (seq, m), 1) % cap).astype(F32)
    for r in range(x_ref.shape[0] // seq):
        rows = slice(r * seq, (r + 1) * seq)
        pos = _expand(pos_ref[:, rows].astype(BF16), 0, cap, m)
        gate = _expand(gate_ref[:, rows].astype(BF16), 0, cap, m)
        w = jnp.where(pos == rank, gate, 0.0).astype(BF16)
        o_ref[rows, :] = x_ref[rows, :] + mod_ref[0, 5:6, :] * _dot(w, y_ref[r * m:(r + 1) * m, :])


def _combine(pos, gate, y, x, mod, *, seq, cap):
    n = x.shape[0]
    nb = n // seq
    rb = MOE_RB if nb % MOE_RB == 0 else 1
    row = lambda b: (b, 0)
    return pl.pallas_call(
        functools.partial(_combine_kernel, cap=cap, seq=seq),
        grid=(nb // rb,),
        in_specs=[pl.BlockSpec((N_EXPERTS, rb * seq), lambda b: (0, b)),
                  pl.BlockSpec((N_EXPERTS, rb * seq), lambda b: (0, b)),
                  pl.BlockSpec((rb * N_EXPERTS * cap, D_MODEL), row),
                  pl.BlockSpec((rb * seq, D_MODEL), row),
                  pl.BlockSpec((1, 8, D_MODEL), lambda b: (0, 0, 0))],
        out_specs=pl.BlockSpec((rb * seq, D_MODEL), row),
        out_shape=jax.ShapeDtypeStruct((n, D_MODEL), F32),
        compiler_params=_params("parallel"),
        name="moe_combine",
    )(pos, gate, y, x, mod)


def _combine_win_kernel(cum_ref, pos_ref, gate_ref, y_ref, x_ref, mod_ref, o_ref, acc_ref, *, cap, nt):
    b, i = pl.program_id(0), pl.program_id(1)
    tt = x_ref.shape[0]
    base = (b * (nt + 1) + i) * N_EXPERTS
    width = MOE_EG * MOE_W
    posb = pos_ref[...].astype(BF16)
    gateb = gate_ref[...].astype(BF16)
    lane = lax.broadcasted_iota(I32, (1, width), 1)
    offset = (lane % MOE_W).astype(F32)
    acc = jnp.zeros((tt, D_MODEL), F32)
    for grp in range(N_EXPERTS // MOE_EG):
        wins = [(e,) + _window(cum_ref, base, e, cap) for e in range(grp * MOE_EG, (grp + 1) * MOE_EG)]
        lo_l = jnp.zeros((1, width), F32)
        ws_l = jnp.zeros((1, width), F32)
        for q, (e, lo, ws) in enumerate(wins):
            mine = lane // MOE_W == q
            lo_l = jnp.where(mine, lo.astype(F32), lo_l)
            ws_l = jnp.where(mine, ws.astype(F32), ws_l)
        pos = _expand(posb, grp * MOE_EG, MOE_W, width)
        gate = _expand(gateb, grp * MOE_EG, MOE_W, width)
        w = jnp.where((pos - ws_l == offset) & (pos >= lo_l), gate, 0.0).astype(BF16)
        ywin = jnp.concatenate([y_ref[pl.ds(e * cap + ws, MOE_W), :] for e, lo, ws in wins], axis=0)
        acc = acc + _dot(w, ywin)
    acc_ref[...] = acc

    offset_rows = lax.broadcasted_iota(I32, (MOE_W, tt), 0).astype(F32)
    for e in range(N_EXPERTS):
        def extra(k, carry, e=e):
            lo, ws = _window(cum_ref, base, e, cap, k)
            pos = pos_ref[e:e + 1, :]
            gate = gate_ref[e:e + 1, :].astype(BF16).astype(F32)
            w_t = jnp.where((pos - ws.astype(F32) == offset_rows) & (pos >= lo.astype(F32)), gate, 0.0).astype(BF16)
            acc_ref[...] += _dot(w_t, y_ref[pl.ds(e * cap + ws, MOE_W), :], TN)
            return carry
        lax.fori_loop(1, _extra_windows(cum_ref, base, e), extra, 0)

    o_ref[...] = x_ref[...] + mod_ref[0, 5:6, :] * acc_ref[...]


def _combine_win(cum, pos, gate, y, x, mod, *, seq, cap):
    n = x.shape[0]
    nb, nt = n // seq, seq // MOE_TT
    row = lambda b, i, c: (b * nt + i, 0)
    return pl.pallas_call(
        functools.partial(_combine_win_kernel, cap=cap, nt=nt),
        grid_spec=pltpu.PrefetchScalarGridSpec(
            num_scalar_prefetch=1,
            grid=(nb, nt),
            in_specs=[pl.BlockSpec((N_EXPERTS, MOE_TT), lambda b, i, c: (0, b * nt + i)),
                      pl.BlockSpec((N_EXPERTS, MOE_TT), lambda b, i, c: (0, b * nt + i)),
                      pl.BlockSpec((N_EXPERTS * cap, D_MODEL), lambda b, i, c: (b, 0)),
                      pl.BlockSpec((MOE_TT, D_MODEL), row),
                      pl.BlockSpec((1, 8, D_MODEL), lambda b, i, c: (1 + b, 0, 0))],
            out_specs=pl.BlockSpec((MOE_TT, D_MODEL), row),
            scratch_shapes=[pltpu.VMEM((MOE_TT, D_MODEL), F32)]),
        out_shape=jax.ShapeDtypeStruct((n, D_MODEL), F32),
        compiler_params=_params("parallel", "parallel"),
        name="moe_combine_win",
    )(cum, pos, gate, y, x, mod)


def _moe_pair(hp, affp, x1p, hs, affs, x1s, mod, w_gate, w_up, w_down, seq_p, seq_s):
    n_p, n_s = hp.shape[0], hs.shape[0]
    nb_p, nb_s = n_p // seq_p, n_s // seq_s
    cap_p = EC_FACTOR * seq_p // N_EXPERTS
    cap_s = EC_FACTOR * seq_s // N_EXPERTS
    assert N_EXPERTS * cap_p <= 512 and cap_s >= MOE_W and cap_s % BF16_ROWS == 0 and seq_s % MOE_TT == 0
    slot_p, gate_p, pos_p, _ = _route(affp, seq_p, cap_p, nseg=min(8, nb_p), tt=seq_p)
    slot_s, gate_s, pos_s, tcnt = _route(affs, seq_s, cap_s, nseg=min(4, nb_s), tt=MOE_TT)
    nt = seq_s // MOE_TT
    counts = tcnt.T.reshape(nb_s, nt, N_EXPERTS).astype(I32)
    cum = jnp.concatenate([jnp.zeros((nb_s, 1, N_EXPERTS), I32), jnp.cumsum(counts, axis=1)], axis=1).reshape(-1)
    xp = _dispatch(slot_p, hp, seq_p, cap_p).reshape(nb_p, N_EXPERTS, cap_p, D_MODEL)
    xs = _dispatch_win(cum, slot_s, hs, seq_s, cap_s).reshape(nb_s, N_EXPERTS, cap_s, D_MODEL)
    ys, yp = _ffn(xs, xp, w_gate, w_up, w_down)
    outp = _combine(pos_p, gate_p, yp.reshape(-1, D_MODEL), x1p, mod, seq=seq_p, cap=cap_p)
    outs = _combine_win(cum, pos_s, gate_s, ys.reshape(-1, D_MODEL), x1s, mod, seq=seq_s, cap=cap_s)
    return outp, outs


def _cond_blocks(mod_ref, tc, cols_per_cond):
    if cols_per_cond is None:
        return [(slice(0, tc), mod_ref[0])]
    first = 1 + pl.program_id(1) * (tc // cols_per_cond)
    return [(slice(s * cols_per_cond, (s + 1) * cols_per_cond), mod_ref[first + s]) for s in range(tc // cols_per_cond)]


def _ssm_in_kernel(x_ref, mod_ref, gain_ref, wt_ref, ut_ref, *, cols_per_cond):
    parts = [_norm_mod(x_ref[0, rows, :], gain_ref[...], m[0:1, :], m[1:2, :]).astype(BF16)
             for rows, m in _cond_blocks(mod_ref, x_ref.shape[1], cols_per_cond)]
    h = jnp.concatenate(parts, axis=0) if len(parts) > 1 else parts[0]
    ut_ref[0] = _dot(wt_ref[...], h, NT)


def _ssm_in(xperm, mod, gain, wt_bf, *, cols_per_cond, tc):
    l, bk, d = xperm.shape
    assert cols_per_cond is None or tc % cols_per_cond == 0
    return pl.pallas_call(
        functools.partial(_ssm_in_kernel, cols_per_cond=cols_per_cond),
        grid=(l, bk // tc),
        in_specs=[pl.BlockSpec((1, tc, d), lambda j, i: (j, i, 0)),
                  pl.BlockSpec(mod.shape, lambda j, i: (0, 0, 0)),
                  pl.BlockSpec((1, d), lambda j, i: (0, 0)),
                  pl.BlockSpec((d, d), lambda j, i: (0, 0))],
        out_specs=pl.BlockSpec((1, d, tc), lambda j, i: (j, 0, i)),
        out_shape=jax.ShapeDtypeStruct((l, d, bk), F32),
        compiler_params=_params("parallel", "parallel"),
        name="ssm_in",
    )(xperm, mod, gain, wt_bf)


def _ssm_core_kernel(utp_ref, uts_ref, lamp_ref, c_ref, bt_ref, dsk_ref, h0_ref, ytp_ref, yts_ref, fs_ref,
                     *, kp, ks, nbp, nbs):
    rows = SSM_ROWS
    p = SSM_STATE
    lc = SSM_CHUNK
    ri = lax.broadcasted_iota(I32, (rows, rows), 0)
    cj = lax.broadcasted_iota(I32, (rows, rows), 1)
    causal = ri // SSM_GROUP >= cj // SSM_GROUP
    anticausal = cj // SSM_GROUP >= ri // SSM_GROUP
    diag = ri == cj
    leftc = lax.broadcasted_iota(I32, (lc, 128), 1) < p
    leftg = lax.broadcasted_iota(I32, (SSM_GROUP, 128), 1) < p
    nrow = lax.broadcasted_iota(I32, (lc, 128), 0).astype(F32)
    eye = lax.broadcasted_iota(I32, (p, 128), 0) == lax.broadcasted_iota(I32, (p, 128), 1)

    def cmul(ar, ai, xr, xi):
        return ar * xr - ai * xi, ar * xi + ai * xr

    def expand_rows(t):
        return jnp.broadcast_to(t[:, None, :], (lc, SSM_GROUP, 128)).reshape(rows, 128)

    def tile_rows(t):
        return jnp.broadcast_to(t[None, :, :], (lc, SSM_GROUP, 128)).reshape(rows, 128)

    def to_col(row):
        return jnp.where(eye, jnp.broadcast_to(row, (p, 128)), 0.0).sum(axis=1, keepdims=True)

    def operands(gg, d):
        lp = lamp_ref[gg, d]
        lre, lim = lp[0:1], lp[1:2]
        dt = jnp.exp(lp[2:3])
        a, th = lre * dt, lim * dt
        ang = nrow * th
        cs, sn = jnp.cos(ang), jnp.sin(ang)
        ep, em = jnp.exp(nrow * a), jnp.exp(-(nrow * a))
        pr, pi = ep * cs, ep * sn
        nr, ni = em * cs, -(em * sn)
        l1r, l1i = pr[1:2], pi[1:2]
        lmr, lmi = pr[lc - 1:lc], pi[lc - 1:lc]
        llr, lli = cmul(lmr, lmi, l1r, l1i)
        den = lre * lre + lim * lim
        cr = ((l1r - 1.0) * lre + l1i * lim) / den
        ci = (l1i * lre - (l1r - 1.0) * lim) / den
        btr, bti = bt_ref[gg, d, 0], bt_ref[gg, d, 1]
        bbr, bbi = cr * btr - ci * bti, cr * bti + ci * btr
        u1 = tile_rows(jnp.where(leftg, bbr, bbi))
        u2 = tile_rows(jnp.where(leftg, bbi, bbr))
        c1 = tile_rows(c_ref[gg, d, 0])
        c2 = tile_rows(c_ref[gg, d, 1])

        def left_form(xr, xi):
            return (c1 * expand_rows(jnp.where(leftc, xr, -xi))
                    + c2 * expand_rows(jnp.where(leftc, -xi, -xr)))

        def right_form(xr, xi):
            return u1 * expand_rows(xr) + u2 * expand_rows(jnp.where(leftc, -xi, xi))

        if d == 0:
            al = left_form(pr, pi)
            brt = right_form(nr, ni)
            rrt = right_form(*cmul(lmr, lmi, nr, ni))
            qq = left_form(*cmul(l1r, l1i, pr, pi))
            mat = jnp.where(causal, _dot3(al, brt, NT), 0.0)
        else:
            al = left_form(nr, ni)
            brt = right_form(pr, pi)
            rrt = brt
            qq = left_form(*cmul(llr, lli, nr, ni))
            mat = jnp.where(anticausal, _dot3(al, brt, NT), 0.0)
        return mat, rrt.T, qq, to_col(llr), to_col(lli)

    def scan(sr, si, lr, li, h0r, h0i, reverse, nchunk, nbatch, sel):
        bk = sr.shape[1]
        lane = lax.broadcasted_iota(I32, (p, bk), 1)
        kidx = lane & (nchunk - 1)
        edge = (nchunk - 1) if reverse else 0
        if h0r is not None:
            h0cr = jnp.zeros((p, bk), F32)
            h0ci = jnp.zeros((p, bk), F32)
            for b in range(nbatch):
                at = lane == (b * nchunk + edge)
                h0cr = jnp.where(at, h0r[:, b:b + 1], h0cr)
                h0ci = jnp.where(at, h0i[:, b:b + 1], h0ci)
            ar, ai = cmul(lr, li, h0cr, h0ci)
            er, ei = sr + ar, si + ai
        else:
            er, ei = sr, si
        ar, ai = lr, li
        s = 1
        while s < nchunk:
            if reverse:
                ok = kidx < nchunk - s
                tr, ti = pltpu.roll(er, bk - s, 1), pltpu.roll(ei, bk - s, 1)
            else:
                ok = kidx >= s
                tr, ti = pltpu.roll(er, s, 1), pltpu.roll(ei, s, 1)
            tr = jnp.where(ok, tr, 0.0)
            ti = jnp.where(ok, ti, 0.0)
            dr, di = cmul(ar, ai, tr, ti)
            er, ei = er + dr, ei + di
            ar, ai = cmul(ar, ai, ar, ai)
            s *= 2
        if reverse:
            inner = kidx < nchunk - 1
            hr, hi = pltpu.roll(er, bk - 1, 1), pltpu.roll(ei, bk - 1, 1)
        else:
            inner = kidx >= 1
            hr, hi = pltpu.roll(er, 1, 1), pltpu.roll(ei, 1, 1)
        hr = jnp.where(inner, hr, h0cr if h0r is not None else 0.0)
        hi = jnp.where(inner, hi, h0ci if h0r is not None else 0.0)
        fin = None if sel is None else (_sel_dot_t(sel, er), _sel_dot_t(sel, ei))
        return hr, hi, fin

    def final_selectors(bk, nchunk, nbatch):
        bat = lax.broadcasted_iota(I32, (nbatch, bk), 0)
        col = lax.broadcasted_iota(I32, (nbatch, bk), 1)
        last = jnp.where(col == bat * nchunk + (nchunk - 1), 1.0, 0.0).astype(BF16)
        first = jnp.where(col == bat * nchunk, 1.0, 0.0).astype(BF16)
        return last, first

    sel_last, sel_first = final_selectors(utp_ref.shape[2], kp, nbp)

    for gg in range(SSM_GB):
        mf, rf, qf, lfr, lfi = operands(gg, 0)
        mb, rb, qb, lbr, lbi = operands(gg, 1)
        skip = jnp.where(diag, jnp.broadcast_to(dsk_ref[gg], (rows, rows)), 0.0)
        stack = jnp.concatenate([mf + mb + skip, rf, rb], axis=0)
        qq = jnp.concatenate([qf, qb], axis=1)
        h0 = h0_ref[gg]
        for ut_ref, yt_ref, nchunk, nbatch, latent in ((utp_ref, ytp_ref, kp, nbp, False),
                                                       (uts_ref, yts_ref, ks, nbs, True)):
            bk = ut_ref.shape[2]
            x = ut_ref[:, gg * SSM_GROUP:(gg + 1) * SSM_GROUP, :].reshape(rows, bk)
            res = _mm(stack, x, SSM_PASSES)
            hfr, hfi, ff = scan(res[rows:rows + p], res[rows + p:rows + 2 * p], lfr, lfi,
                                h0[0] if latent else None, h0[1] if latent else None,
                                False, nchunk, nbatch, None if latent else sel_last)
            hbr, hbi, fb = scan(res[rows + 2 * p:rows + 3 * p], res[rows + 3 * p:rows + 4 * p], lbr, lbi,
                                h0[2] if latent else None, h0[3] if latent else None,
                                True, nchunk, nbatch, None if latent else sel_first)
            states = jnp.concatenate([hfr, hfi, hbr, hbi], axis=0)
            y = res[:rows] + _mm(qq, states, SSM_PASSES)
            yt_ref[:, gg * SSM_GROUP:(gg + 1) * SSM_GROUP, :] = y.reshape(lc, SSM_GROUP, bk)
            if not latent:
                fs_ref[gg, 0] = ff[0]
                fs_ref[gg, 1] = ff[1]
                fs_ref[gg, 2] = fb[0]
                fs_ref[gg, 3] = fb[1]


def _ssm_core(utp, uts, ops, h0, *, kp, ks, nbp, nbs):
    lamp, c2, bt2, dsk = ops
    l, d, bkp = utp.shape
    bks = uts.shape[2]
    g = SSM_GROUPS
    gb = SSM_GB
    lead4 = lambda i: (i, 0, 0, 0)
    lead5 = lambda i: (i, 0, 0, 0, 0)
    ut_spec = lambda bk: pl.BlockSpec((l, gb * SSM_GROUP, bk), lambda i: (0, i, 0))
    return pl.pallas_call(
        functools.partial(_ssm_core_kernel, kp=kp, ks=ks, nbp=nbp, nbs=nbs),
        grid=(g // gb,),
        in_specs=[ut_spec(bkp), ut_spec(bks),
                  pl.BlockSpec((gb, 2, 8, 128), lead4),
                  pl.BlockSpec((gb, 2, 2, SSM_GROUP, 128), lead5),
                  pl.BlockSpec((gb, 2, 2, SSM_GROUP, 128), lead5),
                  pl.BlockSpec((gb, 1, SSM_ROWS), lambda i: (i, 0, 0)),
                  pl.BlockSpec((gb, 4, SSM_STATE, nbs), lead4)],
        out_specs=[ut_spec(bkp), ut_spec(bks),
                   pl.BlockSpec((gb, 4, nbp, SSM_STATE), lead4)],
        out_shape=[jax.ShapeDtypeStruct((l, d, bkp), F32),
                   jax.ShapeDtypeStruct((l, d, bks), F32),
                   jax.ShapeDtypeStruct((g, 4, nbp, SSM_STATE), F32)],
        compiler_params=_params("parallel"),
        name="ssm_core",
    )(utp, uts, lamp, c2, bt2, dsk, h0)


def _ssm_out_kernel(yt_ref, x_ref, mod_ref, w_ref, x1_ref, *, cols_per_cond):
    y = yt_ref[0].T
    act = 0.5 * y * (1.0 + jnp.tanh(0.7978845608028654 * (y + 0.044715 * (y * y * y))))
    ag = _dot(act.astype(BF16), w_ref[...])
    d = x1_ref.shape[2]
    mix = ag[:, :d] * _sigmoid(ag[:, d:])
    for rows, m in _cond_blocks(mod_ref, x_ref.shape[1], cols_per_cond):
        x1_ref[0, rows, :] = x_ref[0, rows, :] + m[2:3, :] * mix[rows, :]


def _ssm_out(yt, xperm, mod, w_bf, *, cols_per_cond, tc):
    l, d, bk = yt.shape
    assert cols_per_cond is None or tc % cols_per_cond == 0
    return pl.pallas_call(
        functools.partial(_ssm_out_kernel, cols_per_cond=cols_per_cond),
        grid=(l, bk // tc),
        in_specs=[pl.BlockSpec((1, d, tc), lambda j, i: (j, 0, i)),
                  pl.BlockSpec((1, tc, d), lambda j, i: (j, i, 0)),
                  pl.BlockSpec(mod.shape, lambda j, i: (0, 0, 0)),
                  pl.BlockSpec((d, 2 * d), lambda j, i: (0, 0))],
        out_specs=pl.BlockSpec((1, tc, d), lambda j, i: (j, i, 0)),
        out_shape=jax.ShapeDtypeStruct((l, bk, d), F32),
        compiler_params=_params("parallel", "parallel"),
        name="ssm_out",
    )(yt, xperm, mod, w_bf)


def _ssm_operand_params(lam_re, lam_im, b_re, b_im, c_re, c_im, log_dt, d_skip):
    g, l = SSM_GROUPS, SSM_CHUNK
    dup = lambda t: jnp.concatenate([t, t], axis=-1)
    lamp = jnp.stack([lam_re, lam_im, jnp.broadcast_to(log_dt[..., None], lam_re.shape)], axis=2)
    lamp = dup(jnp.pad(lamp, ((0, 0), (0, 0), (0, 5), (0, 0)))).transpose(1, 0, 2, 3)
    c2 = dup(jnp.stack([c_re, c_im], axis=2)).transpose(1, 0, 2, 3, 4)
    bt2 = dup(jnp.stack([jnp.swapaxes(b_re, -1, -2), jnp.swapaxes(b_im, -1, -2)], axis=2)).transpose(1, 0, 2, 3, 4)
    dsk = jnp.tile(d_skip.reshape(g, 1, SSM_GROUP), (1, 1, l))
    return lamp, c2, bt2, dsk


def _to_chunks(x, nb, seq):
    k = seq // SSM_CHUNK
    return x.reshape(nb, k, SSM_CHUNK, -1).transpose(2, 0, 1, 3).reshape(SSM_CHUNK, nb * k, -1)


def _from_chunks(x, nb, seq):
    k = seq // SSM_CHUNK
    return x.reshape(SSM_CHUNK, nb, k, -1).transpose(1, 2, 0, 3).reshape(nb * seq, -1)


def _ssm_layer(xp, xs, mod, gain, wt_bf, ops, w_out_bf, h0, *, nbp, sp, nbs, ss):
    kp, ks = sp // SSM_CHUNK, ss // SSM_CHUNK
    tcp, tcs = min(SSM_TC, nbp * kp), min(SSM_TC, nbs * ks)
    xcp, xcs = _to_chunks(xp, nbp, sp), _to_chunks(xs, nbs, ss)
    utp = _ssm_in(xcp, mod, gain, wt_bf, cols_per_cond=None, tc=tcp)
    uts = _ssm_in(xcs, mod, gain, wt_bf, cols_per_cond=ks, tc=tcs)
    ytp, yts, fs = _ssm_core(utp, uts, ops, h0, kp=kp, ks=ks, nbp=nbp, nbs=nbs)
    x1p = _from_chunks(_ssm_out(ytp, xcp, mod, w_out_bf, cols_per_cond=None, tc=tcp), nbp, sp)
    x1s = _from_chunks(_ssm_out(yts, xcs, mod, w_out_bf, cols_per_cond=ks, tc=tcs), nbs, ss)
    return x1p, x1s, fs


def _rope_tables(seq):
    t = jnp.arange(seq)
    row = (t // GRID_W).astype(F32)
    col = (t % GRID_W).astype(F32)
    n_freq = HEAD_DIM // 4
    inv_freq = ROPE_THETA ** (-jnp.arange(n_freq, dtype=F32) / n_freq)
    ang = jnp.concatenate([row[:, None] * inv_freq, col[:, None] * inv_freq], axis=-1)
    cos = jnp.repeat(jnp.cos(ang), 2, axis=-1)
    sin = jnp.repeat(jnp.sin(ang), 2, axis=-1)
    sign = jnp.tile(jnp.array([-1.0, 1.0], F32), HEAD_DIM // 2)
    return jnp.tile(cos, (1, 4)), jnp.tile(sin * sign, (1, 4))


def _head_gains(qn_a, kn_a, qn_b, kn_b):
    scale = HEAD_DIM ** -0.5 * LOG2E
    ones = jnp.ones((N_KV * HEAD_DIM,), F32)
    return jnp.concatenate([jnp.tile(qn_a, N_HEADS) * scale, jnp.tile(kn_a, N_KV), ones,
                            jnp.tile(qn_b, N_HEADS) * scale, jnp.tile(kn_b, N_KV), ones]).reshape(1, QKV_COLS)


def kernel(x_prompt, x_sample, c, cache_k_a_l0, cache_v_a_l0, cache_k_b_l0, cache_v_b_l0, state_ssm_re_l1, state_ssm_im_l1, c_ctx, mod_w_l0, mod_b_l0, norm_mix_l0, attn_w_in_l0, q_norm_a_l0, k_norm_a_l0, q_norm_b_l0, k_norm_b_l0, sink_b_l0, attn_w_out_l0, norm_ffn_l0, router_l0, moe_w_gate_l0, moe_w_up_l0, moe_w_down_l0, mod_w_l1, mod_b_l1, norm_mix_l1, ssm_w_in_l1, ssm_lambda_re_l1, ssm_lambda_im_l1, ssm_b_re_l1, ssm_b_im_l1, ssm_c_re_l1, ssm_c_im_l1, ssm_log_dt_l1, ssm_d_l1, ssm_w_out_l1, norm_ffn_l1, router_l1, moe_w_gate_l1, moe_w_up_l1, moe_w_down_l1):
    bp, sp, d = x_prompt.shape
    bs, ss, _ = x_sample.shape
    past = cache_k_a_l0.shape[1]
    assert d == D_MODEL and bs <= 7 and (bp * sp) % TM == 0 and TM % sp == 0 and ss % TM == 0
    xp = x_prompt.reshape(bp * sp, d)
    xs = x_sample.reshape(bs * ss, d)
    cond8 = jnp.concatenate([c_ctx[None], c, jnp.zeros((7 - bs, d), F32)], axis=0)
    row1 = lambda v: v.reshape(1, -1)

    mod0 = _mod_rows(cond8, mod_w_l0, mod_b_l0)
    w_in = attn_w_in_l0.astype(BF16)
    hgain = _head_gains(q_norm_a_l0, k_norm_a_l0, q_norm_b_l0, k_norm_b_l0)
    lane = np.arange(256)
    bd = jnp.asarray((lane[:, None] // HEAD_DIM == lane[None, :] // HEAD_DIM) / HEAD_DIM, BF16)
    cos_t, sin_t = _rope_tables(ss)
    qp, kap, vap, kbp, vbp = _qkv(xp, mod0, row1(norm_mix_l0), w_in, hgain, bd, cos_t, sin_t,
                                  rows_per_cond=None, seq=sp, rope=False, kv_dtype=F32, transposed_kv=True)
    qs, kas, vas, kbs, vbs = _qkv(xs, mod0, row1(norm_mix_l0), w_in, hgain, bd, cos_t, sin_t,
                                  rows_per_cond=ss, seq=ss, rope=True, kv_dtype=BF16, transposed_kv=False)
    op = _attn_ctx(sink_b_l0, qp, kap, vap, kbp, vbp, sp)
    cache = lambda t: t.reshape(bs, past, N_KV * HEAD_DIM)
    os_ = _attn_lat(sink_b_l0, qs, kas, vas, kbs, vbs, cache(cache_k_a_l0), cache(cache_v_a_l0),
                    cache(cache_k_b_l0), cache(cache_v_b_l0), ss)
    w_out = attn_w_out_l0.astype(BF16)
    x1p, hp, affp = _postmix(op, xp, mod0, w_out, row1(norm_ffn_l0), router_l0.T, rows_per_cond=None)
    x1s, hs, affs = _postmix(os_, xs, mod0, w_out, row1(norm_ffn_l0), router_l0.T, rows_per_cond=ss)
    xp, xs = _moe_pair(hp, affp, x1p, hs, affs, x1s, mod0, moe_w_gate_l0, moe_w_up_l0, moe_w_down_l0, sp, ss)

    mod1 = _mod_rows(cond8, mod_w_l1, mod_b_l1)
    ops = _ssm_operand_params(ssm_lambda_re_l1, ssm_lambda_im_l1, ssm_b_re_l1, ssm_b_im_l1, ssm_c_re_l1, ssm_c_im_l1,
                              ssm_log_dt_l1, ssm_d_l1)
    wt = ssm_w_in_l1.T.astype(BF16)
    w_so = ssm_w_out_l1.astype(BF16)
    h0 = jnp.stack([state_ssm_re_l1[:, 0], state_ssm_im_l1[:, 0], state_ssm_re_l1[:, 1], state_ssm_im_l1[:, 1]],
                   axis=0).transpose(2, 0, 3, 1)
    x1p, x1s, fsp = _ssm_layer(xp, xs, mod1, row1(norm_mix_l1), wt, ops, w_so, h0, nbp=bp, sp=sp, nbs=bs, ss=ss)
    hp, affp = _ffn_pre_call(x1p, mod1, row1(norm_ffn_l1), router_l1.T, rows_per_cond=None)
    hs, affs = _ffn_pre_call(x1s, mod1, row1(norm_ffn_l1), router_l1.T, rows_per_cond=ss)
    xp, xs = _moe_pair(hp, affp, x1p, hs, affs, x1s, mod1, moe_w_gate_l1, moe_w_up_l1, moe_w_down_l1, sp, ss)

    kv_out = lambda t: t.reshape(bp, N_KV, HEAD_DIM, sp).transpose(0, 3, 1, 2)
    fin = fsp.transpose(2, 1, 0, 3)
    ssm_re = jnp.stack([fin[:, 0], fin[:, 2]], axis=1)
    ssm_im = jnp.stack([fin[:, 1], fin[:, 3]], axis=1)
    return (xp.reshape(bp, sp, d), xs.reshape(bs, ss, d), kv_out(kap), kv_out(vap), kv_out(kbp), kv_out(vbp),
            ssm_re, ssm_im)
```
